```python
import math
import jax
import jax.numpy as jnp
from jax import lax
import numpy as np

D_MODEL = 1024
BATCH = 16
SEQ = 256
DEPTH = 4
DEC_BATCH = 8
DEC_SEQ = 2048
PAST_LEN = 512

GRID_W = 64
N_BRANCH = 4
BRANCH_W = D_MODEL // 4
HEAD_DIM = 64
Q_BLOCK = 128
NA_HEADS = BRANCH_W // HEAD_DIM
NA_WIN_R = 8
NA_WIN_C = 16
SWA_HEADS = BRANCH_W // HEAD_DIM
SWA_KV_HEADS = SWA_HEADS // 2
SWA_WINDOW = 128
SWA_BLOCK = 128
HY_WIDTH = BRANCH_W
HY_ORDER = 2
HY_FREQS = 16
HY_EMB = 1 + 2 * HY_FREQS
HY_HIDDEN = 64
HY_MIN_DECAY = 3.07
HY_MAX_DECAY = 15.35
DIFF_HD = 32
DIFF_HEADS = BRANCH_W // (2 * DIFF_HD)
D_FF = 128 * ((8 * D_MODEL // 3 + 127) // 128)
ROPE_BASE = 10000.0
EPS = 1e-6
NEG = -1e30
N_MOD = 9
NA_W = 3 * NA_HEADS * HEAD_DIM
SWA_QW = SWA_HEADS * HEAD_DIM
SWA_KVW = SWA_KV_HEADS * HEAD_DIM
SWA_W = SWA_QW + 2 * SWA_KVW
HY_IN_W = 3 * HY_WIDTH
DIFF_W = 3 * DIFF_HEADS * 2 * DIFF_HD
GATE_W = N_BRANCH * D_MODEL
SPLIT_IDX = (NA_W, NA_W + SWA_W, NA_W + SWA_W + HY_IN_W, NA_W + SWA_W + HY_IN_W + DIFF_W)
IN_W = SPLIT_IDX[-1] + GATE_W

kernel_name = 'hybrid_diffusion_trunk_step'

F32 = jnp.float32


def rmsnorm(x, g):
    xf = x.astype(F32)
    y = xf * lax.rsqrt(jnp.mean(xf * xf, axis=-1, keepdims=True) + EPS)
    return (y * g.astype(F32)).astype(x.dtype)


def swiglu(h, w1, w3, w2):
    return (jax.nn.silu(h @ w1) * (h @ w3)) @ w2


def to_heads(t, n):
    B, L, _ = t.shape
    return t.reshape(B, L, n, -1).transpose(0, 2, 1, 3)


def from_heads(t):
    B, H, L, d = t.shape
    return t.transpose(0, 2, 1, 3).reshape(B, L, H * d)


def expand_kv(t):
    return jnp.repeat(t, SWA_HEADS // SWA_KV_HEADS, axis=1)


def rope_tables(L, dh):
    nf = dh // 4
    t = jnp.arange(L)
    freqs = ROPE_BASE ** (-jnp.arange(nf, dtype=F32) / nf)
    ang_r = (t // GRID_W).astype(F32)[:, None] * freqs
    ang_c = (t % GRID_W).astype(F32)[:, None] * freqs
    return (jnp.cos(ang_r), jnp.sin(ang_r), jnp.cos(ang_c), jnp.sin(ang_c))


def apply_rope(x, tabs):
    cr, sr, cc, sc = tabs
    nf = x.shape[-1] // 4
    xf = x.astype(F32)

    def rot(xa, cos, sin):
        a, b = xa[..., :nf], xa[..., nf:]
        return jnp.concatenate([a * cos - b * sin, b * cos + a * sin], axis=-1)

    half = 2 * nf
    out = jnp.concatenate([rot(xf[..., :half], cr, sr), rot(xf[..., half:], cc, sc)], axis=-1)
    return out.astype(x.dtype)


def rope_pair(x, tabs):
    return jnp.concatenate([apply_rope(x[..., :DIFF_HD], tabs), apply_rope(x[..., DIFF_HD:], tabs)], axis=-1)


def dense_attention(q, k, v, sink):
    B, H, Lq, d = q.shape
    nb = Lq // Q_BLOCK
    scale = d ** -0.5
    qb = jnp.moveaxis(q.reshape(B, H, nb, Q_BLOCK, d), 2, 0)

    def block(qi):
        s = jnp.einsum('bhqd,bhkd->bhqk', qi, k).astype(F32) * scale
        if sink is None:
            p = jax.nn.softmax(s, axis=-1)
        else:
            s_sink = jnp.broadcast_to(sink.astype(F32)[None, :, None, None], s.shape[:-1] + (1,))
            p = jax.nn.softmax(jnp.concatenate([s, s_sink], axis=-1), axis=-1)[..., :-1]
        return jnp.einsum('bhqk,bhkd->bhqd', p.astype(v.dtype), v)

    o = lax.map(block, qb)
    return jnp.moveaxis(o, 0, 2).reshape(B, H, Lq, v.shape[-1])


def diff_attention(q, k, v, lam):
    B, H, Lq, _ = q.shape
    Lk = k.shape[2]
    nb = Lq // Q_BLOCK
    scale = DIFF_HD ** -0.5
    qb = jnp.moveaxis(q.reshape(B, H, nb, Q_BLOCK, 2, DIFF_HD), 2, 0)
    kk = k.reshape(B, H, Lk, 2, DIFF_HD)

    def block(qi):
        s = jnp.einsum('bhqmd,bhkmd->bhmqk', qi, kk).astype(F32) * scale
        p = jax.nn.softmax(s, axis=-1)
        a = p[:, :, 0] - lam * p[:, :, 1]
        return jnp.einsum('bhqk,bhkd->bhqd', a.astype(v.dtype), v)

    o = lax.map(block, qb)
    return jnp.moveaxis(o, 0, 2).reshape(B, H, Lq, v.shape[-1])


def neighbourhood_attention(q, k, v, kc, vc, rpb):
    B, H, L, d = q.shape
    R = L // GRID_W
    kh = min(NA_WIN_R, R)
    scale = d ** -0.5
    qg = q.reshape(B, H, R, GRID_W, d)
    r = jnp.arange(R)
    rows = jnp.clip(r - kh // 2, 0, R - kh)[:, None] + jnp.arange(kh)[None, :]
    k_win = k.reshape(B, H, R, GRID_W, d)[:, :, rows]
    v_win = v.reshape(B, H, R, GRID_W, d)[:, :, rows]
    col = jnp.arange(GRID_W)
    c0 = jnp.clip(col - NA_WIN_C // 2, 0, GRID_W - NA_WIN_C)
    col_ok = (col[None, :] >= c0[:, None]) & (col[None, :] < c0[:, None] + NA_WIN_C)
    d_row = rows - r[:, None] + (NA_WIN_R - 1)
    d_col = jnp.clip(col[None, :] - col[:, None] + (NA_WIN_C - 1), 0, 2 * NA_WIN_C - 2)
    bias = rpb.astype(F32)[:, d_row[:, None, :, None], d_col[None, :, None, :]]
    s = jnp.einsum('bhrqd,bhrjkd->bhrqjk', qg, k_win).astype(F32) * scale + bias[None]
    s = jnp.where(col_ok[:, None, :], s, NEG).reshape(B, H, R, GRID_W, kh * GRID_W)
    s_ctx = jnp.einsum('bhrqd,bhcd->bhrqc', qg, kc).astype(F32) * scale
    p = jax.nn.softmax(jnp.concatenate([s, s_ctx], axis=-1), axis=-1).astype(v.dtype)
    nw = kh * GRID_W
    p_win = p[..., :nw].reshape(B, H, R, GRID_W, kh, GRID_W)
    o = jnp.einsum('bhrqjk,bhrjkd->bhrqd', p_win, v_win) + jnp.einsum('bhrqc,bhcd->bhrqd', p[..., nw:], vc)
    return o.reshape(B, H, L, d)


def window_attention(q, k, v, kc, vc, sink):
    B, H, L, d = q.shape
    nb = L // SWA_BLOCK
    scale = d ** -0.5
    pad = ((0, 0), (0, 0), (SWA_BLOCK, SWA_BLOCK), (0, 0))

    def band(t):
        tb = jnp.pad(t, pad).reshape(B, H, nb + 2, SWA_BLOCK, d)
        return jnp.concatenate([tb[:, :, :-2], tb[:, :, 1:-1], tb[:, :, 2:]], axis=3)

    k_b, v_b = band(k), band(v)
    qb = q.reshape(B, H, nb, SWA_BLOCK, d)
    q_pos = jnp.arange(nb)[:, None] * SWA_BLOCK + jnp.arange(SWA_BLOCK)[None, :]
    k_pos = jnp.arange(nb)[:, None] * SWA_BLOCK - SWA_BLOCK + jnp.arange(3 * SWA_BLOCK)[None, :]
    kp = k_pos[:, None, :]
    ok = (kp >= 0) & (kp < L) & (jnp.abs(kp - q_pos[:, :, None]) <= SWA_WINDOW)
    s = jnp.einsum('bhnqd,bhnkd->bhnqk', qb, k_b).astype(F32) * scale
    s = jnp.where(ok, s, NEG)
    s_ctx = jnp.einsum('bhnqd,bhcd->bhnqc', qb, kc).astype(F32) * scale
    s_sink = jnp.broadcast_to(sink.astype(F32)[None, :, None, None, None], (B, H, nb, SWA_BLOCK, 1))
    p = jax.nn.softmax(jnp.concatenate([s, s_ctx, s_sink], axis=-1), axis=-1).astype(v.dtype)
    nk = 3 * SWA_BLOCK
    Lc = kc.shape[2]
    o = jnp.einsum('bhnqk,bhnkd->bhnqd', p[..., :nk], v_b) + jnp.einsum('bhnqc,bhcd->bhnqd', p[..., nk:nk + Lc], vc)
    return o.reshape(B, H, L, d)


def hyena_filters(L, lp):
    tn = jnp.arange(L, dtype=F32) / L
    ang = 2.0 * math.pi * tn[:, None] * jnp.arange(1, HY_FREQS + 1, dtype=F32)[None, :]
    z = jnp.concatenate([tn[:, None], jnp.cos(ang), jnp.sin(ang)], axis=-1)
    w = lp['hy_sin_freq'].astype(F32)
    g = jnp.sin(w * (z @ lp['hy_w1'].astype(F32) + lp['hy_b1'].astype(F32)))
    g = jnp.sin(w * (g @ lp['hy_w2'].astype(F32) + lp['hy_b2'].astype(F32)))
    hfil = (g @ lp['hy_w3'].astype(F32)).reshape(L, HY_ORDER, 2, HY_WIDTH)
    rate = jnp.exp(lp['hy_log_decay'].astype(F32)).reshape(HY_ORDER, 2, HY_WIDTH)
    hfil = hfil * jnp.exp(-rate[None] * tn[:, None, None, None])
    full = jnp.concatenate([hfil[:, :, 0], jnp.zeros((1, HY_ORDER, HY_WIDTH), F32), hfil[1:, :, 1][::-1]], axis=0)
    return full / (jnp.sum(jnp.abs(full), axis=0, keepdims=True) + EPS)


def hyena(u, lp):
    B, L, _ = u.shape
    w = lp['hy_short_w']
    up = jnp.pad(u, ((0, 0), (1, 1), (0, 0)))
    z = up[:, :L] * w[0] + up[:, 1:L + 1] * w[1] + up[:, 2:] * w[2] + lp['hy_short_b']
    v, x1, x2 = jnp.split(z, 3, axis=-1)
    filt_f = jnp.fft.rfft(hyena_filters(L, lp), axis=0)
    y = v.astype(F32)
    for o, gate in enumerate((x1, x2)):
        yf = jnp.fft.rfft(y, n=2 * L, axis=1)
        conv = jnp.fft.irfft(yf * filt_f[None, :, o], n=2 * L, axis=1)[:, :L]
        y = gate.astype(F32) * (conv + lp['hy_skip'][o].astype(F32) * y)
    return y.astype(u.dtype)


def token_mixing(h, lp, l, ctx_cache):
    B, L, _ = h.shape
    proj = h @ lp['w_in']
    na_in, swa_in, hy_in, diff_in, gate_in = jnp.split(proj, list(SPLIT_IDX), axis=-1)
    na_q, na_k, na_v = [to_heads(t, NA_HEADS) for t in jnp.split(na_in, 3, axis=-1)]
    swa_q = to_heads(swa_in[..., :SWA_QW], SWA_HEADS)
    swa_k = to_heads(swa_in[..., SWA_QW:SWA_QW + SWA_KVW], SWA_KV_HEADS)
    swa_v = to_heads(swa_in[..., SWA_QW + SWA_KVW:], SWA_KV_HEADS)
    d_q, d_k, d_v = [to_heads(t, DIFF_HEADS) for t in jnp.split(diff_in, 3, axis=-1)]
    lam_init = 0.8 - 0.6 * math.exp(-0.3 * l)
    lvec = lp['diff_lambda'].astype(F32)
    lam = jnp.exp(jnp.sum(lvec[0] * lvec[1])) - jnp.exp(jnp.sum(lvec[2] * lvec[3])) + lam_init
    sink = lp['swa_sink']
    if ctx_cache is None:
        a_o = dense_attention(na_q, na_k, na_v, None)
        b_o = dense_attention(swa_q, expand_kv(swa_k), expand_kv(swa_v), sink)
        d_o = diff_attention(d_q, d_k, d_v, lam)
        ctx_out = (na_k, na_v, swa_k, swa_v, d_k, d_v)
    else:
        ck_na, cv_na, ck_swa, cv_swa, ck_d, cv_d = ctx_cache
        a_o = neighbourhood_attention(na_q, na_k, na_v, ck_na, cv_na, lp['na_rpb'])
        tabs = rope_tables(L, HEAD_DIM)
        b_o = window_attention(apply_rope(swa_q, tabs), expand_kv(apply_rope(swa_k, tabs)), expand_kv(swa_v),
                               expand_kv(ck_swa), expand_kv(cv_swa), sink)
        tabs_d = rope_tables(L, DIFF_HD)
        d_o = diff_attention(rope_pair(d_q, tabs_d), jnp.concatenate([rope_pair(d_k, tabs_d), ck_d], axis=2),
                             jnp.concatenate([d_v, cv_d], axis=2), lam)
        ctx_out = None
    d_o = rmsnorm(d_o, lp['diff_subln']) * (1.0 - lam_init)
    c_o = hyena(hy_in, lp)
    branches = (from_heads(a_o), from_heads(b_o), c_o, from_heads(d_o))
    gates = jax.nn.sigmoid(gate_in.reshape(B, L, N_BRANCH, D_MODEL))
    merged = gates[:, :, 0] * (branches[0] @ lp['w_branch'][0])
    for i in range(1, N_BRANCH):
        merged = merged + gates[:, :, i] * (branches[i] @ lp['w_branch'][i])
    return merged @ lp['w_out'], ctx_out


def trunk_layer(x, cond, lp, l, ctx_cache):
    mod = jax.nn.silu(cond) @ lp['w_ada'] + lp['b_ada']
    mod = mod.reshape(mod.shape[0], 1, N_MOD, D_MODEL)
    sh1, sc1, g1, sh2, sc2, g2, sh3, sc3, g3 = [mod[:, :, i] for i in range(N_MOD)]
    h = rmsnorm(x, lp['norm_ffn1']) * (1 + sc1) + sh1
    x = x + 0.5 * g1 * swiglu(h, lp['ffn1_w1'], lp['ffn1_w3'], lp['ffn1_w2'])
    h = rmsnorm(x, lp['norm_mix']) * (1 + sc2) + sh2
    mix, ctx_out = token_mixing(h, lp, l, ctx_cache)
    x = x + g2 * mix
    h = rmsnorm(x, lp['norm_ffn2']) * (1 + sc3) + sh3
    x = x + 0.5 * g3 * swiglu(h, lp['ffn2_w1'], lp['ffn2_w3'], lp['ffn2_w2'])
    return x, ctx_out


def setup_inputs(seed: int = 0) -> dict:
    key = jax.random.key(seed)
    keys = iter(jax.random.split(key, 64))

    def nrm(shape, scale):
        return jax.random.normal(next(keys), shape, jnp.float32) * scale

    D = D_MODEL
    C = HY_WIDTH
    hy_rates = jnp.log(jnp.linspace(HY_MIN_DECAY, HY_MAX_DECAY, HY_ORDER * 2 * C, dtype=jnp.float32))
    return {
        'x_prompt': nrm((BATCH, SEQ, D), 1.0),
        'x_sample': nrm((DEC_BATCH, DEC_SEQ, D), 1.0),
        'cache_na_k': nrm((DEC_BATCH, DEPTH, NA_HEADS, PAST_LEN, HEAD_DIM), 1.0),
        'cache_na_v': nrm((DEC_BATCH, DEPTH, NA_HEADS, PAST_LEN, HEAD_DIM), 1.0),
        'cache_swa_k': nrm((DEC_BATCH, DEPTH, SWA_KV_HEADS, PAST_LEN, HEAD_DIM), 1.0),
        'cache_swa_v': nrm((DEC_BATCH, DEPTH, SWA_KV_HEADS, PAST_LEN, HEAD_DIM), 1.0),
        'cache_diff_k': nrm((DEC_BATCH, DEPTH, DIFF_HEADS, PAST_LEN, 2 * DIFF_HD), 1.0),
        'cache_diff_v': nrm((DEC_BATCH, DEPTH, DIFF_HEADS, PAST_LEN, 2 * DIFF_HD), 1.0),
        'c': nrm((DEC_BATCH, D), 1.0),
        'c_ctx': nrm((D,), 1.0),
        'w_ada': nrm((DEPTH, D, N_MOD * D), D ** -0.5),
        'b_ada': nrm((DEPTH, N_MOD * D), 0.02),
        'norm_ffn1': 1.0 + nrm((DEPTH, D), 0.02),
        'norm_mix': 1.0 + nrm((DEPTH, D), 0.02),
        'norm_ffn2': 1.0 + nrm((DEPTH, D), 0.02),
        'final_norm': 1.0 + nrm((D,), 0.02),
        'ffn1_w1': nrm((DEPTH, D, D_FF), D ** -0.5),
        'ffn1_w3': nrm((DEPTH, D, D_FF), D ** -0.5),
        'ffn1_w2': nrm((DEPTH, D_FF, D), D_FF ** -0.5),
        'ffn2_w1': nrm((DEPTH, D, D_FF), D ** -0.5),
        'ffn2_w3': nrm((DEPTH, D, D_FF), D ** -0.5),
        'ffn2_w2': nrm((DEPTH, D_FF, D), D_FF ** -0.5),
        'w_in': nrm((DEPTH, D, IN_W), D ** -0.5),
        'w_branch': nrm((DEPTH, N_BRANCH, BRANCH_W, D), BRANCH_W ** -0.5),
        'w_out': nrm((DEPTH, D, D), D ** -0.5),
        'na_rpb': nrm((DEPTH, NA_HEADS, 2 * NA_WIN_R - 1, 2 * NA_WIN_C - 1), 0.1),
        'swa_sink': nrm((DEPTH, SWA_HEADS), 0.5),
        'hy_short_w': nrm((DEPTH, 3, HY_IN_W), 3 ** -0.5),
        'hy_short_b': nrm((DEPTH, HY_IN_W), 0.02),
        'hy_w1': nrm((DEPTH, HY_EMB, HY_HIDDEN), HY_EMB ** -0.5),
        'hy_b1': nrm((DEPTH, HY_HIDDEN), 0.1),
        'hy_w2': nrm((DEPTH, HY_HIDDEN, HY_HIDDEN), HY_HIDDEN ** -0.5),
        'hy_b2': nrm((DEPTH, HY_HIDDEN), 0.1),
        'hy_w3': nrm((DEPTH, HY_HIDDEN, HY_ORDER * 2 * C), HY_HIDDEN ** -0.5),
        'hy_sin_freq': 1.0 + nrm((DEPTH, HY_HIDDEN), 0.05),
        'hy_log_decay': hy_rates[None, :] + nrm((DEPTH, HY_ORDER * 2 * C), 0.05),
        'hy_skip': nrm((DEPTH, HY_ORDER, C), 0.5),
        'diff_lambda': nrm((DEPTH, 4, DIFF_HD), 0.1),
        'diff_subln': 1.0 + nrm((DEPTH, 2 * DIFF_HD), 0.02),
    }


def reference(x_prompt, x_sample, cache_na_k, cache_na_v, cache_swa_k, cache_swa_v, cache_diff_k, cache_diff_v,
              c, c_ctx, w_ada, b_ada, norm_ffn1, norm_mix, norm_ffn2, final_norm,
              ffn1_w1, ffn1_w3, ffn1_w2, ffn2_w1, ffn2_w3, ffn2_w2, w_in, w_branch, w_out,
              na_rpb, swa_sink, hy_short_w, hy_short_b, hy_w1, hy_b1, hy_w2, hy_b2, hy_w3,
              hy_sin_freq, hy_log_decay, hy_skip, diff_lambda, diff_subln):
    layer_params = {
        'w_ada': w_ada, 'b_ada': b_ada, 'norm_ffn1': norm_ffn1, 'norm_mix': norm_mix, 'norm_ffn2': norm_ffn2,
        'ffn1_w1': ffn1_w1, 'ffn1_w3': ffn1_w3, 'ffn1_w2': ffn1_w2,
        'ffn2_w1': ffn2_w1, 'ffn2_w3': ffn2_w3, 'ffn2_w2': ffn2_w2,
        'w_in': w_in, 'w_branch': w_branch, 'w_out': w_out, 'na_rpb': na_rpb, 'swa_sink': swa_sink,
        'hy_short_w': hy_short_w, 'hy_short_b': hy_short_b, 'hy_w1': hy_w1, 'hy_b1': hy_b1,
        'hy_w2': hy_w2, 'hy_b2': hy_b2, 'hy_w3': hy_w3, 'hy_sin_freq': hy_sin_freq,
        'hy_log_decay': hy_log_decay, 'hy_skip': hy_skip, 'diff_lambda': diff_lambda, 'diff_subln': diff_subln,
    }
    x = x_prompt
    ctx_cond = c_ctx[None, :]
    collected = ([], [], [], [], [], [])
    for l in range(DEPTH):
        lp = {name: arr[l] for name, arr in layer_params.items()}
        x, ctx_out = trunk_layer(x, ctx_cond, lp, l, None)
        for store, t in zip(collected, ctx_out):
            store.append(t)
    y_prompt = rmsnorm(x, final_norm)
    new_na_k, new_na_v, new_swa_k, new_swa_v, new_diff_k, new_diff_v = [jnp.stack(s, axis=1) for s in collected]

    caches = (cache_na_k, cache_na_v, cache_swa_k, cache_swa_v, cache_diff_k, cache_diff_v)
    x = x_sample
    for l in range(DEPTH):
        lp = {name: arr[l] for name, arr in layer_params.items()}
        x, _ = trunk_layer(x, c, lp, l, tuple(t[:, l] for t in caches))
    y_sample = rmsnorm(x, final_norm)
    return (y_prompt, y_sample, new_na_k, new_na_v, new_swa_k, new_swa_v, new_diff_k, new_diff_v)
```

```python
import functools
import math

import jax
import jax.numpy as jnp
from jax import lax
from jax.experimental import pallas as pl
from jax.experimental.pallas import tpu as pltpu

F32 = jnp.float32
BF16 = jnp.bfloat16

D_MODEL = 1024
DEPTH = 4
GRID_W = 64
N_BRANCH = 4
BRANCH_W = D_MODEL // 4
HEAD_DIM = 64
N_HEADS = BRANCH_W // HEAD_DIM
NA_WIN_R = 8
NA_WIN_C = 16
SWA_KV_HEADS = N_HEADS // 2
SWA_WINDOW = 128
HY_WIDTH = BRANCH_W
HY_ORDER = 2
HY_FREQS = 16
HY_EMB = 1 + 2 * HY_FREQS
HY_HIDDEN = 64
DIFF_HD = 32
D_FF = 128 * ((8 * D_MODEL // 3 + 127) // 128)
ROPE_BASE = 10000.0
EPS = 1e-6
NEG = -1e30
N_MOD = 9
NA_W = 3 * BRANCH_W
SWA_QW = BRANCH_W
SWA_KVW = SWA_KV_HEADS * HEAD_DIM
SWA_W = SWA_QW + 2 * SWA_KVW
HY_IN_W = 3 * HY_WIDTH
DIFF_W = 3 * BRANCH_W
GATE_W = N_BRANCH * D_MODEL
IN_W = NA_W + SWA_W + HY_IN_W + DIFF_W + GATE_W

LANES = 128
MXU_DIM = 256
MIB = 1024 * 1024

FFN_ROWS = 512
PROJ_ROWS = 256
FFN_CHUNK = 768
ATTN_Q = 256
NA_Q_ROWS = ATTN_Q // GRID_W
NA_SLAB_ROWS = NA_Q_ROWS + NA_WIN_R
SWA_SLAB = ATTN_Q + 2 * SWA_WINDOW


def _cparams(vmem_mib):
    return pltpu.CompilerParams(vmem_limit_bytes=vmem_mib * MIB)


def _resident(shape):
    nd = len(shape)
    return pl.BlockSpec(shape, lambda *_: (0,) * nd, pipeline_mode=pl.Buffered(1))


def _dot(a, b):
    return jnp.dot(a, b, preferred_element_type=F32)


def _dot_nt(a, b):
    return lax.dot_general(a, b, (((1,), (1,)), ((), ())), preferred_element_type=F32)


def _dot_exact(a, b):
    return jnp.dot(a, b, preferred_element_type=F32, precision=lax.Precision.HIGHEST)


def _sigmoid(x):
    return 1.0 / (1.0 + jnp.exp(-x))


def _norm_mod(x, nw, shift, scale):
    y = x * lax.rsqrt(jnp.mean(x * x, axis=-1, keepdims=True) + EPS) * nw
    return y * (1.0 + scale) + shift


def _mod_spec(nb, tiles_per_b):
    if nb == 1:
        return pl.BlockSpec((1, N_MOD, D_MODEL), lambda i: (0, 0, 0))
    return pl.BlockSpec((1, N_MOD, D_MODEL), lambda i: (i // tiles_per_b, 0, 0))


def _ada_kernel(c_ref, w_ref, b_ref, o_ref):
    c = c_ref[...]
    s = (c * _sigmoid(c)).astype(BF16)
    o_ref[0] = _dot(s, w_ref[0].astype(BF16)) + b_ref[0]


def ada_modulation(cond, w_ada, b_ada):
    rows = cond.shape[0]
    width = N_MOD * D_MODEL
    tn = 9 * LANES
    return pl.pallas_call(
        _ada_kernel,
        grid=(DEPTH, width // tn),
        in_specs=[pl.BlockSpec((rows, D_MODEL), lambda l, j: (0, 0)),
                  pl.BlockSpec((1, D_MODEL, tn), lambda l, j: (l, 0, j)),
                  pl.BlockSpec((1, 1, tn), lambda l, j: (l, 0, j))],
        out_specs=pl.BlockSpec((1, rows, tn), lambda l, j: (l, 0, j)),
        out_shape=jax.ShapeDtypeStruct((DEPTH, rows, width), F32),
        compiler_params=_cparams(32),
        name="ada_modulation",
    )(cond, w_ada, b_ada.reshape(DEPTH, 1, width))


def _ffn_kernel(x_ref, mod_ref, nw_ref, w1_ref, w3_ref, w2_ref, *rest, mod_base, final):
    if final:
        fw_ref, o_ref = rest
    else:
        (o_ref,) = rest
    x = x_ref[...]
    shift = mod_ref[0, mod_base:mod_base + 1, :]
    scale = mod_ref[0, mod_base + 1:mod_base + 2, :]
    gate = mod_ref[0, mod_base + 2:mod_base + 3, :]
    h = _norm_mod(x, nw_ref[...], shift, scale).astype(BF16)
    acc = jnp.zeros(x.shape, F32)
    for lo in range(0, D_FF, FFN_CHUNK):
        hi = min(lo + FFN_CHUNK, D_FF)
        a = _dot(h, w1_ref[:, lo:hi])
        b = _dot(h, w3_ref[:, lo:hi])
        u = (a * _sigmoid(a) * b).astype(BF16)
        acc = acc + _dot(u, w2_ref[lo:hi, :])
    y = x + 0.5 * gate * acc
    if final:
        y = y * lax.rsqrt(jnp.mean(y * y, axis=-1, keepdims=True) + EPS) * fw_ref[...]
    o_ref[...] = y


def ffn_block(x, mod, nw, w1, w3, w2, mod_base, rows_per_b, final_w=None):
    T = x.shape[0]
    tm = FFN_ROWS
    nb = mod.shape[0]
    final = final_w is not None
    in_specs = [pl.BlockSpec((tm, D_MODEL), lambda i: (i, 0)),
                _mod_spec(nb, rows_per_b // tm),
                _resident((1, D_MODEL)),
                _resident((D_MODEL, D_FF)), _resident((D_MODEL, D_FF)), _resident((D_FF, D_MODEL))]
    args = [x, mod, nw.reshape(1, D_MODEL), w1, w3, w2]
    if final:
        in_specs.append(_resident((1, D_MODEL)))
        args.append(final_w.reshape(1, D_MODEL))
    return pl.pallas_call(
        functools.partial(_ffn_kernel, mod_base=mod_base, final=final),
        grid=(T // tm,),
        in_specs=in_specs,
        out_specs=pl.BlockSpec((tm, D_MODEL), lambda i: (i, 0)),
        out_shape=jax.ShapeDtypeStruct((T, D_MODEL), F32),
        compiler_params=_cparams(56),
        name="ffn_block",
    )(*args)


def _rope_chunk(x, cos, sin_a, sin_b, dist):
    return x * cos + pltpu.roll(x, LANES - dist, 1) * sin_a + pltpu.roll(x, dist, 1) * sin_b


def _proj_kernel(x_ref, mod_ref, nw_ref, w_ref, *rest, rope):
    if rope:
        (cs_ref, sa_ref, sb_ref, cd_ref, da_ref, db_ref,
         na_ref, swa_ref, hy_ref, dif_ref, gate_ref) = rest
    else:
        na_ref, swa_ref, hy_ref, dif_ref, gate_ref = rest
    h = _norm_mod(x_ref[...], nw_ref[...], mod_ref[0, 3:4, :], mod_ref[0, 4:5, :]).astype(BF16)
    o = 0
    na_ref[...] = _dot(h, w_ref[:, o:o + NA_W])
    o += NA_W
    s = _dot(h, w_ref[:, o:o + SWA_W])
    o += SWA_W
    n_rot = (SWA_QW + SWA_KVW) // LANES
    for c in range(SWA_W // LANES):
        sl = slice(c * LANES, (c + 1) * LANES)
        chunk = s[:, sl]
        if rope and c < n_rot:
            chunk = _rope_chunk(chunk, cs_ref[:, sl], sa_ref[:, sl], sb_ref[:, sl], HEAD_DIM // 4)
        swa_ref[:, sl] = chunk
    hy_ref[...] = _dot(h, w_ref[:, o:o + HY_IN_W])
    o += HY_IN_W
    s = _dot(h, w_ref[:, o:o + DIFF_W])
    o += DIFF_W
    n_rot = 2 * BRANCH_W // LANES
    for c in range(DIFF_W // LANES):
        sl = slice(c * LANES, (c + 1) * LANES)
        chunk = s[:, sl]
        if rope and c < n_rot:
            chunk = _rope_chunk(chunk, cd_ref[:, sl], da_ref[:, sl], db_ref[:, sl], DIFF_HD // 4)
        dif_ref[:, sl] = chunk
    for c in range(N_BRANCH):
        sl = slice(c * D_MODEL, (c + 1) * D_MODEL)
        gate_ref[:, sl] = _sigmoid(_dot(h, w_ref[:, o + c * D_MODEL:o + (c + 1) * D_MODEL]))


def in_projection(x, mod, nw, w_in, rows_per_b, rope_tabs):
    T = x.shape[0]
    tm = PROJ_ROWS
    nb = mod.shape[0]
    rope = rope_tabs is not None
    in_specs = [pl.BlockSpec((tm, D_MODEL), lambda i: (i, 0)),
                _mod_spec(nb, rows_per_b // tm),
                _resident((1, D_MODEL)),
                _resident((D_MODEL, IN_W))]
    args = [x, mod, nw.reshape(1, D_MODEL), w_in]
    if rope:
        pos_tiles = rows_per_b // tm
        for t in rope_tabs:
            in_specs.append(pl.BlockSpec((tm, t.shape[1]), lambda i: (i % pos_tiles, 0)))
            args.append(t)
    widths = (NA_W, SWA_W, HY_IN_W, DIFF_W, GATE_W)
    return pl.pallas_call(
        functools.partial(_proj_kernel, rope=rope),
        grid=(T // tm,),
        in_specs=in_specs,
        out_specs=[pl.BlockSpec((tm, w), lambda i: (i, 0)) for w in widths],
        out_shape=[jax.ShapeDtypeStruct((T, w), F32) for w in widths],
        compiler_params=_cparams(56),
        name="in_projection",
    )(*args)


def rope_tables(L, dh, width):
    nf = dh // 4
    t = jnp.arange(L)
    freqs = ROPE_BASE ** (-jnp.arange(nf, dtype=F32) / nf)
    ang_r = (t // GRID_W).astype(F32)[:, None] * freqs
    ang_c = (t % GRID_W).astype(F32)[:, None] * freqs
    cr, sr, cc, sc = jnp.cos(ang_r), jnp.sin(ang_r), jnp.cos(ang_c), jnp.sin(ang_c)
    z = jnp.zeros_like(sr)
    reps = width // dh
    cos = jnp.tile(jnp.concatenate([cr, cr, cc, cc], axis=-1), (1, reps))
    sin_a = jnp.tile(jnp.concatenate([-sr, z, -sc, z], axis=-1), (1, reps))
    sin_b = jnp.tile(jnp.concatenate([z, sr, z, sc], axis=-1), (1, reps))
    return cos, sin_a, sin_b


def _lane_mask(width, lo, n):
    lane = lax.broadcasted_iota(jnp.int32, (1, width), 1)
    return (lane >= lo) & (lane < lo + n)


def _softmax_parts(parts, extra_logit=None):
    m = parts[0].max(axis=-1, keepdims=True)
    for s in parts[1:]:
        m = jnp.maximum(m, s.max(axis=-1, keepdims=True))
    if extra_logit is not None:
        m = jnp.maximum(m, extra_logit)
    ps = [jnp.exp(s - m) for s in parts]
    l = ps[0].sum(axis=-1, keepdims=True)
    for p in ps[1:]:
        l = l + p.sum(axis=-1, keepdims=True)
    if extra_logit is not None:
        l = l + jnp.exp(extra_logit - m)
    return ps, l


def _na_kernel(q_ref, k_ref, v_ref, kc_ref, vc_ref, bias_ref, o_ref, *, n_rows):
    i = pl.program_id(1)
    row0 = jnp.clip(NA_Q_ROWS * i - NA_WIN_R // 2, 0, n_rows - NA_SLAB_ROWS)
    start = pl.multiple_of(row0 * GRID_W, GRID_W)
    slab = NA_SLAB_ROWS * GRID_W
    ks = k_ref[0, pl.ds(start, slab), :].astype(BF16)
    vs = v_ref[0, pl.ds(start, slab), :].astype(BF16)
    kc = kc_ref[0].astype(BF16)
    vc = vc_ref[0].astype(BF16)
    q = q_ref[0] * (HEAD_DIM ** -0.5)
    out = jnp.zeros(q.shape, F32)
    for h in range(N_HEADS):
        hm = _lane_mask(BRANCH_W, h * HEAD_DIM, HEAD_DIM)
        qh = jnp.where(hm, q, 0.0).astype(BF16)
        s_win = _dot_nt(qh, ks) + bias_ref[0, h]
        s_ctx = _dot_nt(qh, kc)
        (p_win, p_ctx), l = _softmax_parts([s_win, s_ctx])
        o = _dot(p_win.astype(BF16), vs) + _dot(p_ctx.astype(BF16), vc)
        out = out + jnp.where(hm, o / l, 0.0)
    o_ref[0] = out


def na_bias_tables(rpb, n_rows):
    qc = jnp.arange(GRID_W)
    kc = jnp.arange(GRID_W)
    d_col = jnp.clip(kc[None, :] - qc[:, None] + (NA_WIN_C - 1), 0, 2 * NA_WIN_C - 2)
    c0 = jnp.clip(qc - NA_WIN_C // 2, 0, GRID_W - NA_WIN_C)
    col_ok = (kc[None, :] >= c0[:, None]) & (kc[None, :] < c0[:, None] + NA_WIN_C)
    n_tiles = n_rows // NA_Q_ROWS
    tabs = []
    for tile in (0, 1, n_tiles - 1):
        slab0 = min(max(NA_Q_ROWS * tile - NA_WIN_R // 2, 0), n_rows - NA_SLAB_ROWS)
        qr = NA_Q_ROWS * tile + jnp.arange(NA_Q_ROWS)
        kr = slab0 + jnp.arange(NA_SLAB_ROWS)
        r0 = jnp.clip(qr - NA_WIN_R // 2, 0, n_rows - NA_WIN_R)
        row_ok = (kr[None, :] >= r0[:, None]) & (kr[None, :] < r0[:, None] + NA_WIN_R)
        d_row = jnp.clip(kr[None, :] - qr[:, None] + (NA_WIN_R - 1), 0, 2 * NA_WIN_R - 2)
        b = rpb.astype(F32)[:, d_row[:, None, :, None], d_col[None, :, None, :]]
        ok = row_ok[:, None, :, None] & col_ok[None, :, None, :]
        b = jnp.where(ok[None], b, NEG)
        tabs.append(b.reshape(N_HEADS, ATTN_Q, NA_SLAB_ROWS * GRID_W))
    return jnp.stack(tabs)


def neighbourhood_attention(na, kc, vc, bias, B, L):
    n_rows = L // GRID_W
    n_tiles = L // ATTN_Q
    na3 = na.reshape(B, L, NA_W)
    P = kc.shape[1]
    slab = NA_SLAB_ROWS * GRID_W

    def kind(b, i):
        return (jnp.where(i == 0, 0, jnp.where(i == n_tiles - 1, 2, 1)), 0, 0, 0)

    return pl.pallas_call(
        functools.partial(_na_kernel, n_rows=n_rows),
        grid=(B, n_tiles),
        in_specs=[pl.BlockSpec((1, ATTN_Q, BRANCH_W), lambda b, i: (b, i, 0)),
                  pl.BlockSpec((1, L, BRANCH_W), lambda b, i: (b, 0, 1)),
                  pl.BlockSpec((1, L, BRANCH_W), lambda b, i: (b, 0, 2)),
                  pl.BlockSpec((1, P, BRANCH_W), lambda b, i: (b, 0, 0)),
                  pl.BlockSpec((1, P, BRANCH_W), lambda b, i: (b, 0, 0)),
                  pl.BlockSpec((1, N_HEADS, ATTN_Q, slab), kind)],
        out_specs=pl.BlockSpec((1, ATTN_Q, BRANCH_W), lambda b, i: (b, i, 0)),
        out_shape=jax.ShapeDtypeStruct((B, L, BRANCH_W), F32),
        compiler_params=_cparams(48),
        name="neighbourhood_attention",
    )(na3, na3, na3, kc, vc, bias).reshape(B * L, BRANCH_W)


def _attn_kernel(*refs, L, window, gqa, has_ctx, has_sink):
    refs = list(refs)
    q_ref, k_ref, v_ref = refs[:3]
    pos = 3
    if has_ctx:
        kc_ref, vc_ref = refs[pos:pos + 2]
        pos += 2
    if has_sink:
        sink_ref = refs[pos]
        pos += 1
    o_ref = refs[pos]
    i = pl.program_id(1)
    if window:
        slab = SWA_SLAB
        start = pl.multiple_of(jnp.clip(i * ATTN_Q - SWA_WINDOW, 0, L - slab), SWA_WINDOW)
        q_pos = i * ATTN_Q + lax.broadcasted_iota(jnp.int32, (ATTN_Q, 1), 0)
        k_pos = start + lax.broadcasted_iota(jnp.int32, (1, slab), 1)
        ok = jnp.abs(k_pos - q_pos) <= SWA_WINDOW
        ks = k_ref[0, pl.ds(start, slab), :].astype(BF16)
        vs = v_ref[0, pl.ds(start, slab), :].astype(BF16)
    else:
        ks = k_ref[0].astype(BF16)
        vs = v_ref[0].astype(BF16)
    if has_ctx:
        kc = kc_ref[0].astype(BF16)
        vc = vc_ref[0].astype(BF16)
    q = q_ref[0] * (HEAD_DIM ** -0.5)
    kv_w = ks.shape[1]
    halves = [jnp.zeros((ATTN_Q, LANES), F32), jnp.zeros((ATTN_Q, LANES), F32)]
    out = jnp.zeros(q.shape, F32)
    for h in range(N_HEADS):
        if gqa:
            kvh, slot = h // 2, h % 2
            qh = q[:, kvh * LANES:(kvh + 1) * LANES]
            if slot != kvh:
                qh = pltpu.roll(qh, HEAD_DIM, 1)
            hm = _lane_mask(kv_w, kvh * HEAD_DIM, HEAD_DIM)
        else:
            qh = q
            hm = _lane_mask(kv_w, h * HEAD_DIM, HEAD_DIM)
        qh = jnp.where(hm, qh, 0.0).astype(BF16)
        s = _dot_nt(qh, ks)
        if window:
            s = jnp.where(ok, s, NEG)
        parts = [s]
        if has_ctx:
            parts.append(_dot_nt(qh, kc))
        ps, l = _softmax_parts(parts, sink_ref[h] if has_sink else None)
        o = _dot(ps[0].astype(BF16), vs)
        if has_ctx:
            o = o + _dot(ps[1].astype(BF16), vc)
        o = jnp.where(hm, o / l, 0.0)
        if gqa:
            if slot != kvh:
                o = pltpu.roll(o, HEAD_DIM, 1)
            halves[kvh] = halves[kvh] + o
        else:
            out = out + o
    if gqa:
        o_ref[0, :, 0:LANES] = halves[0]
        o_ref[0, :, LANES:2 * LANES] = halves[1]
    else:
        o_ref[0] = out


def dense_attention(src, cols, B, L, *, window=False, gqa=False, ctx=None, sink=None):
    W = src.shape[1]
    src3 = src.reshape(B, L, W)
    kv_w = SWA_KVW if gqa else BRANCH_W
    qc, kcol, vcol = cols
    in_specs = [pl.BlockSpec((1, ATTN_Q, BRANCH_W), lambda b, i: (b, i, qc)),
                pl.BlockSpec((1, L, kv_w), lambda b, i: (b, 0, kcol)),
                pl.BlockSpec((1, L, kv_w), lambda b, i: (b, 0, vcol))]
    args = [src3, src3, src3]
    if ctx is not None:
        P = ctx[0].shape[1]
        in_specs += [pl.BlockSpec((1, P, kv_w), lambda b, i: (b, 0, 0))] * 2
        args += list(ctx)
    if sink is not None:
        in_specs.append(pl.BlockSpec(memory_space=pltpu.SMEM))
        args.append(sink)
    return pl.pallas_call(
        functools.partial(_attn_kernel, L=L, window=window, gqa=gqa,
                          has_ctx=ctx is not None, has_sink=sink is not None),
        grid=(B, L // ATTN_Q),
        in_specs=in_specs,
        out_specs=pl.BlockSpec((1, ATTN_Q, BRANCH_W), lambda b, i: (b, i, 0)),
        out_shape=jax.ShapeDtypeStruct((B, L, BRANCH_W), F32),
        compiler_params=_cparams(48),
        name="window_attention" if window else "dense_attention",
    )(*args).reshape(B * L, BRANCH_W)


def _diff_kernel(*refs, lam_init, has_ctx):
    refs = list(refs)
    lam_ref, gain_ref, ones_ref, q_ref, k_ref, v_ref = refs[:6]
    if has_ctx:
        kc_ref, vc_ref, o_ref = refs[6:]
    else:
        (o_ref,) = refs[6:]
    lv = lam_ref[...]
    lam = (jnp.exp(jnp.sum(lv[0:1] * lv[1:2], keepdims=True))
           - jnp.exp(jnp.sum(lv[2:3] * lv[3:4], keepdims=True)) + lam_init)
    kl = k_ref[0].astype(BF16)
    vl = v_ref[0].astype(BF16)
    if has_ctx:
        kc = kc_ref[0].astype(BF16)
        vc = vc_ref[0].astype(BF16)
    q = q_ref[0] * (DIFF_HD ** -0.5)
    out = jnp.zeros(q.shape, F32)
    for h in range(N_HEADS):
        probs = []
        for mp in range(2):
            mm = _lane_mask(BRANCH_W, h * HEAD_DIM + mp * DIFF_HD, DIFF_HD)
            qm = jnp.where(mm, q, 0.0).astype(BF16)
            parts = [_dot_nt(qm, kl)]
            if has_ctx:
                parts.append(_dot_nt(qm, kc))
            ps, l = _softmax_parts(parts)
            inv = 1.0 / l
            probs.append([p * inv for p in ps])
        a = (probs[0][0] - lam * probs[1][0]).astype(BF16)
        o = _dot(a, vl)
        if has_ctx:
            a = (probs[0][1] - lam * probs[1][1]).astype(BF16)
            o = o + _dot(a, vc)
        out = out + jnp.where(_lane_mask(BRANCH_W, h * HEAD_DIM, HEAD_DIM), o, 0.0)
    ms = _dot_exact(out * out, ones_ref[...]) * (1.0 / HEAD_DIM)
    o_ref[0] = out * lax.rsqrt(ms + EPS) * gain_ref[...] * (1.0 - lam_init)


def diff_attention(dif, lam_vec, subln, layer, B, L, ctx=None):
    lam_init = 0.8 - 0.6 * math.exp(-0.3 * layer)
    dif3 = dif.reshape(B, L, DIFF_W)
    head = jnp.arange(BRANCH_W) // HEAD_DIM
    ones = (head[:, None] == head[None, :]).astype(F32)
    gain = jnp.tile(subln.astype(F32), N_HEADS).reshape(1, BRANCH_W)
    in_specs = [_resident((4, DIFF_HD)), _resident((1, BRANCH_W)), _resident((BRANCH_W, BRANCH_W)),
                pl.BlockSpec((1, ATTN_Q, BRANCH_W), lambda b, i: (b, i, 0)),
                pl.BlockSpec((1, L, BRANCH_W), lambda b, i: (b, 0, 1)),
                pl.BlockSpec((1, L, BRANCH_W), lambda b, i: (b, 0, 2))]
    args = [lam_vec, gain, ones, dif3, dif3, dif3]
    if ctx is not None:
        P = ctx[0].shape[1]
        in_specs += [pl.BlockSpec((1, P, BRANCH_W), lambda b, i: (b, 0, 0))] * 2
        args += list(ctx)
    return pl.pallas_call(
        functools.partial(_diff_kernel, lam_init=lam_init, has_ctx=ctx is not None),
        grid=(B, L // ATTN_Q),
        in_specs=in_specs,
        out_specs=pl.BlockSpec((1, ATTN_Q, BRANCH_W), lambda b, i: (b, i, 0)),
        out_shape=jax.ShapeDtypeStruct((B, L, BRANCH_W), F32),
        compiler_params=_cparams(56),
        name="diff_attention",
    )(*args).reshape(B * L, BRANCH_W)


def _filter_kernel(z_ref, w1_ref, b1_ref, w2_ref, b2_ref, w3_ref, fr_ref, ld_ref, o_ref):
    z = z_ref[...]
    tn = z[:, 0:1]
    fr = fr_ref[0]
    g = jnp.sin(fr * (_dot_exact(z, w1_ref[0]) + b1_ref[0]))
    g = jnp.sin(fr * (_dot_exact(g, w2_ref[0]) + b2_ref[0]))
    hf = _dot_exact(g, w3_ref[0]) * jnp.exp(-jnp.exp(ld_ref[0]) * tn)
    row = lax.broadcasted_iota(jnp.int32, (z.shape[0], 1), 0)
    for o in range(HY_ORDER):
        pos = hf[:, (2 * o) * HY_WIDTH:(2 * o + 1) * HY_WIDTH]
        neg = jnp.where(row == 0, 0.0, hf[:, (2 * o + 1) * HY_WIDTH:(2 * o + 2) * HY_WIDTH])
        norm = (jnp.sum(jnp.abs(pos), axis=0, keepdims=True)
                + jnp.sum(jnp.abs(neg), axis=0, keepdims=True) + EPS)
        o_ref[0, o] = pos / norm
        o_ref[0, HY_ORDER + o] = neg / norm


def hyena_filters(L, p):
    tn = jnp.arange(L, dtype=F32) / L
    ang = 2.0 * math.pi * tn[:, None] * jnp.arange(1, HY_FREQS + 1, dtype=F32)[None, :]
    z = jnp.concatenate([tn[:, None], jnp.cos(ang), jnp.sin(ang)], axis=-1)
    z = jnp.pad(z, ((0, 0), (0, HY_HIDDEN - HY_EMB)))
    w1 = jnp.pad(p['hy_w1'], ((0, 0), (0, HY_HIDDEN - HY_EMB), (0, 0)))
    fw = HY_ORDER * 2 * HY_WIDTH
    per_layer = lambda *shape: pl.BlockSpec((1,) + shape, lambda l: (l,) + (0,) * len(shape))
    return pl.pallas_call(
        _filter_kernel,
        grid=(DEPTH,),
        in_specs=[pl.BlockSpec((L, HY_HIDDEN), lambda l: (0, 0)),
                  per_layer(HY_HIDDEN, HY_HIDDEN), per_layer(1, HY_HIDDEN),
                  per_layer(HY_HIDDEN, HY_HIDDEN), per_layer(1, HY_HIDDEN),
                  per_layer(HY_HIDDEN, fw), per_layer(1, HY_HIDDEN), per_layer(1, fw)],
        out_specs=per_layer(2 * HY_ORDER, L, HY_WIDTH),
        out_shape=jax.ShapeDtypeStruct((DEPTH, 2 * HY_ORDER, L, HY_WIDTH), F32),
        compiler_params=_cparams(56),
        name="hyena_filters",
    )(z, w1, p['hy_b1'].reshape(DEPTH, 1, HY_HIDDEN), p['hy_w2'], p['hy_b2'].reshape(DEPTH, 1, HY_HIDDEN),
      p['hy_w3'], p['hy_sin_freq'].reshape(DEPTH, 1, HY_HIDDEN), p['hy_log_decay'].reshape(DEPTH, 1, fw))


def dft_tables(L, tk):
    k = jnp.arange(L, dtype=jnp.int32)[:, None]
    n = jnp.arange(L, dtype=jnp.int32)[None, :]
    ang = (((2 * k + 1) * n) % (4 * L)).astype(F32) * (math.pi / (2 * L))
    c, s = jnp.cos(ang), jnp.sin(ang)
    fwd = jnp.concatenate([c.reshape(L // tk, tk, L), s.reshape(L // tk, tk, L)], axis=1).astype(BF16)
    inv = jnp.concatenate([c.T, -s.T], axis=1).astype(BF16)
    return fwd, inv


def _spectrum_kernel(t_ref, f_ref, o_ref):
    tk = o_ref.shape[2]
    acc = _dot(t_ref[0], f_ref[0].astype(BF16))
    o_ref[0, 0] = acc[:tk]
    o_ref[0, 1] = acc[tk:]


def filter_spectra(filt, fwd):
    G, L, C = filt.shape
    nkt, tk2, _ = fwd.shape
    tk = tk2 // 2
    return pl.pallas_call(
        _spectrum_kernel,
        grid=(nkt, G),
        in_specs=[pl.BlockSpec((1, tk2, L), lambda i, g: (i, 0, 0)),
                  pl.BlockSpec((1, L, C), lambda i, g: (g, 0, 0))],
        out_specs=pl.BlockSpec((1, 2, tk, C), lambda i, g: (g, 0, i, 0)),
        out_shape=jax.ShapeDtypeStruct((G, 2, L, C), F32),
        compiler_params=_cparams(48),
        name="filter_spectra",
    )(fwd, filt)


def _shortconv_kernel(u_ref, w_ref, b_ref, o_ref):
    u = u_ref[0]
    L = u.shape[0]
    row = lax.broadcasted_iota(jnp.int32, (L, 1), 0)
    prev = jnp.where(row == 0, 0.0, pltpu.roll(u, 1, 0))
    nxt = jnp.where(row == L - 1, 0.0, pltpu.roll(u, L - 1, 0))
    o_ref[0] = prev * w_ref[0:1, :] + u * w_ref[1:2, :] + nxt * w_ref[2:3, :] + b_ref[...]


def short_conv(hy, w, b, B, L):
    return pl.pallas_call(
        _shortconv_kernel,
        grid=(B,),
        in_specs=[pl.BlockSpec((1, L, HY_IN_W), lambda i: (i, 0, 0)),
                  _resident((3, HY_IN_W)), _resident((1, HY_IN_W))],
        out_specs=pl.BlockSpec((1, L, HY_IN_W), lambda i: (i, 0, 0)),
        out_shape=jax.ShapeDtypeStruct((B, L, HY_IN_W), F32),
        compiler_params=_cparams(56),
        name="short_conv",
    )(hy.reshape(B, L, HY_IN_W), w, b.reshape(1, HY_IN_W))


def _conv_fwd_kernel(t_ref, y_ref, hp_ref, hn_ref, o_ref):
    tk = o_ref.shape[2]
    acc = _dot(t_ref[0], y_ref[0].astype(BF16))
    yc, ys = acc[:tk], acc[tk:]
    hr = hp_ref[0, 0] + hn_ref[0, 0]
    hi = hn_ref[0, 1] - hp_ref[0, 1]
    o_ref[0, 0] = (yc * hr + ys * hi).astype(BF16)
    o_ref[0, 1] = (yc * hi - ys * hr).astype(BF16)


def conv_forward(y, col, spec, g_pos, g_neg, fwd):
    B, L, _ = y.shape
    C = HY_WIDTH
    nkt, tk2, _ = fwd.shape
    tk = tk2 // 2
    return pl.pallas_call(
        _conv_fwd_kernel,
        grid=(nkt, B),
        in_specs=[pl.BlockSpec((1, tk2, L), lambda i, b: (i, 0, 0)),
                  pl.BlockSpec((1, L, C), lambda i, b: (b, 0, col)),
                  pl.BlockSpec((1, 2, tk, C), lambda i, b: (g_pos, 0, i, 0)),
                  pl.BlockSpec((1, 2, tk, C), lambda i, b: (g_neg, 0, i, 0))],
        out_specs=pl.BlockSpec((1, 2, tk, C), lambda i, b: (b, 0, i, 0)),
        out_shape=jax.ShapeDtypeStruct((B, 2, L, C), BF16),
        compiler_params=_cparams(48),
        name="conv_forward",
    )(fwd, y, spec, spec)


def _conv_inv_kernel(t_ref, z_ref, y_ref, g_ref, skip_ref, o_ref):
    L2 = t_ref.shape[1]
    conv = _dot(t_ref[...], z_ref[0].reshape(L2, HY_WIDTH)) * (2.0 / L2)
    o_ref[0] = g_ref[0] * (conv + skip_ref[...] * y_ref[0])


def conv_inverse(zf, y, ycol, gates, gcol, skip, inv, tt):
    B, _, L, C = zf.shape
    return pl.pallas_call(
        _conv_inv_kernel,
        grid=(L // tt, B),
        in_specs=[pl.BlockSpec((tt, 2 * L), lambda i, b: (i, 0)),
                  pl.BlockSpec((1, 2, L, C), lambda i, b: (b, 0, 0, 0)),
                  pl.BlockSpec((1, tt, C), lambda i, b: (b, i, ycol)),
                  pl.BlockSpec((1, tt, C), lambda i, b: (b, i, gcol)),
                  pl.BlockSpec((1, C), lambda i, b: (0, 0))],
        out_specs=pl.BlockSpec((1, tt, C), lambda i, b: (b, i, 0)),
        out_shape=jax.ShapeDtypeStruct((B, L, C), F32),
        compiler_params=_cparams(48),
        name="conv_inverse",
    )(inv, zf, y, gates, skip.reshape(1, C))


def hyena(hy, lp, layer, spec, tabs, B, L):
    fwd, inv, tt = tabs
    z = short_conv(hy, lp['hy_short_w'], lp['hy_short_b'], B, L)
    y, ycol = z, 0
    for o in range(HY_ORDER):
        g = layer * 2 * HY_ORDER
        zf = conv_forward(y, ycol, spec, g + o, g + HY_ORDER + o, fwd)
        y = conv_inverse(zf, y, ycol, z, 1 + o, lp['hy_skip'][o], inv, tt)
        ycol = 0
    return y.reshape(B * L, HY_WIDTH)


def _merge_kernel(a_ref, b_ref, c_ref, d_ref, gate_ref, x_ref, mod_ref, wb_ref, wo_ref, o_ref):
    merged = None
    for i, br in enumerate((a_ref, b_ref, c_ref, d_ref)):
        t = gate_ref[:, i * D_MODEL:(i + 1) * D_MODEL] * _dot(br[...].astype(BF16), wb_ref[i])
        merged = t if merged is None else merged + t
    o_ref[...] = x_ref[...] + mod_ref[0, 5:6, :] * _dot(merged.astype(BF16), wo_ref[...])


def merge_block(branches, gates, x, mod, wb, wo, rows_per_b):
    T = x.shape[0]
    tm = FFN_ROWS
    nb = mod.shape[0]
    row = lambda w: pl.BlockSpec((tm, w), lambda i: (i, 0))
    return pl.pallas_call(
        _merge_kernel,
        grid=(T // tm,),
        in_specs=[row(BRANCH_W)] * N_BRANCH + [row(GATE_W), row(D_MODEL), _mod_spec(nb, rows_per_b // tm),
                                               _resident((N_BRANCH, BRANCH_W, D_MODEL)),
                                               _resident((D_MODEL, D_MODEL))],
        out_specs=row(D_MODEL),
        out_shape=jax.ShapeDtypeStruct((T, D_MODEL), F32),
        compiler_params=_cparams(56),
        name="merge_block",
    )(*branches, gates, x, mod, wb, wo)


def _heads_out(t, B, L, n):
    return t.reshape(B, L, n, HEAD_DIM).transpose(0, 2, 1, 3)


def _heads_in(t):
    B, H, P, d = t.shape
    return t.transpose(0, 2, 1, 3).reshape(B, P, H * d)


def _run_pass(x, mod_all, p, wts, final_norm, B, L, spec, tabs, caches):
    ctx_pass = caches is None
    collected = []
    if not ctx_pass:
        rope = rope_tables(L, HEAD_DIM, SWA_QW + SWA_KVW) + rope_tables(L, DIFF_HD, 2 * BRANCH_W)
        bias_all = [na_bias_tables(p['na_rpb'][l], L // GRID_W) for l in range(DEPTH)]
    for l in range(DEPTH):
        lp = {k: v[l] for k, v in p.items()}
        w = wts[l]
        mod = mod_all[l]
        x = ffn_block(x, mod, lp['norm_ffn1'], w['ffn1_w1'], w['ffn1_w3'], w['ffn1_w2'], 0, L)
        na, swa, hy, dif, gates = in_projection(x, mod, lp['norm_mix'], w['w_in'], L,
                                                None if ctx_pass else rope)
        if ctx_pass:
            a_o = dense_attention(na, (0, 1, 2), B, L)
            b_o = dense_attention(swa, (0, 2, 3), B, L, gqa=True, sink=lp['swa_sink'])
            d_o = diff_attention(dif, lp['diff_lambda'], lp['diff_subln'], l, B, L)
            collected.append((
                _heads_out(na[:, BRANCH_W:2 * BRANCH_W], B, L, N_HEADS),
                _heads_out(na[:, 2 * BRANCH_W:], B, L, N_HEADS),
                _heads_out(swa[:, SWA_QW:SWA_QW + SWA_KVW], B, L, SWA_KV_HEADS),
                _heads_out(swa[:, SWA_QW + SWA_KVW:], B, L, SWA_KV_HEADS),
                _heads_out(dif[:, BRANCH_W:2 * BRANCH_W], B, L, N_HEADS),
                _heads_out(dif[:, 2 * BRANCH_W:], B, L, N_HEADS)))
        else:
            ck_na, cv_na, ck_swa, cv_swa, ck_d, cv_d = (_heads_in(t[:, l]) for t in caches)
            a_o = neighbourhood_attention(na, ck_na, cv_na, bias_all[l], B, L)
            b_o = dense_attention(swa, (0, 2, 3), B, L, window=True, gqa=True,
                                  ctx=(ck_swa, cv_swa), sink=lp['swa_sink'])
            d_o = diff_attention(dif, lp['diff_lambda'], lp['diff_subln'], l, B, L, ctx=(ck_d, cv_d))
        c_o = hyena(hy, lp, l, spec, tabs, B, L)
        x = merge_block((a_o, b_o, c_o, d_o), gates, x, mod, w['w_branch'], w['w_out'], L)
        x = ffn_block(x, mod, lp['norm_ffn2'], w['ffn2_w1'], w['ffn2_w3'], w['ffn2_w2'], 6, L,
                      final_w=final_norm if l == DEPTH - 1 else None)
    return x, collected


def _hyena_setup(L, p, tk, tt):
    fwd, inv = dft_tables(L, tk)
    filt = hyena_filters(L, p)
    spec = filter_spectra(filt.reshape(DEPTH * 2 * HY_ORDER, L, HY_WIDTH), fwd)
    return spec, (fwd, inv, tt)


def kernel(x_prompt, x_sample, cache_na_k, cache_na_v, cache_swa_k, cache_swa_v, cache_diff_k, cache_diff_v, c, c_ctx, w_ada, b_ada, norm_ffn1, norm_mix, norm_ffn2, final_norm, ffn1_w1, ffn1_w3, ffn1_w2, ffn2_w1, ffn2_w3, ffn2_w2, w_in, w_branch, w_out, na_rpb, swa_sink, hy_short_w, hy_short_b, hy_w1, hy_b1, hy_w2, hy_b2, hy_w3, hy_sin_freq, hy_log_decay, hy_skip, diff_lambda, diff_subln):
    B_ctx, L_ctx, _ = x_prompt.shape
    B_den, L_den, _ = x_sample.shape
    p = {
        'norm_ffn1': norm_ffn1, 'norm_mix': norm_mix, 'norm_ffn2': norm_ffn2,
        'na_rpb': na_rpb, 'swa_sink': swa_sink, 'hy_short_w': hy_short_w, 'hy_short_b': hy_short_b,
        'hy_w1': hy_w1, 'hy_b1': hy_b1, 'hy_w2': hy_w2, 'hy_b2': hy_b2, 'hy_w3': hy_w3,
        'hy_sin_freq': hy_sin_freq, 'hy_log_decay': hy_log_decay, 'hy_skip': hy_skip,
        'diff_lambda': diff_lambda, 'diff_subln': diff_subln,
    }
    big = {'ffn1_w1': ffn1_w1, 'ffn1_w3': ffn1_w3, 'ffn1_w2': ffn1_w2, 'ffn2_w1': ffn2_w1,
           'ffn2_w3': ffn2_w3, 'ffn2_w2': ffn2_w2, 'w_in': w_in, 'w_branch': w_branch, 'w_out': w_out}
    wts = [{k: v[l].astype(BF16) for k, v in big.items()} for l in range(DEPTH)]

    cond = jnp.concatenate([c, c_ctx[None, :]], axis=0)
    rows = 8 * ((cond.shape[0] + 7) // 8)
    cond = jnp.pad(cond, ((0, rows - cond.shape[0]), (0, 0)))
    mod = ada_modulation(cond, w_ada, b_ada).reshape(DEPTH, rows, N_MOD, D_MODEL)
    mod_den = mod[:, :B_den]
    mod_ctx = mod[:, B_den:B_den + 1]

    spec_c, tabs_c = _hyena_setup(L_ctx, p, min(L_ctx, 256), min(L_ctx, 256))
    y_ctx, collected = _run_pass(x_prompt.reshape(B_ctx * L_ctx, D_MODEL), mod_ctx, p, wts, final_norm,
                                 B_ctx, L_ctx, spec_c, tabs_c, None)
    new_caches = [jnp.stack([col[j] for col in collected], axis=1) for j in range(6)]

    spec_d, tabs_d = _hyena_setup(L_den, p, 512, 512)
    caches = (cache_na_k, cache_na_v, cache_swa_k, cache_swa_v, cache_diff_k, cache_diff_v)
    y_den, _ = _run_pass(x_sample.reshape(B_den * L_den, D_MODEL), mod_den, p, wts, final_norm,
                         B_den, L_den, spec_d, tabs_d, caches)
    return (y_ctx.reshape(B_ctx, L_ctx, D_MODEL), y_den.reshape(B_den, L_den, D_MODEL), *new_caches)
```

```python
import functools
import math

import jax
import jax.numpy as jnp
from jax import lax
from jax.experimental import pallas as pl
from jax.experimental.pallas import tpu as pltpu

F32 = jnp.float32
BF16 = jnp.bfloat16

D_MODEL = 1024
DEPTH = 4
GRID_W = 64
N_BRANCH = 4
BRANCH_W = D_MODEL // 4
HEAD_DIM = 64
N_HEADS = BRANCH_W // HEAD_DIM
NA_WIN_R = 8
NA_WIN_C = 16
SWA_KV_HEADS = N_HEADS // 2
SWA_WINDOW = 128
HY_WIDTH = BRANCH_W
HY_ORDER = 2
HY_FREQS = 16
HY_EMB = 1 + 2 * HY_FREQS
HY_HIDDEN = 64
DIFF_HD = 32
D_FF = 128 * ((8 * D_MODEL // 3 + 127) // 128)
ROPE_BASE = 10000.0
EPS = 1e-6
NEG = -1e30
N_MOD = 9
NA_W = 3 * BRANCH_W
SWA_QW = BRANCH_W
SWA_KVW = SWA_KV_HEADS * HEAD_DIM
SWA_W = SWA_QW + 2 * SWA_KVW
HY_IN_W = 3 * HY_WIDTH
DIFF_W = 3 * BRANCH_W
GATE_W = N_BRANCH * D_MODEL
MIX_W = NA_W + SWA_W + HY_IN_W + DIFF_W

LANES = 128
MXU_DIM = 256
MIB = 1024 * 1024

FFN_ROWS = 512
PROJ_ROWS = 512
FFN_CHUNK = 768
ATTN_Q = 256
NA_Q_ROWS = ATTN_Q // GRID_W
NA_SLAB_ROWS = NA_Q_ROWS + NA_WIN_R
SWA_SLAB = ATTN_Q + 2 * SWA_WINDOW


def _cparams(vmem_mib):
    return pltpu.CompilerParams(vmem_limit_bytes=vmem_mib * MIB)


def _resident(shape):
    nd = len(shape)
    return pl.BlockSpec(shape, lambda *_: (0,) * nd, pipeline_mode=pl.Buffered(1))


def _dot(a, b):
    return jnp.dot(a, b, preferred_element_type=F32)


def _dot_nt(a, b):
    return lax.dot_general(a, b, (((1,), (1,)), ((), ())), preferred_element_type=F32)


def _dot_exact(a, b):
    return jnp.dot(a, b, preferred_element_type=F32, precision=lax.Precision.HIGHEST)


def _sigmoid(x):
    return 1.0 / (1.0 + jnp.exp(-x))


def _norm_mod(x, nw, shift, scale):
    y = x * lax.rsqrt(jnp.mean(x * x, axis=-1, keepdims=True) + EPS) * nw
    return y * (1.0 + scale) + shift


def _mod_spec(nb, tiles_per_b):
    if nb == 1:
        return pl.BlockSpec((1, N_MOD, D_MODEL), lambda i: (0, 0, 0))
    return pl.BlockSpec((1, N_MOD, D_MODEL), lambda i: (i // tiles_per_b, 0, 0))


def _ada_kernel(c_ref, w_ref, b_ref, o_ref):
    c = c_ref[...]
    s = (c * _sigmoid(c)).astype(BF16)
    o_ref[0] = _dot(s, w_ref[0].astype(BF16)) + b_ref[0]


def ada_modulation(cond, w_ada, b_ada):
    rows = cond.shape[0]
    width = N_MOD * D_MODEL
    tn = 9 * LANES
    return pl.pallas_call(
        _ada_kernel,
        grid=(DEPTH, width // tn),
        in_specs=[pl.BlockSpec((rows, D_MODEL), lambda l, j: (0, 0)),
                  pl.BlockSpec((1, D_MODEL, tn), lambda l, j: (l, 0, j)),
                  pl.BlockSpec((1, 1, tn), lambda l, j: (l, 0, j))],
        out_specs=pl.BlockSpec((1, rows, tn), lambda l, j: (l, 0, j)),
        out_shape=jax.ShapeDtypeStruct((DEPTH, rows, width), F32),
        compiler_params=_cparams(32),
        name="ada_modulation",
    )(cond, w_ada, b_ada.reshape(DEPTH, 1, width))


def _ffn_kernel(x_ref, mod_ref, nw_ref, w1_ref, w3_ref, w2_ref, *rest, mod_base, final):
    if final:
        fw_ref, o_ref = rest
    else:
        (o_ref,) = rest
    x = x_ref[...]
    shift = mod_ref[0, mod_base:mod_base + 1, :]
    scale = mod_ref[0, mod_base + 1:mod_base + 2, :]
    gate = mod_ref[0, mod_base + 2:mod_base + 3, :]
    h = _norm_mod(x, nw_ref[...], shift, scale).astype(BF16)
    acc = jnp.zeros(x.shape, F32)
    for lo in range(0, D_FF, FFN_CHUNK):
        hi = min(lo + FFN_CHUNK, D_FF)
        a = _dot(h, w1_ref[:, lo:hi])
        b = _dot(h, w3_ref[:, lo:hi])
        u = (a * _sigmoid(a) * b).astype(BF16)
        acc = acc + _dot(u, w2_ref[lo:hi, :])
    y = x + 0.5 * gate * acc
    if final:
        y = y * lax.rsqrt(jnp.mean(y * y, axis=-1, keepdims=True) + EPS) * fw_ref[...]
    o_ref[...] = y


def ffn_block(x, mod, nw, w1, w3, w2, mod_base, rows_per_b, final_w=None):
    T = x.shape[0]
    tm = FFN_ROWS
    nb = mod.shape[0]
    final = final_w is not None
    in_specs = [pl.BlockSpec((tm, D_MODEL), lambda i: (i, 0)),
                _mod_spec(nb, rows_per_b // tm),
                _resident((1, D_MODEL)),
                _resident((D_MODEL, D_FF)), _resident((D_MODEL, D_FF)), _resident((D_FF, D_MODEL))]
    args = [x, mod, nw.reshape(1, D_MODEL), w1, w3, w2]
    if final:
        in_specs.append(_resident((1, D_MODEL)))
        args.append(final_w.reshape(1, D_MODEL))
    return pl.pallas_call(
        functools.partial(_ffn_kernel, mod_base=mod_base, final=final),
        grid=(T // tm,),
        in_specs=in_specs,
        out_specs=pl.BlockSpec((tm, D_MODEL), lambda i: (i, 0)),
        out_shape=jax.ShapeDtypeStruct((T, D_MODEL), F32),
        compiler_params=_cparams(56),
        name="ffn_block",
    )(*args)


def _rope_chunk(x, cos, sin_a, sin_b, dist):
    return x * cos + pltpu.roll(x, LANES - dist, 1) * sin_a + pltpu.roll(x, dist, 1) * sin_b


def _proj_kernel(x_ref, mod_ref, nw_ref, w_ref, *rest, rope):
    if rope:
        (cs_ref, sa_ref, sb_ref, cd_ref, da_ref, db_ref,
         na_ref, swa_ref, hy_ref, dif_ref) = rest
    else:
        na_ref, swa_ref, hy_ref, dif_ref = rest
    h = _norm_mod(x_ref[...], nw_ref[...], mod_ref[0, 3:4, :], mod_ref[0, 4:5, :]).astype(BF16)
    o = 0
    na_ref[...] = _dot(h, w_ref[:, o:o + NA_W])
    o += NA_W
    s = _dot(h, w_ref[:, o:o + SWA_W])
    o += SWA_W
    n_rot = (SWA_QW + SWA_KVW) // LANES
    for c in range(SWA_W // LANES):
        sl = slice(c * LANES, (c + 1) * LANES)
        chunk = s[:, sl]
        if rope and c < n_rot:
            chunk = _rope_chunk(chunk, cs_ref[:, sl], sa_ref[:, sl], sb_ref[:, sl], HEAD_DIM // 4)
        swa_ref[:, sl] = chunk
    hy_ref[...] = _dot(h, w_ref[:, o:o + HY_IN_W])
    o += HY_IN_W
    s = _dot(h, w_ref[:, o:o + DIFF_W])
    o += DIFF_W
    n_rot = 2 * BRANCH_W // LANES
    for c in range(DIFF_W // LANES):
        sl = slice(c * LANES, (c + 1) * LANES)
        chunk = s[:, sl]
        if rope and c < n_rot:
            chunk = _rope_chunk(chunk, cd_ref[:, sl], da_ref[:, sl], db_ref[:, sl], DIFF_HD // 4)
        dif_ref[:, sl] = chunk


def in_projection(x, mod, nw, w_in, rows_per_b, rope_tabs):
    T = x.shape[0]
    tm = PROJ_ROWS
    nb = mod.shape[0]
    rope = rope_tabs is not None
    in_specs = [pl.BlockSpec((tm, D_MODEL), lambda i: (i, 0)),
                _mod_spec(nb, rows_per_b // tm),
                _resident((1, D_MODEL)),
                _resident((D_MODEL, MIX_W))]
    args = [x, mod, nw.reshape(1, D_MODEL), w_in]
    if rope:
        pos_tiles = rows_per_b // tm
        for t in rope_tabs:
            in_specs.append(pl.BlockSpec((tm, t.shape[1]), lambda i: (i % pos_tiles, 0)))
            args.append(t)
    widths = (NA_W, SWA_W, HY_IN_W, DIFF_W)
    return pl.pallas_call(
        functools.partial(_proj_kernel, rope=rope),
        grid=(T // tm,),
        in_specs=in_specs,
        out_specs=[pl.BlockSpec((tm, w), lambda i: (i, 0)) for w in widths],
        out_shape=[jax.ShapeDtypeStruct((T, w), F32) for w in widths],
        compiler_params=_cparams(56),
        name="in_projection",
    )(*args)


def rope_tables(L, dh, width):
    nf = dh // 4
    t = jnp.arange(L)
    freqs = ROPE_BASE ** (-jnp.arange(nf, dtype=F32) / nf)
    ang_r = (t // GRID_W).astype(F32)[:, None] * freqs
    ang_c = (t % GRID_W).astype(F32)[:, None] * freqs
    cr, sr, cc, sc = jnp.cos(ang_r), jnp.sin(ang_r), jnp.cos(ang_c), jnp.sin(ang_c)
    z = jnp.zeros_like(sr)
    reps = width // dh
    cos = jnp.tile(jnp.concatenate([cr, cr, cc, cc], axis=-1), (1, reps))
    sin_a = jnp.tile(jnp.concatenate([-sr, z, -sc, z], axis=-1), (1, reps))
    sin_b = jnp.tile(jnp.concatenate([z, sr, z, sc], axis=-1), (1, reps))
    return cos, sin_a, sin_b


def _lane_mask(width, lo, n):
    lane = lax.broadcasted_iota(jnp.int32, (1, width), 1)
    return (lane >= lo) & (lane < lo + n)


LOG2E = math.log2(math.e)


def _softmax_parts(parts, extra_logit=None):
    m = parts[0].max(axis=-1, keepdims=True)
    for s in parts[1:]:
        m = jnp.maximum(m, s.max(axis=-1, keepdims=True))
    if extra_logit is not None:
        m = jnp.maximum(m, extra_logit)
    ps = [jnp.exp2(s - m) for s in parts]
    l = ps[0].sum(axis=-1, keepdims=True)
    for p in ps[1:]:
        l = l + p.sum(axis=-1, keepdims=True)
    if extra_logit is not None:
        l = l + jnp.exp2(extra_logit - m)
    return ps, l


def _na_kernel(q_ref, k_ref, v_ref, kc_ref, vc_ref, bias_ref, o_ref, *, n_rows):
    i = pl.program_id(1)
    row0 = jnp.clip(NA_Q_ROWS * i - NA_WIN_R // 2, 0, n_rows - NA_SLAB_ROWS)
    start = pl.multiple_of(row0 * GRID_W, GRID_W)
    slab = NA_SLAB_ROWS * GRID_W
    ks = k_ref[0, pl.ds(start, slab), :].astype(BF16)
    vs = v_ref[0, pl.ds(start, slab), :].astype(BF16)
    kc = kc_ref[0].astype(BF16)
    vc = vc_ref[0].astype(BF16)
    q = q_ref[0] * (HEAD_DIM ** -0.5 * LOG2E)
    out = jnp.zeros(q.shape, F32)
    for h in range(N_HEADS):
        hm = _lane_mask(BRANCH_W, h * HEAD_DIM, HEAD_DIM)
        qh = jnp.where(hm, q, 0.0).astype(BF16)
        s_win = _dot_nt(qh, ks) + bias_ref[0, 0, h]
        s_ctx = _dot_nt(qh, kc)
        (p_win, p_ctx), l = _softmax_parts([s_win, s_ctx])
        o = _dot(p_win.astype(BF16), vs) + _dot(p_ctx.astype(BF16), vc)
        out = out + jnp.where(hm, o / l, 0.0)
    o_ref[0] = out


NA_DROWS = 2 * NA_WIN_R - 1
NA_DCOLS = 2 * NA_WIN_C - 1


def _na_tile_geometry(n_rows):
    n_tiles = n_rows // NA_Q_ROWS

    def geometry(tile):
        slab0 = min(max(NA_Q_ROWS * tile - NA_WIN_R // 2, 0), n_rows - NA_SLAB_ROWS)
        rows = []
        for j in range(NA_Q_ROWS):
            qr = NA_Q_ROWS * tile + j
            r0 = min(max(qr - NA_WIN_R // 2, 0), n_rows - NA_WIN_R)
            rows.append([slab0 + m - qr + NA_WIN_R - 1 if r0 <= slab0 + m < r0 + NA_WIN_R else None
                         for m in range(NA_SLAB_ROWS)])
        return rows

    kinds = [geometry(0), geometry(1), geometry(n_tiles - 1)]
    assert all(geometry(t) == kinds[1] for t in range(1, n_tiles - 1))
    return kinds


def _na_bias_kernel(rpb_ref, o_ref, band_ref, *, kinds):
    base = (pl.program_id(0) * N_HEADS + pl.program_id(1)) * (NA_DROWS * NA_DCOLS)
    qc = lax.broadcasted_iota(jnp.int32, (GRID_W, LANES), 0)
    lane = lax.broadcasted_iota(jnp.int32, (GRID_W, LANES), 1)
    kc = lane % GRID_W
    d_col = kc - qc + (NA_WIN_C - 1)
    c0 = jnp.clip(qc - NA_WIN_C // 2, 0, GRID_W - NA_WIN_C)
    col_ok = (kc >= c0) & (kc < c0 + NA_WIN_C)
    for d in range(NA_DROWS):
        t = jnp.full((GRID_W, LANES), NEG, F32)
        for e in range(NA_DCOLS):
            t = jnp.where(d_col == e, rpb_ref[base + d * NA_DCOLS + e] * LOG2E, t)
        band_ref[d] = jnp.where(col_ok, t, NEG)
    masked = jnp.full((GRID_W, LANES), NEG, F32)
    left = lane < GRID_W
    for kind, rows in enumerate(kinds):
        for j, drow in enumerate(rows):
            for m in range(0, NA_SLAB_ROWS, 2):
                a = masked if drow[m] is None else band_ref[drow[m]]
                b = masked if drow[m + 1] is None else band_ref[drow[m + 1]]
                o_ref[0, kind, 0, j * GRID_W:(j + 1) * GRID_W, m * GRID_W:(m + 2) * GRID_W] = jnp.where(left, a, b)


def na_bias_tables(rpb, n_rows):
    slab = NA_SLAB_ROWS * GRID_W
    return pl.pallas_call(
        functools.partial(_na_bias_kernel, kinds=_na_tile_geometry(n_rows)),
        grid=(DEPTH, N_HEADS),
        in_specs=[pl.BlockSpec(memory_space=pltpu.SMEM)],
        out_specs=pl.BlockSpec((1, 3, 1, ATTN_Q, slab), lambda l, h: (l, 0, h, 0, 0)),
        out_shape=jax.ShapeDtypeStruct((DEPTH, 3, N_HEADS, ATTN_Q, slab), F32),
        scratch_shapes=[pltpu.VMEM((NA_DROWS, GRID_W, LANES), F32)],
        compiler_params=_cparams(32),
        name="na_bias_tables",
    )(rpb.astype(F32).reshape(-1))


def neighbourhood_attention(na, kc, vc, bias, layer, B, L):
    n_rows = L // GRID_W
    n_tiles = L // ATTN_Q
    na3 = na.reshape(B, L, NA_W)
    P = kc.shape[1]
    slab = NA_SLAB_ROWS * GRID_W

    def kind(b, i):
        return (layer, jnp.where(i == 0, 0, jnp.where(i == n_tiles - 1, 2, 1)), 0, 0, 0)

    return pl.pallas_call(
        functools.partial(_na_kernel, n_rows=n_rows),
        grid=(B, n_tiles),
        in_specs=[pl.BlockSpec((1, ATTN_Q, BRANCH_W), lambda b, i: (b, i, 0)),
                  pl.BlockSpec((1, L, BRANCH_W), lambda b, i: (b, 0, 1)),
                  pl.BlockSpec((1, L, BRANCH_W), lambda b, i: (b, 0, 2)),
                  pl.BlockSpec((1, P, BRANCH_W), lambda b, i: (b, 0, 0)),
                  pl.BlockSpec((1, P, BRANCH_W), lambda b, i: (b, 0, 0)),
                  pl.BlockSpec((1, 1, N_HEADS, ATTN_Q, slab), kind)],
        out_specs=pl.BlockSpec((1, ATTN_Q, BRANCH_W), lambda b, i: (b, i, 0)),
        out_shape=jax.ShapeDtypeStruct((B, L, BRANCH_W), F32),
        compiler_params=_cparams(48),
        name="neighbourhood_attention",
    )(na3, na3, na3, kc, vc, bias).reshape(B * L, BRANCH_W)


def _attn_kernel(*refs, L, window, gqa, has_ctx, has_sink):
    refs = list(refs)
    q_ref, k_ref, v_ref = refs[:3]
    pos = 3
    if has_ctx:
        kc_ref, vc_ref = refs[pos:pos + 2]
        pos += 2
    if has_sink:
        sink_ref = refs[pos]
        pos += 1
    o_ref = refs[pos]
    i = pl.program_id(1)
    if window:
        slab = SWA_SLAB
        start = pl.multiple_of(jnp.clip(i * ATTN_Q - SWA_WINDOW, 0, L - slab), SWA_WINDOW)
        q_pos = i * ATTN_Q + lax.broadcasted_iota(jnp.int32, (ATTN_Q, 1), 0)
        k_pos = start + lax.broadcasted_iota(jnp.int32, (1, slab), 1)
        ok = jnp.abs(k_pos - q_pos) <= SWA_WINDOW
        ks = k_ref[0, pl.ds(start, slab), :].astype(BF16)
        vs = v_ref[0, pl.ds(start, slab), :].astype(BF16)
    else:
        ks = k_ref[0].astype(BF16)
        vs = v_ref[0].astype(BF16)
    if has_ctx:
        kc = kc_ref[0].astype(BF16)
        vc = vc_ref[0].astype(BF16)
    q = q_ref[0] * (HEAD_DIM ** -0.5 * LOG2E)
    kv_w = ks.shape[1]
    halves = [jnp.zeros((ATTN_Q, LANES), F32), jnp.zeros((ATTN_Q, LANES), F32)]
    out = jnp.zeros(q.shape, F32)
    for h in range(N_HEADS):
        if gqa:
            kvh, slot = h // 2, h % 2
            qh = q[:, kvh * LANES:(kvh + 1) * LANES]
            if slot != kvh:
                qh = pltpu.roll(qh, HEAD_DIM, 1)
            hm = _lane_mask(kv_w, kvh * HEAD_DIM, HEAD_DIM)
        else:
            qh = q
            hm = _lane_mask(kv_w, h * HEAD_DIM, HEAD_DIM)
        qh = jnp.where(hm, qh, 0.0).astype(BF16)
        s = _dot_nt(qh, ks)
        if window:
            s = jnp.where(ok, s, NEG)
        parts = [s]
        if has_ctx:
            parts.append(_dot_nt(qh, kc))
        ps, l = _softmax_parts(parts, sink_ref[h] * LOG2E if has_sink else None)
        o = _dot(ps[0].astype(BF16), vs)
        if has_ctx:
            o = o + _dot(ps[1].astype(BF16), vc)
        o = jnp.where(hm, o / l, 0.0)
        if gqa:
            if slot != kvh:
                o = pltpu.roll(o, HEAD_DIM, 1)
            halves[kvh] = halves[kvh] + o
        else:
            out = out + o
    if gqa:
        o_ref[0, :, 0:LANES] = halves[0]
        o_ref[0, :, LANES:2 * LANES] = halves[1]
    else:
        o_ref[0] = out


def dense_attention(src, cols, B, L, *, window=False, gqa=False, ctx=None, sink=None):
    W = src.shape[1]
    src3 = src.reshape(B, L, W)
    kv_w = SWA_KVW if gqa else BRANCH_W
    qc, kcol, vcol = cols
    in_specs = [pl.BlockSpec((1, ATTN_Q, BRANCH_W), lambda b, i: (b, i, qc)),
                pl.BlockSpec((1, L, kv_w), lambda b, i: (b, 0, kcol)),
                pl.BlockSpec((1, L, kv_w), lambda b, i: (b, 0, vcol))]
    args = [src3, src3, src3]
    if ctx is not None:
        P = ctx[0].shape[1]
        in_specs += [pl.BlockSpec((1, P, kv_w), lambda b, i: (b, 0, 0))] * 2
        args += list(ctx)
    if sink is not None:
        in_specs.append(pl.BlockSpec(memory_space=pltpu.SMEM))
        args.append(sink)
    return pl.pallas_call(
        functools.partial(_attn_kernel, L=L, window=window, gqa=gqa,
                          has_ctx=ctx is not None, has_sink=sink is not None),
        grid=(B, L // ATTN_Q),
        in_specs=in_specs,
        out_specs=pl.BlockSpec((1, ATTN_Q, BRANCH_W), lambda b, i: (b, i, 0)),
        out_shape=jax.ShapeDtypeStruct((B, L, BRANCH_W), F32),
        compiler_params=_cparams(48),
        name="window_attention" if window else "dense_attention",
    )(*args).reshape(B * L, BRANCH_W)


def _diff_kernel(*refs, lam_init, has_ctx):
    refs = list(refs)
    lam_ref, gain_ref, ones_ref, q_ref, k_ref, v_ref = refs[:6]
    if has_ctx:
        kc_ref, vc_ref, o_ref = refs[6:]
    else:
        (o_ref,) = refs[6:]
    lv = lam_ref[...]
    lam = (jnp.exp(jnp.sum(lv[0:1] * lv[1:2], keepdims=True))
           - jnp.exp(jnp.sum(lv[2:3] * lv[3:4], keepdims=True)) + lam_init)
    kl = k_ref[0].astype(BF16)
    vl = v_ref[0].astype(BF16)
    if has_ctx:
        kc = kc_ref[0].astype(BF16)
        vc = vc_ref[0].astype(BF16)
    q = q_ref[0] * (DIFF_HD ** -0.5 * LOG2E)
    out = jnp.zeros(q.shape, F32)
    for h in range(N_HEADS):
        o = None
        for mp in range(2):
            mm = _lane_mask(BRANCH_W, h * HEAD_DIM + mp * DIFF_HD, DIFF_HD)
            qm = jnp.where(mm, q, 0.0).astype(BF16)
            parts = [_dot_nt(qm, kl)]
            if has_ctx:
                parts.append(_dot_nt(qm, kc))
            ps, l = _softmax_parts(parts)
            pv = _dot(ps[0].astype(BF16), vl)
            if has_ctx:
                pv = pv + _dot(ps[1].astype(BF16), vc)
            o = pv / l if mp == 0 else o - pv * (lam / l)
        out = out + jnp.where(_lane_mask(BRANCH_W, h * HEAD_DIM, HEAD_DIM), o, 0.0)
    ms = _dot_exact(out * out, ones_ref[...]) * (1.0 / HEAD_DIM)
    o_ref[0] = out * lax.rsqrt(ms + EPS) * gain_ref[...] * (1.0 - lam_init)


def diff_attention(dif, lam_vec, subln, layer, B, L, ctx=None):
    lam_init = 0.8 - 0.6 * math.exp(-0.3 * layer)
    dif3 = dif.reshape(B, L, DIFF_W)
    head = jnp.arange(BRANCH_W) // HEAD_DIM
    ones = (head[:, None] == head[None, :]).astype(F32)
    gain = jnp.tile(subln.astype(F32), N_HEADS).reshape(1, BRANCH_W)
    in_specs = [_resident((4, DIFF_HD)), _resident((1, BRANCH_W)), _resident((BRANCH_W, BRANCH_W)),
                pl.BlockSpec((1, ATTN_Q, BRANCH_W), lambda b, i: (b, i, 0)),
                pl.BlockSpec((1, L, BRANCH_W), lambda b, i: (b, 0, 1)),
                pl.BlockSpec((1, L, BRANCH_W), lambda b, i: (b, 0, 2))]
    args = [lam_vec, gain, ones, dif3, dif3, dif3]
    if ctx is not None:
        P = ctx[0].shape[1]
        in_specs += [pl.BlockSpec((1, P, BRANCH_W), lambda b, i: (b, 0, 0))] * 2
        args += list(ctx)
    return pl.pallas_call(
        functools.partial(_diff_kernel, lam_init=lam_init, has_ctx=ctx is not None),
        grid=(B, L // ATTN_Q),
        in_specs=in_specs,
        out_specs=pl.BlockSpec((1, ATTN_Q, BRANCH_W), lambda b, i: (b, i, 0)),
        out_shape=jax.ShapeDtypeStruct((B, L, BRANCH_W), F32),
        compiler_params=_cparams(56),
        name="diff_attention",
    )(*args).reshape(B * L, BRANCH_W)


def _filter_kernel(z_ref, w1_ref, b1_ref, w2_ref, b2_ref, w3_ref, fr_ref, ld_ref, o_ref):
    z = z_ref[...]
    tn = z[:, 0:1]
    fr = fr_ref[0]
    g = jnp.sin(fr * (_dot_exact(z, w1_ref[0]) + b1_ref[0]))
    g = jnp.sin(fr * (_dot_exact(g, w2_ref[0]) + b2_ref[0]))
    hf = _dot_exact(g, w3_ref[0]) * jnp.exp(-jnp.exp(ld_ref[0]) * tn)
    row = lax.broadcasted_iota(jnp.int32, (z.shape[0], 1), 0)
    for o in range(HY_ORDER):
        pos = hf[:, (2 * o) * HY_WIDTH:(2 * o + 1) * HY_WIDTH]
        neg = jnp.where(row == 0, 0.0, hf[:, (2 * o + 1) * HY_WIDTH:(2 * o + 2) * HY_WIDTH])
        norm = (jnp.sum(jnp.abs(pos), axis=0, keepdims=True)
                + jnp.sum(jnp.abs(neg), axis=0, keepdims=True) + EPS)
        o_ref[0, o] = pos / norm
        o_ref[0, HY_ORDER + o] = neg / norm


def hyena_filters(L, p):
    tn = jnp.arange(L, dtype=F32) / L
    ang = 2.0 * math.pi * tn[:, None] * jnp.arange(1, HY_FREQS + 1, dtype=F32)[None, :]
    z = jnp.concatenate([tn[:, None], jnp.cos(ang), jnp.sin(ang)], axis=-1)
    z = jnp.pad(z, ((0, 0), (0, HY_HIDDEN - HY_EMB)))
    w1 = jnp.pad(p['hy_w1'], ((0, 0), (0, HY_HIDDEN - HY_EMB), (0, 0)))
    fw = HY_ORDER * 2 * HY_WIDTH
    per_layer = lambda *shape: pl.BlockSpec((1,) + shape, lambda l: (l,) + (0,) * len(shape))
    return pl.pallas_call(
        _filter_kernel,
        grid=(DEPTH,),
        in_specs=[pl.BlockSpec((L, HY_HIDDEN), lambda l: (0, 0)),
                  per_layer(HY_HIDDEN, HY_HIDDEN), per_layer(1, HY_HIDDEN),
                  per_layer(HY_HIDDEN, HY_HIDDEN), per_layer(1, HY_HIDDEN),
                  per_layer(HY_HIDDEN, fw), per_layer(1, HY_HIDDEN), per_layer(1, fw)],
        out_specs=per_layer(2 * HY_ORDER, L, HY_WIDTH),
        out_shape=jax.ShapeDtypeStruct((DEPTH, 2 * HY_ORDER, L, HY_WIDTH), F32),
        compiler_params=_cparams(56),
        name="hyena_filters",
    )(z, w1, p['hy_b1'].reshape(DEPTH, 1, HY_HIDDEN), p['hy_w2'], p['hy_b2'].reshape(DEPTH, 1, HY_HIDDEN),
      p['hy_w3'], p['hy_sin_freq'].reshape(DEPTH, 1, HY_HIDDEN), p['hy_log_decay'].reshape(DEPTH, 1, fw))


def dft_tables(L, tk):
    k = jnp.arange(L, dtype=jnp.int32)[:, None]
    n = jnp.arange(L, dtype=jnp.int32)[None, :]
    ang = (((2 * k + 1) * n) % (4 * L)).astype(F32) * (math.pi / (2 * L))
    c, s = jnp.cos(ang), jnp.sin(ang)
    fwd = jnp.concatenate([c.reshape(L // tk, tk, L), s.reshape(L // tk, tk, L)], axis=1).astype(BF16)
    inv = jnp.concatenate([c.T, -s.T], axis=1).astype(BF16)
    return fwd, inv


def _spectrum_kernel(t_ref, f_ref, o_ref):
    tk = o_ref.shape[2]
    acc = _dot(t_ref[0], f_ref[0].astype(BF16))
    o_ref[0, 0] = acc[:tk]
    o_ref[0, 1] = acc[tk:]


def filter_spectra(filt, fwd):
    G, L, C = filt.shape
    nkt, tk2, _ = fwd.shape
    tk = tk2 // 2
    return pl.pallas_call(
        _spectrum_kernel,
        grid=(nkt, G),
        in_specs=[pl.BlockSpec((1, tk2, L), lambda i, g: (i, 0, 0)),
                  pl.BlockSpec((1, L, C), lambda i, g: (g, 0, 0))],
        out_specs=pl.BlockSpec((1, 2, tk, C), lambda i, g: (g, 0, i, 0)),
        out_shape=jax.ShapeDtypeStruct((G, 2, L, C), F32),
        compiler_params=_cparams(48),
        name="filter_spectra",
    )(fwd, filt)


def _shortconv_kernel(u_ref, w_ref, b_ref, o_ref):
    u = u_ref[0]
    L = u.shape[0]
    row = lax.broadcasted_iota(jnp.int32, (L, 1), 0)
    prev = jnp.where(row == 0, 0.0, pltpu.roll(u, 1, 0))
    nxt = jnp.where(row == L - 1, 0.0, pltpu.roll(u, L - 1, 0))
    o_ref[0] = prev * w_ref[0:1, :] + u * w_ref[1:2, :] + nxt * w_ref[2:3, :] + b_ref[...]


def short_conv(hy, w, b, B, L):
    return pl.pallas_call(
        _shortconv_kernel,
        grid=(B,),
        in_specs=[pl.BlockSpec((1, L, HY_IN_W), lambda i: (i, 0, 0)),
                  _resident((3, HY_IN_W)), _resident((1, HY_IN_W))],
        out_specs=pl.BlockSpec((1, L, HY_IN_W), lambda i: (i, 0, 0)),
        out_shape=jax.ShapeDtypeStruct((B, L, HY_IN_W), F32),
        compiler_params=_cparams(56),
        name="short_conv",
    )(hy.reshape(B, L, HY_IN_W), w, b.reshape(1, HY_IN_W))


def _conv_fwd_kernel(t_ref, y_ref, hp_ref, hn_ref, o_ref):
    tk = o_ref.shape[2]
    acc = _dot(t_ref[0], y_ref[0].astype(BF16))
    yc, ys = acc[:tk], acc[tk:]
    hr = hp_ref[0, 0] + hn_ref[0, 0]
    hi = hn_ref[0, 1] - hp_ref[0, 1]
    o_ref[0, 0] = (yc * hr + ys * hi).astype(BF16)
    o_ref[0, 1] = (yc * hi - ys * hr).astype(BF16)


def conv_forward(y, col, spec, g_pos, g_neg, fwd):
    B, L, _ = y.shape
    C = HY_WIDTH
    nkt, tk2, _ = fwd.shape
    tk = tk2 // 2
    return pl.pallas_call(
        _conv_fwd_kernel,
        grid=(nkt, B),
        in_specs=[pl.BlockSpec((1, tk2, L), lambda i, b: (i, 0, 0)),
                  pl.BlockSpec((1, L, C), lambda i, b: (b, 0, col)),
                  pl.BlockSpec((1, 2, tk, C), lambda i, b: (g_pos, 0, i, 0)),
                  pl.BlockSpec((1, 2, tk, C), lambda i, b: (g_neg, 0, i, 0))],
        out_specs=pl.BlockSpec((1, 2, tk, C), lambda i, b: (b, 0, i, 0)),
        out_shape=jax.ShapeDtypeStruct((B, 2, L, C), BF16),
        compiler_params=_cparams(48),
        name="conv_forward",
    )(fwd, y, spec, spec)


def _conv_inv_kernel(t_ref, z_ref, y_ref, g_ref, skip_ref, o_ref):
    L2 = t_ref.shape[1]
    conv = _dot(t_ref[...], z_ref[0].reshape(L2, HY_WIDTH)) * (2.0 / L2)
    o_ref[0] = g_ref[0] * (conv + skip_ref[...] * y_ref[0])


def conv_inverse(zf, y, ycol, gates, gcol, skip, inv, tt):
    B, _, L, C = zf.shape
    return pl.pallas_call(
        _conv_inv_kernel,
        grid=(L // tt, B),
        in_specs=[pl.BlockSpec((tt, 2 * L), lambda i, b: (i, 0)),
                  pl.BlockSpec((1, 2, L, C), lambda i, b: (b, 0, 0, 0)),
                  pl.BlockSpec((1, tt, C), lambda i, b: (b, i, ycol)),
                  pl.BlockSpec((1, tt, C), lambda i, b: (b, i, gcol)),
                  pl.BlockSpec((1, C), lambda i, b: (0, 0))],
        out_specs=pl.BlockSpec((1, tt, C), lambda i, b: (b, i, 0)),
        out_shape=jax.ShapeDtypeStruct((B, L, C), F32),
        compiler_params=_cparams(48),
        name="conv_inverse",
    )(inv, zf, y, gates, skip.reshape(1, C))


def hyena(hy, lp, layer, spec, tabs, B, L):
    fwd, inv, tt = tabs
    z = short_conv(hy, lp['hy_short_w'], lp['hy_short_b'], B, L)
    y, ycol = z, 0
    for o in range(HY_ORDER):
        g = layer * 2 * HY_ORDER
        zf = conv_forward(y, ycol, spec, g + o, g + HY_ORDER + o, fwd)
        y = conv_inverse(zf, y, ycol, z, 1 + o, lp['hy_skip'][o], inv, tt)
        ycol = 0
    return y.reshape(B * L, HY_WIDTH)


def _merge_kernel(a_ref, b_ref, c_ref, d_ref, x_ref, mod_ref, nw_ref, wg_ref, wb_ref, wo_ref, o_ref):
    x = x_ref[...]
    h = _norm_mod(x, nw_ref[...], mod_ref[0, 3:4, :], mod_ref[0, 4:5, :]).astype(BF16)
    merged = None
    for i, br in enumerate((a_ref, b_ref, c_ref, d_ref)):
        gate = _sigmoid(_dot(h, wg_ref[:, i * D_MODEL:(i + 1) * D_MODEL]))
        t = gate * _dot(br[...].astype(BF16), wb_ref[i])
        merged = t if merged is None else merged + t
    o_ref[...] = x + mod_ref[0, 5:6, :] * _dot(merged.astype(BF16), wo_ref[...])


def merge_block(branches, x, mod, nw, wg, wb, wo, rows_per_b):
    T = x.shape[0]
    tm = FFN_ROWS
    nb = mod.shape[0]
    row = lambda w: pl.BlockSpec((tm, w), lambda i: (i, 0))
    return pl.pallas_call(
        _merge_kernel,
        grid=(T // tm,),
        in_specs=[row(BRANCH_W)] * N_BRANCH + [row(D_MODEL), _mod_spec(nb, rows_per_b // tm),
                                               _resident((1, D_MODEL)),
                                               _resident((D_MODEL, GATE_W)),
                                               _resident((N_BRANCH, BRANCH_W, D_MODEL)),
                                               _resident((D_MODEL, D_MODEL))],
        out_specs=row(D_MODEL),
        out_shape=jax.ShapeDtypeStruct((T, D_MODEL), F32),
        compiler_params=_cparams(56),
        name="merge_block",
    )(*branches, x, mod, nw.reshape(1, D_MODEL), wg, wb, wo)


def _heads_out(t, B, L, n):
    return t.reshape(B, L, n, HEAD_DIM).transpose(0, 2, 1, 3)


def _heads_in(t):
    B, H, P, d = t.shape
    return t.transpose(0, 2, 1, 3).reshape(B, P, H * d)


def _run_pass(x, mod_all, p, wts, final_norm, B, L, spec, tabs, caches):
    ctx_pass = caches is None
    collected = []
    if not ctx_pass:
        rope = rope_tables(L, HEAD_DIM, SWA_QW + SWA_KVW) + rope_tables(L, DIFF_HD, 2 * BRANCH_W)
        bias_all = na_bias_tables(p['na_rpb'], L // GRID_W)
    for l in range(DEPTH):
        lp = {k: v[l] for k, v in p.items()}
        w = wts[l]
        mod = mod_all[l]
        x = ffn_block(x, mod, lp['norm_ffn1'], w['ffn1_w1'], w['ffn1_w3'], w['ffn1_w2'], 0, L)
        na, swa, hy, dif = in_projection(x, mod, lp['norm_mix'], w['w_mix'], L, None if ctx_pass else rope)
        if ctx_pass:
            a_o = dense_attention(na, (0, 1, 2), B, L)
            b_o = dense_attention(swa, (0, 2, 3), B, L, gqa=True, sink=lp['swa_sink'])
            d_o = diff_attention(dif, lp['diff_lambda'], lp['diff_subln'], l, B, L)
            collected.append((
                _heads_out(na[:, BRANCH_W:2 * BRANCH_W], B, L, N_HEADS),
                _heads_out(na[:, 2 * BRANCH_W:], B, L, N_HEADS),
                _heads_out(swa[:, SWA_QW:SWA_QW + SWA_KVW], B, L, SWA_KV_HEADS),
                _heads_out(swa[:, SWA_QW + SWA_KVW:], B, L, SWA_KV_HEADS),
                _heads_out(dif[:, BRANCH_W:2 * BRANCH_W], B, L, N_HEADS),
                _heads_out(dif[:, 2 * BRANCH_W:], B, L, N_HEADS)))
        else:
            ck_na, cv_na, ck_swa, cv_swa, ck_d, cv_d = (_heads_in(t[:, l]) for t in caches)
            a_o = neighbourhood_attention(na, ck_na, cv_na, bias_all, l, B, L)
            b_o = dense_attention(swa, (0, 2, 3), B, L, window=True, gqa=True,
                                  ctx=(ck_swa, cv_swa), sink=lp['swa_sink'])
            d_o = diff_attention(dif, lp['diff_lambda'], lp['diff_subln'], l, B, L, ctx=(ck_d, cv_d))
        c_o = hyena(hy, lp, l, spec, tabs, B, L)
        x = merge_block((a_o, b_o, c_o, d_o), x, mod, lp['norm_mix'], w['w_gate'], w['w_branch'], w['w_out'], L)
        x = ffn_block(x, mod, lp['norm_ffn2'], w['ffn2_w1'], w['ffn2_w3'], w['ffn2_w2'], 6, L,
                      final_w=final_norm if l == DEPTH - 1 else None)
    return x, collected


def _hyena_setup(L, p, tk, tt):
    fwd, inv = dft_tables(L, tk)
    filt = hyena_filters(L, p)
    spec = filter_spectra(filt.reshape(DEPTH * 2 * HY_ORDER, L, HY_WIDTH), fwd)
    return spec, (fwd, inv, tt)


def kernel(x_prompt, x_sample, cache_na_k, cache_na_v, cache_swa_k, cache_swa_v, cache_diff_k, cache_diff_v, c, c_ctx, w_ada, b_ada, norm_ffn1, norm_mix, norm_ffn2, final_norm, ffn1_w1, ffn1_w3, ffn1_w2, ffn2_w1, ffn2_w3, ffn2_w2, w_in, w_branch, w_out, na_rpb, swa_sink, hy_short_w, hy_short_b, hy_w1, hy_b1, hy_w2, hy_b2, hy_w3, hy_sin_freq, hy_log_decay, hy_skip, diff_lambda, diff_subln):
    B_ctx, L_ctx, _ = x_prompt.shape
    B_den, L_den, _ = x_sample.shape
    p = {
        'norm_ffn1': norm_ffn1, 'norm_mix': norm_mix, 'norm_ffn2': norm_ffn2,
        'na_rpb': na_rpb, 'swa_sink': swa_sink, 'hy_short_w': hy_short_w, 'hy_short_b': hy_short_b,
        'hy_w1': hy_w1, 'hy_b1': hy_b1, 'hy_w2': hy_w2, 'hy_b2': hy_b2, 'hy_w3': hy_w3,
        'hy_sin_freq': hy_sin_freq, 'hy_log_decay': hy_log_decay, 'hy_skip': hy_skip,
        'diff_lambda': diff_lambda, 'diff_subln': diff_subln,
    }
    big = {'ffn1_w1': ffn1_w1, 'ffn1_w3': ffn1_w3, 'ffn1_w2': ffn1_w2, 'ffn2_w1': ffn2_w1,
           'ffn2_w3': ffn2_w3, 'ffn2_w2': ffn2_w2, 'w_mix': w_in[:, :, :MIX_W], 'w_gate': w_in[:, :, MIX_W:],
           'w_branch': w_branch, 'w_out': w_out}
    wts = [{k: v[l].astype(BF16) for k, v in big.items()} for l in range(DEPTH)]

    cond = jnp.concatenate([c, c_ctx[None, :]], axis=0)
    rows = 8 * ((cond.shape[0] + 7) // 8)
    cond = jnp.pad(cond, ((0, rows - cond.shape[0]), (0, 0)))
    mod = ada_modulation(cond, w_ada, b_ada).reshape(DEPTH, rows, N_MOD, D_MODEL)
    mod_den = mod[:, :B_den]
    mod_ctx = mod[:, B_den:B_den + 1]

    spec_c, tabs_c = _hyena_setup(L_ctx, p, min(L_ctx, 256), min(L_ctx, 256))
    y_ctx, collected = _run_pass(x_prompt.reshape(B_ctx * L_ctx, D_MODEL), mod_ctx, p, wts, final_norm,
                                 B_ctx, L_ctx, spec_c, tabs_c, None)
    new_caches = [jnp.stack([col[j] for col in collected], axis=1) for j in range(6)]

    spec_d, tabs_d = _hyena_setup(L_den, p, 512, 512)
    caches = (cache_na_k, cache_na_v, cache_swa_k, cache_swa_v, cache_diff_k, cache_diff_v)
    y_den, _ = _run_pass(x_sample.reshape(B_den * L_den, D_MODEL), mod_den, p, wts, final_norm,
                         B_den, L_den, spec_d, tabs_d, caches)
    return (y_ctx.reshape(B_ctx, L_ctx, D_MODEL), y_den.reshape(B_den, L_den, D_MODEL), *new_caches)
```

```python
import functools
import math

import jax
import jax.numpy as jnp
from jax import lax
from jax.experimental import pallas as pl
from jax.experimental.pallas import tpu as pltpu

F32 = jnp.float32
BF16 = jnp.bfloat16

D_MODEL = 1024
DEPTH = 4
GRID_W = 64
N_BRANCH = 4
BRANCH_W = D_MODEL // 4
HEAD_DIM = 64
N_HEADS = BRANCH_W // HEAD_DIM
NA_WIN_R = 8
NA_WIN_C = 16
SWA_KV_HEADS = N_HEADS // 2
SWA_WINDOW = 128
HY_WIDTH = BRANCH_W
HY_ORDER = 2
HY_FREQS = 16
HY_EMB = 1 + 2 * HY_FREQS
HY_HIDDEN = 64
DIFF_HD = 32
D_FF = 128 * ((8 * D_MODEL // 3 + 127) // 128)
ROPE_BASE = 10000.0
EPS = 1e-6
NEG = -1e30
N_MOD = 9
NA_W = 3 * BRANCH_W
SWA_QW = BRANCH_W
SWA_KVW = SWA_KV_HEADS * HEAD_DIM
SWA_W = SWA_QW + 2 * SWA_KVW
HY_IN_W = 3 * HY_WIDTH
DIFF_W = 3 * BRANCH_W
GATE_W = N_BRANCH * D_MODEL
MIX_W = NA_W + SWA_W + HY_IN_W + DIFF_W

LOG2E = math.log2(math.e)
QK_SCALE = HEAD_DIM ** -0.5 * LOG2E
DIFF_QK_SCALE = DIFF_HD ** -0.5 * LOG2E

LANES = 128
MXU_DIM = 256
MIB = 1024 * 1024

FFN_ROWS = 512
PROJ_ROWS = 512
FFN_CHUNK = 768
ATTN_Q = 256
NA_Q_ROWS = ATTN_Q // GRID_W
NA_SLAB_ROWS = NA_Q_ROWS + NA_WIN_R
SWA_SLAB = ATTN_Q + 2 * SWA_WINDOW


def _cparams(vmem_mib):
    return pltpu.CompilerParams(vmem_limit_bytes=vmem_mib * MIB)


def _resident(shape):
    nd = len(shape)
    return pl.BlockSpec(shape, lambda *_: (0,) * nd, pipeline_mode=pl.Buffered(1))


def _dot(a, b):
    return jnp.dot(a, b, preferred_element_type=F32)


def _dot_nt(a, b):
    return lax.dot_general(a, b, (((1,), (1,)), ((), ())), preferred_element_type=F32)


def _dot_exact(a, b):
    return jnp.dot(a, b, preferred_element_type=F32, precision=lax.Precision.HIGHEST)


def _sigmoid(x):
    return 1.0 / (1.0 + jnp.exp(-x))


def _norm_mod(x, nw, shift, scale):
    y = x * lax.rsqrt(jnp.mean(x * x, axis=-1, keepdims=True) + EPS) * nw
    return y * (1.0 + scale) + shift


def _mod_spec(nb, tiles_per_b):
    if nb == 1:
        return pl.BlockSpec((1, N_MOD, D_MODEL), lambda i: (0, 0, 0))
    return pl.BlockSpec((1, N_MOD, D_MODEL), lambda i: (i // tiles_per_b, 0, 0))


def _ada_kernel(c_ref, w_ref, b_ref, o_ref):
    c = c_ref[...]
    s = (c * _sigmoid(c)).astype(BF16)
    o_ref[0] = _dot(s, w_ref[0].astype(BF16)) + b_ref[0]


def ada_modulation(cond, w_ada, b_ada):
    rows = cond.shape[0]
    width = N_MOD * D_MODEL
    tn = 9 * LANES
    return pl.pallas_call(
        _ada_kernel,
        grid=(DEPTH, width // tn),
        in_specs=[pl.BlockSpec((rows, D_MODEL), lambda l, j: (0, 0)),
                  pl.BlockSpec((1, D_MODEL, tn), lambda l, j: (l, 0, j)),
                  pl.BlockSpec((1, 1, tn), lambda l, j: (l, 0, j))],
        out_specs=pl.BlockSpec((1, rows, tn), lambda l, j: (l, 0, j)),
        out_shape=jax.ShapeDtypeStruct((DEPTH, rows, width), F32),
        compiler_params=_cparams(32),
        name="ada_modulation",
    )(cond, w_ada, b_ada.reshape(DEPTH, 1, width))


def _ffn_kernel(x_ref, mod_ref, nw_ref, w1_ref, w3_ref, w2_ref, *rest, mod_base, final):
    if final:
        fw_ref, o_ref = rest
    else:
        (o_ref,) = rest
    x = x_ref[...]
    shift = mod_ref[0, mod_base:mod_base + 1, :]
    scale = mod_ref[0, mod_base + 1:mod_base + 2, :]
    gate = mod_ref[0, mod_base + 2:mod_base + 3, :]
    h = _norm_mod(x, nw_ref[...], shift, scale).astype(BF16)
    acc = jnp.zeros(x.shape, F32)
    for lo in range(0, D_FF, FFN_CHUNK):
        hi = min(lo + FFN_CHUNK, D_FF)
        a = _dot(h, w1_ref[:, lo:hi])
        b = _dot(h, w3_ref[:, lo:hi])
        u = (a * _sigmoid(a) * b).astype(BF16)
        acc = acc + _dot(u, w2_ref[lo:hi, :])
    y = x + 0.5 * gate * acc
    if final:
        y = y * lax.rsqrt(jnp.mean(y * y, axis=-1, keepdims=True) + EPS) * fw_ref[...]
    o_ref[...] = y


def ffn_block(x, mod, nw, w1, w3, w2, mod_base, rows_per_b, final_w=None):
    T = x.shape[0]
    tm = FFN_ROWS
    nb = mod.shape[0]
    final = final_w is not None
    in_specs = [pl.BlockSpec((tm, D_MODEL), lambda i: (i, 0)),
                _mod_spec(nb, rows_per_b // tm),
                _resident((1, D_MODEL)),
                _resident((D_MODEL, D_FF)), _resident((D_MODEL, D_FF)), _resident((D_FF, D_MODEL))]
    args = [x, mod, nw.reshape(1, D_MODEL), w1, w3, w2]
    if final:
        in_specs.append(_resident((1, D_MODEL)))
        args.append(final_w.reshape(1, D_MODEL))
    return pl.pallas_call(
        functools.partial(_ffn_kernel, mod_base=mod_base, final=final),
        grid=(T // tm,),
        in_specs=in_specs,
        out_specs=pl.BlockSpec((tm, D_MODEL), lambda i: (i, 0)),
        out_shape=jax.ShapeDtypeStruct((T, D_MODEL), F32),
        compiler_params=_cparams(56),
        name="ffn_block",
    )(*args)


def _rope_chunk(x, cos, sin_a, sin_b, dist):
    return x * cos + pltpu.roll(x, LANES - dist, 1) * sin_a + pltpu.roll(x, dist, 1) * sin_b


def _proj_kernel(x_ref, mod_ref, nw_ref, w_ref, *rest, rope):
    if rope:
        (cs_ref, sa_ref, sb_ref, cd_ref, da_ref, db_ref,
         na_ref, swa_ref, hy_ref, dif_ref) = rest
    else:
        na_ref, swa_ref, hy_ref, dif_ref = rest
    h = _norm_mod(x_ref[...], nw_ref[...], mod_ref[0, 3:4, :], mod_ref[0, 4:5, :]).astype(BF16)
    q_chunks = BRANCH_W // LANES
    o = 0
    s = _dot(h, w_ref[:, o:o + NA_W])
    o += NA_W
    na_ref[:, :BRANCH_W] = (s[:, :BRANCH_W] * QK_SCALE).astype(na_ref.dtype)
    na_ref[:, BRANCH_W:] = s[:, BRANCH_W:].astype(na_ref.dtype)
    s = _dot(h, w_ref[:, o:o + SWA_W])
    o += SWA_W
    n_rot = (SWA_QW + SWA_KVW) // LANES
    for c in range(SWA_W // LANES):
        sl = slice(c * LANES, (c + 1) * LANES)
        chunk = s[:, sl]
        if rope and c < n_rot:
            chunk = _rope_chunk(chunk, cs_ref[:, sl], sa_ref[:, sl], sb_ref[:, sl], HEAD_DIM // 4)
        if c < q_chunks:
            chunk = chunk * QK_SCALE
        swa_ref[:, sl] = chunk.astype(swa_ref.dtype)
    hy_ref[...] = _dot(h, w_ref[:, o:o + HY_IN_W])
    o += HY_IN_W
    s = _dot(h, w_ref[:, o:o + DIFF_W])
    o += DIFF_W
    n_rot = 2 * BRANCH_W // LANES
    for c in range(DIFF_W // LANES):
        sl = slice(c * LANES, (c + 1) * LANES)
        chunk = s[:, sl]
        if rope and c < n_rot:
            chunk = _rope_chunk(chunk, cd_ref[:, sl], da_ref[:, sl], db_ref[:, sl], DIFF_HD // 4)
        if c < q_chunks:
            chunk = chunk * DIFF_QK_SCALE
        dif_ref[:, sl] = chunk.astype(dif_ref.dtype)


def in_projection(x, mod, nw, w_in, rows_per_b, rope_tabs, qkv_dtype):
    T = x.shape[0]
    tm = PROJ_ROWS
    nb = mod.shape[0]
    rope = rope_tabs is not None
    in_specs = [pl.BlockSpec((tm, D_MODEL), lambda i: (i, 0)),
                _mod_spec(nb, rows_per_b // tm),
                _resident((1, D_MODEL)),
                _resident((D_MODEL, MIX_W))]
    args = [x, mod, nw.reshape(1, D_MODEL), w_in]
    if rope:
        pos_tiles = rows_per_b // tm
        for t in rope_tabs:
            in_specs.append(pl.BlockSpec((tm, t.shape[1]), lambda i: (i % pos_tiles, 0)))
            args.append(t)
    widths = (NA_W, SWA_W, HY_IN_W, DIFF_W)
    dtypes = (qkv_dtype, qkv_dtype, F32, qkv_dtype)
    return pl.pallas_call(
        functools.partial(_proj_kernel, rope=rope),
        grid=(T // tm,),
        in_specs=in_specs,
        out_specs=[pl.BlockSpec((tm, w), lambda i: (i, 0)) for w in widths],
        out_shape=[jax.ShapeDtypeStruct((T, w), dt) for w, dt in zip(widths, dtypes)],
        compiler_params=_cparams(56),
        name="in_projection",
    )(*args)


def rope_tables(L, dh, width):
    nf = dh // 4
    t = jnp.arange(L)
    freqs = ROPE_BASE ** (-jnp.arange(nf, dtype=F32) / nf)
    ang_r = (t // GRID_W).astype(F32)[:, None] * freqs
    ang_c = (t % GRID_W).astype(F32)[:, None] * freqs
    cr, sr, cc, sc = jnp.cos(ang_r), jnp.sin(ang_r), jnp.cos(ang_c), jnp.sin(ang_c)
    z = jnp.zeros_like(sr)
    reps = width // dh
    cos = jnp.tile(jnp.concatenate([cr, cr, cc, cc], axis=-1), (1, reps))
    sin_a = jnp.tile(jnp.concatenate([-sr, z, -sc, z], axis=-1), (1, reps))
    sin_b = jnp.tile(jnp.concatenate([z, sr, z, sc], axis=-1), (1, reps))
    return cos, sin_a, sin_b


def _lane_mask(width, lo, n):
    lane = lax.broadcasted_iota(jnp.int32, (1, width), 1)
    return (lane >= lo) & (lane < lo + n)


def _softmax_parts(parts, extra_logit=None):
    m = parts[0].max(axis=-1, keepdims=True)
    for s in parts[1:]:
        m = jnp.maximum(m, s.max(axis=-1, keepdims=True))
    if extra_logit is not None:
        m = jnp.maximum(m, extra_logit)
    ps = [jnp.exp2(s - m) for s in parts]
    l = ps[0].sum(axis=-1, keepdims=True)
    for p in ps[1:]:
        l = l + p.sum(axis=-1, keepdims=True)
    if extra_logit is not None:
        l = l + jnp.exp2(extra_logit - m)
    return ps, l


def _na_kernel(q_ref, k_ref, v_ref, kc_ref, vc_ref, bias_ref, o_ref, *, n_rows):
    i = pl.program_id(1)
    row0 = jnp.clip(NA_Q_ROWS * i - NA_WIN_R // 2, 0, n_rows - NA_SLAB_ROWS)
    start = pl.multiple_of(row0 * GRID_W, GRID_W)
    slab = NA_SLAB_ROWS * GRID_W
    ks = k_ref[0, pl.ds(start, slab), :].astype(BF16)
    vs = v_ref[0, pl.ds(start, slab), :].astype(BF16)
    kc = kc_ref[0].astype(BF16)
    vc = vc_ref[0].astype(BF16)
    q = q_ref[0].astype(BF16)
    out = jnp.zeros(q.shape, F32)
    for h in range(N_HEADS):
        hm = _lane_mask(BRANCH_W, h * HEAD_DIM, HEAD_DIM)
        qh = jnp.where(hm, q, 0.0)
        s_win = _dot_nt(qh, ks) + bias_ref[0, 0, h]
        s_ctx = _dot_nt(qh, kc)
        (p_win, p_ctx), l = _softmax_parts([s_win, s_ctx])
        o = _dot(p_win.astype(BF16), vs) + _dot(p_ctx.astype(BF16), vc)
        out = out + jnp.where(hm, o / l, 0.0)
    o_ref[0] = out


NA_DROWS = 2 * NA_WIN_R - 1
NA_DCOLS = 2 * NA_WIN_C - 1


def _na_tile_geometry(n_rows):
    n_tiles = n_rows // NA_Q_ROWS

    def geometry(tile):
        slab0 = min(max(NA_Q_ROWS * tile - NA_WIN_R // 2, 0), n_rows - NA_SLAB_ROWS)
        rows = []
        for j in range(NA_Q_ROWS):
            qr = NA_Q_ROWS * tile + j
            r0 = min(max(qr - NA_WIN_R // 2, 0), n_rows - NA_WIN_R)
            rows.append([slab0 + m - qr + NA_WIN_R - 1 if r0 <= slab0 + m < r0 + NA_WIN_R else None
                         for m in range(NA_SLAB_ROWS)])
        return rows

    kinds = [geometry(0), geometry(1), geometry(n_tiles - 1)]
    assert all(geometry(t) == kinds[1] for t in range(1, n_tiles - 1))
    return kinds


def _na_bias_kernel(rpb_ref, o_ref, band_ref, *, kinds):
    base = (pl.program_id(0) * N_HEADS + pl.program_id(1)) * (NA_DROWS * NA_DCOLS)
    qc = lax.broadcasted_iota(jnp.int32, (GRID_W, LANES), 0)
    lane = lax.broadcasted_iota(jnp.int32, (GRID_W, LANES), 1)
    kc = lane % GRID_W
    d_col = kc - qc + (NA_WIN_C - 1)
    c0 = jnp.clip(qc - NA_WIN_C // 2, 0, GRID_W - NA_WIN_C)
    col_ok = (kc >= c0) & (kc < c0 + NA_WIN_C)
    for d in range(NA_DROWS):
        t = jnp.full((GRID_W, LANES), NEG, F32)
        for e in range(NA_DCOLS):
            t = jnp.where(d_col == e, rpb_ref[base + d * NA_DCOLS + e] * LOG2E, t)
        band_ref[d] = jnp.where(col_ok, t, NEG)
    masked = jnp.full((GRID_W, LANES), NEG, F32)
    left = lane < GRID_W
    for kind, rows in enumerate(kinds):
        for j, drow in enumerate(rows):
            for m in range(0, NA_SLAB_ROWS, 2):
                a = masked if drow[m] is None else band_ref[drow[m]]
                b = masked if drow[m + 1] is None else band_ref[drow[m + 1]]
                o_ref[0, kind, 0, j * GRID_W:(j + 1) * GRID_W, m * GRID_W:(m + 2) * GRID_W] = jnp.where(left, a, b)


def na_bias_tables(rpb, n_rows):
    slab = NA_SLAB_ROWS * GRID_W
    return pl.pallas_call(
        functools.partial(_na_bias_kernel, kinds=_na_tile_geometry(n_rows)),
        grid=(DEPTH, N_HEADS),
        in_specs=[pl.BlockSpec(memory_space=pltpu.SMEM)],
        out_specs=pl.BlockSpec((1, 3, 1, ATTN_Q, slab), lambda l, h: (l, 0, h, 0, 0)),
        out_shape=jax.ShapeDtypeStruct((DEPTH, 3, N_HEADS, ATTN_Q, slab), F32),
        scratch_shapes=[pltpu.VMEM((NA_DROWS, GRID_W, LANES), F32)],
        compiler_params=_cparams(32),
        name="na_bias_tables",
    )(rpb.astype(F32).reshape(-1))


def neighbourhood_attention(na, kc, vc, bias, layer, B, L):
    n_rows = L // GRID_W
    n_tiles = L // ATTN_Q
    na3 = na.reshape(B, L, NA_W)
    P = kc.shape[1]
    slab = NA_SLAB_ROWS * GRID_W

    def kind(b, i):
        return (layer, jnp.where(i == 0, 0, jnp.where(i == n_tiles - 1, 2, 1)), 0, 0, 0)

    return pl.pallas_call(
        functools.partial(_na_kernel, n_rows=n_rows),
        grid=(B, n_tiles),
        in_specs=[pl.BlockSpec((1, ATTN_Q, BRANCH_W), lambda b, i: (b, i, 0)),
                  pl.BlockSpec((1, L, BRANCH_W), lambda b, i: (b, 0, 1)),
                  pl.BlockSpec((1, L, BRANCH_W), lambda b, i: (b, 0, 2)),
                  pl.BlockSpec((1, P, BRANCH_W), lambda b, i: (b, 0, 0)),
                  pl.BlockSpec((1, P, BRANCH_W), lambda b, i: (b, 0, 0)),
                  pl.BlockSpec((1, 1, N_HEADS, ATTN_Q, slab), kind)],
        out_specs=pl.BlockSpec((1, ATTN_Q, BRANCH_W), lambda b, i: (b, i, 0)),
        out_shape=jax.ShapeDtypeStruct((B, L, BRANCH_W), F32),
        compiler_params=_cparams(48),
        name="neighbourhood_attention",
    )(na3, na3, na3, kc, vc, bias).reshape(B * L, BRANCH_W)


def _attn_kernel(*refs, L, window, gqa, has_ctx, has_sink):
    refs = list(refs)
    q_ref, k_ref, v_ref = refs[:3]
    pos = 3
    if has_ctx:
        kc_ref, vc_ref = refs[pos:pos + 2]
        pos += 2
    if has_sink:
        sink_ref = refs[pos]
        pos += 1
    o_ref = refs[pos]
    i = pl.program_id(1)
    if window:
        slab = SWA_SLAB
        start = pl.multiple_of(jnp.clip(i * ATTN_Q - SWA_WINDOW, 0, L - slab), SWA_WINDOW)
        q_pos = i * ATTN_Q + lax.broadcasted_iota(jnp.int32, (ATTN_Q, 1), 0)
        k_pos = start + lax.broadcasted_iota(jnp.int32, (1, slab), 1)
        ok = jnp.abs(k_pos - q_pos) <= SWA_WINDOW
        ks = k_ref[0, pl.ds(start, slab), :].astype(BF16)
        vs = v_ref[0, pl.ds(start, slab), :].astype(BF16)
    else:
        ks = k_ref[0].astype(BF16)
        vs = v_ref[0].astype(BF16)
    if has_ctx:
        kc = kc_ref[0].astype(BF16)
        vc = vc_ref[0].astype(BF16)
    q = q_ref[0].astype(F32)
    kv_w = ks.shape[1]
    halves = [jnp.zeros((ATTN_Q, LANES), F32), jnp.zeros((ATTN_Q, LANES), F32)]
    out = jnp.zeros(q.shape, F32)
    for h in range(N_HEADS):
        if gqa:
            kvh, slot = h // 2, h % 2
            qh = q[:, kvh * LANES:(kvh + 1) * LANES]
            if slot != kvh:
                qh = pltpu.roll(qh, HEAD_DIM, 1)
            hm = _lane_mask(kv_w, kvh * HEAD_DIM, HEAD_DIM)
        else:
            qh = q
            hm = _lane_mask(kv_w, h * HEAD_DIM, HEAD_DIM)
        qh = jnp.where(hm, qh, 0.0).astype(BF16)
        s = _dot_nt(qh, ks)
        if window:
            s = jnp.where(ok, s, NEG)
        parts = [s]
        if has_ctx:
            parts.append(_dot_nt(qh, kc))
        ps, l = _softmax_parts(parts, sink_ref[h] * LOG2E if has_sink else None)
        o = _dot(ps[0].astype(BF16), vs)
        if has_ctx:
            o = o + _dot(ps[1].astype(BF16), vc)
        o = jnp.where(hm, o / l, 0.0)
        if gqa:
            if slot != kvh:
                o = pltpu.roll(o, HEAD_DIM, 1)
            halves[kvh] = halves[kvh] + o
        else:
            out = out + o
    if gqa:
        o_ref[0, :, 0:LANES] = halves[0]
        o_ref[0, :, LANES:2 * LANES] = halves[1]
    else:
        o_ref[0] = out


def dense_attention(src, cols, B, L, *, window=False, gqa=False, ctx=None, sink=None):
    W = src.shape[1]
    src3 = src.reshape(B, L, W)
    kv_w = SWA_KVW if gqa else BRANCH_W
    qc, kcol, vcol = cols
    in_specs = [pl.BlockSpec((1, ATTN_Q, BRANCH_W), lambda b, i: (b, i, qc)),
                pl.BlockSpec((1, L, kv_w), lambda b, i: (b, 0, kcol)),
                pl.BlockSpec((1, L, kv_w), lambda b, i: (b, 0, vcol))]
    args = [src3, src3, src3]
    if ctx is not None:
        P = ctx[0].shape[1]
        in_specs += [pl.BlockSpec((1, P, kv_w), lambda b, i: (b, 0, 0))] * 2
        args += list(ctx)
    if sink is not None:
        in_specs.append(pl.BlockSpec(memory_space=pltpu.SMEM))
        args.append(sink)
    return pl.pallas_call(
        functools.partial(_attn_kernel, L=L, window=window, gqa=gqa,
                          has_ctx=ctx is not None, has_sink=sink is not None),
        grid=(B, L // ATTN_Q),
        in_specs=in_specs,
        out_specs=pl.BlockSpec((1, ATTN_Q, BRANCH_W), lambda b, i: (b, i, 0)),
        out_shape=jax.ShapeDtypeStruct((B, L, BRANCH_W), F32),
        compiler_params=_cparams(48),
        name="window_attention" if window else "dense_attention",
    )(*args).reshape(B * L, BRANCH_W)


def _diff_kernel(*refs, lam_init, has_ctx):
    refs = list(refs)
    lam_ref, gain_ref, ones_ref, q_ref, k_ref, v_ref = refs[:6]
    if has_ctx:
        kc_ref, vc_ref, o_ref = refs[6:]
    else:
        (o_ref,) = refs[6:]
    lv = lam_ref[...]
    lam = (jnp.exp(jnp.sum(lv[0:1] * lv[1:2], keepdims=True))
           - jnp.exp(jnp.sum(lv[2:3] * lv[3:4], keepdims=True)) + lam_init)
    kl = k_ref[0].astype(BF16)
    vl = v_ref[0].astype(BF16)
    if has_ctx:
        kc = kc_ref[0].astype(BF16)
        vc = vc_ref[0].astype(BF16)
    q = q_ref[0].astype(BF16)
    out = jnp.zeros(q.shape, F32)
    for h in range(N_HEADS):
        o = None
        for mp in range(2):
            mm = _lane_mask(BRANCH_W, h * HEAD_DIM + mp * DIFF_HD, DIFF_HD)
            qm = jnp.where(mm, q, 0.0)
            parts = [_dot_nt(qm, kl)]
            if has_ctx:
                parts.append(_dot_nt(qm, kc))
            ps, l = _softmax_parts(parts)
            pv = _dot(ps[0].astype(BF16), vl)
            if has_ctx:
                pv = pv + _dot(ps[1].astype(BF16), vc)
            o = pv / l if mp == 0 else o - pv * (lam / l)
        out = out + jnp.where(_lane_mask(BRANCH_W, h * HEAD_DIM, HEAD_DIM), o, 0.0)
    ms = _dot_exact(out * out, ones_ref[...]) * (1.0 / HEAD_DIM)
    o_ref[0] = out * lax.rsqrt(ms + EPS) * gain_ref[...] * (1.0 - lam_init)


def diff_attention(dif, lam_vec, subln, layer, B, L, ctx=None):
    lam_init = 0.8 - 0.6 * math.exp(-0.3 * layer)
    dif3 = dif.reshape(B, L, DIFF_W)
    head = jnp.arange(BRANCH_W) // HEAD_DIM
    ones = (head[:, None] == head[None, :]).astype(F32)
    gain = jnp.tile(subln.astype(F32), N_HEADS).reshape(1, BRANCH_W)
    in_specs = [_resident((4, DIFF_HD)), _resident((1, BRANCH_W)), _resident((BRANCH_W, BRANCH_W)),
                pl.BlockSpec((1, ATTN_Q, BRANCH_W), lambda b, i: (b, i, 0)),
                pl.BlockSpec((1, L, BRANCH_W), lambda b, i: (b, 0, 1)),
                pl.BlockSpec((1, L, BRANCH_W), lambda b, i: (b, 0, 2))]
    args = [lam_vec, gain, ones, dif3, dif3, dif3]
    if ctx is not None:
        P = ctx[0].shape[1]
        in_specs += [pl.BlockSpec((1, P, BRANCH_W), lambda b, i: (b, 0, 0))] * 2
        args += list(ctx)
    return pl.pallas_call(
        functools.partial(_diff_kernel, lam_init=lam_init, has_ctx=ctx is not None),
        grid=(B, L // ATTN_Q),
        in_specs=in_specs,
        out_specs=pl.BlockSpec((1, ATTN_Q, BRANCH_W), lambda b, i: (b, i, 0)),
        out_shape=jax.ShapeDtypeStruct((B, L, BRANCH_W), F32),
        compiler_params=_cparams(56),
        name="diff_attention",
    )(*args).reshape(B * L, BRANCH_W)


def _filter_kernel(z_ref, w1_ref, b1_ref, w2_ref, b2_ref, w3_ref, fr_ref, ld_ref, o_ref):
    z = z_ref[...]
    tn = z[:, 0:1]
    fr = fr_ref[0]
    g = jnp.sin(fr * (_dot_exact(z, w1_ref[0]) + b1_ref[0]))
    g = jnp.sin(fr * (_dot_exact(g, w2_ref[0]) + b2_ref[0]))
    hf = _dot_exact(g, w3_ref[0]) * jnp.exp(-jnp.exp(ld_ref[0]) * tn)
    row = lax.broadcasted_iota(jnp.int32, (z.shape[0], 1), 0)
    for o in range(HY_ORDER):
        pos = hf[:, (2 * o) * HY_WIDTH:(2 * o + 1) * HY_WIDTH]
        neg = jnp.where(row == 0, 0.0, hf[:, (2 * o + 1) * HY_WIDTH:(2 * o + 2) * HY_WIDTH])
        norm = (jnp.sum(jnp.abs(pos), axis=0, keepdims=True)
                + jnp.sum(jnp.abs(neg), axis=0, keepdims=True) + EPS)
        o_ref[0, o] = pos / norm
        o_ref[0, HY_ORDER + o] = neg / norm


def hyena_filters(L, p):
    tn = jnp.arange(L, dtype=F32) / L
    ang = 2.0 * math.pi * tn[:, None] * jnp.arange(1, HY_FREQS + 1, dtype=F32)[None, :]
    z = jnp.concatenate([tn[:, None], jnp.cos(ang), jnp.sin(ang)], axis=-1)
    z = jnp.pad(z, ((0, 0), (0, HY_HIDDEN - HY_EMB)))
    w1 = jnp.pad(p['hy_w1'], ((0, 0), (0, HY_HIDDEN - HY_EMB), (0, 0)))
    fw = HY_ORDER * 2 * HY_WIDTH
    per_layer = lambda *shape: pl.BlockSpec((1,) + shape, lambda l: (l,) + (0,) * len(shape))
    return pl.pallas_call(
        _filter_kernel,
        grid=(DEPTH,),
        in_specs=[pl.BlockSpec((L, HY_HIDDEN), lambda l: (0, 0)),
                  per_layer(HY_HIDDEN, HY_HIDDEN), per_layer(1, HY_HIDDEN),
                  per_layer(HY_HIDDEN, HY_HIDDEN), per_layer(1, HY_HIDDEN),
                  per_layer(HY_HIDDEN, fw), per_layer(1, HY_HIDDEN), per_layer(1, fw)],
        out_specs=per_layer(2 * HY_ORDER, L, HY_WIDTH),
        out_shape=jax.ShapeDtypeStruct((DEPTH, 2 * HY_ORDER, L, HY_WIDTH), F32),
        compiler_params=_cparams(56),
        name="hyena_filters",
    )(z, w1, p['hy_b1'].reshape(DEPTH, 1, HY_HIDDEN), p['hy_w2'], p['hy_b2'].reshape(DEPTH, 1, HY_HIDDEN),
      p['hy_w3'], p['hy_sin_freq'].reshape(DEPTH, 1, HY_HIDDEN), p['hy_log_decay'].reshape(DEPTH, 1, fw))


def dft_tables(L, tk):
    k = jnp.arange(L, dtype=jnp.int32)[:, None]
    n = jnp.arange(L, dtype=jnp.int32)[None, :]
    ang = (((2 * k + 1) * n) % (4 * L)).astype(F32) * (math.pi / (2 * L))
    c, s = jnp.cos(ang), jnp.sin(ang)
    fwd = jnp.concatenate([c.reshape(L // tk, tk, L), s.reshape(L // tk, tk, L)], axis=1).astype(BF16)
    inv = jnp.concatenate([c.T, -s.T], axis=1).astype(BF16)
    return fwd, inv


def _spectrum_kernel(t_ref, f_ref, o_ref):
    tk = o_ref.shape[2]
    acc = _dot(t_ref[0], f_ref[0].astype(BF16))
    o_ref[0, 0] = acc[:tk]
    o_ref[0, 1] = acc[tk:]


def filter_spectra(filt, fwd):
    G, L, C = filt.shape
    nkt, tk2, _ = fwd.shape
    tk = tk2 // 2
    return pl.pallas_call(
        _spectrum_kernel,
        grid=(nkt, G),
        in_specs=[pl.BlockSpec((1, tk2, L), lambda i, g: (i, 0, 0)),
                  pl.BlockSpec((1, L, C), lambda i, g: (g, 0, 0))],
        out_specs=pl.BlockSpec((1, 2, tk, C), lambda i, g: (g, 0, i, 0)),
        out_shape=jax.ShapeDtypeStruct((G, 2, L, C), F32),
        compiler_params=_cparams(48),
        name="filter_spectra",
    )(fwd, filt)


def _short_conv(u, w_ref, b_ref):
    L = u.shape[0]
    row = lax.broadcasted_iota(jnp.int32, (L, 1), 0)
    prev = jnp.where(row == 0, 0.0, pltpu.roll(u, 1, 0))
    nxt = jnp.where(row == L - 1, 0.0, pltpu.roll(u, L - 1, 0))
    return prev * w_ref[0:1, :] + u * w_ref[1:2, :] + nxt * w_ref[2:3, :] + b_ref[...]


def _conv_fwd_kernel(t_ref, y_ref, hp_ref, hn_ref, *rest, first):
    if first:
        w_ref, b_ref, o_ref = rest
        y = _short_conv(y_ref[0], w_ref, b_ref)
    else:
        (o_ref,) = rest
        y = y_ref[0]
    y = y.astype(BF16)
    nkt, tk2, _ = t_ref.shape
    tk = tk2 // 2
    for i in range(nkt):
        rows = slice(i * tk, (i + 1) * tk)
        acc = _dot(t_ref[i], y)
        yc, ys = acc[:tk], acc[tk:]
        hr = hp_ref[0, 0, rows, :] + hn_ref[0, 0, rows, :]
        hi = hn_ref[0, 1, rows, :] - hp_ref[0, 1, rows, :]
        o_ref[0, 0, rows, :] = (yc * hr + ys * hi).astype(BF16)
        o_ref[0, 1, rows, :] = (yc * hi - ys * hr).astype(BF16)


def _col_spec(L, col):
    return pl.BlockSpec((1, L, HY_WIDTH), lambda b: (b, 0, col))


def _short_specs(col):
    return [pl.BlockSpec((3, HY_WIDTH), lambda b: (0, col)), pl.BlockSpec((1, HY_WIDTH), lambda b: (0, col))]


def conv_forward(y, spec, g_pos, g_neg, fwd, short=None):
    B, L, _ = y.shape
    C = HY_WIDTH
    first = short is not None
    in_specs = [_resident(fwd.shape), _col_spec(L, 0),
                pl.BlockSpec((1, 2, L, C), lambda b: (g_pos, 0, 0, 0), pipeline_mode=pl.Buffered(1)),
                pl.BlockSpec((1, 2, L, C), lambda b: (g_neg, 0, 0, 0), pipeline_mode=pl.Buffered(1))]
    args = [fwd, y, spec, spec]
    if first:
        in_specs += _short_specs(0)
        args += list(short)
    return pl.pallas_call(
        functools.partial(_conv_fwd_kernel, first=first),
        grid=(B,),
        in_specs=in_specs,
        out_specs=pl.BlockSpec((1, 2, L, C), lambda b: (b, 0, 0, 0)),
        out_shape=jax.ShapeDtypeStruct((B, 2, L, C), BF16),
        compiler_params=_cparams(56),
        name="conv_forward",
    )(*args)


def _conv_inv_kernel(t_ref, z_ref, g_ref, wg_ref, bg_ref, skip_ref, y_ref, *rest, first, tt):
    if first:
        wy_ref, by_ref, o_ref = rest
        y = _short_conv(y_ref[0], wy_ref, by_ref)
    else:
        (o_ref,) = rest
        y = y_ref[0]
    gate = _short_conv(g_ref[0], wg_ref, bg_ref)
    L, L2 = t_ref.shape
    z = z_ref[0].reshape(L2, HY_WIDTH)
    for i in range(L // tt):
        rows = slice(i * tt, (i + 1) * tt)
        conv = _dot(t_ref[rows, :], z) * (2.0 / L2)
        o_ref[0, rows, :] = gate[rows] * (conv + skip_ref[...] * y[rows])


def conv_inverse(zf, hy, order, short_w, short_b, skip, inv, tt, y=None):
    B, _, L, C = zf.shape
    first = y is None
    in_specs = [_resident(inv.shape),
                pl.BlockSpec((1, 2, L, C), lambda b: (b, 0, 0, 0)),
                _col_spec(L, 1 + order), *_short_specs(1 + order),
                pl.BlockSpec((1, C), lambda b: (0, 0)),
                _col_spec(L, 0)]
    args = [inv, zf, hy, short_w, short_b, skip.reshape(1, C), hy if first else y]
    if first:
        in_specs += _short_specs(0)
        args += [short_w, short_b]
    return pl.pallas_call(
        functools.partial(_conv_inv_kernel, first=first, tt=tt),
        grid=(B,),
        in_specs=in_specs,
        out_specs=pl.BlockSpec((1, L, C), lambda b: (b, 0, 0)),
        out_shape=jax.ShapeDtypeStruct((B, L, C), F32),
        compiler_params=_cparams(56),
        name="conv_inverse",
    )(*args)


def hyena(hy, lp, layer, spec, tabs, B, L):
    fwd, inv, tt = tabs
    hy = hy.reshape(B, L, HY_IN_W)
    short_w = lp['hy_short_w']
    short_b = lp['hy_short_b'].reshape(1, HY_IN_W)
    g = layer * 2 * HY_ORDER
    y = None
    for o in range(HY_ORDER):
        if y is None:
            zf = conv_forward(hy, spec, g + o, g + HY_ORDER + o, fwd, short=(short_w, short_b))
        else:
            zf = conv_forward(y, spec, g + o, g + HY_ORDER + o, fwd)
        y = conv_inverse(zf, hy, o, short_w, short_b, lp['hy_skip'][o], inv, tt, y=y)
    return y.reshape(B * L, HY_WIDTH)


def _merge_kernel(a_ref, b_ref, c_ref, d_ref, x_ref, mod_ref, nw_ref, wg_ref, wb_ref, wo_ref, o_ref):
    x = x_ref[...]
    h = _norm_mod(x, nw_ref[...], mod_ref[0, 3:4, :], mod_ref[0, 4:5, :]).astype(BF16)
    merged = None
    for i, br in enumerate((a_ref, b_ref, c_ref, d_ref)):
        gate = _sigmoid(_dot(h, wg_ref[:, i * D_MODEL:(i + 1) * D_MODEL]))
        t = gate * _dot(br[...].astype(BF16), wb_ref[i])
        merged = t if merged is None else merged + t
    o_ref[...] = x + mod_ref[0, 5:6, :] * _dot(merged.astype(BF16), wo_ref[...])


def merge_block(branches, x, mod, nw, wg, wb, wo, rows_per_b):
    T = x.shape[0]
    tm = FFN_ROWS
    nb = mod.shape[0]
    row = lambda w: pl.BlockSpec((tm, w), lambda i: (i, 0))
    return pl.pallas_call(
        _merge_kernel,
        grid=(T // tm,),
        in_specs=[row(BRANCH_W)] * N_BRANCH + [row(D_MODEL), _mod_spec(nb, rows_per_b // tm),
                                               _resident((1, D_MODEL)),
                                               _resident((D_MODEL, GATE_W)),
                                               _resident((N_BRANCH, BRANCH_W, D_MODEL)),
                                               _resident((D_MODEL, D_MODEL))],
        out_specs=row(D_MODEL),
        out_shape=jax.ShapeDtypeStruct((T, D_MODEL), F32),
        compiler_params=_cparams(56),
        name="merge_block",
    )(*branches, x, mod, nw.reshape(1, D_MODEL), wg, wb, wo)


def _heads_out(t, B, L, n):
    return t.reshape(B, L, n, HEAD_DIM).transpose(0, 2, 1, 3)


def _heads_in(t):
    B, H, P, d = t.shape
    return t.transpose(0, 2, 1, 3).reshape(B, P, H * d)


def _run_pass(x, mod_all, p, wts, final_norm, B, L, spec, tabs, caches):
    ctx_pass = caches is None
    collected = []
    if not ctx_pass:
        rope = rope_tables(L, HEAD_DIM, SWA_QW + SWA_KVW) + rope_tables(L, DIFF_HD, 2 * BRANCH_W)
        bias_all = na_bias_tables(p['na_rpb'], L // GRID_W)
    for l in range(DEPTH):
        lp = {k: v[l] for k, v in p.items()}
        w = wts[l]
        mod = mod_all[l]
        x = ffn_block(x, mod, lp['norm_ffn1'], w['ffn1_w1'], w['ffn1_w3'], w['ffn1_w2'], 0, L)
        na, swa, hy, dif = in_projection(x, mod, lp['norm_mix'], w['w_mix'], L,
                                         None if ctx_pass else rope, F32 if ctx_pass else BF16)
        if ctx_pass:
            a_o = dense_attention(na, (0, 1, 2), B, L)
            b_o = dense_attention(swa, (0, 2, 3), B, L, gqa=True, sink=lp['swa_sink'])
            d_o = diff_attention(dif, lp['diff_lambda'], lp['diff_subln'], l, B, L)
            collected.append((
                _heads_out(na[:, BRANCH_W:2 * BRANCH_W], B, L, N_HEADS),
                _heads_out(na[:, 2 * BRANCH_W:], B, L, N_HEADS),
                _heads_out(swa[:, SWA_QW:SWA_QW + SWA_KVW], B, L, SWA_KV_HEADS),
                _heads_out(swa[:, SWA_QW + SWA_KVW:], B, L, SWA_KV_HEADS),
                _heads_out(dif[:, BRANCH_W:2 * BRANCH_W], B, L, N_HEADS),
                _heads_out(dif[:, 2 * BRANCH_W:], B, L, N_HEADS)))
        else:
            ck_na, cv_na, ck_swa, cv_swa, ck_d, cv_d = (_heads_in(t[:, l]).astype(BF16) for t in caches)
            a_o = neighbourhood_attention(na, ck_na, cv_na, bias_all, l, B, L)
            b_o = dense_attention(swa, (0, 2, 3), B, L, window=True, gqa=True,
                                  ctx=(ck_swa, cv_swa), sink=lp['swa_sink'])
            d_o = diff_attention(dif, lp['diff_lambda'], lp['diff_subln'], l, B, L, ctx=(ck_d, cv_d))
        c_o = hyena(hy, lp, l, spec, tabs, B, L)
        x = merge_block((a_o, b_o, c_o, d_o), x, mod, lp['norm_mix'], w['w_gate'], w['w_branch'], w['w_out'], L)
        x = ffn_block(x, mod, lp['norm_ffn2'], w['ffn2_w1'], w['ffn2_w3'], w['ffn2_w2'], 6, L,
                      final_w=final_norm if l == DEPTH - 1 else None)
    return x, collected


def _hyena_setup(L, p, tk, tt):
    fwd, inv = dft_tables(L, tk)
    filt = hyena_filters(L, p)
    spec = filter_spectra(filt.reshape(DEPTH * 2 * HY_ORDER, L, HY_WIDTH), fwd)
    return spec, (fwd, inv, tt)


def kernel(x_prompt, x_sample, cache_na_k, cache_na_v, cache_swa_k, cache_swa_v, cache_diff_k, cache_diff_v, c, c_ctx, w_ada, b_ada, norm_ffn1, norm_mix, norm_ffn2, final_norm, ffn1_w1, ffn1_w3, ffn1_w2, ffn2_w1, ffn2_w3, ffn2_w2, w_in, w_branch, w_out, na_rpb, swa_sink, hy_short_w, hy_short_b, hy_w1, hy_b1, hy_w2, hy_b2, hy_w3, hy_sin_freq, hy_log_decay, hy_skip, diff_lambda, diff_subln):
    B_ctx, L_ctx, _ = x_prompt.shape
    B_den, L_den, _ = x_sample.shape
    p = {
        'norm_ffn1': norm_ffn1, 'norm_mix': norm_mix, 'norm_ffn2': norm_ffn2,
        'na_rpb': na_rpb, 'swa_sink': swa_sink, 'hy_short_w': hy_short_w, 'hy_short_b': hy_short_b,
        'hy_w1': hy_w1, 'hy_b1': hy_b1, 'hy_w2': hy_w2, 'hy_b2': hy_b2, 'hy_w3': hy_w3,
        'hy_sin_freq': hy_sin_freq, 'hy_log_decay': hy_log_decay, 'hy_skip': hy_skip,
        'diff_lambda': diff_lambda, 'diff_subln': diff_subln,
    }
    big = {'ffn1_w1': ffn1_w1, 'ffn1_w3': ffn1_w3, 'ffn1_w2': ffn1_w2, 'ffn2_w1': ffn2_w1,
           'ffn2_w3': ffn2_w3, 'ffn2_w2': ffn2_w2, 'w_mix': w_in[:, :, :MIX_W], 'w_gate': w_in[:, :, MIX_W:],
           'w_branch': w_branch, 'w_out': w_out}
    wts = [{k: v[l].astype(BF16) for k, v in big.items()} for l in range(DEPTH)]

    cond = jnp.concatenate([c, c_ctx[None, :]], axis=0)
    rows = 8 * ((cond.shape[0] + 7) // 8)
    cond = jnp.pad(cond, ((0, rows - cond.shape[0]), (0, 0)))
    mod = ada_modulation(cond, w_ada, b_ada).reshape(DEPTH, rows, N_MOD, D_MODEL)
    mod_den = mod[:, :B_den]
    mod_ctx = mod[:, B_den:B_den + 1]

    spec_c, tabs_c = _hyena_setup(L_ctx, p, min(L_ctx, 256), min(L_ctx, 256))
    y_ctx, collected = _run_pass(x_prompt.reshape(B_ctx * L_ctx, D_MODEL), mod_ctx, p, wts, final_norm,
                                 B_ctx, L_ctx, spec_c, tabs_c, None)
    new_caches = [jnp.stack([col[j] for col in collected], axis=1) for j in range(6)]

    spec_d, tabs_d = _hyena_setup(L_den, p, 512, 512)
    caches = (cache_na_k, cache_na_v, cache_swa_k, cache_swa_v, cache_diff_k, cache_diff_v)
    y_den, _ = _run_pass(x_sample.reshape(B_den * L_den, D_MODEL), mod_den, p, wts, final_norm,
                         B_den, L_den, spec_d, tabs_d, caches)
    return (y_ctx.reshape(B_ctx, L_ctx, D_MODEL), y_den.reshape(B_den, L_den, D_MODEL), *new_caches)
```

```python
import functools
import math

import jax
import jax.numpy as jnp
from jax import lax
from jax.experimental import pallas as pl
from jax.experimental.pallas import tpu as pltpu

F32 = jnp.float32
BF16 = jnp.bfloat16

D_MODEL = 1024
DEPTH = 4
GRID_W = 64
N_BRANCH = 4
BRANCH_W = D_MODEL // 4
HEAD_DIM = 64
N_HEADS = BRANCH_W // HEAD_DIM
NA_WIN_R = 8
NA_WIN_C = 16
SWA_KV_HEADS = N_HEADS // 2
SWA_WINDOW = 128
HY_WIDTH = BRANCH_W
HY_ORDER = 2
HY_FREQS = 16
HY_EMB = 1 + 2 * HY_FREQS
HY_HIDDEN = 64
DIFF_HD = 32
D_FF = 128 * ((8 * D_MODEL // 3 + 127) // 128)
ROPE_BASE = 10000.0
EPS = 1e-6
NEG = -1e30
N_MOD = 9
NA_W = 3 * BRANCH_W
SWA_QW = BRANCH_W
SWA_KVW = SWA_KV_HEADS * HEAD_DIM
SWA_W = SWA_QW + 2 * SWA_KVW
HY_IN_W = 3 * HY_WIDTH
DIFF_W = 3 * BRANCH_W
GATE_W = N_BRANCH * D_MODEL
MIX_W = NA_W + SWA_W + HY_IN_W + DIFF_W

LOG2E = math.log2(math.e)
QK_SCALE = HEAD_DIM ** -0.5 * LOG2E
DIFF_QK_SCALE = DIFF_HD ** -0.5 * LOG2E

LANES = 128
MXU_DIM = 256
MIB = 1024 * 1024

FFN_ROWS = 512
PROJ_ROWS = 512
FFN_CHUNK = 768
ATTN_Q = 256
NA_Q_ROWS = ATTN_Q // GRID_W
NA_SLAB_ROWS = NA_Q_ROWS + NA_WIN_R
SWA_SLAB = ATTN_Q + 2 * SWA_WINDOW


def _cparams(vmem_mib):
    return pltpu.CompilerParams(vmem_limit_bytes=vmem_mib * MIB)


def _resident(shape):
    nd = len(shape)
    return pl.BlockSpec(shape, lambda *_: (0,) * nd, pipeline_mode=pl.Buffered(1))


def _resident_layer(stacked, layer):
    shape = stacked.shape[1:]
    return pl.BlockSpec((None,) + shape, lambda *_: (layer,) + (0,) * len(shape), pipeline_mode=pl.Buffered(1))


def _dot(a, b):
    return jnp.dot(a, b, preferred_element_type=F32)


def _dot_nt(a, b):
    return lax.dot_general(a, b, (((1,), (1,)), ((), ())), preferred_element_type=F32)


def _dot_exact(a, b):
    return jnp.dot(a, b, preferred_element_type=F32, precision=lax.Precision.HIGHEST)


def _sigmoid(x):
    return 1.0 / (1.0 + jnp.exp(-x))


def _norm_mod(x, nw, shift, scale):
    y = x * lax.rsqrt(jnp.mean(x * x, axis=-1, keepdims=True) + EPS) * nw
    return y * (1.0 + scale) + shift


def _mod_spec(nb, tiles_per_b):
    if nb == 1:
        return pl.BlockSpec((1, N_MOD, D_MODEL), lambda i: (0, 0, 0))
    return pl.BlockSpec((1, N_MOD, D_MODEL), lambda i: (i // tiles_per_b, 0, 0))


def _ada_kernel(c_ref, w_ref, b_ref, o_ref):
    c = c_ref[...]
    s = (c * _sigmoid(c)).astype(BF16)
    o_ref[0] = _dot(s, w_ref[0].astype(BF16)) + b_ref[0]


def ada_modulation(cond, w_ada, b_ada):
    rows = cond.shape[0]
    width = N_MOD * D_MODEL
    tn = 9 * LANES
    return pl.pallas_call(
        _ada_kernel,
        grid=(DEPTH, width // tn),
        in_specs=[pl.BlockSpec((rows, D_MODEL), lambda l, j: (0, 0)),
                  pl.BlockSpec((1, D_MODEL, tn), lambda l, j: (l, 0, j)),
                  pl.BlockSpec((1, 1, tn), lambda l, j: (l, 0, j))],
        out_specs=pl.BlockSpec((1, rows, tn), lambda l, j: (l, 0, j)),
        out_shape=jax.ShapeDtypeStruct((DEPTH, rows, width), F32),
        compiler_params=_cparams(32),
        name="ada_modulation",
    )(cond, w_ada, b_ada.reshape(DEPTH, 1, width))


def _ffn_kernel(x_ref, mod_ref, nw_ref, w1_ref, w3_ref, w2_ref, *rest, mod_base, final):
    if final:
        fw_ref, o_ref = rest
    else:
        (o_ref,) = rest
    x = x_ref[...]
    shift = mod_ref[0, mod_base:mod_base + 1, :]
    scale = mod_ref[0, mod_base + 1:mod_base + 2, :]
    gate = mod_ref[0, mod_base + 2:mod_base + 3, :]
    h = _norm_mod(x, nw_ref[...], shift, scale).astype(BF16)
    acc = jnp.zeros(x.shape, F32)
    for lo in range(0, D_FF, FFN_CHUNK):
        hi = min(lo + FFN_CHUNK, D_FF)
        a = _dot(h, w1_ref[:, lo:hi])
        b = _dot(h, w3_ref[:, lo:hi])
        u = (a * _sigmoid(a) * b).astype(BF16)
        acc = acc + _dot(u, w2_ref[lo:hi, :])
    y = x + 0.5 * gate * acc
    if final:
        y = y * lax.rsqrt(jnp.mean(y * y, axis=-1, keepdims=True) + EPS) * fw_ref[...]
    o_ref[...] = y


def ffn_block(x, mod, nw, w1, w3, w2, layer, mod_base, rows_per_b, final_w=None):
    T = x.shape[0]
    tm = FFN_ROWS
    nb = mod.shape[0]
    final = final_w is not None
    in_specs = [pl.BlockSpec((tm, D_MODEL), lambda i: (i, 0)),
                _mod_spec(nb, rows_per_b // tm),
                _resident((1, D_MODEL)),
                _resident_layer(w1, layer), _resident_layer(w3, layer), _resident_layer(w2, layer)]
    args = [x, mod, nw.reshape(1, D_MODEL), w1, w3, w2]
    if final:
        in_specs.append(_resident((1, D_MODEL)))
        args.append(final_w.reshape(1, D_MODEL))
    return pl.pallas_call(
        functools.partial(_ffn_kernel, mod_base=mod_base, final=final),
        grid=(T // tm,),
        in_specs=in_specs,
        out_specs=pl.BlockSpec((tm, D_MODEL), lambda i: (i, 0)),
        out_shape=jax.ShapeDtypeStruct((T, D_MODEL), F32),
        compiler_params=_cparams(56),
        name="ffn_block",
    )(*args)


def _rope_chunk(x, cos, sin_a, sin_b, dist):
    return x * cos + pltpu.roll(x, LANES - dist, 1) * sin_a + pltpu.roll(x, dist, 1) * sin_b


def _proj_kernel(x_ref, mod_ref, nw_ref, w_ref, *rest, rope):
    if rope:
        (cs_ref, sa_ref, sb_ref, cd_ref, da_ref, db_ref,
         na_ref, swa_ref, hy_ref, dif_ref) = rest
    else:
        na_ref, swa_ref, hy_ref, dif_ref = rest
    h = _norm_mod(x_ref[...], nw_ref[...], mod_ref[0, 3:4, :], mod_ref[0, 4:5, :]).astype(BF16)
    q_chunks = BRANCH_W // LANES
    o = 0
    s = _dot(h, w_ref[:, o:o + NA_W])
    o += NA_W
    na_ref[:, :BRANCH_W] = (s[:, :BRANCH_W] * QK_SCALE).astype(na_ref.dtype)
    na_ref[:, BRANCH_W:] = s[:, BRANCH_W:].astype(na_ref.dtype)
    s = _dot(h, w_ref[:, o:o + SWA_W])
    o += SWA_W
    n_rot = (SWA_QW + SWA_KVW) // LANES
    for c in range(SWA_W // LANES):
        sl = slice(c * LANES, (c + 1) * LANES)
        chunk = s[:, sl]
        if rope and c < n_rot:
            chunk = _rope_chunk(chunk, cs_ref[:, sl], sa_ref[:, sl], sb_ref[:, sl], HEAD_DIM // 4)
        if c < q_chunks:
            chunk = chunk * QK_SCALE
        swa_ref[:, sl] = chunk.astype(swa_ref.dtype)
    hy_ref[...] = _dot(h, w_ref[:, o:o + HY_IN_W])
    o += HY_IN_W
    s = _dot(h, w_ref[:, o:o + DIFF_W])
    o += DIFF_W
    n_rot = 2 * BRANCH_W // LANES
    for c in range(DIFF_W // LANES):
        sl = slice(c * LANES, (c + 1) * LANES)
        chunk = s[:, sl]
        if rope and c < n_rot:
            chunk = _rope_chunk(chunk, cd_ref[:, sl], da_ref[:, sl], db_ref[:, sl], DIFF_HD // 4)
        if c < q_chunks:
            chunk = chunk * DIFF_QK_SCALE
        dif_ref[:, sl] = chunk.astype(dif_ref.dtype)


def in_projection(x, mod, nw, w_mix, layer, rows_per_b, rope_tabs, qkv_dtype):
    T = x.shape[0]
    tm = PROJ_ROWS
    nb = mod.shape[0]
    rope = rope_tabs is not None
    in_specs = [pl.BlockSpec((tm, D_MODEL), lambda i: (i, 0)),
                _mod_spec(nb, rows_per_b // tm),
                _resident((1, D_MODEL)),
                _resident_layer(w_mix, layer)]
    args = [x, mod, nw.reshape(1, D_MODEL), w_mix]
    if rope:
        pos_tiles = rows_per_b // tm
        for t in rope_tabs:
            in_specs.append(pl.BlockSpec((tm, t.shape[1]), lambda i: (i % pos_tiles, 0)))
            args.append(t)
    widths = (NA_W, SWA_W, HY_IN_W, DIFF_W)
    dtypes = (qkv_dtype, qkv_dtype, F32, qkv_dtype)
    return pl.pallas_call(
        functools.partial(_proj_kernel, rope=rope),
        grid=(T // tm,),
        in_specs=in_specs,
        out_specs=[pl.BlockSpec((tm, w), lambda i: (i, 0)) for w in widths],
        out_shape=[jax.ShapeDtypeStruct((T, w), dt) for w, dt in zip(widths, dtypes)],
        compiler_params=_cparams(56),
        name="in_projection",
    )(*args)


def rope_tables(L, dh, width):
    nf = dh // 4
    t = jnp.arange(L)
    freqs = ROPE_BASE ** (-jnp.arange(nf, dtype=F32) / nf)
    ang_r = (t // GRID_W).astype(F32)[:, None] * freqs
    ang_c = (t % GRID_W).astype(F32)[:, None] * freqs
    cr, sr, cc, sc = jnp.cos(ang_r), jnp.sin(ang_r), jnp.cos(ang_c), jnp.sin(ang_c)
    z = jnp.zeros_like(sr)
    reps = width // dh
    cos = jnp.tile(jnp.concatenate([cr, cr, cc, cc], axis=-1), (1, reps))
    sin_a = jnp.tile(jnp.concatenate([-sr, z, -sc, z], axis=-1), (1, reps))
    sin_b = jnp.tile(jnp.concatenate([z, sr, z, sc], axis=-1), (1, reps))
    return cos, sin_a, sin_b


def _lane_mask(width, lo, n):
    lane = lax.broadcasted_iota(jnp.int32, (1, width), 1)
    return (lane >= lo) & (lane < lo + n)


def _softmax_parts(parts, extra_logit=None):
    m = parts[0].max(axis=-1, keepdims=True)
    for s in parts[1:]:
        m = jnp.maximum(m, s.max(axis=-1, keepdims=True))
    if extra_logit is not None:
        m = jnp.maximum(m, extra_logit)
    ps = [jnp.exp2(s - m) for s in parts]
    l = ps[0].sum(axis=-1, keepdims=True)
    for p in ps[1:]:
        l = l + p.sum(axis=-1, keepdims=True)
    if extra_logit is not None:
        l = l + jnp.exp2(extra_logit - m)
    return ps, l


def _na_kernel(q_ref, k_ref, v_ref, kc_ref, vc_ref, bias_ref, o_ref, *, n_rows):
    i = pl.program_id(1)
    row0 = jnp.clip(NA_Q_ROWS * i - NA_WIN_R // 2, 0, n_rows - NA_SLAB_ROWS)
    start = pl.multiple_of(row0 * GRID_W, GRID_W)
    slab = NA_SLAB_ROWS * GRID_W
    ks = k_ref[0, pl.ds(start, slab), :].astype(BF16)
    vs = v_ref[0, pl.ds(start, slab), :].astype(BF16)
    kc = kc_ref[0].astype(BF16)
    vc = vc_ref[0].astype(BF16)
    q = q_ref[0].astype(BF16)
    out = jnp.zeros(q.shape, F32)
    for h in range(N_HEADS):
        hm = _lane_mask(BRANCH_W, h * HEAD_DIM, HEAD_DIM)
        qh = jnp.where(hm, q, 0.0)
        s_win = _dot_nt(qh, ks) + bias_ref[0, 0, h]
        s_ctx = _dot_nt(qh, kc)
        (p_win, p_ctx), l = _softmax_parts([s_win, s_ctx])
        o = _dot(p_win.astype(BF16), vs) + _dot(p_ctx.astype(BF16), vc)
        out = out + jnp.where(hm, o / l, 0.0)
    o_ref[0] = out


NA_DROWS = 2 * NA_WIN_R - 1
NA_DCOLS = 2 * NA_WIN_C - 1


def _na_tile_geometry(n_rows):
    n_tiles = n_rows // NA_Q_ROWS

    def geometry(tile):
        slab0 = min(max(NA_Q_ROWS * tile - NA_WIN_R // 2, 0), n_rows - NA_SLAB_ROWS)
        rows = []
        for j in range(NA_Q_ROWS):
            qr = NA_Q_ROWS * tile + j
            r0 = min(max(qr - NA_WIN_R // 2, 0), n_rows - NA_WIN_R)
            rows.append([slab0 + m - qr + NA_WIN_R - 1 if r0 <= slab0 + m < r0 + NA_WIN_R else None
                         for m in range(NA_SLAB_ROWS)])
        return rows

    kinds = [geometry(0), geometry(1), geometry(n_tiles - 1)]
    assert all(geometry(t) == kinds[1] for t in range(1, n_tiles - 1))
    return kinds


def _na_bias_kernel(rpb_ref, o_ref, band_ref, *, kinds):
    base = (pl.program_id(0) * N_HEADS + pl.program_id(1)) * (NA_DROWS * NA_DCOLS)
    qc = lax.broadcasted_iota(jnp.int32, (GRID_W, LANES), 0)
    lane = lax.broadcasted_iota(jnp.int32, (GRID_W, LANES), 1)
    kc = lane % GRID_W
    d_col = kc - qc + (NA_WIN_C - 1)
    c0 = jnp.clip(qc - NA_WIN_C // 2, 0, GRID_W - NA_WIN_C)
    col_ok = (kc >= c0) & (kc < c0 + NA_WIN_C)
    for d in range(NA_DROWS):
        t = jnp.full((GRID_W, LANES), NEG, F32)
        for e in range(NA_DCOLS):
            t = jnp.where(d_col == e, rpb_ref[base + d * NA_DCOLS + e] * LOG2E, t)
        band_ref[d] = jnp.where(col_ok, t, NEG)
    masked = jnp.full((GRID_W, LANES), NEG, F32)
    left = lane < GRID_W
    for kind, rows in enumerate(kinds):
        for j, drow in enumerate(rows):
            for m in range(0, NA_SLAB_ROWS, 2):
                a = masked if drow[m] is None else band_ref[drow[m]]
                b = masked if drow[m + 1] is None else band_ref[drow[m + 1]]
                o_ref[0, kind, 0, j * GRID_W:(j + 1) * GRID_W, m * GRID_W:(m + 2) * GRID_W] = jnp.where(left, a, b)


def na_bias_tables(rpb, n_rows):
    slab = NA_SLAB_ROWS * GRID_W
    return pl.pallas_call(
        functools.partial(_na_bias_kernel, kinds=_na_tile_geometry(n_rows)),
        grid=(DEPTH, N_HEADS),
        in_specs=[pl.BlockSpec(memory_space=pltpu.SMEM)],
        out_specs=pl.BlockSpec((1, 3, 1, ATTN_Q, slab), lambda l, h: (l, 0, h, 0, 0)),
        out_shape=jax.ShapeDtypeStruct((DEPTH, 3, N_HEADS, ATTN_Q, slab), F32),
        scratch_shapes=[pltpu.VMEM((NA_DROWS, GRID_W, LANES), F32)],
        compiler_params=_cparams(32),
        name="na_bias_tables",
    )(rpb.astype(F32).reshape(-1))


def neighbourhood_attention(na, kc, vc, bias, layer, B, L):
    n_rows = L // GRID_W
    n_tiles = L // ATTN_Q
    na3 = na.reshape(B, L, NA_W)
    P = kc.shape[1]
    slab = NA_SLAB_ROWS * GRID_W

    def kind(b, i):
        return (layer, jnp.where(i == 0, 0, jnp.where(i == n_tiles - 1, 2, 1)), 0, 0, 0)

    return pl.pallas_call(
        functools.partial(_na_kernel, n_rows=n_rows),
        grid=(B, n_tiles),
        in_specs=[pl.BlockSpec((1, ATTN_Q, BRANCH_W), lambda b, i: (b, i, 0)),
                  pl.BlockSpec((1, L, BRANCH_W), lambda b, i: (b, 0, 1)),
                  pl.BlockSpec((1, L, BRANCH_W), lambda b, i: (b, 0, 2)),
                  pl.BlockSpec((1, P, BRANCH_W), lambda b, i: (b, 0, 0)),
                  pl.BlockSpec((1, P, BRANCH_W), lambda b, i: (b, 0, 0)),
                  pl.BlockSpec((1, 1, N_HEADS, ATTN_Q, slab), kind)],
        out_specs=pl.BlockSpec((1, ATTN_Q, BRANCH_W), lambda b, i: (b, i, 0)),
        out_shape=jax.ShapeDtypeStruct((B, L, BRANCH_W), F32),
        compiler_params=_cparams(48),
        name="neighbourhood_attention",
    )(na3, na3, na3, kc, vc, bias).reshape(B * L, BRANCH_W)


def _attn_kernel(*refs, L, window, gqa, has_ctx, has_sink):
    refs = list(refs)
    q_ref, k_ref, v_ref = refs[:3]
    pos = 3
    if has_ctx:
        kc_ref, vc_ref = refs[pos:pos + 2]
        pos += 2
    if has_sink:
        sink_ref = refs[pos]
        pos += 1
    o_ref = refs[pos]
    i = pl.program_id(1)
    if window:
        slab = SWA_SLAB
        start = pl.multiple_of(jnp.clip(i * ATTN_Q - SWA_WINDOW, 0, L - slab), SWA_WINDOW)
        q_pos = i * ATTN_Q + lax.broadcasted_iota(jnp.int32, (ATTN_Q, 1), 0)
        k_pos = start + lax.broadcasted_iota(jnp.int32, (1, slab), 1)
        ok = jnp.abs(k_pos - q_pos) <= SWA_WINDOW
        ks = k_ref[0, pl.ds(start, slab), :].astype(BF16)
        vs = v_ref[0, pl.ds(start, slab), :].astype(BF16)
    else:
        ks = k_ref[0].astype(BF16)
        vs = v_ref[0].astype(BF16)
    if has_ctx:
        kc = kc_ref[0].astype(BF16)
        vc = vc_ref[0].astype(BF16)
    q = q_ref[0].astype(F32)
    kv_w = ks.shape[1]
    halves = [jnp.zeros((ATTN_Q, LANES), F32), jnp.zeros((ATTN_Q, LANES), F32)]
    out = jnp.zeros(q.shape, F32)
    for h in range(N_HEADS):
        if gqa:
            kvh, slot = h // 2, h % 2
            qh = q[:, kvh * LANES:(kvh + 1) * LANES]
            if slot != kvh:
                qh = pltpu.roll(qh, HEAD_DIM, 1)
            hm = _lane_mask(kv_w, kvh * HEAD_DIM, HEAD_DIM)
        else:
            qh = q
            hm = _lane_mask(kv_w, h * HEAD_DIM, HEAD_DIM)
        qh = jnp.where(hm, qh, 0.0).astype(BF16)
        s = _dot_nt(qh, ks)
        if window:
            s = jnp.where(ok, s, NEG)
        parts = [s]
        if has_ctx:
            parts.append(_dot_nt(qh, kc))
        ps, l = _softmax_parts(parts, sink_ref[h] * LOG2E if has_sink else None)
        o = _dot(ps[0].astype(BF16), vs)
        if has_ctx:
            o = o + _dot(ps[1].astype(BF16), vc)
        o = jnp.where(hm, o / l, 0.0)
        if gqa:
            if slot != kvh:
                o = pltpu.roll(o, HEAD_DIM, 1)
            halves[kvh] = halves[kvh] + o
        else:
            out = out + o
    if gqa:
        o_ref[0, :, 0:LANES] = halves[0]
        o_ref[0, :, LANES:2 * LANES] = halves[1]
    else:
        o_ref[0] = out


def dense_attention(src, cols, B, L, *, window=False, gqa=False, ctx=None, sink=None):
    W = src.shape[1]
    src3 = src.reshape(B, L, W)
    kv_w = SWA_KVW if gqa else BRANCH_W
    qc, kcol, vcol = cols
    in_specs = [pl.BlockSpec((1, ATTN_Q, BRANCH_W), lambda b, i: (b, i, qc)),
                pl.BlockSpec((1, L, kv_w), lambda b, i: (b, 0, kcol)),
                pl.BlockSpec((1, L, kv_w), lambda b, i: (b, 0, vcol))]
    args = [src3, src3, src3]
    if ctx is not None:
        P = ctx[0].shape[1]
        in_specs += [pl.BlockSpec((1, P, kv_w), lambda b, i: (b, 0, 0))] * 2
        args += list(ctx)
    if sink is not None:
        in_specs.append(pl.BlockSpec(memory_space=pltpu.SMEM))
        args.append(sink)
    return pl.pallas_call(
        functools.partial(_attn_kernel, L=L, window=window, gqa=gqa,
                          has_ctx=ctx is not None, has_sink=sink is not None),
        grid=(B, L // ATTN_Q),
        in_specs=in_specs,
        out_specs=pl.BlockSpec((1, ATTN_Q, BRANCH_W), lambda b, i: (b, i, 0)),
        out_shape=jax.ShapeDtypeStruct((B, L, BRANCH_W), F32),
        compiler_params=_cparams(48),
        name="window_attention" if window else "dense_attention",
    )(*args).reshape(B * L, BRANCH_W)


def _diff_kernel(*refs, lam_init, has_ctx):
    refs = list(refs)
    lam_ref, gain_ref, ones_ref, q_ref, k_ref, v_ref = refs[:6]
    if has_ctx:
        kc_ref, vc_ref, o_ref = refs[6:]
    else:
        (o_ref,) = refs[6:]
    lv = lam_ref[...]
    lam = (jnp.exp(jnp.sum(lv[0:1] * lv[1:2], keepdims=True))
           - jnp.exp(jnp.sum(lv[2:3] * lv[3:4], keepdims=True)) + lam_init)
    kl = k_ref[0].astype(BF16)
    vl = v_ref[0].astype(BF16)
    if has_ctx:
        kc = kc_ref[0].astype(BF16)
        vc = vc_ref[0].astype(BF16)
    q = q_ref[0].astype(BF16)
    out = jnp.zeros(q.shape, F32)
    for h in range(N_HEADS):
        o = None
        for mp in range(2):
            mm = _lane_mask(BRANCH_W, h * HEAD_DIM + mp * DIFF_HD, DIFF_HD)
            qm = jnp.where(mm, q, 0.0)
            parts = [_dot_nt(qm, kl)]
            if has_ctx:
                parts.append(_dot_nt(qm, kc))
            ps, l = _softmax_parts(parts)
            pv = _dot(ps[0].astype(BF16), vl)
            if has_ctx:
                pv = pv + _dot(ps[1].astype(BF16), vc)
            o = pv / l if mp == 0 else o - pv * (lam / l)
        out = out + jnp.where(_lane_mask(BRANCH_W, h * HEAD_DIM, HEAD_DIM), o, 0.0)
    ms = _dot_exact(out * out, ones_ref[...]) * (1.0 / HEAD_DIM)
    o_ref[0] = out * lax.rsqrt(ms + EPS) * gain_ref[...] * (1.0 - lam_init)


def diff_attention(dif, lam_vec, subln, layer, B, L, ctx=None):
    lam_init = 0.8 - 0.6 * math.exp(-0.3 * layer)
    dif3 = dif.reshape(B, L, DIFF_W)
    head = jnp.arange(BRANCH_W) // HEAD_DIM
    ones = (head[:, None] == head[None, :]).astype(F32)
    gain = jnp.tile(subln.astype(F32), N_HEADS).reshape(1, BRANCH_W)
    in_specs = [_resident((4, DIFF_HD)), _resident((1, BRANCH_W)), _resident((BRANCH_W, BRANCH_W)),
                pl.BlockSpec((1, ATTN_Q, BRANCH_W), lambda b, i: (b, i, 0)),
                pl.BlockSpec((1, L, BRANCH_W), lambda b, i: (b, 0, 1)),
                pl.BlockSpec((1, L, BRANCH_W), lambda b, i: (b, 0, 2))]
    args = [lam_vec, gain, ones, dif3, dif3, dif3]
    if ctx is not None:
        P = ctx[0].shape[1]
        in_specs += [pl.BlockSpec((1, P, BRANCH_W), lambda b, i: (b, 0, 0))] * 2
        args += list(ctx)
    return pl.pallas_call(
        functools.partial(_diff_kernel, lam_init=lam_init, has_ctx=ctx is not None),
        grid=(B, L // ATTN_Q),
        in_specs=in_specs,
        out_specs=pl.BlockSpec((1, ATTN_Q, BRANCH_W), lambda b, i: (b, i, 0)),
        out_shape=jax.ShapeDtypeStruct((B, L, BRANCH_W), F32),
        compiler_params=_cparams(56),
        name="diff_attention",
    )(*args).reshape(B * L, BRANCH_W)


def _filter_kernel(z_ref, w1_ref, b1_ref, w2_ref, b2_ref, w3_ref, fr_ref, ld_ref, o_ref):
    z = z_ref[...]
    tn = z[:, 0:1]
    fr = fr_ref[0]
    g = jnp.sin(fr * (_dot_exact(z, w1_ref[0]) + b1_ref[0]))
    g = jnp.sin(fr * (_dot_exact(g, w2_ref[0]) + b2_ref[0]))
    hf = _dot_exact(g, w3_ref[0]) * jnp.exp(-jnp.exp(ld_ref[0]) * tn)
    row = lax.broadcasted_iota(jnp.int32, (z.shape[0], 1), 0)
    for o in range(HY_ORDER):
        pos = hf[:, (2 * o) * HY_WIDTH:(2 * o + 1) * HY_WIDTH]
        neg = jnp.where(row == 0, 0.0, hf[:, (2 * o + 1) * HY_WIDTH:(2 * o + 2) * HY_WIDTH])
        norm = (jnp.sum(jnp.abs(pos), axis=0, keepdims=True)
                + jnp.sum(jnp.abs(neg), axis=0, keepdims=True) + EPS)
        o_ref[0, o, 0] = (neg + pos) / norm
        o_ref[0, o, 1] = (neg - pos) / norm


def hyena_filters(L, p):
    tn = jnp.arange(L, dtype=F32) / L
    ang = 2.0 * math.pi * tn[:, None] * jnp.arange(1, HY_FREQS + 1, dtype=F32)[None, :]
    z = jnp.concatenate([tn[:, None], jnp.cos(ang), jnp.sin(ang)], axis=-1)
    z = jnp.pad(z, ((0, 0), (0, HY_HIDDEN - HY_EMB)))
    w1 = jnp.pad(p['hy_w1'], ((0, 0), (0, HY_HIDDEN - HY_EMB), (0, 0)))
    fw = HY_ORDER * 2 * HY_WIDTH
    per_layer = lambda *shape: pl.BlockSpec((1,) + shape, lambda l: (l,) + (0,) * len(shape))
    return pl.pallas_call(
        _filter_kernel,
        grid=(DEPTH,),
        in_specs=[pl.BlockSpec((L, HY_HIDDEN), lambda l: (0, 0)),
                  per_layer(HY_HIDDEN, HY_HIDDEN), per_layer(1, HY_HIDDEN),
                  per_layer(HY_HIDDEN, HY_HIDDEN), per_layer(1, HY_HIDDEN),
                  per_layer(HY_HIDDEN, fw), per_layer(1, HY_HIDDEN), per_layer(1, fw)],
        out_specs=per_layer(HY_ORDER, 2, L, HY_WIDTH),
        out_shape=jax.ShapeDtypeStruct((DEPTH, HY_ORDER, 2, L, HY_WIDTH), F32),
        compiler_params=_cparams(56),
        name="hyena_filters",
    )(z, w1, p['hy_b1'].reshape(DEPTH, 1, HY_HIDDEN), p['hy_w2'], p['hy_b2'].reshape(DEPTH, 1, HY_HIDDEN),
      p['hy_w3'], p['hy_sin_freq'].reshape(DEPTH, 1, HY_HIDDEN), p['hy_log_decay'].reshape(DEPTH, 1, fw))


def dft_tables(L, tk):
    step = GRID_W
    k = 2 * jnp.arange(L, dtype=jnp.int32)[:, None] + 1

    def grid(n):
        ang = ((k * n[None, :]) % (4 * L)).astype(F32) * (math.pi / (2 * L))
        return jnp.cos(ang)[:, :, None], jnp.sin(ang)[:, :, None]

    (c_hi, s_hi) = grid(step * jnp.arange(L // step, dtype=jnp.int32))
    (c_lo, s_lo) = (t.reshape(L, 1, step) for t in grid(jnp.arange(step, dtype=jnp.int32)))
    c = (c_hi * c_lo - s_hi * s_lo).reshape(L, L)
    s = (s_hi * c_lo + c_hi * s_lo).reshape(L, L)
    fwd = jnp.concatenate([c.reshape(L // tk, tk, L), s.reshape(L // tk, tk, L)], axis=1).astype(BF16)
    inv = jnp.concatenate([c.T, -s.T], axis=1).astype(BF16)
    return fwd, inv


def _spectrum_kernel(t_ref, f_ref, o_ref):
    tk = o_ref.shape[2]
    o_ref[0, 0] = _dot(t_ref[0, :tk, :], f_ref[0, 0].astype(BF16))
    o_ref[0, 1] = _dot(t_ref[0, tk:, :], f_ref[0, 1].astype(BF16))


def filter_spectra(filt, fwd):
    G, _, L, C = filt.shape
    nkt, tk2, _ = fwd.shape
    tk = tk2 // 2
    return pl.pallas_call(
        _spectrum_kernel,
        grid=(nkt, G),
        in_specs=[pl.BlockSpec((1, tk2, L), lambda i, g: (i, 0, 0)),
                  pl.BlockSpec((1, 2, L, C), lambda i, g: (g, 0, 0, 0))],
        out_specs=pl.BlockSpec((1, 2, tk, C), lambda i, g: (g, 0, i, 0)),
        out_shape=jax.ShapeDtypeStruct((G, 2, L, C), F32),
        compiler_params=_cparams(48),
        name="filter_spectra",
    )(fwd, filt)


def _short_conv(u, w_ref, b_ref):
    L = u.shape[0]
    row = lax.broadcasted_iota(jnp.int32, (L, 1), 0)
    prev = jnp.where(row == 0, 0.0, pltpu.roll(u, 1, 0))
    nxt = jnp.where(row == L - 1, 0.0, pltpu.roll(u, L - 1, 0))
    return prev * w_ref[0:1, :] + u * w_ref[1:2, :] + nxt * w_ref[2:3, :] + b_ref[...]


def _conv_fwd_kernel(t_ref, y_ref, h_ref, *rest, first):
    if first:
        w_ref, b_ref, o_ref = rest
        y = _short_conv(y_ref[0], w_ref, b_ref)
    else:
        (o_ref,) = rest
        y = y_ref[0]
    y = y.astype(BF16)
    nkt, tk2, _ = t_ref.shape
    tk = tk2 // 2
    for i in range(nkt):
        rows = slice(i * tk, (i + 1) * tk)
        acc = _dot(t_ref[i], y)
        yc, ys = acc[:tk], acc[tk:]
        hr = h_ref[0, 0, rows, :]
        hi = h_ref[0, 1, rows, :]
        o_ref[0, 0, rows, :] = (yc * hr + ys * hi).astype(BF16)
        o_ref[0, 1, rows, :] = (yc * hi - ys * hr).astype(BF16)


def _col_spec(L, col):
    return pl.BlockSpec((1, L, HY_WIDTH), lambda b: (b, 0, col))


def _short_specs(col):
    return [pl.BlockSpec((3, HY_WIDTH), lambda b: (0, col)), pl.BlockSpec((1, HY_WIDTH), lambda b: (0, col))]


def conv_forward(y, spec, g, fwd, short=None):
    B, L, _ = y.shape
    C = HY_WIDTH
    first = short is not None
    in_specs = [_resident(fwd.shape), _col_spec(L, 0),
                pl.BlockSpec((1, 2, L, C), lambda b: (g, 0, 0, 0), pipeline_mode=pl.Buffered(1))]
    args = [fwd, y, spec]
    if first:
        in_specs += _short_specs(0)
        args += list(short)
    return pl.pallas_call(
        functools.partial(_conv_fwd_kernel, first=first),
        grid=(B,),
        in_specs=in_specs,
        out_specs=pl.BlockSpec((1, 2, L, C), lambda b: (b, 0, 0, 0)),
        out_shape=jax.ShapeDtypeStruct((B, 2, L, C), BF16),
        compiler_params=_cparams(56),
        name="conv_forward",
    )(*args)


def _conv_inv_kernel(t_ref, z_ref, g_ref, wg_ref, bg_ref, skip_ref, y_ref, *rest, first, tt):
    if first:
        wy_ref, by_ref, o_ref = rest
        y = _short_conv(y_ref[0], wy_ref, by_ref)
    else:
        (o_ref,) = rest
        y = y_ref[0]
    gate = _short_conv(g_ref[0], wg_ref, bg_ref)
    L, L2 = t_ref.shape
    z = z_ref[0].reshape(L2, HY_WIDTH)
    for i in range(L // tt):
        rows = slice(i * tt, (i + 1) * tt)
        conv = _dot(t_ref[rows, :], z) * (2.0 / L2)
        o_ref[0, rows, :] = gate[rows] * (conv + skip_ref[...] * y[rows])


def conv_inverse(zf, hy, order, short_w, short_b, skip, inv, tt, y=None):
    B, _, L, C = zf.shape
    first = y is None
    in_specs = [_resident(inv.shape),
                pl.BlockSpec((1, 2, L, C), lambda b: (b, 0, 0, 0)),
                _col_spec(L, 1 + order), *_short_specs(1 + order),
                pl.BlockSpec((1, C), lambda b: (0, 0)),
                _col_spec(L, 0)]
    args = [inv, zf, hy, short_w, short_b, skip.reshape(1, C), hy if first else y]
    if first:
        in_specs += _short_specs(0)
        args += [short_w, short_b]
    return pl.pallas_call(
        functools.partial(_conv_inv_kernel, first=first, tt=tt),
        grid=(B,),
        in_specs=in_specs,
        out_specs=pl.BlockSpec((1, L, C), lambda b: (b, 0, 0)),
        out_shape=jax.ShapeDtypeStruct((B, L, C), F32),
        compiler_params=_cparams(56),
        name="conv_inverse",
    )(*args)


def hyena(hy, lp, layer, spec, tabs, B, L):
    fwd, inv, tt = tabs
    hy = hy.reshape(B, L, HY_IN_W)
    short_w = lp['hy_short_w']
    short_b = lp['hy_short_b'].reshape(1, HY_IN_W)
    y = None
    for o in range(HY_ORDER):
        g = layer * HY_ORDER + o
        if y is None:
            zf = conv_forward(hy, spec, g, fwd, short=(short_w, short_b))
        else:
            zf = conv_forward(y, spec, g, fwd)
        y = conv_inverse(zf, hy, o, short_w, short_b, lp['hy_skip'][o], inv, tt, y=y)
    return y.reshape(B * L, HY_WIDTH)


def _merge_kernel(a_ref, b_ref, c_ref, d_ref, x_ref, mod_ref, nw_ref, wg_ref, wb_ref, wo_ref, o_ref):
    x = x_ref[...]
    h = _norm_mod(x, nw_ref[...], mod_ref[0, 3:4, :], mod_ref[0, 4:5, :]).astype(BF16)
    merged = None
    for i, br in enumerate((a_ref, b_ref, c_ref, d_ref)):
        gate = _sigmoid(_dot(h, wg_ref[:, i * D_MODEL:(i + 1) * D_MODEL]))
        t = gate * _dot(br[...].astype(BF16), wb_ref[i])
        merged = t if merged is None else merged + t
    o_ref[...] = x + mod_ref[0, 5:6, :] * _dot(merged.astype(BF16), wo_ref[...])


def merge_block(branches, x, mod, nw, wg, wb, wo, layer, rows_per_b):
    T = x.shape[0]
    tm = FFN_ROWS
    nb = mod.shape[0]
    row = lambda w: pl.BlockSpec((tm, w), lambda i: (i, 0))
    return pl.pallas_call(
        _merge_kernel,
        grid=(T // tm,),
        in_specs=[row(BRANCH_W)] * N_BRANCH + [row(D_MODEL), _mod_spec(nb, rows_per_b // tm),
                                               _resident((1, D_MODEL)),
                                               _resident_layer(wg, layer),
                                               _resident_layer(wb, layer),
                                               _resident_layer(wo, layer)],
        out_specs=row(D_MODEL),
        out_shape=jax.ShapeDtypeStruct((T, D_MODEL), F32),
        compiler_params=_cparams(56),
        name="merge_block",
    )(*branches, x, mod, nw.reshape(1, D_MODEL), wg, wb, wo)


def _heads_in(t):
    B, H, P, d = t.shape
    return t.transpose(0, 2, 1, 3).reshape(B, P, H * d)


def _run_pass(x, mod_all, p, wts, final_norm, B, L, spec, tabs, caches):
    ctx_pass = caches is None
    collected = []
    if not ctx_pass:
        rope = rope_tables(L, HEAD_DIM, SWA_QW + SWA_KVW) + rope_tables(L, DIFF_HD, 2 * BRANCH_W)
        bias_all = na_bias_tables(p['na_rpb'], L // GRID_W)
    for l in range(DEPTH):
        lp = {k: v[l] for k, v in p.items()}
        w = wts
        mod = mod_all[l]
        x = ffn_block(x, mod, lp['norm_ffn1'], w['ffn1_w1'], w['ffn1_w3'], w['ffn1_w2'], l, 0, L)
        na, swa, hy, dif = in_projection(x, mod, lp['norm_mix'], w['w_mix'], l, L,
                                         None if ctx_pass else rope, F32 if ctx_pass else BF16)
        if ctx_pass:
            a_o = dense_attention(na, (0, 1, 2), B, L)
            b_o = dense_attention(swa, (0, 2, 3), B, L, gqa=True, sink=lp['swa_sink'])
            d_o = diff_attention(dif, lp['diff_lambda'], lp['diff_subln'], l, B, L)
            collected.append((na, swa, dif))
        else:
            ck_na, cv_na, ck_swa, cv_swa, ck_d, cv_d = (_heads_in(t[:, l]).astype(BF16) for t in caches)
            a_o = neighbourhood_attention(na, ck_na, cv_na, bias_all, l, B, L)
            b_o = dense_attention(swa, (0, 2, 3), B, L, window=True, gqa=True,
                                  ctx=(ck_swa, cv_swa), sink=lp['swa_sink'])
            d_o = diff_attention(dif, lp['diff_lambda'], lp['diff_subln'], l, B, L, ctx=(ck_d, cv_d))
        c_o = hyena(hy, lp, l, spec, tabs, B, L)
        x = merge_block((a_o, b_o, c_o, d_o), x, mod, lp['norm_mix'], w['w_gate'], w['w_branch'], w['w_out'], l, L)
        x = ffn_block(x, mod, lp['norm_ffn2'], w['ffn2_w1'], w['ffn2_w3'], w['ffn2_w2'], l, 6, L,
                      final_w=final_norm if l == DEPTH - 1 else None)
    return x, collected


def _new_caches(collected, B, L):
    def heads(ts, lo, n):
        t = jnp.stack(ts, axis=1).reshape(B, DEPTH, L, -1, HEAD_DIM)[:, :, :, lo:lo + n]
        return t.transpose(0, 1, 3, 2, 4)

    na, swa, dif = ([c[j].reshape(B, L, -1) for c in collected] for j in range(3))
    h, g = N_HEADS, SWA_KV_HEADS
    return (heads(na, h, h), heads(na, 2 * h, h), heads(swa, h, g), heads(swa, h + g, g),
            heads(dif, h, h), heads(dif, 2 * h, h))


def _hyena_setup(L, p, tk, tt):
    fwd, inv = dft_tables(L, tk)
    filt = hyena_filters(L, p)
    spec = filter_spectra(filt.reshape(DEPTH * HY_ORDER, 2, L, HY_WIDTH), fwd)
    return spec, (fwd, inv, tt)


def kernel(x_prompt, x_sample, cache_na_k, cache_na_v, cache_swa_k, cache_swa_v, cache_diff_k, cache_diff_v, c, c_ctx, w_ada, b_ada, norm_ffn1, norm_mix, norm_ffn2, final_norm, ffn1_w1, ffn1_w3, ffn1_w2, ffn2_w1, ffn2_w3, ffn2_w2, w_in, w_branch, w_out, na_rpb, swa_sink, hy_short_w, hy_short_b, hy_w1, hy_b1, hy_w2, hy_b2, hy_w3, hy_sin_freq, hy_log_decay, hy_skip, diff_lambda, diff_subln):
    B_ctx, L_ctx, _ = x_prompt.shape
    B_den, L_den, _ = x_sample.shape
    p = {
        'norm_ffn1': norm_ffn1, 'norm_mix': norm_mix, 'norm_ffn2': norm_ffn2,
        'na_rpb': na_rpb, 'swa_sink': swa_sink, 'hy_short_w': hy_short_w, 'hy_short_b': hy_short_b,
        'hy_w1': hy_w1, 'hy_b1': hy_b1, 'hy_w2': hy_w2, 'hy_b2': hy_b2, 'hy_w3': hy_w3,
        'hy_sin_freq': hy_sin_freq, 'hy_log_decay': hy_log_decay, 'hy_skip': hy_skip,
        'diff_lambda': diff_lambda, 'diff_subln': diff_subln,
    }
    big = {'ffn1_w1': ffn1_w1, 'ffn1_w3': ffn1_w3, 'ffn1_w2': ffn1_w2, 'ffn2_w1': ffn2_w1,
           'ffn2_w3': ffn2_w3, 'ffn2_w2': ffn2_w2, 'w_mix': w_in[:, :, :MIX_W], 'w_gate': w_in[:, :, MIX_W:],
           'w_branch': w_branch, 'w_out': w_out}
    wts = {k: v.astype(BF16) for k, v in big.items()}

    cond = jnp.concatenate([c, c_ctx[None, :]], axis=0)
    rows = 8 * ((cond.shape[0] + 7) // 8)
    cond = jnp.pad(cond, ((0, rows - cond.shape[0]), (0, 0)))
    mod = ada_modulation(cond, w_ada, b_ada).reshape(DEPTH, rows, N_MOD, D_MODEL)
    mod_den = mod[:, :B_den]
    mod_ctx = mod[:, B_den:B_den + 1]

    spec_c, tabs_c = _hyena_setup(L_ctx, p, min(L_ctx, 256), min(L_ctx, 256))
    y_ctx, collected = _run_pass(x_prompt.reshape(B_ctx * L_ctx, D_MODEL), mod_ctx, p, wts, final_norm,
                                 B_ctx, L_ctx, spec_c, tabs_c, None)
    new_caches = _new_caches(collected, B_ctx, L_ctx)

    spec_d, tabs_d = _hyena_setup(L_den, p, 512, 512)
    caches = (cache_na_k, cache_na_v, cache_swa_k, cache_swa_v, cache_diff_k, cache_diff_v)
    y_den, _ = _run_pass(x_sample.reshape(B_den * L_den, D_MODEL), mod_den, p, wts, final_norm,
                         B_den, L_den, spec_d, tabs_d, caches)
    return (y_ctx.reshape(B_ctx, L_ctx, D_MODEL), y_den.reshape(B_den, L_den, D_MODEL), *new_caches)
```

```python
import functools
import math

import jax
import jax.numpy as jnp
from jax import lax
from jax.experimental import pallas as pl
from jax.experimental.pallas import tpu as pltpu

F32 = jnp.float32
BF16 = jnp.bfloat16

D_MODEL = 1024
DEPTH = 4
GRID_W = 64
N_BRANCH = 4
BRANCH_W = D_MODEL // 4
HEAD_DIM = 64
N_HEADS = BRANCH_W // HEAD_DIM
NA_WIN_R = 8
NA_WIN_C = 16
SWA_KV_HEADS = N_HEADS // 2
SWA_WINDOW = 128
HY_WIDTH = BRANCH_W
HY_ORDER = 2
HY_FREQS = 16
HY_EMB = 1 + 2 * HY_FREQS
HY_HIDDEN = 64
DIFF_HD = 32
D_FF = 128 * ((8 * D_MODEL // 3 + 127) // 128)
ROPE_BASE = 10000.0
EPS = 1e-6
NEG = -1e30
N_MOD = 9
NA_W = 3 * BRANCH_W
SWA_QW = BRANCH_W
SWA_KVW = SWA_KV_HEADS * HEAD_DIM
SWA_W = SWA_QW + 2 * SWA_KVW
HY_IN_W = 3 * HY_WIDTH
DIFF_W = 3 * BRANCH_W
GATE_W = N_BRANCH * D_MODEL
MIX_W = NA_W + SWA_W + HY_IN_W + DIFF_W

LOG2E = math.log2(math.e)
QK_SCALE = HEAD_DIM ** -0.5 * LOG2E
DIFF_QK_SCALE = DIFF_HD ** -0.5 * LOG2E

LANES = 128
MXU_DIM = 256
MIB = 1024 * 1024

FFN_ROWS = 512
PROJ_ROWS = 512
FFN_CHUNK = 768
ATTN_Q = 256
DIFF_Q = 512
NA_Q_ROWS = ATTN_Q // GRID_W
NA_SLAB_ROWS = NA_Q_ROWS + NA_WIN_R
SWA_SLAB = ATTN_Q + 2 * SWA_WINDOW


def _cparams(vmem_mib):
    return pltpu.CompilerParams(vmem_limit_bytes=vmem_mib * MIB)


def _resident(shape):
    nd = len(shape)
    return pl.BlockSpec(shape, lambda *_: (0,) * nd, pipeline_mode=pl.Buffered(1))


def _resident_layer(stacked, layer):
    shape = stacked.shape[1:]
    return pl.BlockSpec((None,) + shape, lambda *_: (layer,) + (0,) * len(shape), pipeline_mode=pl.Buffered(1))


def _dot(a, b):
    return jnp.dot(a, b, preferred_element_type=F32)


def _dot_nt(a, b):
    return lax.dot_general(a, b, (((1,), (1,)), ((), ())), preferred_element_type=F32)


def _dot_exact(a, b):
    return jnp.dot(a, b, preferred_element_type=F32, precision=lax.Precision.HIGHEST)


def _sigmoid(x):
    return 1.0 / (1.0 + jnp.exp(-x))


def _norm_mod(x, nw, shift, scale):
    y = x * lax.rsqrt(jnp.mean(x * x, axis=-1, keepdims=True) + EPS) * nw
    return y * (1.0 + scale) + shift


def _mod_spec(nb, tiles_per_b):
    if nb == 1:
        return pl.BlockSpec((1, N_MOD, D_MODEL), lambda i: (0, 0, 0))
    return pl.BlockSpec((1, N_MOD, D_MODEL), lambda i: (i // tiles_per_b, 0, 0))


def _ada_kernel(c_ref, w_ref, b_ref, o_ref):
    c = c_ref[...]
    s = (c * _sigmoid(c)).astype(BF16)
    o_ref[0] = _dot(s, w_ref[0].astype(BF16)) + b_ref[0]


def ada_modulation(cond, w_ada, b_ada):
    rows = cond.shape[0]
    width = N_MOD * D_MODEL
    tn = 9 * LANES
    return pl.pallas_call(
        _ada_kernel,
        grid=(DEPTH, width // tn),
        in_specs=[pl.BlockSpec((rows, D_MODEL), lambda l, j: (0, 0)),
                  pl.BlockSpec((1, D_MODEL, tn), lambda l, j: (l, 0, j)),
                  pl.BlockSpec((1, 1, tn), lambda l, j: (l, 0, j))],
        out_specs=pl.BlockSpec((1, rows, tn), lambda l, j: (l, 0, j)),
        out_shape=jax.ShapeDtypeStruct((DEPTH, rows, width), F32),
        compiler_params=_cparams(32),
        name="ada_modulation",
    )(cond, w_ada, b_ada.reshape(DEPTH, 1, width))


def _ffn_kernel(x_ref, mod_ref, nw_ref, w1_ref, w3_ref, w2_ref, *rest, mod_base, final):
    if final:
        fw_ref, o_ref = rest
    else:
        (o_ref,) = rest
    x = x_ref[...]
    shift = mod_ref[0, mod_base:mod_base + 1, :]
    scale = mod_ref[0, mod_base + 1:mod_base + 2, :]
    gate = mod_ref[0, mod_base + 2:mod_base + 3, :]
    h = _norm_mod(x, nw_ref[...], shift, scale).astype(BF16)
    acc = jnp.zeros(x.shape, F32)
    for lo in range(0, D_FF, FFN_CHUNK):
        hi = min(lo + FFN_CHUNK, D_FF)
        a = _dot(h, w1_ref[:, lo:hi])
        b = _dot(h, w3_ref[:, lo:hi])
        u = (a * _sigmoid(a) * b).astype(BF16)
        acc = acc + _dot(u, w2_ref[lo:hi, :])
    y = x + 0.5 * gate * acc
    if final:
        y = y * lax.rsqrt(jnp.mean(y * y, axis=-1, keepdims=True) + EPS) * fw_ref[...]
    o_ref[...] = y


def ffn_block(x, mod, nw, w1, w3, w2, layer, mod_base, rows_per_b, final_w=None):
    T = x.shape[0]
    tm = FFN_ROWS
    nb = mod.shape[0]
    final = final_w is not None
    in_specs = [pl.BlockSpec((tm, D_MODEL), lambda i: (i, 0)),
                _mod_spec(nb, rows_per_b // tm),
                _resident((1, D_MODEL)),
                _resident_layer(w1, layer), _resident_layer(w3, layer), _resident_layer(w2, layer)]
    args = [x, mod, nw.reshape(1, D_MODEL), w1, w3, w2]
    if final:
        in_specs.append(_resident((1, D_MODEL)))
        args.append(final_w.reshape(1, D_MODEL))
    return pl.pallas_call(
        functools.partial(_ffn_kernel, mod_base=mod_base, final=final),
        grid=(T // tm,),
        in_specs=in_specs,
        out_specs=pl.BlockSpec((tm, D_MODEL), lambda i: (i, 0)),
        out_shape=jax.ShapeDtypeStruct((T, D_MODEL), F32),
        compiler_params=_cparams(56),
        name="ffn_block",
    )(*args)


def _rope_chunk(x, cos, sin_a, sin_b, dist):
    return x * cos + pltpu.roll(x, LANES - dist, 1) * sin_a + pltpu.roll(x, dist, 1) * sin_b


def _proj_kernel(x_ref, mod_ref, nw_ref, w_ref, *rest, rope):
    if rope:
        (cs_ref, sa_ref, sb_ref, cd_ref, da_ref, db_ref,
         na_ref, swa_ref, hy_ref, dif_ref) = rest
    else:
        na_ref, swa_ref, hy_ref, dif_ref = rest
    h = _norm_mod(x_ref[...], nw_ref[...], mod_ref[0, 3:4, :], mod_ref[0, 4:5, :]).astype(BF16)
    q_chunks = BRANCH_W // LANES
    o = 0
    s = _dot(h, w_ref[:, o:o + NA_W])
    o += NA_W
    na_ref[:, :BRANCH_W] = (s[:, :BRANCH_W] * QK_SCALE).astype(na_ref.dtype)
    na_ref[:, BRANCH_W:] = s[:, BRANCH_W:].astype(na_ref.dtype)
    s = _dot(h, w_ref[:, o:o + SWA_W])
    o += SWA_W
    n_rot = (SWA_QW + SWA_KVW) // LANES
    for c in range(SWA_W // LANES):
        sl = slice(c * LANES, (c + 1) * LANES)
        chunk = s[:, sl]
        if rope and c < n_rot:
            chunk = _rope_chunk(chunk, cs_ref[:, sl], sa_ref[:, sl], sb_ref[:, sl], HEAD_DIM // 4)
        if c < q_chunks:
            chunk = chunk * QK_SCALE
        swa_ref[:, sl] = chunk.astype(swa_ref.dtype)
    hy_ref[...] = _dot(h, w_ref[:, o:o + HY_IN_W])
    o += HY_IN_W
    s = _dot(h, w_ref[:, o:o + DIFF_W])
    o += DIFF_W
    n_rot = 2 * BRANCH_W // LANES
    for c in range(DIFF_W // LANES):
        sl = slice(c * LANES, (c + 1) * LANES)
        chunk = s[:, sl]
        if rope and c < n_rot:
            chunk = _rope_chunk(chunk, cd_ref[:, sl], da_ref[:, sl], db_ref[:, sl], DIFF_HD // 4)
        if c < q_chunks:
            chunk = chunk * DIFF_QK_SCALE
        dif_ref[:, sl] = chunk.astype(dif_ref.dtype)


def in_projection(x, mod, nw, w_in, layer, rows_per_b, rope_tabs, qkv_dtype):
    T = x.shape[0]
    tm = PROJ_ROWS
    nb = mod.shape[0]
    rope = rope_tabs is not None
    in_specs = [pl.BlockSpec((tm, D_MODEL), lambda i: (i, 0)),
                _mod_spec(nb, rows_per_b // tm),
                _resident((1, D_MODEL)),
                pl.BlockSpec((None, D_MODEL, MIX_W), lambda *_: (layer, 0, 0), pipeline_mode=pl.Buffered(1))]
    args = [x, mod, nw.reshape(1, D_MODEL), w_in]
    if rope:
        pos_tiles = rows_per_b // tm
        for t in rope_tabs:
            in_specs.append(pl.BlockSpec((tm, t.shape[1]), lambda i: (i % pos_tiles, 0)))
            args.append(t)
    widths = (NA_W, SWA_W, HY_IN_W, DIFF_W)
    dtypes = (qkv_dtype, qkv_dtype, F32, qkv_dtype)
    return pl.pallas_call(
        functools.partial(_proj_kernel, rope=rope),
        grid=(T // tm,),
        in_specs=in_specs,
        out_specs=[pl.BlockSpec((tm, w), lambda i: (i, 0)) for w in widths],
        out_shape=[jax.ShapeDtypeStruct((T, w), dt) for w, dt in zip(widths, dtypes)],
        compiler_params=_cparams(56),
        name="in_projection",
    )(*args)


def rope_tables(L, dh, width):
    nf = dh // 4
    t = jnp.arange(L)
    freqs = ROPE_BASE ** (-jnp.arange(nf, dtype=F32) / nf)
    ang_r = (t // GRID_W).astype(F32)[:, None] * freqs
    ang_c = (t % GRID_W).astype(F32)[:, None] * freqs
    cr, sr, cc, sc = jnp.cos(ang_r), jnp.sin(ang_r), jnp.cos(ang_c), jnp.sin(ang_c)
    z = jnp.zeros_like(sr)
    reps = width // dh
    cos = jnp.tile(jnp.concatenate([cr, cr, cc, cc], axis=-1), (1, reps))
    sin_a = jnp.tile(jnp.concatenate([-sr, z, -sc, z], axis=-1), (1, reps))
    sin_b = jnp.tile(jnp.concatenate([z, sr, z, sc], axis=-1), (1, reps))
    return cos, sin_a, sin_b


def _lane_mask(width, lo, n):
    lane = lax.broadcasted_iota(jnp.int32, (1, width), 1)
    return (lane >= lo) & (lane < lo + n)


def _softmax_parts(parts, extra_logit=None):
    m = parts[0].max(axis=-1, keepdims=True)
    for s in parts[1:]:
        m = jnp.maximum(m, s.max(axis=-1, keepdims=True))
    if extra_logit is not None:
        m = jnp.maximum(m, extra_logit)
    ps = [jnp.exp2(s - m) for s in parts]
    l = ps[0].sum(axis=-1, keepdims=True)
    for p in ps[1:]:
        l = l + p.sum(axis=-1, keepdims=True)
    if extra_logit is not None:
        l = l + jnp.exp2(extra_logit - m)
    return ps, l


def _na_kernel(q_ref, k_ref, v_ref, kc_ref, vc_ref, bias_ref, o_ref, *, n_rows):
    i = pl.program_id(1)
    row0 = jnp.clip(NA_Q_ROWS * i - NA_WIN_R // 2, 0, n_rows - NA_SLAB_ROWS)
    start = pl.multiple_of(row0 * GRID_W, GRID_W)
    slab = NA_SLAB_ROWS * GRID_W
    ks = k_ref[0, pl.ds(start, slab), :].astype(BF16)
    vs = v_ref[0, pl.ds(start, slab), :].astype(BF16)
    kc = kc_ref[0].astype(BF16)
    vc = vc_ref[0].astype(BF16)
    q = q_ref[0].astype(BF16)
    out = jnp.zeros(q.shape, F32)
    for h in range(N_HEADS):
        hm = _lane_mask(BRANCH_W, h * HEAD_DIM, HEAD_DIM)
        qh = jnp.where(hm, q, 0.0)
        s_win = _dot_nt(qh, ks) + bias_ref[0, 0, h]
        s_ctx = _dot_nt(qh, kc)
        (p_win, p_ctx), l = _softmax_parts([s_win, s_ctx])
        o = _dot(p_win.astype(BF16), vs) + _dot(p_ctx.astype(BF16), vc)
        out = out + jnp.where(hm, o / l, 0.0)
    o_ref[0] = out


NA_DROWS = 2 * NA_WIN_R - 1
NA_DCOLS = 2 * NA_WIN_C - 1


def _na_tile_geometry(n_rows):
    n_tiles = n_rows // NA_Q_ROWS

    def geometry(tile):
        slab0 = min(max(NA_Q_ROWS * tile - NA_WIN_R // 2, 0), n_rows - NA_SLAB_ROWS)
        rows = []
        for j in range(NA_Q_ROWS):
            qr = NA_Q_ROWS * tile + j
            r0 = min(max(qr - NA_WIN_R // 2, 0), n_rows - NA_WIN_R)
            rows.append([slab0 + m - qr + NA_WIN_R - 1 if r0 <= slab0 + m < r0 + NA_WIN_R else None
                         for m in range(NA_SLAB_ROWS)])
        return rows

    kinds = [geometry(0), geometry(1), geometry(n_tiles - 1)]
    assert all(geometry(t) == kinds[1] for t in range(1, n_tiles - 1))
    return kinds


def _na_bias_kernel(rpb_ref, o_ref, band_ref, *, kinds):
    base = (pl.program_id(0) * N_HEADS + pl.program_id(1)) * (NA_DROWS * NA_DCOLS)
    qc = lax.broadcasted_iota(jnp.int32, (GRID_W, LANES), 0)
    lane = lax.broadcasted_iota(jnp.int32, (GRID_W, LANES), 1)
    kc = lane % GRID_W
    d_col = kc - qc + (NA_WIN_C - 1)
    c0 = jnp.clip(qc - NA_WIN_C // 2, 0, GRID_W - NA_WIN_C)
    col_ok = (kc >= c0) & (kc < c0 + NA_WIN_C)
    for d in range(NA_DROWS):
        t = jnp.full((GRID_W, LANES), NEG, F32)
        for e in range(NA_DCOLS):
            t = jnp.where(d_col == e, rpb_ref[base + d * NA_DCOLS + e] * LOG2E, t)
        band_ref[d] = jnp.where(col_ok, t, NEG)
    masked = jnp.full((GRID_W, LANES), NEG, F32)
    left = lane < GRID_W
    for kind, rows in enumerate(kinds):
        for j, drow in enumerate(rows):
            for m in range(0, NA_SLAB_ROWS, 2):
                a = masked if drow[m] is None else band_ref[drow[m]]
                b = masked if drow[m + 1] is None else band_ref[drow[m + 1]]
                o_ref[0, kind, 0, j * GRID_W:(j + 1) * GRID_W, m * GRID_W:(m + 2) * GRID_W] = jnp.where(left, a, b)


def na_bias_tables(rpb, n_rows):
    slab = NA_SLAB_ROWS * GRID_W
    return pl.pallas_call(
        functools.partial(_na_bias_kernel, kinds=_na_tile_geometry(n_rows)),
        grid=(DEPTH, N_HEADS),
        in_specs=[pl.BlockSpec(memory_space=pltpu.SMEM)],
        out_specs=pl.BlockSpec((1, 3, 1, ATTN_Q, slab), lambda l, h: (l, 0, h, 0, 0)),
        out_shape=jax.ShapeDtypeStruct((DEPTH, 3, N_HEADS, ATTN_Q, slab), F32),
        scratch_shapes=[pltpu.VMEM((NA_DROWS, GRID_W, LANES), F32)],
        compiler_params=_cparams(32),
        name="na_bias_tables",
    )(rpb.astype(F32).reshape(-1))


def neighbourhood_attention(na, kc, vc, bias, layer, B, L):
    n_rows = L // GRID_W
    n_tiles = L // ATTN_Q
    na3 = na.reshape(B, L, NA_W)
    P = kc.shape[1]
    slab = NA_SLAB_ROWS * GRID_W

    def kind(b, i):
        return (layer, jnp.where(i == 0, 0, jnp.where(i == n_tiles - 1, 2, 1)), 0, 0, 0)

    return pl.pallas_call(
        functools.partial(_na_kernel, n_rows=n_rows),
        grid=(B, n_tiles),
        in_specs=[pl.BlockSpec((1, ATTN_Q, BRANCH_W), lambda b, i: (b, i, 0)),
                  pl.BlockSpec((1, L, BRANCH_W), lambda b, i: (b, 0, 1)),
                  pl.BlockSpec((1, L, BRANCH_W), lambda b, i: (b, 0, 2)),
                  pl.BlockSpec((1, P, BRANCH_W), lambda b, i: (b, 0, 0)),
                  pl.BlockSpec((1, P, BRANCH_W), lambda b, i: (b, 0, 0)),
                  pl.BlockSpec((1, 1, N_HEADS, ATTN_Q, slab), kind)],
        out_specs=pl.BlockSpec((1, ATTN_Q, BRANCH_W), lambda b, i: (b, i, 0)),
        out_shape=jax.ShapeDtypeStruct((B, L, BRANCH_W), F32),
        compiler_params=_cparams(48),
        name="neighbourhood_attention",
    )(na3, na3, na3, kc, vc, bias).reshape(B * L, BRANCH_W)


def _attn_kernel(*refs, L, window, gqa, has_ctx, has_sink):
    refs = list(refs)
    q_ref, k_ref, v_ref = refs[:3]
    pos = 3
    if has_ctx:
        kc_ref, vc_ref = refs[pos:pos + 2]
        pos += 2
    if has_sink:
        sink_ref = refs[pos]
        pos += 1
    o_ref = refs[pos]
    i = pl.program_id(1)
    if window:
        slab = SWA_SLAB
        start = pl.multiple_of(jnp.clip(i * ATTN_Q - SWA_WINDOW, 0, L - slab), SWA_WINDOW)
        q_pos = i * ATTN_Q + lax.broadcasted_iota(jnp.int32, (ATTN_Q, 1), 0)
        k_pos = start + lax.broadcasted_iota(jnp.int32, (1, slab), 1)
        ok = jnp.abs(k_pos - q_pos) <= SWA_WINDOW
        ks = k_ref[0, pl.ds(start, slab), :].astype(BF16)
        vs = v_ref[0, pl.ds(start, slab), :].astype(BF16)
    else:
        ks = k_ref[0].astype(BF16)
        vs = v_ref[0].astype(BF16)
    if has_ctx:
        kc = kc_ref[0].astype(BF16)
        vc = vc_ref[0].astype(BF16)
    q = q_ref[0].astype(F32)
    kv_w = ks.shape[1]
    halves = [jnp.zeros((ATTN_Q, LANES), F32), jnp.zeros((ATTN_Q, LANES), F32)]
    out = jnp.zeros(q.shape, F32)
    for h in range(N_HEADS):
        if gqa:
            kvh, slot = h // 2, h % 2
            qh = q[:, kvh * LANES:(kvh + 1) * LANES]
            if slot != kvh:
                qh = pltpu.roll(qh, HEAD_DIM, 1)
            hm = _lane_mask(kv_w, kvh * HEAD_DIM, HEAD_DIM)
        else:
            qh = q
            hm = _lane_mask(kv_w, h * HEAD_DIM, HEAD_DIM)
        qh = jnp.where(hm, qh, 0.0).astype(BF16)
        s = _dot_nt(qh, ks)
        if window:
            s = jnp.where(ok, s, NEG)
        parts = [s]
        if has_ctx:
            parts.append(_dot_nt(qh, kc))
        ps, l = _softmax_parts(parts, sink_ref[h] * LOG2E if has_sink else None)
        o = _dot(ps[0].astype(BF16), vs)
        if has_ctx:
            o = o + _dot(ps[1].astype(BF16), vc)
        o = jnp.where(hm, o / l, 0.0)
        if gqa:
            if slot != kvh:
                o = pltpu.roll(o, HEAD_DIM, 1)
            halves[kvh] = halves[kvh] + o
        else:
            out = out + o
    if gqa:
        o_ref[0, :, 0:LANES] = halves[0]
        o_ref[0, :, LANES:2 * LANES] = halves[1]
    else:
        o_ref[0] = out


def dense_attention(src, cols, B, L, *, window=False, gqa=False, ctx=None, sink=None):
    W = src.shape[1]
    src3 = src.reshape(B, L, W)
    kv_w = SWA_KVW if gqa else BRANCH_W
    qc, kcol, vcol = cols
    in_specs = [pl.BlockSpec((1, ATTN_Q, BRANCH_W), lambda b, i: (b, i, qc)),
                pl.BlockSpec((1, L, kv_w), lambda b, i: (b, 0, kcol)),
                pl.BlockSpec((1, L, kv_w), lambda b, i: (b, 0, vcol))]
    args = [src3, src3, src3]
    if ctx is not None:
        P = ctx[0].shape[1]
        in_specs += [pl.BlockSpec((1, P, kv_w), lambda b, i: (b, 0, 0))] * 2
        args += list(ctx)
    if sink is not None:
        in_specs.append(pl.BlockSpec(memory_space=pltpu.SMEM))
        args.append(sink)
    return pl.pallas_call(
        functools.partial(_attn_kernel, L=L, window=window, gqa=gqa,
                          has_ctx=ctx is not None, has_sink=sink is not None),
        grid=(B, L // ATTN_Q),
        in_specs=in_specs,
        out_specs=pl.BlockSpec((1, ATTN_Q, BRANCH_W), lambda b, i: (b, i, 0)),
        out_shape=jax.ShapeDtypeStruct((B, L, BRANCH_W), F32),
        compiler_params=_cparams(48),
        name="window_attention" if window else "dense_attention",
    )(*args).reshape(B * L, BRANCH_W)


def _diff_kernel(*refs, lam_init, has_ctx):
    refs = list(refs)
    lam_ref, gain_ref, ones_ref, q_ref, k_ref, v_ref = refs[:6]
    if has_ctx:
        kc_ref, vc_ref, o_ref = refs[6:]
    else:
        (o_ref,) = refs[6:]
    lv = lam_ref[...]
    lam = (jnp.exp(jnp.sum(lv[0:1] * lv[1:2], keepdims=True))
           - jnp.exp(jnp.sum(lv[2:3] * lv[3:4], keepdims=True)) + lam_init)
    kl = k_ref[0].astype(BF16)
    vl = v_ref[0].astype(BF16)
    if has_ctx:
        kc = kc_ref[0].astype(BF16)
        vc = vc_ref[0].astype(BF16)
    q = q_ref[0].astype(BF16)
    out = jnp.zeros(q.shape, F32)
    for h in range(N_HEADS):
        o = None
        for mp in range(2):
            mm = _lane_mask(BRANCH_W, h * HEAD_DIM + mp * DIFF_HD, DIFF_HD)
            qm = jnp.where(mm, q, 0.0)
            parts = [_dot_nt(qm, kl)]
            if has_ctx:
                parts.append(_dot_nt(qm, kc))
            ps, l = _softmax_parts(parts)
            pv = _dot(ps[0].astype(BF16), vl)
            if has_ctx:
                pv = pv + _dot(ps[1].astype(BF16), vc)
            o = pv / l if mp == 0 else o - pv * (lam / l)
        out = out + jnp.where(_lane_mask(BRANCH_W, h * HEAD_DIM, HEAD_DIM), o, 0.0)
    ms = _dot_exact(out * out, ones_ref[...]) * (1.0 / HEAD_DIM)
    o_ref[0] = out * lax.rsqrt(ms + EPS) * gain_ref[...] * (1.0 - lam_init)


def diff_attention(dif, lam_vec, subln, layer, B, L, ctx=None):
    lam_init = 0.8 - 0.6 * math.exp(-0.3 * layer)
    dif3 = dif.reshape(B, L, DIFF_W)
    head = jnp.arange(BRANCH_W) // HEAD_DIM
    ones = (head[:, None] == head[None, :]).astype(F32)
    gain = jnp.tile(subln.astype(F32), N_HEADS).reshape(1, BRANCH_W)
    tq = min(DIFF_Q, L)
    in_specs = [_resident((4, DIFF_HD)), _resident((1, BRANCH_W)), _resident((BRANCH_W, BRANCH_W)),
                pl.BlockSpec((1, tq, BRANCH_W), lambda b, i: (b, i, 0)),
                pl.BlockSpec((1, L, BRANCH_W), lambda b, i: (b, 0, 1)),
                pl.BlockSpec((1, L, BRANCH_W), lambda b, i: (b, 0, 2))]
    args = [lam_vec, gain, ones, dif3, dif3, dif3]
    if ctx is not None:
        P = ctx[0].shape[1]
        in_specs += [pl.BlockSpec((1, P, BRANCH_W), lambda b, i: (b, 0, 0))] * 2
        args += list(ctx)
    return pl.pallas_call(
        functools.partial(_diff_kernel, lam_init=lam_init, has_ctx=ctx is not None),
        grid=(B, L // tq),
        in_specs=in_specs,
        out_specs=pl.BlockSpec((1, tq, BRANCH_W), lambda b, i: (b, i, 0)),
        out_shape=jax.ShapeDtypeStruct((B, L, BRANCH_W), F32),
        compiler_params=_cparams(56),
        name="diff_attention",
    )(*args).reshape(B * L, BRANCH_W)


def _filter_kernel(z_ref, w1_ref, b1_ref, w2_ref, b2_ref, w3_ref, fr_ref, ld_ref, o_ref):
    z = z_ref[...]
    tn = z[:, 0:1]
    fr = fr_ref[0]
    g = jnp.sin(fr * (_dot_exact(z, w1_ref[0]) + b1_ref[0]))
    g = jnp.sin(fr * (_dot_exact(g, w2_ref[0]) + b2_ref[0]))
    hf = _dot_exact(g, w3_ref[0]) * jnp.exp(-jnp.exp(ld_ref[0]) * tn)
    row = lax.broadcasted_iota(jnp.int32, (z.shape[0], 1), 0)
    for o in range(HY_ORDER):
        pos = hf[:, (2 * o) * HY_WIDTH:(2 * o + 1) * HY_WIDTH]
        neg = jnp.where(row == 0, 0.0, hf[:, (2 * o + 1) * HY_WIDTH:(2 * o + 2) * HY_WIDTH])
        norm = (jnp.sum(jnp.abs(pos), axis=0, keepdims=True)
                + jnp.sum(jnp.abs(neg), axis=0, keepdims=True) + EPS)
        o_ref[0, o, 0] = (neg + pos) / norm
        o_ref[0, o, 1] = (neg - pos) / norm


def hyena_filters(L, p):
    tn = jnp.arange(L, dtype=F32) / L
    ang = 2.0 * math.pi * tn[:, None] * jnp.arange(1, HY_FREQS + 1, dtype=F32)[None, :]
    z = jnp.concatenate([tn[:, None], jnp.cos(ang), jnp.sin(ang)], axis=-1)
    z = jnp.pad(z, ((0, 0), (0, HY_HIDDEN - HY_EMB)))
    w1 = jnp.pad(p['hy_w1'], ((0, 0), (0, HY_HIDDEN - HY_EMB), (0, 0)))
    fw = HY_ORDER * 2 * HY_WIDTH
    per_layer = lambda *shape: pl.BlockSpec((1,) + shape, lambda l: (l,) + (0,) * len(shape))
    return pl.pallas_call(
        _filter_kernel,
        grid=(DEPTH,),
        in_specs=[pl.BlockSpec((L, HY_HIDDEN), lambda l: (0, 0)),
                  per_layer(HY_HIDDEN, HY_HIDDEN), per_layer(1, HY_HIDDEN),
                  per_layer(HY_HIDDEN, HY_HIDDEN), per_layer(1, HY_HIDDEN),
                  per_layer(HY_HIDDEN, fw), per_layer(1, HY_HIDDEN), per_layer(1, fw)],
        out_specs=per_layer(HY_ORDER, 2, L, HY_WIDTH),
        out_shape=jax.ShapeDtypeStruct((DEPTH, HY_ORDER, 2, L, HY_WIDTH), F32),
        compiler_params=_cparams(56),
        name="hyena_filters",
    )(z, w1, p['hy_b1'].reshape(DEPTH, 1, HY_HIDDEN), p['hy_w2'], p['hy_b2'].reshape(DEPTH, 1, HY_HIDDEN),
      p['hy_w3'], p['hy_sin_freq'].reshape(DEPTH, 1, HY_HIDDEN), p['hy_log_decay'].reshape(DEPTH, 1, fw))


def dft_tables(L, tk):
    step = GRID_W
    k = 2 * jnp.arange(L, dtype=jnp.int32)[:, None] + 1

    def grid(n):
        ang = ((k * n[None, :]) % (4 * L)).astype(F32) * (math.pi / (2 * L))
        return jnp.cos(ang)[:, :, None], jnp.sin(ang)[:, :, None]

    (c_hi, s_hi) = grid(step * jnp.arange(L // step, dtype=jnp.int32))
    (c_lo, s_lo) = (t.reshape(L, 1, step) for t in grid(jnp.arange(step, dtype=jnp.int32)))
    c = (c_hi * c_lo - s_hi * s_lo).reshape(L, L)
    s = (s_hi * c_lo + c_hi * s_lo).reshape(L, L)
    fwd = jnp.concatenate([c.reshape(L // tk, tk, L), s.reshape(L // tk, tk, L)], axis=1).astype(BF16)
    inv = jnp.concatenate([c.T, -s.T], axis=1).astype(BF16)
    return fwd, inv


def _spectrum_kernel(t_ref, f_ref, o_ref):
    tk = o_ref.shape[2]
    o_ref[0, 0] = _dot(t_ref[0, :tk, :], f_ref[0, 0].astype(BF16))
    o_ref[0, 1] = _dot(t_ref[0, tk:, :], f_ref[0, 1].astype(BF16))


def filter_spectra(filt, fwd):
    G, _, L, C = filt.shape
    nkt, tk2, _ = fwd.shape
    tk = tk2 // 2
    return pl.pallas_call(
        _spectrum_kernel,
        grid=(nkt, G),
        in_specs=[pl.BlockSpec((1, tk2, L), lambda i, g: (i, 0, 0)),
                  pl.BlockSpec((1, 2, L, C), lambda i, g: (g, 0, 0, 0))],
        out_specs=pl.BlockSpec((1, 2, tk, C), lambda i, g: (g, 0, i, 0)),
        out_shape=jax.ShapeDtypeStruct((G, 2, L, C), F32),
        compiler_params=_cparams(48),
        name="filter_spectra",
    )(fwd, filt)


def _short_conv(u, w_ref, b_ref):
    L = u.shape[0]
    row = lax.broadcasted_iota(jnp.int32, (L, 1), 0)
    prev = jnp.where(row == 0, 0.0, pltpu.roll(u, 1, 0))
    nxt = jnp.where(row == L - 1, 0.0, pltpu.roll(u, L - 1, 0))
    return prev * w_ref[0:1, :] + u * w_ref[1:2, :] + nxt * w_ref[2:3, :] + b_ref[...]


def _conv_fwd_kernel(t_ref, y_ref, h_ref, *rest, first):
    if first:
        w_ref, b_ref, o_ref = rest
        y = _short_conv(y_ref[0], w_ref, b_ref)
    else:
        (o_ref,) = rest
        y = y_ref[0]
    y = y.astype(BF16)
    nkt, tk2, _ = t_ref.shape
    tk = tk2 // 2
    for i in range(nkt):
        rows = slice(i * tk, (i + 1) * tk)
        acc = _dot(t_ref[i], y)
        yc, ys = acc[:tk], acc[tk:]
        hr = h_ref[0, 0, rows, :]
        hi = h_ref[0, 1, rows, :]
        o_ref[0, 0, rows, :] = (yc * hr + ys * hi).astype(BF16)
        o_ref[0, 1, rows, :] = (yc * hi - ys * hr).astype(BF16)


def _col_spec(L, col):
    return pl.BlockSpec((1, L, HY_WIDTH), lambda b: (b, 0, col))


def _short_specs(col):
    return [pl.BlockSpec((3, HY_WIDTH), lambda b: (0, col)), pl.BlockSpec((1, HY_WIDTH), lambda b: (0, col))]


def conv_forward(y, spec, g, fwd, short=None):
    B, L, _ = y.shape
    C = HY_WIDTH
    first = short is not None
    in_specs = [_resident(fwd.shape), _col_spec(L, 0),
                pl.BlockSpec((1, 2, L, C), lambda b: (g, 0, 0, 0), pipeline_mode=pl.Buffered(1))]
    args = [fwd, y, spec]
    if first:
        in_specs += _short_specs(0)
        args += list(short)
    return pl.pallas_call(
        functools.partial(_conv_fwd_kernel, first=first),
        grid=(B,),
        in_specs=in_specs,
        out_specs=pl.BlockSpec((1, 2, L, C), lambda b: (b, 0, 0, 0)),
        out_shape=jax.ShapeDtypeStruct((B, 2, L, C), BF16),
        compiler_params=_cparams(56),
        name="conv_forward",
    )(*args)


def _conv_inv_kernel(t_ref, z_ref, g_ref, wg_ref, bg_ref, skip_ref, y_ref, *rest, first, tt):
    if first:
        wy_ref, by_ref, o_ref = rest
        y = _short_conv(y_ref[0], wy_ref, by_ref)
    else:
        (o_ref,) = rest
        y = y_ref[0]
    gate = _short_conv(g_ref[0], wg_ref, bg_ref)
    L, L2 = t_ref.shape
    z = z_ref[0].reshape(L2, HY_WIDTH)
    for i in range(L // tt):
        rows = slice(i * tt, (i + 1) * tt)
        conv = _dot(t_ref[rows, :], z) * (2.0 / L2)
        o_ref[0, rows, :] = gate[rows] * (conv + skip_ref[...] * y[rows])


def conv_inverse(zf, hy, order, short_w, short_b, skip, inv, tt, y=None):
    B, _, L, C = zf.shape
    first = y is None
    in_specs = [_resident(inv.shape),
                pl.BlockSpec((1, 2, L, C), lambda b: (b, 0, 0, 0)),
                _col_spec(L, 1 + order), *_short_specs(1 + order),
                pl.BlockSpec((1, C), lambda b: (0, 0)),
                _col_spec(L, 0)]
    args = [inv, zf, hy, short_w, short_b, skip.reshape(1, C), hy if first else y]
    if first:
        in_specs += _short_specs(0)
        args += [short_w, short_b]
    return pl.pallas_call(
        functools.partial(_conv_inv_kernel, first=first, tt=tt),
        grid=(B,),
        in_specs=in_specs,
        out_specs=pl.BlockSpec((1, L, C), lambda b: (b, 0, 0)),
        out_shape=jax.ShapeDtypeStruct((B, L, C), F32),
        compiler_params=_cparams(56),
        name="conv_inverse",
    )(*args)


def hyena(hy, lp, layer, spec, tabs, B, L):
    fwd, inv, tt = tabs
    hy = hy.reshape(B, L, HY_IN_W)
    short_w = lp['hy_short_w']
    short_b = lp['hy_short_b'].reshape(1, HY_IN_W)
    y = None
    for o in range(HY_ORDER):
        g = layer * HY_ORDER + o
        if y is None:
            zf = conv_forward(hy, spec, g, fwd, short=(short_w, short_b))
        else:
            zf = conv_forward(y, spec, g, fwd)
        y = conv_inverse(zf, hy, o, short_w, short_b, lp['hy_skip'][o], inv, tt, y=y)
    return y.reshape(B * L, HY_WIDTH)


def _merge_kernel(a_ref, b_ref, c_ref, d_ref, x_ref, mod_ref, nw_ref, wg_ref, wb_ref, wo_ref, o_ref):
    x = x_ref[...]
    h = _norm_mod(x, nw_ref[...], mod_ref[0, 3:4, :], mod_ref[0, 4:5, :]).astype(BF16)
    merged = None
    for i, br in enumerate((a_ref, b_ref, c_ref, d_ref)):
        gate = _sigmoid(_dot(h, wg_ref[:, MIX_W + i * D_MODEL:MIX_W + (i + 1) * D_MODEL]))
        t = gate * _dot(br[...].astype(BF16), wb_ref[i])
        merged = t if merged is None else merged + t
    o_ref[...] = x + mod_ref[0, 5:6, :] * _dot(merged.astype(BF16), wo_ref[...])


def merge_block(branches, x, mod, nw, wg, wb, wo, layer, rows_per_b):
    T = x.shape[0]
    tm = FFN_ROWS
    nb = mod.shape[0]
    row = lambda w: pl.BlockSpec((tm, w), lambda i: (i, 0))
    return pl.pallas_call(
        _merge_kernel,
        grid=(T // tm,),
        in_specs=[row(BRANCH_W)] * N_BRANCH + [row(D_MODEL), _mod_spec(nb, rows_per_b // tm),
                                               _resident((1, D_MODEL)),
                                               _resident_layer(wg, layer),
                                               _resident_layer(wb, layer),
                                               _resident_layer(wo, layer)],
        out_specs=row(D_MODEL),
        out_shape=jax.ShapeDtypeStruct((T, D_MODEL), F32),
        compiler_params=_cparams(56),
        name="merge_block",
    )(*branches, x, mod, nw.reshape(1, D_MODEL), wg, wb, wo)


def _heads_in(t):
    B, H, P, d = t.shape
    return t.transpose(0, 2, 1, 3).reshape(B, P, H * d)


def _run_pass(x, mod_all, p, wts, final_norm, B, L, spec, tabs, caches):
    ctx_pass = caches is None
    collected = []
    if not ctx_pass:
        rope = rope_tables(L, HEAD_DIM, SWA_QW + SWA_KVW) + rope_tables(L, DIFF_HD, 2 * BRANCH_W)
        bias_all = na_bias_tables(p['na_rpb'], L // GRID_W)
    for l in range(DEPTH):
        lp = {k: v[l] for k, v in p.items()}
        w = wts
        mod = mod_all[l]
        x = ffn_block(x, mod, lp['norm_ffn1'], w['ffn1_w1'], w['ffn1_w3'], w['ffn1_w2'], l, 0, L)
        na, swa, hy, dif = in_projection(x, mod, lp['norm_mix'], w['w_in'], l, L,
                                         None if ctx_pass else rope, F32 if ctx_pass else BF16)
        if ctx_pass:
            a_o = dense_attention(na, (0, 1, 2), B, L)
            b_o = dense_attention(swa, (0, 2, 3), B, L, gqa=True, sink=lp['swa_sink'])
            d_o = diff_attention(dif, lp['diff_lambda'], lp['diff_subln'], l, B, L)
            collected.append((na, swa, dif))
        else:
            ck_na, cv_na, ck_swa, cv_swa, ck_d, cv_d = (_heads_in(t[:, l]).astype(BF16) for t in caches)
            a_o = neighbourhood_attention(na, ck_na, cv_na, bias_all, l, B, L)
            b_o = dense_attention(swa, (0, 2, 3), B, L, window=True, gqa=True,
                                  ctx=(ck_swa, cv_swa), sink=lp['swa_sink'])
            d_o = diff_attention(dif, lp['diff_lambda'], lp['diff_subln'], l, B, L, ctx=(ck_d, cv_d))
        c_o = hyena(hy, lp, l, spec, tabs, B, L)
        x = merge_block((a_o, b_o, c_o, d_o), x, mod, lp['norm_mix'], w['w_in'], w['w_branch'], w['w_out'], l, L)
        x = ffn_block(x, mod, lp['norm_ffn2'], w['ffn2_w1'], w['ffn2_w3'], w['ffn2_w2'], l, 6, L,
                      final_w=final_norm if l == DEPTH - 1 else None)
    return x, collected


_CACHE_SLOTS = ((0, N_HEADS, N_HEADS), (0, 2 * N_HEADS, N_HEADS),
                (1, N_HEADS, SWA_KV_HEADS), (1, N_HEADS + SWA_KV_HEADS, SWA_KV_HEADS),
                (2, N_HEADS, N_HEADS), (2, 2 * N_HEADS, N_HEADS))


def _cache_kernel(*refs):
    srcs, outs = refs[:3 * DEPTH], refs[3 * DEPTH:]
    for l in range(DEPTH):
        for o_ref, (src, slot0, n) in zip(outs, _CACHE_SLOTS):
            x_ref = srcs[3 * l + src]
            for h in range(n):
                o_ref[0, l, h] = x_ref[:, (slot0 + h) * HEAD_DIM:(slot0 + h + 1) * HEAD_DIM]


def _new_caches(collected, B, L):
    srcs = [t for layer in collected for t in layer]
    return pl.pallas_call(
        _cache_kernel,
        grid=(B,),
        in_specs=[pl.BlockSpec((L, t.shape[1]), lambda b: (b, 0)) for t in srcs],
        out_specs=[pl.BlockSpec((1, DEPTH, n, L, HEAD_DIM), lambda b: (b, 0, 0, 0, 0)) for _, _, n in _CACHE_SLOTS],
        out_shape=[jax.ShapeDtypeStruct((B, DEPTH, n, L, HEAD_DIM), F32) for _, _, n in _CACHE_SLOTS],
        compiler_params=_cparams(48),
        name="cache_outputs",
    )(*srcs)


def _hyena_setup(L, p, tk, tt):
    fwd, inv = dft_tables(L, tk)
    filt = hyena_filters(L, p)
    spec = filter_spectra(filt.reshape(DEPTH * HY_ORDER, 2, L, HY_WIDTH), fwd)
    return spec, (fwd, inv, tt)


def kernel(x_prompt, x_sample, cache_na_k, cache_na_v, cache_swa_k, cache_swa_v, cache_diff_k, cache_diff_v, c, c_ctx, w_ada, b_ada, norm_ffn1, norm_mix, norm_ffn2, final_norm, ffn1_w1, ffn1_w3, ffn1_w2, ffn2_w1, ffn2_w3, ffn2_w2, w_in, w_branch, w_out, na_rpb, swa_sink, hy_short_w, hy_short_b, hy_w1, hy_b1, hy_w2, hy_b2, hy_w3, hy_sin_freq, hy_log_decay, hy_skip, diff_lambda, diff_subln):
    B_ctx, L_ctx, _ = x_prompt.shape
    B_den, L_den, _ = x_sample.shape
    p = {
        'norm_ffn1': norm_ffn1, 'norm_mix': norm_mix, 'norm_ffn2': norm_ffn2,
        'na_rpb': na_rpb, 'swa_sink': swa_sink, 'hy_short_w': hy_short_w, 'hy_short_b': hy_short_b,
        'hy_w1': hy_w1, 'hy_b1': hy_b1, 'hy_w2': hy_w2, 'hy_b2': hy_b2, 'hy_w3': hy_w3,
        'hy_sin_freq': hy_sin_freq, 'hy_log_decay': hy_log_decay, 'hy_skip': hy_skip,
        'diff_lambda': diff_lambda, 'diff_subln': diff_subln,
    }
    big = {'ffn1_w1': ffn1_w1, 'ffn1_w3': ffn1_w3, 'ffn1_w2': ffn1_w2, 'ffn2_w1': ffn2_w1,
           'ffn2_w3': ffn2_w3, 'ffn2_w2': ffn2_w2, 'w_in': w_in, 'w_branch': w_branch, 'w_out': w_out}
    wts = {k: v.astype(BF16) for k, v in big.items()}

    cond = jnp.concatenate([c, c_ctx[None, :]], axis=0)
    rows = 8 * ((cond.shape[0] + 7) // 8)
    cond = jnp.pad(cond, ((0, rows - cond.shape[0]), (0, 0)))
    mod = ada_modulation(cond, w_ada, b_ada).reshape(DEPTH, rows, N_MOD, D_MODEL)
    mod_den = mod[:, :B_den]
    mod_ctx = mod[:, B_den:B_den + 1]

    spec_c, tabs_c = _hyena_setup(L_ctx, p, min(L_ctx, 256), min(L_ctx, 256))
    y_ctx, collected = _run_pass(x_prompt.reshape(B_ctx * L_ctx, D_MODEL), mod_ctx, p, wts, final_norm,
                                 B_ctx, L_ctx, spec_c, tabs_c, None)
    new_caches = _new_caches(collected, B_ctx, L_ctx)

    spec_d, tabs_d = _hyena_setup(L_den, p, 512, 512)
    caches = (cache_na_k, cache_na_v, cache_swa_k, cache_swa_v, cache_diff_k, cache_diff_v)
    y_den, _ = _run_pass(x_sample.reshape(B_den * L_den, D_MODEL), mod_den, p, wts, final_norm,
                         B_den, L_den, spec_d, tabs_d, caches)
    return (y_ctx.reshape(B_ctx, L_ctx, D_MODEL), y_den.reshape(B_den, L_den, D_MODEL), *new_caches)
```

```python
import functools
import math

import jax
import jax.numpy as jnp
from jax import lax
from jax.experimental import pallas as pl
from jax.experimental.pallas import tpu as pltpu

F32 = jnp.float32
BF16 = jnp.bfloat16

D_MODEL = 1024
DEPTH = 4
GRID_W = 64
N_BRANCH = 4
BRANCH_W = D_MODEL // 4
HEAD_DIM = 64
N_HEADS = BRANCH_W // HEAD_DIM
NA_WIN_R = 8
NA_WIN_C = 16
SWA_KV_HEADS = N_HEADS // 2
SWA_WINDOW = 128
HY_WIDTH = BRANCH_W
HY_ORDER = 2
HY_FREQS = 16
HY_EMB = 1 + 2 * HY_FREQS
HY_HIDDEN = 64
DIFF_HD = 32
D_FF = 128 * ((8 * D_MODEL // 3 + 127) // 128)
ROPE_BASE = 10000.0
EPS = 1e-6
NEG = -1e30
N_MOD = 9
NA_W = 3 * BRANCH_W
SWA_QW = BRANCH_W
SWA_KVW = SWA_KV_HEADS * HEAD_DIM
SWA_W = SWA_QW + 2 * SWA_KVW
HY_IN_W = 3 * HY_WIDTH
DIFF_W = 3 * BRANCH_W
GATE_W = N_BRANCH * D_MODEL
MIX_W = NA_W + SWA_W + HY_IN_W + DIFF_W

LOG2E = math.log2(math.e)
QK_SCALE = HEAD_DIM ** -0.5 * LOG2E
DIFF_QK_SCALE = DIFF_HD ** -0.5 * LOG2E

LANES = 128
MXU_DIM = 256
MIB = 1024 * 1024

FFN_ROWS = 1024
MERGE_ROWS = 512
PROJ_ROWS = 512
FFN_CHUNK = 768
ATTN_Q = 256
DIFF_Q = 512
NA_Q_ROWS = ATTN_Q // GRID_W
NA_SLAB_ROWS = NA_Q_ROWS + NA_WIN_R
SWA_SLAB = ATTN_Q + 2 * SWA_WINDOW


def _cparams(vmem_mib):
    return pltpu.CompilerParams(vmem_limit_bytes=vmem_mib * MIB)


def _resident(shape):
    nd = len(shape)
    return pl.BlockSpec(shape, lambda *_: (0,) * nd, pipeline_mode=pl.Buffered(1))


def _resident_layer(stacked, layer):
    shape = stacked.shape[1:]
    return pl.BlockSpec((None,) + shape, lambda *_: (layer,) + (0,) * len(shape), pipeline_mode=pl.Buffered(1))


def _dot(a, b):
    return jnp.dot(a, b, preferred_element_type=F32)


def _dot_nt(a, b):
    return lax.dot_general(a, b, (((1,), (1,)), ((), ())), preferred_element_type=F32)


def _dot_exact(a, b):
    return jnp.dot(a, b, preferred_element_type=F32, precision=lax.Precision.HIGHEST)


def _sigmoid(x):
    return 1.0 / (1.0 + jnp.exp(-x))


def _norm_mod(x, nw, shift, scale):
    y = x * lax.rsqrt(jnp.mean(x * x, axis=-1, keepdims=True) + EPS) * nw
    return y * (1.0 + scale) + shift


def _mod_spec(nb, tiles_per_b):
    if nb == 1:
        return pl.BlockSpec((1, N_MOD, D_MODEL), lambda i: (0, 0, 0))
    return pl.BlockSpec((1, N_MOD, D_MODEL), lambda i: (i // tiles_per_b, 0, 0))


def _ada_kernel(c_ref, w_ref, b_ref, o_ref):
    c = c_ref[...]
    s = (c * _sigmoid(c)).astype(BF16)
    o_ref[0] = _dot(s, w_ref[0].astype(BF16)) + b_ref[0]


def ada_modulation(cond, w_ada, b_ada):
    rows = cond.shape[0]
    width = N_MOD * D_MODEL
    tn = 9 * LANES
    return pl.pallas_call(
        _ada_kernel,
        grid=(DEPTH, width // tn),
        in_specs=[pl.BlockSpec((rows, D_MODEL), lambda l, j: (0, 0)),
                  pl.BlockSpec((1, D_MODEL, tn), lambda l, j: (l, 0, j)),
                  pl.BlockSpec((1, 1, tn), lambda l, j: (l, 0, j))],
        out_specs=pl.BlockSpec((1, rows, tn), lambda l, j: (l, 0, j)),
        out_shape=jax.ShapeDtypeStruct((DEPTH, rows, width), F32),
        compiler_params=_cparams(32),
        name="ada_modulation",
    )(cond, w_ada, b_ada.reshape(DEPTH, 1, width))


def _ffn_kernel(x_ref, mod_ref, nw_ref, w1_ref, w3_ref, w2_ref, *rest, mod_base, final):
    if final:
        fw_ref, o_ref = rest
    else:
        (o_ref,) = rest
    x = x_ref[...]
    shift = mod_ref[0, mod_base:mod_base + 1, :]
    scale = mod_ref[0, mod_base + 1:mod_base + 2, :]
    gate = mod_ref[0, mod_base + 2:mod_base + 3, :]
    h = _norm_mod(x, nw_ref[...], shift, scale).astype(BF16)
    acc = jnp.zeros(x.shape, F32)
    for lo in range(0, D_FF, FFN_CHUNK):
        hi = min(lo + FFN_CHUNK, D_FF)
        a = _dot(h, w1_ref[:, lo:hi])
        b = _dot(h, w3_ref[:, lo:hi])
        u = (a * _sigmoid(a) * b).astype(BF16)
        acc = acc + _dot(u, w2_ref[lo:hi, :])
    y = x + 0.5 * gate * acc
    if final:
        y = y * lax.rsqrt(jnp.mean(y * y, axis=-1, keepdims=True) + EPS) * fw_ref[...]
    o_ref[...] = y


def ffn_block(x, mod, nw, w1, w3, w2, layer, mod_base, rows_per_b, final_w=None):
    T = x.shape[0]
    tm = FFN_ROWS
    nb = mod.shape[0]
    final = final_w is not None
    in_specs = [pl.BlockSpec((tm, D_MODEL), lambda i: (i, 0)),
                _mod_spec(nb, rows_per_b // tm),
                _resident((1, D_MODEL)),
                _resident_layer(w1, layer), _resident_layer(w3, layer), _resident_layer(w2, layer)]
    args = [x, mod, nw.reshape(1, D_MODEL), w1, w3, w2]
    if final:
        in_specs.append(_resident((1, D_MODEL)))
        args.append(final_w.reshape(1, D_MODEL))
    return pl.pallas_call(
        functools.partial(_ffn_kernel, mod_base=mod_base, final=final),
        grid=(T // tm,),
        in_specs=in_specs,
        out_specs=pl.BlockSpec((tm, D_MODEL), lambda i: (i, 0)),
        out_shape=jax.ShapeDtypeStruct((T, D_MODEL), F32),
        compiler_params=_cparams(56),
        name="ffn_block",
    )(*args)


def _rope_chunk(x, cos, sin_a, sin_b, dist):
    return x * cos + pltpu.roll(x, LANES - dist, 1) * sin_a + pltpu.roll(x, dist, 1) * sin_b


def _proj_kernel(x_ref, mod_ref, nw_ref, w_ref, *rest, rope):
    if rope:
        (cs_ref, sa_ref, sb_ref, cd_ref, da_ref, db_ref,
         na_ref, swa_ref, hy_ref, dif_ref) = rest
    else:
        na_ref, swa_ref, hy_ref, dif_ref = rest
    h = _norm_mod(x_ref[...], nw_ref[...], mod_ref[0, 3:4, :], mod_ref[0, 4:5, :]).astype(BF16)
    q_chunks = BRANCH_W // LANES
    o = 0
    s = _dot(h, w_ref[:, o:o + NA_W])
    o += NA_W
    na_ref[:, :BRANCH_W] = (s[:, :BRANCH_W] * QK_SCALE).astype(na_ref.dtype)
    na_ref[:, BRANCH_W:] = s[:, BRANCH_W:].astype(na_ref.dtype)
    s = _dot(h, w_ref[:, o:o + SWA_W])
    o += SWA_W
    n_rot = (SWA_QW + SWA_KVW) // LANES
    for c in range(SWA_W // LANES):
        sl = slice(c * LANES, (c + 1) * LANES)
        chunk = s[:, sl]
        if rope and c < n_rot:
            chunk = _rope_chunk(chunk, cs_ref[:, sl], sa_ref[:, sl], sb_ref[:, sl], HEAD_DIM // 4)
        if c < q_chunks:
            chunk = chunk * QK_SCALE
        swa_ref[:, sl] = chunk.astype(swa_ref.dtype)
    hy_ref[...] = _dot(h, w_ref[:, o:o + HY_IN_W])
    o += HY_IN_W
    s = _dot(h, w_ref[:, o:o + DIFF_W])
    o += DIFF_W
    n_rot = 2 * BRANCH_W // LANES
    for c in range(DIFF_W // LANES):
        sl = slice(c * LANES, (c + 1) * LANES)
        chunk = s[:, sl]
        if rope and c < n_rot:
            chunk = _rope_chunk(chunk, cd_ref[:, sl], da_ref[:, sl], db_ref[:, sl], DIFF_HD // 4)
        if c < q_chunks:
            chunk = chunk * DIFF_QK_SCALE
        dif_ref[:, sl] = chunk.astype(dif_ref.dtype)


def in_projection(x, mod, nw, w_in, layer, rows_per_b, rope_tabs, qkv_dtype):
    T = x.shape[0]
    tm = PROJ_ROWS
    nb = mod.shape[0]
    rope = rope_tabs is not None
    in_specs = [pl.BlockSpec((tm, D_MODEL), lambda i: (i, 0)),
                _mod_spec(nb, rows_per_b // tm),
                _resident((1, D_MODEL)),
                pl.BlockSpec((None, D_MODEL, MIX_W), lambda *_: (layer, 0, 0), pipeline_mode=pl.Buffered(1))]
    args = [x, mod, nw.reshape(1, D_MODEL), w_in]
    if rope:
        pos_tiles = rows_per_b // tm
        for t in rope_tabs:
            in_specs.append(pl.BlockSpec((tm, t.shape[1]), lambda i: (i % pos_tiles, 0)))
            args.append(t)
    widths = (NA_W, SWA_W, HY_IN_W, DIFF_W)
    dtypes = (qkv_dtype, qkv_dtype, F32, qkv_dtype)
    return pl.pallas_call(
        functools.partial(_proj_kernel, rope=rope),
        grid=(T // tm,),
        in_specs=in_specs,
        out_specs=[pl.BlockSpec((tm, w), lambda i: (i, 0)) for w in widths],
        out_shape=[jax.ShapeDtypeStruct((T, w), dt) for w, dt in zip(widths, dtypes)],
        compiler_params=_cparams(56),
        name="in_projection",
    )(*args)


def rope_tables(L, dh, width):
    nf = dh // 4
    t = jnp.arange(L)
    freqs = ROPE_BASE ** (-jnp.arange(nf, dtype=F32) / nf)
    ang_r = (t // GRID_W).astype(F32)[:, None] * freqs
    ang_c = (t % GRID_W).astype(F32)[:, None] * freqs
    cr, sr, cc, sc = jnp.cos(ang_r), jnp.sin(ang_r), jnp.cos(ang_c), jnp.sin(ang_c)
    z = jnp.zeros_like(sr)
    reps = width // dh
    cos = jnp.tile(jnp.concatenate([cr, cr, cc, cc], axis=-1), (1, reps))
    sin_a = jnp.tile(jnp.concatenate([-sr, z, -sc, z], axis=-1), (1, reps))
    sin_b = jnp.tile(jnp.concatenate([z, sr, z, sc], axis=-1), (1, reps))
    return cos, sin_a, sin_b


def _lane_mask(width, lo, n):
    lane = lax.broadcasted_iota(jnp.int32, (1, width), 1)
    return (lane >= lo) & (lane < lo + n)


def _softmax_parts(parts, extra_logit=None):
    m = parts[0].max(axis=-1, keepdims=True)
    for s in parts[1:]:
        m = jnp.maximum(m, s.max(axis=-1, keepdims=True))
    if extra_logit is not None:
        m = jnp.maximum(m, extra_logit)
    ps = [jnp.exp2(s - m) for s in parts]
    l = ps[0].sum(axis=-1, keepdims=True)
    for p in ps[1:]:
        l = l + p.sum(axis=-1, keepdims=True)
    if extra_logit is not None:
        l = l + jnp.exp2(extra_logit - m)
    return ps, l


def _na_kernel(q_ref, k_ref, v_ref, kc_ref, vc_ref, bias_ref, o_ref, *, n_rows):
    i = pl.program_id(1)
    row0 = jnp.clip(NA_Q_ROWS * i - NA_WIN_R // 2, 0, n_rows - NA_SLAB_ROWS)
    start = pl.multiple_of(row0 * GRID_W, GRID_W)
    slab = NA_SLAB_ROWS * GRID_W
    ks = k_ref[0, pl.ds(start, slab), :].astype(BF16)
    vs = v_ref[0, pl.ds(start, slab), :].astype(BF16)
    kc = kc_ref[0].astype(BF16)
    vc = vc_ref[0].astype(BF16)
    q = q_ref[0].astype(BF16)
    out = jnp.zeros(q.shape, F32)
    for h in range(N_HEADS):
        hm = _lane_mask(BRANCH_W, h * HEAD_DIM, HEAD_DIM)
        qh = jnp.where(hm, q, 0.0)
        s_win = _dot_nt(qh, ks) + bias_ref[0, 0, h]
        s_ctx = _dot_nt(qh, kc)
        (p_win, p_ctx), l = _softmax_parts([s_win, s_ctx])
        o = _dot(p_win.astype(BF16), vs) + _dot(p_ctx.astype(BF16), vc)
        out = out + jnp.where(hm, o / l, 0.0)
    o_ref[0] = out


NA_DROWS = 2 * NA_WIN_R - 1
NA_DCOLS = 2 * NA_WIN_C - 1


def _na_tile_geometry(n_rows):
    n_tiles = n_rows // NA_Q_ROWS

    def geometry(tile):
        slab0 = min(max(NA_Q_ROWS * tile - NA_WIN_R // 2, 0), n_rows - NA_SLAB_ROWS)
        rows = []
        for j in range(NA_Q_ROWS):
            qr = NA_Q_ROWS * tile + j
            r0 = min(max(qr - NA_WIN_R // 2, 0), n_rows - NA_WIN_R)
            rows.append([slab0 + m - qr + NA_WIN_R - 1 if r0 <= slab0 + m < r0 + NA_WIN_R else None
                         for m in range(NA_SLAB_ROWS)])
        return rows

    kinds = [geometry(0), geometry(1), geometry(n_tiles - 1)]
    assert all(geometry(t) == kinds[1] for t in range(1, n_tiles - 1))
    return kinds


def _na_bias_kernel(rpb_ref, o_ref, band_ref, *, kinds):
    base = (pl.program_id(0) * N_HEADS + pl.program_id(1)) * (NA_DROWS * NA_DCOLS)
    qc = lax.broadcasted_iota(jnp.int32, (GRID_W, LANES), 0)
    lane = lax.broadcasted_iota(jnp.int32, (GRID_W, LANES), 1)
    kc = lane % GRID_W
    d_col = kc - qc + (NA_WIN_C - 1)
    c0 = jnp.clip(qc - NA_WIN_C // 2, 0, GRID_W - NA_WIN_C)
    col_ok = (kc >= c0) & (kc < c0 + NA_WIN_C)
    for d in range(NA_DROWS):
        t = jnp.full((GRID_W, LANES), NEG, F32)
        for e in range(NA_DCOLS):
            t = jnp.where(d_col == e, rpb_ref[base + d * NA_DCOLS + e] * LOG2E, t)
        band_ref[d] = jnp.where(col_ok, t, NEG)
    masked = jnp.full((GRID_W, LANES), NEG, F32)
    left = lane < GRID_W
    for kind, rows in enumerate(kinds):
        for j, drow in enumerate(rows):
            for m in range(0, NA_SLAB_ROWS, 2):
                a = masked if drow[m] is None else band_ref[drow[m]]
                b = masked if drow[m + 1] is None else band_ref[drow[m + 1]]
                o_ref[0, kind, 0, j * GRID_W:(j + 1) * GRID_W, m * GRID_W:(m + 2) * GRID_W] = jnp.where(left, a, b)


def na_bias_tables(rpb, n_rows):
    slab = NA_SLAB_ROWS * GRID_W
    return pl.pallas_call(
        functools.partial(_na_bias_kernel, kinds=_na_tile_geometry(n_rows)),
        grid=(DEPTH, N_HEADS),
        in_specs=[pl.BlockSpec(memory_space=pltpu.SMEM)],
        out_specs=pl.BlockSpec((1, 3, 1, ATTN_Q, slab), lambda l, h: (l, 0, h, 0, 0)),
        out_shape=jax.ShapeDtypeStruct((DEPTH, 3, N_HEADS, ATTN_Q, slab), F32),
        scratch_shapes=[pltpu.VMEM((NA_DROWS, GRID_W, LANES), F32)],
        compiler_params=_cparams(32),
        name="na_bias_tables",
    )(rpb.astype(F32).reshape(-1))


def neighbourhood_attention(na, kc, vc, bias, layer, B, L):
    n_rows = L // GRID_W
    n_tiles = L // ATTN_Q
    na3 = na.reshape(B, L, NA_W)
    P = kc.shape[1]
    slab = NA_SLAB_ROWS * GRID_W

    def kind(b, i):
        return (layer, jnp.where(i == 0, 0, jnp.where(i == n_tiles - 1, 2, 1)), 0, 0, 0)

    return pl.pallas_call(
        functools.partial(_na_kernel, n_rows=n_rows),
        grid=(B, n_tiles),
        in_specs=[pl.BlockSpec((1, ATTN_Q, BRANCH_W), lambda b, i: (b, i, 0)),
                  pl.BlockSpec((1, L, BRANCH_W), lambda b, i: (b, 0, 1)),
                  pl.BlockSpec((1, L, BRANCH_W), lambda b, i: (b, 0, 2)),
                  pl.BlockSpec((1, P, BRANCH_W), lambda b, i: (b, 0, 0)),
                  pl.BlockSpec((1, P, BRANCH_W), lambda b, i: (b, 0, 0)),
                  pl.BlockSpec((1, 1, N_HEADS, ATTN_Q, slab), kind)],
        out_specs=pl.BlockSpec((1, ATTN_Q, BRANCH_W), lambda b, i: (b, i, 0)),
        out_shape=jax.ShapeDtypeStruct((B, L, BRANCH_W), F32),
        compiler_params=_cparams(48),
        name="neighbourhood_attention",
    )(na3, na3, na3, kc, vc, bias).reshape(B * L, BRANCH_W)


def _attn_kernel(*refs, L, window, gqa, has_ctx, has_sink):
    refs = list(refs)
    q_ref, k_ref, v_ref = refs[:3]
    pos = 3
    if has_ctx:
        kc_ref, vc_ref = refs[pos:pos + 2]
        pos += 2
    if has_sink:
        sink_ref = refs[pos]
        pos += 1
    o_ref = refs[pos]
    i = pl.program_id(1)
    if window:
        slab = SWA_SLAB
        start = pl.multiple_of(jnp.clip(i * ATTN_Q - SWA_WINDOW, 0, L - slab), SWA_WINDOW)
        q_pos = i * ATTN_Q + lax.broadcasted_iota(jnp.int32, (ATTN_Q, 1), 0)
        k_pos = start + lax.broadcasted_iota(jnp.int32, (1, slab), 1)
        ok = jnp.abs(k_pos - q_pos) <= SWA_WINDOW
        ks = k_ref[0, pl.ds(start, slab), :].astype(BF16)
        vs = v_ref[0, pl.ds(start, slab), :].astype(BF16)
    else:
        ks = k_ref[0].astype(BF16)
        vs = v_ref[0].astype(BF16)
    if has_ctx:
        kc = kc_ref[0].astype(BF16)
        vc = vc_ref[0].astype(BF16)
    q = q_ref[0].astype(F32)
    kv_w = ks.shape[1]
    halves = [jnp.zeros((ATTN_Q, LANES), F32), jnp.zeros((ATTN_Q, LANES), F32)]
    out = jnp.zeros(q.shape, F32)
    for h in range(N_HEADS):
        if gqa:
            kvh, slot = h // 2, h % 2
            qh = q[:, kvh * LANES:(kvh + 1) * LANES]
            if slot != kvh:
                qh = pltpu.roll(qh, HEAD_DIM, 1)
            hm = _lane_mask(kv_w, kvh * HEAD_DIM, HEAD_DIM)
        else:
            qh = q
            hm = _lane_mask(kv_w, h * HEAD_DIM, HEAD_DIM)
        qh = jnp.where(hm, qh, 0.0).astype(BF16)
        s = _dot_nt(qh, ks)
        if window:
            s = jnp.where(ok, s, NEG)
        parts = [s]
        if has_ctx:
            parts.append(_dot_nt(qh, kc))
        ps, l = _softmax_parts(parts, sink_ref[h] * LOG2E if has_sink else None)
        o = _dot(ps[0].astype(BF16), vs)
        if has_ctx:
            o = o + _dot(ps[1].astype(BF16), vc)
        o = jnp.where(hm, o / l, 0.0)
        if gqa:
            if slot != kvh:
                o = pltpu.roll(o, HEAD_DIM, 1)
            halves[kvh] = halves[kvh] + o
        else:
            out = out + o
    if gqa:
        o_ref[0, :, 0:LANES] = halves[0]
        o_ref[0, :, LANES:2 * LANES] = halves[1]
    else:
        o_ref[0] = out


def dense_attention(src, cols, B, L, *, window=False, gqa=False, ctx=None, sink=None):
    W = src.shape[1]
    src3 = src.reshape(B, L, W)
    kv_w = SWA_KVW if gqa else BRANCH_W
    qc, kcol, vcol = cols
    in_specs = [pl.BlockSpec((1, ATTN_Q, BRANCH_W), lambda b, i: (b, i, qc)),
                pl.BlockSpec((1, L, kv_w), lambda b, i: (b, 0, kcol)),
                pl.BlockSpec((1, L, kv_w), lambda b, i: (b, 0, vcol))]
    args = [src3, src3, src3]
    if ctx is not None:
        P = ctx[0].shape[1]
        in_specs += [pl.BlockSpec((1, P, kv_w), lambda b, i: (b, 0, 0))] * 2
        args += list(ctx)
    if sink is not None:
        in_specs.append(pl.BlockSpec(memory_space=pltpu.SMEM))
        args.append(sink)
    return pl.pallas_call(
        functools.partial(_attn_kernel, L=L, window=window, gqa=gqa,
                          has_ctx=ctx is not None, has_sink=sink is not None),
        grid=(B, L // ATTN_Q),
        in_specs=in_specs,
        out_specs=pl.BlockSpec((1, ATTN_Q, BRANCH_W), lambda b, i: (b, i, 0)),
        out_shape=jax.ShapeDtypeStruct((B, L, BRANCH_W), F32),
        compiler_params=_cparams(48),
        name="window_attention" if window else "dense_attention",
    )(*args).reshape(B * L, BRANCH_W)


def _diff_kernel(*refs, lam_init, has_ctx):
    refs = list(refs)
    lam_ref, gain_ref, ones_ref, q_ref, k_ref, v_ref = refs[:6]
    if has_ctx:
        kc_ref, vc_ref, o_ref = refs[6:]
    else:
        (o_ref,) = refs[6:]
    lv = lam_ref[...]
    lam = (jnp.exp(jnp.sum(lv[0:1] * lv[1:2], keepdims=True))
           - jnp.exp(jnp.sum(lv[2:3] * lv[3:4], keepdims=True)) + lam_init)
    kl = k_ref[0].astype(BF16)
    vl = v_ref[0].astype(BF16)
    if has_ctx:
        kc = kc_ref[0].astype(BF16)
        vc = vc_ref[0].astype(BF16)
    q = q_ref[0].astype(BF16)
    out = jnp.zeros(q.shape, F32)
    for h in range(N_HEADS):
        o = None
        for mp in range(2):
            mm = _lane_mask(BRANCH_W, h * HEAD_DIM + mp * DIFF_HD, DIFF_HD)
            qm = jnp.where(mm, q, 0.0)
            parts = [_dot_nt(qm, kl)]
            if has_ctx:
                parts.append(_dot_nt(qm, kc))
            ps, l = _softmax_parts(parts)
            pv = _dot(ps[0].astype(BF16), vl)
            if has_ctx:
                pv = pv + _dot(ps[1].astype(BF16), vc)
            o = pv / l if mp == 0 else o - pv * (lam / l)
        out = out + jnp.where(_lane_mask(BRANCH_W, h * HEAD_DIM, HEAD_DIM), o, 0.0)
    ms = _dot_exact(out * out, ones_ref[...]) * (1.0 / HEAD_DIM)
    o_ref[0] = out * lax.rsqrt(ms + EPS) * gain_ref[...] * (1.0 - lam_init)


def diff_attention(dif, lam_vec, subln, layer, B, L, ctx=None):
    lam_init = 0.8 - 0.6 * math.exp(-0.3 * layer)
    dif3 = dif.reshape(B, L, DIFF_W)
    head = jnp.arange(BRANCH_W) // HEAD_DIM
    ones = (head[:, None] == head[None, :]).astype(F32)
    gain = jnp.tile(subln.astype(F32), N_HEADS).reshape(1, BRANCH_W)
    tq = min(DIFF_Q, L)
    in_specs = [_resident((4, DIFF_HD)), _resident((1, BRANCH_W)), _resident((BRANCH_W, BRANCH_W)),
                pl.BlockSpec((1, tq, BRANCH_W), lambda b, i: (b, i, 0)),
                pl.BlockSpec((1, L, BRANCH_W), lambda b, i: (b, 0, 1)),
                pl.BlockSpec((1, L, BRANCH_W), lambda b, i: (b, 0, 2))]
    args = [lam_vec, gain, ones, dif3, dif3, dif3]
    if ctx is not None:
        P = ctx[0].shape[1]
        in_specs += [pl.BlockSpec((1, P, BRANCH_W), lambda b, i: (b, 0, 0))] * 2
        args += list(ctx)
    return pl.pallas_call(
        functools.partial(_diff_kernel, lam_init=lam_init, has_ctx=ctx is not None),
        grid=(B, L // tq),
        in_specs=in_specs,
        out_specs=pl.BlockSpec((1, tq, BRANCH_W), lambda b, i: (b, i, 0)),
        out_shape=jax.ShapeDtypeStruct((B, L, BRANCH_W), F32),
        compiler_params=_cparams(56),
        name="diff_attention",
    )(*args).reshape(B * L, BRANCH_W)


def _filter_kernel(z_ref, w1_ref, b1_ref, w2_ref, b2_ref, w3_ref, fr_ref, ld_ref, o_ref):
    z = z_ref[...]
    tn = z[:, 0:1]
    fr = fr_ref[0]
    g = jnp.sin(fr * (_dot_exact(z, w1_ref[0]) + b1_ref[0]))
    g = jnp.sin(fr * (_dot_exact(g, w2_ref[0]) + b2_ref[0]))
    hf = _dot_exact(g, w3_ref[0]) * jnp.exp(-jnp.exp(ld_ref[0]) * tn)
    row = lax.broadcasted_iota(jnp.int32, (z.shape[0], 1), 0)
    for o in range(HY_ORDER):
        pos = hf[:, (2 * o) * HY_WIDTH:(2 * o + 1) * HY_WIDTH]
        neg = jnp.where(row == 0, 0.0, hf[:, (2 * o + 1) * HY_WIDTH:(2 * o + 2) * HY_WIDTH])
        norm = (jnp.sum(jnp.abs(pos), axis=0, keepdims=True)
                + jnp.sum(jnp.abs(neg), axis=0, keepdims=True) + EPS)
        o_ref[0, o, 0] = (neg + pos) / norm
        o_ref[0, o, 1] = (neg - pos) / norm


def hyena_filters(L, p):
    tn = jnp.arange(L, dtype=F32) / L
    ang = 2.0 * math.pi * tn[:, None] * jnp.arange(1, HY_FREQS + 1, dtype=F32)[None, :]
    z = jnp.concatenate([tn[:, None], jnp.cos(ang), jnp.sin(ang)], axis=-1)
    z = jnp.pad(z, ((0, 0), (0, HY_HIDDEN - HY_EMB)))
    w1 = jnp.pad(p['hy_w1'], ((0, 0), (0, HY_HIDDEN - HY_EMB), (0, 0)))
    fw = HY_ORDER * 2 * HY_WIDTH
    per_layer = lambda *shape: pl.BlockSpec((1,) + shape, lambda l: (l,) + (0,) * len(shape))
    return pl.pallas_call(
        _filter_kernel,
        grid=(DEPTH,),
        in_specs=[pl.BlockSpec((L, HY_HIDDEN), lambda l: (0, 0)),
                  per_layer(HY_HIDDEN, HY_HIDDEN), per_layer(1, HY_HIDDEN),
                  per_layer(HY_HIDDEN, HY_HIDDEN), per_layer(1, HY_HIDDEN),
                  per_layer(HY_HIDDEN, fw), per_layer(1, HY_HIDDEN), per_layer(1, fw)],
        out_specs=per_layer(HY_ORDER, 2, L, HY_WIDTH),
        out_shape=jax.ShapeDtypeStruct((DEPTH, HY_ORDER, 2, L, HY_WIDTH), F32),
        compiler_params=_cparams(56),
        name="hyena_filters",
    )(z, w1, p['hy_b1'].reshape(DEPTH, 1, HY_HIDDEN), p['hy_w2'], p['hy_b2'].reshape(DEPTH, 1, HY_HIDDEN),
      p['hy_w3'], p['hy_sin_freq'].reshape(DEPTH, 1, HY_HIDDEN), p['hy_log_decay'].reshape(DEPTH, 1, fw))


def dft_tables(L, tk):
    step = GRID_W
    k = 2 * jnp.arange(L, dtype=jnp.int32)[:, None] + 1

    def grid(n):
        ang = ((k * n[None, :]) % (4 * L)).astype(F32) * (math.pi / (2 * L))
        return jnp.cos(ang)[:, :, None], jnp.sin(ang)[:, :, None]

    (c_hi, s_hi) = grid(step * jnp.arange(L // step, dtype=jnp.int32))
    (c_lo, s_lo) = (t.reshape(L, 1, step) for t in grid(jnp.arange(step, dtype=jnp.int32)))
    c = (c_hi * c_lo - s_hi * s_lo).reshape(L, L)
    s = (s_hi * c_lo + c_hi * s_lo).reshape(L, L)
    fwd = jnp.concatenate([c.reshape(L // tk, tk, L), s.reshape(L // tk, tk, L)], axis=1).astype(BF16)
    inv = jnp.concatenate([c.T, -s.T], axis=1).astype(BF16)
    return fwd, inv


def _spectrum_kernel(t_ref, f_ref, o_ref):
    tk = o_ref.shape[2]
    o_ref[0, 0] = _dot(t_ref[0, :tk, :], f_ref[0, 0].astype(BF16))
    o_ref[0, 1] = _dot(t_ref[0, tk:, :], f_ref[0, 1].astype(BF16))


def filter_spectra(filt, fwd):
    G, _, L, C = filt.shape
    nkt, tk2, _ = fwd.shape
    tk = tk2 // 2
    return pl.pallas_call(
        _spectrum_kernel,
        grid=(nkt, G),
        in_specs=[pl.BlockSpec((1, tk2, L), lambda i, g: (i, 0, 0)),
                  pl.BlockSpec((1, 2, L, C), lambda i, g: (g, 0, 0, 0))],
        out_specs=pl.BlockSpec((1, 2, tk, C), lambda i, g: (g, 0, i, 0)),
        out_shape=jax.ShapeDtypeStruct((G, 2, L, C), F32),
        compiler_params=_cparams(48),
        name="filter_spectra",
    )(fwd, filt)


def _short_conv(u, w_ref, b_ref):
    L = u.shape[0]
    row = lax.broadcasted_iota(jnp.int32, (L, 1), 0)
    prev = jnp.where(row == 0, 0.0, pltpu.roll(u, 1, 0))
    nxt = jnp.where(row == L - 1, 0.0, pltpu.roll(u, L - 1, 0))
    return prev * w_ref[0:1, :] + u * w_ref[1:2, :] + nxt * w_ref[2:3, :] + b_ref[...]


def _conv_fwd_kernel(t_ref, y_ref, h_ref, *rest, first):
    if first:
        w_ref, b_ref, o_ref = rest
        y = _short_conv(y_ref[0], w_ref, b_ref)
    else:
        (o_ref,) = rest
        y = y_ref[0]
    y = y.astype(BF16)
    nkt, tk2, _ = t_ref.shape
    tk = tk2 // 2
    for i in range(nkt):
        rows = slice(i * tk, (i + 1) * tk)
        acc = _dot(t_ref[i], y)
        yc, ys = acc[:tk], acc[tk:]
        hr = h_ref[0, 0, rows, :]
        hi = h_ref[0, 1, rows, :]
        o_ref[0, 0, rows, :] = (yc * hr + ys * hi).astype(BF16)
        o_ref[0, 1, rows, :] = (yc * hi - ys * hr).astype(BF16)


def _col_spec(L, col):
    return pl.BlockSpec((1, L, HY_WIDTH), lambda b: (b, 0, col))


def _short_specs(col):
    return [pl.BlockSpec((3, HY_WIDTH), lambda b: (0, col)), pl.BlockSpec((1, HY_WIDTH), lambda b: (0, col))]


def conv_forward(y, spec, g, fwd, short=None):
    B, L, _ = y.shape
    C = HY_WIDTH
    first = short is not None
    in_specs = [_resident(fwd.shape), _col_spec(L, 0),
                pl.BlockSpec((1, 2, L, C), lambda b: (g, 0, 0, 0), pipeline_mode=pl.Buffered(1))]
    args = [fwd, y, spec]
    if first:
        in_specs += _short_specs(0)
        args += list(short)
    return pl.pallas_call(
        functools.partial(_conv_fwd_kernel, first=first),
        grid=(B,),
        in_specs=in_specs,
        out_specs=pl.BlockSpec((1, 2, L, C), lambda b: (b, 0, 0, 0)),
        out_shape=jax.ShapeDtypeStruct((B, 2, L, C), BF16),
        compiler_params=_cparams(56),
        name="conv_forward",
    )(*args)


def _conv_inv_kernel(t_ref, z_ref, g_ref, wg_ref, bg_ref, skip_ref, y_ref, *rest, first, tt):
    if first:
        wy_ref, by_ref, o_ref = rest
        y = _short_conv(y_ref[0], wy_ref, by_ref)
    else:
        (o_ref,) = rest
        y = y_ref[0]
    gate = _short_conv(g_ref[0], wg_ref, bg_ref)
    L, L2 = t_ref.shape
    z = z_ref[0].reshape(L2, HY_WIDTH)
    for i in range(L // tt):
        rows = slice(i * tt, (i + 1) * tt)
        conv = _dot(t_ref[rows, :], z) * (2.0 / L2)
        o_ref[0, rows, :] = gate[rows] * (conv + skip_ref[...] * y[rows])


def conv_inverse(zf, hy, order, short_w, short_b, skip, inv, tt, y=None):
    B, _, L, C = zf.shape
    first = y is None
    in_specs = [_resident(inv.shape),
                pl.BlockSpec((1, 2, L, C), lambda b: (b, 0, 0, 0)),
                _col_spec(L, 1 + order), *_short_specs(1 + order),
                pl.BlockSpec((1, C), lambda b: (0, 0)),
                _col_spec(L, 0)]
    args = [inv, zf, hy, short_w, short_b, skip.reshape(1, C), hy if first else y]
    if first:
        in_specs += _short_specs(0)
        args += [short_w, short_b]
    return pl.pallas_call(
        functools.partial(_conv_inv_kernel, first=first, tt=tt),
        grid=(B,),
        in_specs=in_specs,
        out_specs=pl.BlockSpec((1, L, C), lambda b: (b, 0, 0)),
        out_shape=jax.ShapeDtypeStruct((B, L, C), F32),
        compiler_params=_cparams(56),
        name="conv_inverse",
    )(*args)


def hyena(hy, lp, layer, spec, tabs, B, L):
    fwd, inv, tt = tabs
    hy = hy.reshape(B, L, HY_IN_W)
    short_w = lp['hy_short_w']
    short_b = lp['hy_short_b'].reshape(1, HY_IN_W)
    y = None
    for o in range(HY_ORDER):
        g = layer * HY_ORDER + o
        if y is None:
            zf = conv_forward(hy, spec, g, fwd, short=(short_w, short_b))
        else:
            zf = conv_forward(y, spec, g, fwd)
        y = conv_inverse(zf, hy, o, short_w, short_b, lp['hy_skip'][o], inv, tt, y=y)
    return y.reshape(B * L, HY_WIDTH)


def _merge_kernel(a_ref, b_ref, c_ref, d_ref, x_ref, mod_ref, nw_ref, wg_ref, wb_ref, wo_ref, o_ref):
    x = x_ref[...]
    h = _norm_mod(x, nw_ref[...], mod_ref[0, 3:4, :], mod_ref[0, 4:5, :]).astype(BF16)
    merged = None
    for i, br in enumerate((a_ref, b_ref, c_ref, d_ref)):
        gate = _sigmoid(_dot(h, wg_ref[:, MIX_W + i * D_MODEL:MIX_W + (i + 1) * D_MODEL]))
        t = gate * _dot(br[...].astype(BF16), wb_ref[i])
        merged = t if merged is None else merged + t
    o_ref[...] = x + mod_ref[0, 5:6, :] * _dot(merged.astype(BF16), wo_ref[...])


def merge_block(branches, x, mod, nw, wg, wb, wo, layer, rows_per_b):
    T = x.shape[0]
    tm = MERGE_ROWS
    nb = mod.shape[0]
    row = lambda w: pl.BlockSpec((tm, w), lambda i: (i, 0))
    return pl.pallas_call(
        _merge_kernel,
        grid=(T // tm,),
        in_specs=[row(BRANCH_W)] * N_BRANCH + [row(D_MODEL), _mod_spec(nb, rows_per_b // tm),
                                               _resident((1, D_MODEL)),
                                               _resident_layer(wg, layer),
                                               _resident_layer(wb, layer),
                                               _resident_layer(wo, layer)],
        out_specs=row(D_MODEL),
        out_shape=jax.ShapeDtypeStruct((T, D_MODEL), F32),
        compiler_params=_cparams(56),
        name="merge_block",
    )(*branches, x, mod, nw.reshape(1, D_MODEL), wg, wb, wo)


def _heads_in(t):
    B, H, P, d = t.shape
    return t.transpose(0, 2, 1, 3).reshape(B, P, H * d)


def _run_pass(x, mod_all, p, wts, final_norm, B, L, spec, tabs, caches):
    ctx_pass = caches is None
    collected = []
    if not ctx_pass:
        rope = rope_tables(L, HEAD_DIM, SWA_QW + SWA_KVW) + rope_tables(L, DIFF_HD, 2 * BRANCH_W)
        bias_all = na_bias_tables(p['na_rpb'], L // GRID_W)
    for l in range(DEPTH):
        lp = {k: v[l] for k, v in p.items()}
        w = wts
        mod = mod_all[l]
        x = ffn_block(x, mod, lp['norm_ffn1'], w['ffn1_w1'], w['ffn1_w3'], w['ffn1_w2'], l, 0, L)
        na, swa, hy, dif = in_projection(x, mod, lp['norm_mix'], w['w_in'], l, L,
                                         None if ctx_pass else rope, F32 if ctx_pass else BF16)
        if ctx_pass:
            a_o = dense_attention(na, (0, 1, 2), B, L)
            b_o = dense_attention(swa, (0, 2, 3), B, L, gqa=True, sink=lp['swa_sink'])
            d_o = diff_attention(dif, lp['diff_lambda'], lp['diff_subln'], l, B, L)
            collected.append((na, swa, dif))
        else:
            ck_na, cv_na, ck_swa, cv_swa, ck_d, cv_d = (_heads_in(t[:, l]).astype(BF16) for t in caches)
            a_o = neighbourhood_attention(na, ck_na, cv_na, bias_all, l, B, L)
            b_o = dense_attention(swa, (0, 2, 3), B, L, window=True, gqa=True,
                                  ctx=(ck_swa, cv_swa), sink=lp['swa_sink'])
            d_o = diff_attention(dif, lp['diff_lambda'], lp['diff_subln'], l, B, L, ctx=(ck_d, cv_d))
        c_o = hyena(hy, lp, l, spec, tabs, B, L)
        x = merge_block((a_o, b_o, c_o, d_o), x, mod, lp['norm_mix'], w['w_in'], w['w_branch'], w['w_out'], l, L)
        x = ffn_block(x, mod, lp['norm_ffn2'], w['ffn2_w1'], w['ffn2_w3'], w['ffn2_w2'], l, 6, L,
                      final_w=final_norm if l == DEPTH - 1 else None)
    return x, collected


_CACHE_SLOTS = ((0, N_HEADS, N_HEADS), (0, 2 * N_HEADS, N_HEADS),
                (1, N_HEADS, SWA_KV_HEADS), (1, N_HEADS + SWA_KV_HEADS, SWA_KV_HEADS),
                (2, N_HEADS, N_HEADS), (2, 2 * N_HEADS, N_HEADS))


def _cache_kernel(*refs):
    srcs, outs = refs[:3 * DEPTH], refs[3 * DEPTH:]
    for l in range(DEPTH):
        for o_ref, (src, slot0, n) in zip(outs, _CACHE_SLOTS):
            x_ref = srcs[3 * l + src]
            for h in range(n):
                o_ref[0, l, h] = x_ref[:, (slot0 + h) * HEAD_DIM:(slot0 + h + 1) * HEAD_DIM]


def _new_caches(collected, B, L):
    srcs = [t for layer in collected for t in layer]
    return pl.pallas_call(
        _cache_kernel,
        grid=(B,),
        in_specs=[pl.BlockSpec((L, t.shape[1]), lambda b: (b, 0)) for t in srcs],
        out_specs=[pl.BlockSpec((1, DEPTH, n, L, HEAD_DIM), lambda b: (b, 0, 0, 0, 0)) for _, _, n in _CACHE_SLOTS],
        out_shape=[jax.ShapeDtypeStruct((B, DEPTH, n, L, HEAD_DIM), F32) for _, _, n in _CACHE_SLOTS],
        compiler_params=_cparams(48),
        name="cache_outputs",
    )(*srcs)


def _hyena_setup(L, p, tk, tt):
    fwd, inv = dft_tables(L, tk)
    filt = hyena_filters(L, p)
    spec = filter_spectra(filt.reshape(DEPTH * HY_ORDER, 2, L, HY_WIDTH), fwd)
    return spec, (fwd, inv, tt)


def kernel(x_prompt, x_sample, cache_na_k, cache_na_v, cache_swa_k, cache_swa_v, cache_diff_k, cache_diff_v, c, c_ctx, w_ada, b_ada, norm_ffn1, norm_mix, norm_ffn2, final_norm, ffn1_w1, ffn1_w3, ffn1_w2, ffn2_w1, ffn2_w3, ffn2_w2, w_in, w_branch, w_out, na_rpb, swa_sink, hy_short_w, hy_short_b, hy_w1, hy_b1, hy_w2, hy_b2, hy_w3, hy_sin_freq, hy_log_decay, hy_skip, diff_lambda, diff_subln):
    B_ctx, L_ctx, _ = x_prompt.shape
    B_den, L_den, _ = x_sample.shape
    p = {
        'norm_ffn1': norm_ffn1, 'norm_mix': norm_mix, 'norm_ffn2': norm_ffn2,
        'na_rpb': na_rpb, 'swa_sink': swa_sink, 'hy_short_w': hy_short_w, 'hy_short_b': hy_short_b,
        'hy_w1': hy_w1, 'hy_b1': hy_b1, 'hy_w2': hy_w2, 'hy_b2': hy_b2, 'hy_w3': hy_w3,
        'hy_sin_freq': hy_sin_freq, 'hy_log_decay': hy_log_decay, 'hy_skip': hy_skip,
        'diff_lambda': diff_lambda, 'diff_subln': diff_subln,
    }
    big = {'ffn1_w1': ffn1_w1, 'ffn1_w3': ffn1_w3, 'ffn1_w2': ffn1_w2, 'ffn2_w1': ffn2_w1,
           'ffn2_w3': ffn2_w3, 'ffn2_w2': ffn2_w2, 'w_in': w_in, 'w_branch': w_branch, 'w_out': w_out}
    wts = {k: v.astype(BF16) for k, v in big.items()}

    cond = jnp.concatenate([c, c_ctx[None, :]], axis=0)
    rows = 8 * ((cond.shape[0] + 7) // 8)
    cond = jnp.pad(cond, ((0, rows - cond.shape[0]), (0, 0)))
    mod = ada_modulation(cond, w_ada, b_ada).reshape(DEPTH, rows, N_MOD, D_MODEL)
    mod_den = mod[:, :B_den]
    mod_ctx = mod[:, B_den:B_den + 1]

    spec_c, tabs_c = _hyena_setup(L_ctx, p, min(L_ctx, 256), min(L_ctx, 256))
    y_ctx, collected = _run_pass(x_prompt.reshape(B_ctx * L_ctx, D_MODEL), mod_ctx, p, wts, final_norm,
                                 B_ctx, L_ctx, spec_c, tabs_c, None)
    new_caches = _new_caches(collected, B_ctx, L_ctx)

    spec_d, tabs_d = _hyena_setup(L_den, p, 512, 512)
    caches = (cache_na_k, cache_na_v, cache_swa_k, cache_swa_v, cache_diff_k, cache_diff_v)
    y_den, _ = _run_pass(x_sample.reshape(B_den * L_den, D_MODEL), mod_den, p, wts, final_norm,
                         B_den, L_den, spec_d, tabs_d, caches)
    return (y_ctx.reshape(B_ctx, L_ctx, D_MODEL), y_den.reshape(B_den, L_den, D_MODEL), *new_caches)
```

```python
import functools
import math

import jax
import jax.numpy as jnp
from jax import lax
from jax.experimental import pallas as pl
from jax.experimental.pallas import tpu as pltpu

F32 = jnp.float32
BF16 = jnp.bfloat16

D_MODEL = 1024
DEPTH = 4
GRID_W = 64
N_BRANCH = 4
BRANCH_W = D_MODEL // 4
HEAD_DIM = 64
N_HEADS = BRANCH_W // HEAD_DIM
NA_WIN_R = 8
NA_WIN_C = 16
SWA_KV_HEADS = N_HEADS // 2
SWA_WINDOW = 128
HY_WIDTH = BRANCH_W
HY_ORDER = 2
HY_FREQS = 16
HY_EMB = 1 + 2 * HY_FREQS
HY_HIDDEN = 64
DIFF_HD = 32
D_FF = 128 * ((8 * D_MODEL // 3 + 127) // 128)
ROPE_BASE = 10000.0
EPS = 1e-6
NEG = -1e30
N_MOD = 9
NA_W = 3 * BRANCH_W
SWA_QW = BRANCH_W
SWA_KVW = SWA_KV_HEADS * HEAD_DIM
SWA_W = SWA_QW + 2 * SWA_KVW
HY_IN_W = 3 * HY_WIDTH
DIFF_W = 3 * BRANCH_W
GATE_W = N_BRANCH * D_MODEL
MIX_W = NA_W + SWA_W + HY_IN_W + DIFF_W

LOG2E = math.log2(math.e)
QK_SCALE = HEAD_DIM ** -0.5 * LOG2E
DIFF_QK_SCALE = DIFF_HD ** -0.5 * LOG2E

LANES = 128
MXU_DIM = 256
MIB = 1024 * 1024

FFN_ROWS = 1024
FFN_SPLIT = 2
MERGE_ROWS = 512
PROJ_ROWS = 512
FFN_CHUNK = 768
ATTN_Q = 256
HY_FREQ_TILE = 512
DIFF_Q = 512
NA_Q_ROWS = ATTN_Q // GRID_W
NA_SLAB_ROWS = NA_Q_ROWS + NA_WIN_R
SWA_SLAB = ATTN_Q + 2 * SWA_WINDOW


def _cparams(vmem_mib):
    return pltpu.CompilerParams(vmem_limit_bytes=vmem_mib * MIB)


def _resident(shape):
    nd = len(shape)
    return pl.BlockSpec(shape, lambda *_: (0,) * nd, pipeline_mode=pl.Buffered(1))


def _resident_layer(stacked, layer):
    shape = stacked.shape[1:]
    return pl.BlockSpec((None,) + shape, lambda *_: (layer,) + (0,) * len(shape), pipeline_mode=pl.Buffered(1))


def _dot(a, b):
    return jnp.dot(a, b, preferred_element_type=F32)


def _dot_nt(a, b):
    return lax.dot_general(a, b, (((1,), (1,)), ((), ())), preferred_element_type=F32)


def _dot_exact(a, b):
    return jnp.dot(a, b, preferred_element_type=F32, precision=lax.Precision.HIGHEST)


def _sigmoid(x):
    return 1.0 / (1.0 + jnp.exp(-x))


def _norm_mod(x, nw, shift, scale):
    return x * lax.rsqrt(jnp.mean(x * x, axis=-1, keepdims=True) + EPS) * (nw * (1.0 + scale)) + shift


def _mod_spec(nb, tiles_per_b):
    if nb == 1:
        return pl.BlockSpec((1, N_MOD, D_MODEL), lambda i: (0, 0, 0))
    return pl.BlockSpec((1, N_MOD, D_MODEL), lambda i: (i // tiles_per_b, 0, 0))


def _ada_kernel(c_ref, w_ref, b_ref, o_ref):
    c = c_ref[...]
    s = (c * _sigmoid(c)).astype(BF16)
    o_ref[0] = _dot(s, w_ref[0].astype(BF16)) + b_ref[0]


def ada_modulation(cond, w_ada, b_ada):
    rows = cond.shape[0]
    width = N_MOD * D_MODEL
    tn = 9 * LANES
    return pl.pallas_call(
        _ada_kernel,
        grid=(DEPTH, width // tn),
        in_specs=[pl.BlockSpec((rows, D_MODEL), lambda l, j: (0, 0)),
                  pl.BlockSpec((1, D_MODEL, tn), lambda l, j: (l, 0, j)),
                  pl.BlockSpec((1, 1, tn), lambda l, j: (l, 0, j))],
        out_specs=pl.BlockSpec((1, rows, tn), lambda l, j: (l, 0, j)),
        out_shape=jax.ShapeDtypeStruct((DEPTH, rows, width), F32),
        compiler_params=_cparams(32),
        name="ada_modulation",
    )(cond, w_ada, b_ada.reshape(DEPTH, 1, width))


def _ffn_kernel(x_ref, mod_ref, nw_ref, w1_ref, w3_ref, w2_ref, *rest, mod_base, final):
    if final:
        fw_ref, o_ref = rest
    else:
        (o_ref,) = rest
    shift = mod_ref[0, mod_base:mod_base + 1, :]
    scale = mod_ref[0, mod_base + 1:mod_base + 2, :]
    gate = mod_ref[0, mod_base + 2:mod_base + 3, :]
    rows = x_ref.shape[0] // FFN_SPLIT
    groups = [slice(g * rows, (g + 1) * rows) for g in range(FFN_SPLIT)]
    hs = [_norm_mod(x_ref[g, :], nw_ref[...], shift, scale).astype(BF16) for g in groups]
    for g, h in zip(groups, hs):
        acc = jnp.zeros((rows, D_MODEL), F32)
        for lo in range(0, D_FF, FFN_CHUNK):
            hi = min(lo + FFN_CHUNK, D_FF)
            a = _dot(h, w1_ref[:, lo:hi])
            b = _dot(h, w3_ref[:, lo:hi])
            u = (a * _sigmoid(a) * b).astype(BF16)
            acc = acc + _dot(u, w2_ref[lo:hi, :])
        y = x_ref[g, :] + 0.5 * gate * acc
        if final:
            y = y * lax.rsqrt(jnp.mean(y * y, axis=-1, keepdims=True) + EPS) * fw_ref[...]
        o_ref[g, :] = y


def ffn_block(x, mod, nw, w1, w3, w2, layer, mod_base, rows_per_b, final_w=None):
    T = x.shape[0]
    tm = FFN_ROWS
    nb = mod.shape[0]
    final = final_w is not None
    in_specs = [pl.BlockSpec((tm, D_MODEL), lambda i: (i, 0)),
                _mod_spec(nb, rows_per_b // tm),
                _resident((1, D_MODEL)),
                _resident_layer(w1, layer), _resident_layer(w3, layer), _resident_layer(w2, layer)]
    args = [x, mod, nw.reshape(1, D_MODEL), w1, w3, w2]
    if final:
        in_specs.append(_resident((1, D_MODEL)))
        args.append(final_w.reshape(1, D_MODEL))
    return pl.pallas_call(
        functools.partial(_ffn_kernel, mod_base=mod_base, final=final),
        grid=(T // tm,),
        in_specs=in_specs,
        out_specs=pl.BlockSpec((tm, D_MODEL), lambda i: (i, 0)),
        out_shape=jax.ShapeDtypeStruct((T, D_MODEL), F32),
        compiler_params=_cparams(56),
        name="ffn_block",
    )(*args)


def _rope_chunk(x, cos, sin_a, sin_b, dist):
    return x * cos + pltpu.roll(x, LANES - dist, 1) * sin_a + pltpu.roll(x, dist, 1) * sin_b


def _proj_kernel(x_ref, mod_ref, nw_ref, w_ref, *rest, rope):
    if rope:
        (cs_ref, sa_ref, sb_ref, cd_ref, da_ref, db_ref,
         na_ref, swa_ref, hy_ref, dif_ref) = rest
    else:
        na_ref, swa_ref, hy_ref, dif_ref = rest
    h = _norm_mod(x_ref[...], nw_ref[...], mod_ref[0, 3:4, :], mod_ref[0, 4:5, :]).astype(BF16)
    q_chunks = BRANCH_W // LANES
    o = 0
    s = _dot(h, w_ref[:, o:o + NA_W])
    o += NA_W
    na_ref[:, :BRANCH_W] = (s[:, :BRANCH_W] * QK_SCALE).astype(na_ref.dtype)
    na_ref[:, BRANCH_W:] = s[:, BRANCH_W:].astype(na_ref.dtype)
    s = _dot(h, w_ref[:, o:o + SWA_W])
    o += SWA_W
    n_rot = (SWA_QW + SWA_KVW) // LANES
    for c in range(SWA_W // LANES):
        sl = slice(c * LANES, (c + 1) * LANES)
        chunk = s[:, sl]
        if rope and c < n_rot:
            chunk = _rope_chunk(chunk, cs_ref[:, sl], sa_ref[:, sl], sb_ref[:, sl], HEAD_DIM // 4)
        if c < q_chunks:
            chunk = chunk * QK_SCALE
        swa_ref[:, sl] = chunk.astype(swa_ref.dtype)
    hy_ref[...] = _dot(h, w_ref[:, o:o + HY_IN_W])
    o += HY_IN_W
    s = _dot(h, w_ref[:, o:o + DIFF_W])
    o += DIFF_W
    n_rot = 2 * BRANCH_W // LANES
    for c in range(DIFF_W // LANES):
        sl = slice(c * LANES, (c + 1) * LANES)
        chunk = s[:, sl]
        if rope and c < n_rot:
            chunk = _rope_chunk(chunk, cd_ref[:, sl], da_ref[:, sl], db_ref[:, sl], DIFF_HD // 4)
        if c < q_chunks:
            chunk = chunk * DIFF_QK_SCALE
        dif_ref[:, sl] = chunk.astype(dif_ref.dtype)


def in_projection(x, mod, nw, w_in, layer, rows_per_b, rope_tabs, qkv_dtype):
    T = x.shape[0]
    tm = PROJ_ROWS
    nb = mod.shape[0]
    rope = rope_tabs is not None
    in_specs = [pl.BlockSpec((tm, D_MODEL), lambda i: (i, 0)),
                _mod_spec(nb, rows_per_b // tm),
                _resident((1, D_MODEL)),
                pl.BlockSpec((None, D_MODEL, MIX_W), lambda *_: (layer, 0, 0), pipeline_mode=pl.Buffered(1))]
    args = [x, mod, nw.reshape(1, D_MODEL), w_in]
    if rope:
        pos_tiles = rows_per_b // tm
        for t in rope_tabs:
            in_specs.append(pl.BlockSpec((tm, t.shape[1]), lambda i: (i % pos_tiles, 0)))
            args.append(t)
    widths = (NA_W, SWA_W, HY_IN_W, DIFF_W)
    dtypes = (qkv_dtype, qkv_dtype, F32, qkv_dtype)
    return pl.pallas_call(
        functools.partial(_proj_kernel, rope=rope),
        grid=(T // tm,),
        in_specs=in_specs,
        out_specs=[pl.BlockSpec((tm, w), lambda i: (i, 0)) for w in widths],
        out_shape=[jax.ShapeDtypeStruct((T, w), dt) for w, dt in zip(widths, dtypes)],
        compiler_params=_cparams(56),
        name="in_projection",
    )(*args)


def rope_tables(L, dh, width):
    nf = dh // 4
    t = jnp.arange(L)
    freqs = ROPE_BASE ** (-jnp.arange(nf, dtype=F32) / nf)
    ang_r = (t // GRID_W).astype(F32)[:, None] * freqs
    ang_c = (t % GRID_W).astype(F32)[:, None] * freqs
    cr, sr, cc, sc = jnp.cos(ang_r), jnp.sin(ang_r), jnp.cos(ang_c), jnp.sin(ang_c)
    z = jnp.zeros_like(sr)
    reps = width // dh
    cos = jnp.tile(jnp.concatenate([cr, cr, cc, cc], axis=-1), (1, reps))
    sin_a = jnp.tile(jnp.concatenate([-sr, z, -sc, z], axis=-1), (1, reps))
    sin_b = jnp.tile(jnp.concatenate([z, sr, z, sc], axis=-1), (1, reps))
    return cos, sin_a, sin_b


def _lane_mask(width, lo, n):
    lane = lax.broadcasted_iota(jnp.int32, (1, width), 1)
    return (lane >= lo) & (lane < lo + n)


def _softmax_parts(parts, extra_logit=None):
    m = parts[0].max(axis=-1, keepdims=True)
    for s in parts[1:]:
        m = jnp.maximum(m, s.max(axis=-1, keepdims=True))
    if extra_logit is not None:
        m = jnp.maximum(m, extra_logit)
    ps = [jnp.exp2(s - m) for s in parts]
    l = ps[0].sum(axis=-1, keepdims=True)
    for p in ps[1:]:
        l = l + p.sum(axis=-1, keepdims=True)
    if extra_logit is not None:
        l = l + jnp.exp2(extra_logit - m)
    return ps, l


def _na_kernel(q_ref, k_ref, v_ref, kc_ref, vc_ref, bias_ref, o_ref, *, n_rows):
    i = pl.program_id(1)
    row0 = jnp.clip(NA_Q_ROWS * i - NA_WIN_R // 2, 0, n_rows - NA_SLAB_ROWS)
    start = pl.multiple_of(row0 * GRID_W, GRID_W)
    slab = NA_SLAB_ROWS * GRID_W
    ks = k_ref[0, pl.ds(start, slab), :].astype(BF16)
    vs = v_ref[0, pl.ds(start, slab), :].astype(BF16)
    kc = kc_ref[0].astype(BF16)
    vc = vc_ref[0].astype(BF16)
    q = q_ref[0].astype(BF16)
    out = jnp.zeros(q.shape, F32)
    for h in range(N_HEADS):
        hm = _lane_mask(BRANCH_W, h * HEAD_DIM, HEAD_DIM)
        qh = jnp.where(hm, q, 0.0)
        s_win = _dot_nt(qh, ks) + bias_ref[0, 0, h]
        s_ctx = _dot_nt(qh, kc)
        (p_win, p_ctx), l = _softmax_parts([s_win, s_ctx])
        o = _dot(p_win.astype(BF16), vs) + _dot(p_ctx.astype(BF16), vc)
        out = out + jnp.where(hm, o / l, 0.0)
    o_ref[0] = out


NA_DROWS = 2 * NA_WIN_R - 1
NA_DCOLS = 2 * NA_WIN_C - 1


def _na_tile_geometry(n_rows):
    n_tiles = n_rows // NA_Q_ROWS

    def geometry(tile):
        slab0 = min(max(NA_Q_ROWS * tile - NA_WIN_R // 2, 0), n_rows - NA_SLAB_ROWS)
        rows = []
        for j in range(NA_Q_ROWS):
            qr = NA_Q_ROWS * tile + j
            r0 = min(max(qr - NA_WIN_R // 2, 0), n_rows - NA_WIN_R)
            rows.append([slab0 + m - qr + NA_WIN_R - 1 if r0 <= slab0 + m < r0 + NA_WIN_R else None
                         for m in range(NA_SLAB_ROWS)])
        return rows

    kinds = [geometry(0), geometry(1), geometry(n_tiles - 1)]
    assert all(geometry(t) == kinds[1] for t in range(1, n_tiles - 1))
    return kinds


def _na_bias_kernel(rpb_ref, o_ref, band_ref, *, kinds):
    base = (pl.program_id(0) * N_HEADS + pl.program_id(1)) * (NA_DROWS * NA_DCOLS)
    qc = lax.broadcasted_iota(jnp.int32, (GRID_W, LANES), 0)
    lane = lax.broadcasted_iota(jnp.int32, (GRID_W, LANES), 1)
    kc = lane % GRID_W
    d_col = kc - qc + (NA_WIN_C - 1)
    c0 = jnp.clip(qc - NA_WIN_C // 2, 0, GRID_W - NA_WIN_C)
    col_ok = (kc >= c0) & (kc < c0 + NA_WIN_C)
    for d in range(NA_DROWS):
        t = jnp.full((GRID_W, LANES), NEG, F32)
        for e in range(NA_DCOLS):
            t = jnp.where(d_col == e, rpb_ref[base + d * NA_DCOLS + e] * LOG2E, t)
        band_ref[d] = jnp.where(col_ok, t, NEG)
    masked = jnp.full((GRID_W, LANES), NEG, F32)
    left = lane < GRID_W
    for kind, rows in enumerate(kinds):
        for j, drow in enumerate(rows):
            for m in range(0, NA_SLAB_ROWS, 2):
                a = masked if drow[m] is None else band_ref[drow[m]]
                b = masked if drow[m + 1] is None else band_ref[drow[m + 1]]
                o_ref[0, kind, 0, j * GRID_W:(j + 1) * GRID_W, m * GRID_W:(m + 2) * GRID_W] = jnp.where(left, a, b)


def na_bias_tables(rpb, n_rows):
    slab = NA_SLAB_ROWS * GRID_W
    return pl.pallas_call(
        functools.partial(_na_bias_kernel, kinds=_na_tile_geometry(n_rows)),
        grid=(DEPTH, N_HEADS),
        in_specs=[pl.BlockSpec(memory_space=pltpu.SMEM)],
        out_specs=pl.BlockSpec((1, 3, 1, ATTN_Q, slab), lambda l, h: (l, 0, h, 0, 0)),
        out_shape=jax.ShapeDtypeStruct((DEPTH, 3, N_HEADS, ATTN_Q, slab), F32),
        scratch_shapes=[pltpu.VMEM((NA_DROWS, GRID_W, LANES), F32)],
        compiler_params=_cparams(32),
        name="na_bias_tables",
    )(rpb.astype(F32).reshape(-1))


def neighbourhood_attention(na, kc, vc, bias, layer, B, L):
    n_rows = L // GRID_W
    n_tiles = L // ATTN_Q
    na3 = na.reshape(B, L, NA_W)
    P = kc.shape[1]
    slab = NA_SLAB_ROWS * GRID_W

    def kind(b, i):
        return (layer, jnp.where(i == 0, 0, jnp.where(i == n_tiles - 1, 2, 1)), 0, 0, 0)

    return pl.pallas_call(
        functools.partial(_na_kernel, n_rows=n_rows),
        grid=(B, n_tiles),
        in_specs=[pl.BlockSpec((1, ATTN_Q, BRANCH_W), lambda b, i: (b, i, 0)),
                  pl.BlockSpec((1, L, BRANCH_W), lambda b, i: (b, 0, 1)),
                  pl.BlockSpec((1, L, BRANCH_W), lambda b, i: (b, 0, 2)),
                  pl.BlockSpec((1, P, BRANCH_W), lambda b, i: (b, 0, 0)),
                  pl.BlockSpec((1, P, BRANCH_W), lambda b, i: (b, 0, 0)),
                  pl.BlockSpec((1, 1, N_HEADS, ATTN_Q, slab), kind)],
        out_specs=pl.BlockSpec((1, ATTN_Q, BRANCH_W), lambda b, i: (b, i, 0)),
        out_shape=jax.ShapeDtypeStruct((B, L, BRANCH_W), F32),
        compiler_params=_cparams(48),
        name="neighbourhood_attention",
    )(na3, na3, na3, kc, vc, bias).reshape(B * L, BRANCH_W)


def _attn_kernel(*refs, L, window, gqa, has_ctx, has_sink):
    refs = list(refs)
    q_ref, k_ref, v_ref = refs[:3]
    pos = 3
    if has_ctx:
        kc_ref, vc_ref = refs[pos:pos + 2]
        pos += 2
    if has_sink:
        sink_ref = refs[pos]
        pos += 1
    o_ref = refs[pos]
    i = pl.program_id(1)
    if window:
        slab = SWA_SLAB
        start = pl.multiple_of(jnp.clip(i * ATTN_Q - SWA_WINDOW, 0, L - slab), SWA_WINDOW)
        q_pos = i * ATTN_Q + lax.broadcasted_iota(jnp.int32, (ATTN_Q, 1), 0)
        k_pos = start + lax.broadcasted_iota(jnp.int32, (1, slab), 1)
        ok = jnp.abs(k_pos - q_pos) <= SWA_WINDOW
        ks = k_ref[0, pl.ds(start, slab), :].astype(BF16)
        vs = v_ref[0, pl.ds(start, slab), :].astype(BF16)
    else:
        ks = k_ref[0].astype(BF16)
        vs = v_ref[0].astype(BF16)
    if has_ctx:
        kc = kc_ref[0].astype(BF16)
        vc = vc_ref[0].astype(BF16)
    q = q_ref[0].astype(F32)
    kv_w = ks.shape[1]
    halves = [jnp.zeros((ATTN_Q, LANES), F32), jnp.zeros((ATTN_Q, LANES), F32)]
    out = jnp.zeros(q.shape, F32)
    for h in range(N_HEADS):
        if gqa:
            kvh, slot = h // 2, h % 2
            qh = q[:, kvh * LANES:(kvh + 1) * LANES]
            if slot != kvh:
                qh = pltpu.roll(qh, HEAD_DIM, 1)
            hm = _lane_mask(kv_w, kvh * HEAD_DIM, HEAD_DIM)
        else:
            qh = q
            hm = _lane_mask(kv_w, h * HEAD_DIM, HEAD_DIM)
        qh = jnp.where(hm, qh, 0.0).astype(BF16)
        s = _dot_nt(qh, ks)
        if window:
            s = jnp.where(ok, s, NEG)
        parts = [s]
        if has_ctx:
            parts.append(_dot_nt(qh, kc))
        ps, l = _softmax_parts(parts, sink_ref[h] * LOG2E if has_sink else None)
        o = _dot(ps[0].astype(BF16), vs)
        if has_ctx:
            o = o + _dot(ps[1].astype(BF16), vc)
        o = jnp.where(hm, o / l, 0.0)
        if gqa:
            if slot != kvh:
                o = pltpu.roll(o, HEAD_DIM, 1)
            halves[kvh] = halves[kvh] + o
        else:
            out = out + o
    if gqa:
        o_ref[0, :, 0:LANES] = halves[0]
        o_ref[0, :, LANES:2 * LANES] = halves[1]
    else:
        o_ref[0] = out


def dense_attention(src, cols, B, L, *, window=False, gqa=False, ctx=None, sink=None):
    W = src.shape[1]
    src3 = src.reshape(B, L, W)
    kv_w = SWA_KVW if gqa else BRANCH_W
    qc, kcol, vcol = cols
    in_specs = [pl.BlockSpec((1, ATTN_Q, BRANCH_W), lambda b, i: (b, i, qc)),
                pl.BlockSpec((1, L, kv_w), lambda b, i: (b, 0, kcol)),
                pl.BlockSpec((1, L, kv_w), lambda b, i: (b, 0, vcol))]
    args = [src3, src3, src3]
    if ctx is not None:
        P = ctx[0].shape[1]
        in_specs += [pl.BlockSpec((1, P, kv_w), lambda b, i: (b, 0, 0))] * 2
        args += list(ctx)
    if sink is not None:
        in_specs.append(pl.BlockSpec(memory_space=pltpu.SMEM))
        args.append(sink)
    return pl.pallas_call(
        functools.partial(_attn_kernel, L=L, window=window, gqa=gqa,
                          has_ctx=ctx is not None, has_sink=sink is not None),
        grid=(B, L // ATTN_Q),
        in_specs=in_specs,
        out_specs=pl.BlockSpec((1, ATTN_Q, BRANCH_W), lambda b, i: (b, i, 0)),
        out_shape=jax.ShapeDtypeStruct((B, L, BRANCH_W), F32),
        compiler_params=_cparams(48),
        name="window_attention" if window else "dense_attention",
    )(*args).reshape(B * L, BRANCH_W)


def _diff_kernel(*refs, lam_init, has_ctx):
    refs = list(refs)
    lam_ref, gain_ref, ones_ref, q_ref, k_ref, v_ref = refs[:6]
    if has_ctx:
        kc_ref, vc_ref, o_ref = refs[6:]
    else:
        (o_ref,) = refs[6:]
    lv = lam_ref[...]
    lam = (jnp.exp(jnp.sum(lv[0:1] * lv[1:2], keepdims=True))
           - jnp.exp(jnp.sum(lv[2:3] * lv[3:4], keepdims=True)) + lam_init)
    kl = k_ref[0].astype(BF16)
    vl = v_ref[0].astype(BF16)
    if has_ctx:
        kc = kc_ref[0].astype(BF16)
        vc = vc_ref[0].astype(BF16)
    q = q_ref[0].astype(BF16)
    out = jnp.zeros(q.shape, F32)
    for h in range(N_HEADS):
        o = None
        for mp in range(2):
            mm = _lane_mask(BRANCH_W, h * HEAD_DIM + mp * DIFF_HD, DIFF_HD)
            qm = jnp.where(mm, q, 0.0)
            parts = [_dot_nt(qm, kl)]
            if has_ctx:
                parts.append(_dot_nt(qm, kc))
            ps, l = _softmax_parts(parts)
            pv = _dot(ps[0].astype(BF16), vl)
            if has_ctx:
                pv = pv + _dot(ps[1].astype(BF16), vc)
            o = pv / l if mp == 0 else o - pv * (lam / l)
        out = out + jnp.where(_lane_mask(BRANCH_W, h * HEAD_DIM, HEAD_DIM), o, 0.0)
    ms = _dot_exact(out * out, ones_ref[...]) * (1.0 / HEAD_DIM)
    o_ref[0] = out * lax.rsqrt(ms + EPS) * gain_ref[...] * (1.0 - lam_init)


def diff_attention(dif, lam_vec, subln, layer, B, L, ctx=None):
    lam_init = 0.8 - 0.6 * math.exp(-0.3 * layer)
    dif3 = dif.reshape(B, L, DIFF_W)
    head = jnp.arange(BRANCH_W) // HEAD_DIM
    ones = (head[:, None] == head[None, :]).astype(F32)
    gain = jnp.tile(subln.astype(F32), N_HEADS).reshape(1, BRANCH_W)
    tq = min(DIFF_Q, L)
    in_specs = [_resident((4, DIFF_HD)), _resident((1, BRANCH_W)), _resident((BRANCH_W, BRANCH_W)),
                pl.BlockSpec((1, tq, BRANCH_W), lambda b, i: (b, i, 0)),
                pl.BlockSpec((1, L, BRANCH_W), lambda b, i: (b, 0, 1)),
                pl.BlockSpec((1, L, BRANCH_W), lambda b, i: (b, 0, 2))]
    args = [lam_vec, gain, ones, dif3, dif3, dif3]
    if ctx is not None:
        P = ctx[0].shape[1]
        in_specs += [pl.BlockSpec((1, P, BRANCH_W), lambda b, i: (b, 0, 0))] * 2
        args += list(ctx)
    return pl.pallas_call(
        functools.partial(_diff_kernel, lam_init=lam_init, has_ctx=ctx is not None),
        grid=(B, L // tq),
        in_specs=in_specs,
        out_specs=pl.BlockSpec((1, tq, BRANCH_W), lambda b, i: (b, i, 0)),
        out_shape=jax.ShapeDtypeStruct((B, L, BRANCH_W), F32),
        compiler_params=_cparams(56),
        name="diff_attention",
    )(*args).reshape(B * L, BRANCH_W)


def _filter_kernel(z_ref, w1_ref, b1_ref, w2_ref, b2_ref, w3_ref, fr_ref, ld_ref, o_ref):
    z = z_ref[...]
    tn = z[:, 0:1]
    fr = fr_ref[0]
    g = jnp.sin(fr * (_dot_exact(z, w1_ref[0]) + b1_ref[0]))
    g = jnp.sin(fr * (_dot_exact(g, w2_ref[0]) + b2_ref[0]))
    hf = _dot_exact(g, w3_ref[0]) * jnp.exp(-jnp.exp(ld_ref[0]) * tn)
    row = lax.broadcasted_iota(jnp.int32, (z.shape[0], 1), 0)
    for o in range(HY_ORDER):
        pos = hf[:, (2 * o) * HY_WIDTH:(2 * o + 1) * HY_WIDTH]
        neg = jnp.where(row == 0, 0.0, hf[:, (2 * o + 1) * HY_WIDTH:(2 * o + 2) * HY_WIDTH])
        norm = (jnp.sum(jnp.abs(pos), axis=0, keepdims=True)
                + jnp.sum(jnp.abs(neg), axis=0, keepdims=True) + EPS)
        o_ref[0, o, 0] = (neg + pos) / norm
        o_ref[0, o, 1] = (neg - pos) / norm


def hyena_filters(L, p):
    tn = jnp.arange(L, dtype=F32) / L
    ang = 2.0 * math.pi * tn[:, None] * jnp.arange(1, HY_FREQS + 1, dtype=F32)[None, :]
    z = jnp.concatenate([tn[:, None], jnp.cos(ang), jnp.sin(ang)], axis=-1)
    z = jnp.pad(z, ((0, 0), (0, HY_HIDDEN - HY_EMB)))
    w1 = jnp.pad(p['hy_w1'], ((0, 0), (0, HY_HIDDEN - HY_EMB), (0, 0)))
    fw = HY_ORDER * 2 * HY_WIDTH
    per_layer = lambda *shape: pl.BlockSpec((1,) + shape, lambda l: (l,) + (0,) * len(shape))
    return pl.pallas_call(
        _filter_kernel,
        grid=(DEPTH,),
        in_specs=[pl.BlockSpec((L, HY_HIDDEN), lambda l: (0, 0)),
                  per_layer(HY_HIDDEN, HY_HIDDEN), per_layer(1, HY_HIDDEN),
                  per_layer(HY_HIDDEN, HY_HIDDEN), per_layer(1, HY_HIDDEN),
                  per_layer(HY_HIDDEN, fw), per_layer(1, HY_HIDDEN), per_layer(1, fw)],
        out_specs=per_layer(HY_ORDER, 2, L, HY_WIDTH),
        out_shape=jax.ShapeDtypeStruct((DEPTH, HY_ORDER, 2, L, HY_WIDTH), F32),
        compiler_params=_cparams(56),
        name="hyena_filters",
    )(z, w1, p['hy_b1'].reshape(DEPTH, 1, HY_HIDDEN), p['hy_w2'], p['hy_b2'].reshape(DEPTH, 1, HY_HIDDEN),
      p['hy_w3'], p['hy_sin_freq'].reshape(DEPTH, 1, HY_HIDDEN), p['hy_log_decay'].reshape(DEPTH, 1, fw))


def dft_table(L, tk, half_sample):
    step = 2 * GRID_W
    k = 2 * jnp.arange(L, dtype=jnp.int32)[:, None] + 1

    def grid(m):
        ang = ((k * m[None, :]) % (8 * L)).astype(F32) * (math.pi / (4 * L))
        return jnp.cos(ang)[:, :, None], jnp.sin(ang)[:, :, None]

    (c_hi, s_hi) = grid(step * jnp.arange(2 * L // step, dtype=jnp.int32))
    lo = 2 * jnp.arange(step // 2, dtype=jnp.int32) + (1 if half_sample else 0)
    (c_lo, s_lo) = (t.reshape(L, 1, step // 2) for t in grid(lo))
    c = (c_hi * c_lo - s_hi * s_lo).reshape(L, L)
    s = (s_hi * c_lo + c_hi * s_lo).reshape(L, L)
    return jnp.concatenate([c.reshape(L // tk, tk, L), s.reshape(L // tk, tk, L)], axis=1).astype(BF16)


def _spectrum_kernel(t_ref, f_ref, o_ref):
    tk = o_ref.shape[2]
    o_ref[0, 0] = _dot(t_ref[0, :tk, :], f_ref[0, 0].astype(BF16))
    o_ref[0, 1] = _dot(t_ref[0, tk:, :], f_ref[0, 1].astype(BF16))


def filter_spectra(filt, fwd):
    G, _, L, C = filt.shape
    nkt, tk2, _ = fwd.shape
    tk = tk2 // 2
    return pl.pallas_call(
        _spectrum_kernel,
        grid=(nkt, G),
        in_specs=[pl.BlockSpec((1, tk2, L), lambda i, g: (i, 0, 0)),
                  pl.BlockSpec((1, 2, L, C), lambda i, g: (g, 0, 0, 0))],
        out_specs=pl.BlockSpec((1, 2, tk, C), lambda i, g: (g, 0, i, 0)),
        out_shape=jax.ShapeDtypeStruct((G, 2, L, C), F32),
        compiler_params=_cparams(48),
        name="filter_spectra",
    )(fwd, filt)


def _hyena_kernel(t_ref, h_ref, u_ref, w_ref, b_ref, skip_ref, o_ref, z_ref, y_ref):
    nkt, tk2, L = t_ref.shape
    tk = tk2 // 2
    C = HY_WIDTH
    row = lax.broadcasted_iota(jnp.int32, (L, 1), 0)

    def short_conv(col):
        sl = slice(col * C, (col + 1) * C)
        u = u_ref[0, :, sl]
        prev = jnp.where(row == 0, 0.0, pltpu.roll(u, 1, 0))
        nxt = jnp.where(row == L - 1, 0.0, pltpu.roll(u, L - 1, 0))
        return prev * w_ref[0:1, sl] + u * w_ref[1:2, sl] + nxt * w_ref[2:3, sl] + b_ref[:, sl]

    y_ref[...] = short_conv(0)
    for o in range(HY_ORDER):
        y = y_ref[...].astype(BF16)
        for i in range(nkt):
            rows = slice(i * tk, (i + 1) * tk)
            acc = _dot(t_ref[i], y)
            yc, ys = acc[:tk], acc[tk:]
            hr = h_ref[0, o, 0, rows, :]
            hi = h_ref[0, o, 1, rows, :]
            z_ref[0, rows, :] = (yc * hr + ys * hi).astype(BF16)
            z_ref[1, rows, :] = (yc * hi - ys * hr).astype(BF16)
        gate = short_conv(1 + o)
        zr, zi = z_ref[0], z_ref[1]
        for i in range(nkt):
            rows = slice(i * tk, (i + 1) * tk)
            conv = (_dot(t_ref[i, :tk, :], zr) - _dot(t_ref[i, tk:, :], zi)) * (1.0 / L)
            new = gate[rows] * (conv + skip_ref[o:o + 1, :] * y_ref[rows, :])
            if o == HY_ORDER - 1:
                o_ref[0, rows, :] = new
            else:
                y_ref[rows, :] = new


def hyena(hy, lp, layer, spec, table, B, L):
    C = HY_WIDTH
    return pl.pallas_call(
        _hyena_kernel,
        grid=(B,),
        in_specs=[_resident(table.shape),
                  pl.BlockSpec((1, HY_ORDER, 2, L, C), lambda b: (layer, 0, 0, 0, 0), pipeline_mode=pl.Buffered(1)),
                  pl.BlockSpec((1, L, HY_IN_W), lambda b: (b, 0, 0)),
                  _resident((3, HY_IN_W)), _resident((1, HY_IN_W)), _resident((HY_ORDER, C))],
        out_specs=pl.BlockSpec((1, L, C), lambda b: (b, 0, 0)),
        out_shape=jax.ShapeDtypeStruct((B, L, C), F32),
        scratch_shapes=[pltpu.VMEM((2, L, C), BF16), pltpu.VMEM((L, C), F32)],
        compiler_params=_cparams(56),
        name="hyena",
    )(table, spec, hy.reshape(B, L, HY_IN_W), lp['hy_short_w'], lp['hy_short_b'].reshape(1, HY_IN_W),
      lp['hy_skip']).reshape(B * L, C)


def _merge_kernel(a_ref, b_ref, c_ref, d_ref, x_ref, mod_ref, nw_ref, wg_ref, wb_ref, wo_ref, o_ref):
    x = x_ref[...]
    h = _norm_mod(x, nw_ref[...], mod_ref[0, 3:4, :], mod_ref[0, 4:5, :]).astype(BF16)
    merged = None
    for i, br in enumerate((a_ref, b_ref, c_ref, d_ref)):
        gate = _sigmoid(_dot(h, wg_ref[:, MIX_W + i * D_MODEL:MIX_W + (i + 1) * D_MODEL]))
        t = gate * _dot(br[...].astype(BF16), wb_ref[i])
        merged = t if merged is None else merged + t
    o_ref[...] = x + mod_ref[0, 5:6, :] * _dot(merged.astype(BF16), wo_ref[...])


def merge_block(branches, x, mod, nw, wg, wb, wo, layer, rows_per_b):
    T = x.shape[0]
    tm = MERGE_ROWS
    nb = mod.shape[0]
    row = lambda w: pl.BlockSpec((tm, w), lambda i: (i, 0))
    return pl.pallas_call(
        _merge_kernel,
        grid=(T // tm,),
        in_specs=[row(BRANCH_W)] * N_BRANCH + [row(D_MODEL), _mod_spec(nb, rows_per_b // tm),
                                               _resident((1, D_MODEL)),
                                               _resident_layer(wg, layer),
                                               _resident_layer(wb, layer),
                                               _resident_layer(wo, layer)],
        out_specs=row(D_MODEL),
        out_shape=jax.ShapeDtypeStruct((T, D_MODEL), F32),
        compiler_params=_cparams(56),
        name="merge_block",
    )(*branches, x, mod, nw.reshape(1, D_MODEL), wg, wb, wo)


def _heads_in(t):
    B, H, P, d = t.shape
    return t.transpose(0, 2, 1, 3).reshape(B, P, H * d)


def _run_pass(x, mod_all, p, wts, final_norm, B, L, spec, tabs, caches):
    ctx_pass = caches is None
    collected = []
    if not ctx_pass:
        rope = rope_tables(L, HEAD_DIM, SWA_QW + SWA_KVW) + rope_tables(L, DIFF_HD, 2 * BRANCH_W)
        bias_all = na_bias_tables(p['na_rpb'], L // GRID_W)
    for l in range(DEPTH):
        lp = {k: v[l] for k, v in p.items()}
        w = wts
        mod = mod_all[l]
        x = ffn_block(x, mod, lp['norm_ffn1'], w['ffn1_w1'], w['ffn1_w3'], w['ffn1_w2'], l, 0, L)
        na, swa, hy, dif = in_projection(x, mod, lp['norm_mix'], w['w_in'], l, L,
                                         None if ctx_pass else rope, F32 if ctx_pass else BF16)
        if ctx_pass:
            a_o = dense_attention(na, (0, 1, 2), B, L)
            b_o = dense_attention(swa, (0, 2, 3), B, L, gqa=True, sink=lp['swa_sink'])
            d_o = diff_attention(dif, lp['diff_lambda'], lp['diff_subln'], l, B, L)
            collected.append((na, swa, dif))
        else:
            ck_na, cv_na, ck_swa, cv_swa, ck_d, cv_d = (_heads_in(t[:, l]).astype(BF16) for t in caches)
            a_o = neighbourhood_attention(na, ck_na, cv_na, bias_all, l, B, L)
            b_o = dense_attention(swa, (0, 2, 3), B, L, window=True, gqa=True,
                                  ctx=(ck_swa, cv_swa), sink=lp['swa_sink'])
            d_o = diff_attention(dif, lp['diff_lambda'], lp['diff_subln'], l, B, L, ctx=(ck_d, cv_d))
        c_o = hyena(hy, lp, l, spec, tabs, B, L)
        x = merge_block((a_o, b_o, c_o, d_o), x, mod, lp['norm_mix'], w['w_in'], w['w_branch'], w['w_out'], l, L)
        x = ffn_block(x, mod, lp['norm_ffn2'], w['ffn2_w1'], w['ffn2_w3'], w['ffn2_w2'], l, 6, L,
                      final_w=final_norm if l == DEPTH - 1 else None)
    return x, collected


_CACHE_SLOTS = ((0, N_HEADS, N_HEADS), (0, 2 * N_HEADS, N_HEADS),
                (1, N_HEADS, SWA_KV_HEADS), (1, N_HEADS + SWA_KV_HEADS, SWA_KV_HEADS),
                (2, N_HEADS, N_HEADS), (2, 2 * N_HEADS, N_HEADS))


def _cache_kernel(*refs):
    srcs, outs = refs[:3 * DEPTH], refs[3 * DEPTH:]
    for l in range(DEPTH):
        for o_ref, (src, slot0, n) in zip(outs, _CACHE_SLOTS):
            x_ref = srcs[3 * l + src]
            for h in range(n):
                o_ref[0, l, h] = x_ref[:, (slot0 + h) * HEAD_DIM:(slot0 + h + 1) * HEAD_DIM]


def _new_caches(collected, B, L):
    srcs = [t for layer in collected for t in layer]
    return pl.pallas_call(
        _cache_kernel,
        grid=(B,),
        in_specs=[pl.BlockSpec((L, t.shape[1]), lambda b: (b, 0)) for t in srcs],
        out_specs=[pl.BlockSpec((1, DEPTH, n, L, HEAD_DIM), lambda b: (b, 0, 0, 0, 0)) for _, _, n in _CACHE_SLOTS],
        out_shape=[jax.ShapeDtypeStruct((B, DEPTH, n, L, HEAD_DIM), F32) for _, _, n in _CACHE_SLOTS],
        compiler_params=_cparams(48),
        name="cache_outputs",
    )(*srcs)


def _hyena_setup(L, p, tk):
    filt = hyena_filters(L, p)
    spec = filter_spectra(filt.reshape(DEPTH * HY_ORDER, 2, L, HY_WIDTH), dft_table(L, tk, half_sample=False))
    return spec.reshape(DEPTH, HY_ORDER, 2, L, HY_WIDTH), dft_table(L, tk, half_sample=True)


def kernel(x_prompt, x_sample, cache_na_k, cache_na_v, cache_swa_k, cache_swa_v, cache_diff_k, cache_diff_v, c, c_ctx, w_ada, b_ada, norm_ffn1, norm_mix, norm_ffn2, final_norm, ffn1_w1, ffn1_w3, ffn1_w2, ffn2_w1, ffn2_w3, ffn2_w2, w_in, w_branch, w_out, na_rpb, swa_sink, hy_short_w, hy_short_b, hy_w1, hy_b1, hy_w2, hy_b2, hy_w3, hy_sin_freq, hy_log_decay, hy_skip, diff_lambda, diff_subln):
    B_ctx, L_ctx, _ = x_prompt.shape
    B_den, L_den, _ = x_sample.shape
    p = {
        'norm_ffn1': norm_ffn1, 'norm_mix': norm_mix, 'norm_ffn2': norm_ffn2,
        'na_rpb': na_rpb, 'swa_sink': swa_sink, 'hy_short_w': hy_short_w, 'hy_short_b': hy_short_b,
        'hy_w1': hy_w1, 'hy_b1': hy_b1, 'hy_w2': hy_w2, 'hy_b2': hy_b2, 'hy_w3': hy_w3,
        'hy_sin_freq': hy_sin_freq, 'hy_log_decay': hy_log_decay, 'hy_skip': hy_skip,
        'diff_lambda': diff_lambda, 'diff_subln': diff_subln,
    }
    big = {'ffn1_w1': ffn1_w1, 'ffn1_w3': ffn1_w3, 'ffn1_w2': ffn1_w2, 'ffn2_w1': ffn2_w1,
           'ffn2_w3': ffn2_w3, 'ffn2_w2': ffn2_w2, 'w_in': w_in, 'w_branch': w_branch, 'w_out': w_out}
    wts = {k: v.astype(BF16) for k, v in big.items()}

    cond = jnp.concatenate([c, c_ctx[None, :]], axis=0)
    rows = 8 * ((cond.shape[0] + 7) // 8)
    cond = jnp.pad(cond, ((0, rows - cond.shape[0]), (0, 0)))
    mod = ada_modulation(cond, w_ada, b_ada).reshape(DEPTH, rows, N_MOD, D_MODEL)
    mod_den = mod[:, :B_den]
    mod_ctx = mod[:, B_den:B_den + 1]

    spec_c, tabs_c = _hyena_setup(L_ctx, p, min(L_ctx, HY_FREQ_TILE))
    y_ctx, collected = _run_pass(x_prompt.reshape(B_ctx * L_ctx, D_MODEL), mod_ctx, p, wts, final_norm,
                                 B_ctx, L_ctx, spec_c, tabs_c, None)
    new_caches = _new_caches(collected, B_ctx, L_ctx)

    spec_d, tabs_d = _hyena_setup(L_den, p, min(L_den, HY_FREQ_TILE))
    caches = (cache_na_k, cache_na_v, cache_swa_k, cache_swa_v, cache_diff_k, cache_diff_v)
    y_den, _ = _run_pass(x_sample.reshape(B_den * L_den, D_MODEL), mod_den, p, wts, final_norm,
                         B_den, L_den, spec_d, tabs_d, caches)
    return (y_ctx.reshape(B_ctx, L_ctx, D_MODEL), y_den.reshape(B_den, L_den, D_MODEL), *new_caches)
```

```python
import functools
import math

import jax
import jax.numpy as jnp
from jax import lax
from jax.experimental import pallas as pl
from jax.experimental.pallas import tpu as pltpu

F32 = jnp.float32
BF16 = jnp.bfloat16

D_MODEL = 1024
DEPTH = 4
GRID_W = 64
N_BRANCH = 4
BRANCH_W = D_MODEL // 4
HEAD_DIM = 64
N_HEADS = BRANCH_W // HEAD_DIM
NA_WIN_R = 8
NA_WIN_C = 16
SWA_KV_HEADS = N_HEADS // 2
SWA_WINDOW = 128
HY_WIDTH = BRANCH_W
HY_ORDER = 2
HY_FREQS = 16
HY_EMB = 1 + 2 * HY_FREQS
HY_HIDDEN = 64
DIFF_HD = 32
D_FF = 128 * ((8 * D_MODEL // 3 + 127) // 128)
ROPE_BASE = 10000.0
EPS = 1e-6
NEG = -1e30
N_MOD = 9
NA_W = 3 * BRANCH_W
SWA_QW = BRANCH_W
SWA_KVW = SWA_KV_HEADS * HEAD_DIM
SWA_W = SWA_QW + 2 * SWA_KVW
HY_IN_W = 3 * HY_WIDTH
DIFF_W = 3 * BRANCH_W
GATE_W = N_BRANCH * D_MODEL
MIX_W = NA_W + SWA_W + HY_IN_W + DIFF_W

LOG2E = math.log2(math.e)
QK_SCALE = HEAD_DIM ** -0.5 * LOG2E
DIFF_QK_SCALE = DIFF_HD ** -0.5 * LOG2E

LANES = 128
MXU_DIM = 256
MIB = 1024 * 1024

FFN_ROWS = 1024
FFN_SPLIT = 2
MERGE_ROWS = 512
PROJ_ROWS = 512
PROJ_SPLIT = 2
FFN_CHUNK = 768
ATTN_Q = 256
ATTN_TILES = 2
HY_FREQ_TILE = 512
DIFF_Q = 512
NA_Q_ROWS = ATTN_Q // GRID_W
NA_SLAB_ROWS = NA_Q_ROWS + NA_WIN_R
SWA_SLAB = ATTN_Q + 2 * SWA_WINDOW


def _cparams(vmem_mib):
    return pltpu.CompilerParams(vmem_limit_bytes=vmem_mib * MIB)


def _resident(shape):
    nd = len(shape)
    return pl.BlockSpec(shape, lambda *_: (0,) * nd, pipeline_mode=pl.Buffered(1))


def _resident_layer(stacked, layer):
    shape = stacked.shape[1:]
    return pl.BlockSpec((None,) + shape, lambda *_: (layer,) + (0,) * len(shape), pipeline_mode=pl.Buffered(1))


def _dot(a, b):
    return jnp.dot(a, b, preferred_element_type=F32)


def _dot_nt(a, b):
    return lax.dot_general(a, b, (((1,), (1,)), ((), ())), preferred_element_type=F32)


def _dot_exact(a, b):
    return jnp.dot(a, b, preferred_element_type=F32, precision=lax.Precision.HIGHEST)


def _sigmoid(x):
    return 1.0 / (1.0 + jnp.exp(-x))


def _norm_mod(x, nw, shift, scale):
    return x * lax.rsqrt(jnp.mean(x * x, axis=-1, keepdims=True) + EPS) * (nw * (1.0 + scale)) + shift


def _mod_spec(nb, tiles_per_b):
    if nb == 1:
        return pl.BlockSpec((1, N_MOD, D_MODEL), lambda i: (0, 0, 0))
    return pl.BlockSpec((1, N_MOD, D_MODEL), lambda i: (i // tiles_per_b, 0, 0))


def _ada_kernel(c_ref, w_ref, b_ref, o_ref):
    c = c_ref[...]
    s = (c * _sigmoid(c)).astype(BF16)
    o_ref[0] = _dot(s, w_ref[0].astype(BF16)) + b_ref[0]


def ada_modulation(cond, w_ada, b_ada):
    rows = cond.shape[0]
    width = N_MOD * D_MODEL
    tn = 9 * LANES
    return pl.pallas_call(
        _ada_kernel,
        grid=(DEPTH, width // tn),
        in_specs=[pl.BlockSpec((rows, D_MODEL), lambda l, j: (0, 0)),
                  pl.BlockSpec((1, D_MODEL, tn), lambda l, j: (l, 0, j)),
                  pl.BlockSpec((1, 1, tn), lambda l, j: (l, 0, j))],
        out_specs=pl.BlockSpec((1, rows, tn), lambda l, j: (l, 0, j)),
        out_shape=jax.ShapeDtypeStruct((DEPTH, rows, width), F32),
        compiler_params=_cparams(32),
        name="ada_modulation",
    )(cond, w_ada, b_ada.reshape(DEPTH, 1, width))


def _ffn_kernel(x_ref, mod_ref, nw_ref, w1_ref, w3_ref, w2_ref, *rest, mod_base, final):
    if final:
        fw_ref, o_ref = rest
    else:
        (o_ref,) = rest
    shift = mod_ref[0, mod_base:mod_base + 1, :]
    scale = mod_ref[0, mod_base + 1:mod_base + 2, :]
    gate = mod_ref[0, mod_base + 2:mod_base + 3, :]
    rows = x_ref.shape[0] // FFN_SPLIT
    groups = [slice(g * rows, (g + 1) * rows) for g in range(FFN_SPLIT)]
    hs = [_norm_mod(x_ref[g, :], nw_ref[...], shift, scale).astype(BF16) for g in groups]
    for g, h in zip(groups, hs):
        acc = jnp.zeros((rows, D_MODEL), F32)
        for lo in range(0, D_FF, FFN_CHUNK):
            hi = min(lo + FFN_CHUNK, D_FF)
            a = _dot(h, w1_ref[:, lo:hi])
            b = _dot(h, w3_ref[:, lo:hi])
            u = (a * _sigmoid(a) * b).astype(BF16)
            acc = acc + _dot(u, w2_ref[lo:hi, :])
        y = x_ref[g, :] + 0.5 * gate * acc
        if final:
            y = y * lax.rsqrt(jnp.mean(y * y, axis=-1, keepdims=True) + EPS) * fw_ref[...]
        o_ref[g, :] = y


def ffn_block(x, mod, nw, w1, w3, w2, layer, mod_base, rows_per_b, final_w=None):
    T = x.shape[0]
    tm = FFN_ROWS
    nb = mod.shape[0]
    final = final_w is not None
    in_specs = [pl.BlockSpec((tm, D_MODEL), lambda i: (i, 0)),
                _mod_spec(nb, rows_per_b // tm),
                _resident((1, D_MODEL)),
                _resident_layer(w1, layer), _resident_layer(w3, layer), _resident_layer(w2, layer)]
    args = [x, mod, nw.reshape(1, D_MODEL), w1, w3, w2]
    if final:
        in_specs.append(_resident((1, D_MODEL)))
        args.append(final_w.reshape(1, D_MODEL))
    return pl.pallas_call(
        functools.partial(_ffn_kernel, mod_base=mod_base, final=final),
        grid=(T // tm,),
        in_specs=in_specs,
        out_specs=pl.BlockSpec((tm, D_MODEL), lambda i: (i, 0)),
        out_shape=jax.ShapeDtypeStruct((T, D_MODEL), F32),
        compiler_params=_cparams(56),
        name="ffn_block",
    )(*args)


def _rope_chunk(x, cos, sin_a, sin_b, dist):
    return x * cos + pltpu.roll(x, LANES - dist, 1) * sin_a + pltpu.roll(x, dist, 1) * sin_b


def _proj_kernel(x_ref, mod_ref, nw_ref, w_ref, *rest, rope):
    if rope:
        (cs_ref, sa_ref, sb_ref, cd_ref, da_ref, db_ref,
         na_ref, swa_ref, hy_ref, dif_ref) = rest
    else:
        na_ref, swa_ref, hy_ref, dif_ref = rest
    rows = x_ref.shape[0] // PROJ_SPLIT
    groups = [slice(g * rows, (g + 1) * rows) for g in range(PROJ_SPLIT)]
    hs = [_norm_mod(x_ref[g, :], nw_ref[...], mod_ref[0, 3:4, :], mod_ref[0, 4:5, :]).astype(BF16) for g in groups]
    q_chunks = BRANCH_W // LANES
    for g, h in zip(groups, hs):
        o = 0
        s = _dot(h, w_ref[:, o:o + NA_W])
        o += NA_W
        na_ref[g, :BRANCH_W] = (s[:, :BRANCH_W] * QK_SCALE).astype(na_ref.dtype)
        na_ref[g, BRANCH_W:] = s[:, BRANCH_W:].astype(na_ref.dtype)
        s = _dot(h, w_ref[:, o:o + SWA_W])
        o += SWA_W
        n_rot = (SWA_QW + SWA_KVW) // LANES
        for c in range(SWA_W // LANES):
            sl = slice(c * LANES, (c + 1) * LANES)
            chunk = s[:, sl]
            if rope and c < n_rot:
                chunk = _rope_chunk(chunk, cs_ref[g, sl], sa_ref[g, sl], sb_ref[g, sl], HEAD_DIM // 4)
            if c < q_chunks:
                chunk = chunk * QK_SCALE
            swa_ref[g, sl] = chunk.astype(swa_ref.dtype)
        hy_ref[g, :] = _dot(h, w_ref[:, o:o + HY_IN_W])
        o += HY_IN_W
        s = _dot(h, w_ref[:, o:o + DIFF_W])
        o += DIFF_W
        n_rot = 2 * BRANCH_W // LANES
        for c in range(DIFF_W // LANES):
            sl = slice(c * LANES, (c + 1) * LANES)
            chunk = s[:, sl]
            if rope and c < n_rot:
                chunk = _rope_chunk(chunk, cd_ref[g, sl], da_ref[g, sl], db_ref[g, sl], DIFF_HD // 4)
            if c < q_chunks:
                chunk = chunk * DIFF_QK_SCALE
            dif_ref[g, sl] = chunk.astype(dif_ref.dtype)


def in_projection(x, mod, nw, w_in, layer, rows_per_b, rope_tabs, qkv_dtype):
    T = x.shape[0]
    tm = PROJ_ROWS
    nb = mod.shape[0]
    rope = rope_tabs is not None
    in_specs = [pl.BlockSpec((tm, D_MODEL), lambda i: (i, 0)),
                _mod_spec(nb, rows_per_b // tm),
                _resident((1, D_MODEL)),
                pl.BlockSpec((None, D_MODEL, MIX_W), lambda *_: (layer, 0, 0), pipeline_mode=pl.Buffered(1))]
    args = [x, mod, nw.reshape(1, D_MODEL), w_in]
    if rope:
        pos_tiles = rows_per_b // tm
        for t in rope_tabs:
            in_specs.append(pl.BlockSpec((tm, t.shape[1]), lambda i: (i % pos_tiles, 0)))
            args.append(t)
    widths = (NA_W, SWA_W, HY_IN_W, DIFF_W)
    dtypes = (qkv_dtype, qkv_dtype, F32, qkv_dtype)
    return pl.pallas_call(
        functools.partial(_proj_kernel, rope=rope),
        grid=(T // tm,),
        in_specs=in_specs,
        out_specs=[pl.BlockSpec((tm, w), lambda i: (i, 0)) for w in widths],
        out_shape=[jax.ShapeDtypeStruct((T, w), dt) for w, dt in zip(widths, dtypes)],
        compiler_params=_cparams(56),
        name="in_projection",
    )(*args)


def rope_tables(L, dh, width):
    nf = dh // 4
    t = jnp.arange(L)
    freqs = ROPE_BASE ** (-jnp.arange(nf, dtype=F32) / nf)
    ang_r = (t // GRID_W).astype(F32)[:, None] * freqs
    ang_c = (t % GRID_W).astype(F32)[:, None] * freqs
    cr, sr, cc, sc = jnp.cos(ang_r), jnp.sin(ang_r), jnp.cos(ang_c), jnp.sin(ang_c)
    z = jnp.zeros_like(sr)
    reps = width // dh
    cos = jnp.tile(jnp.concatenate([cr, cr, cc, cc], axis=-1), (1, reps))
    sin_a = jnp.tile(jnp.concatenate([-sr, z, -sc, z], axis=-1), (1, reps))
    sin_b = jnp.tile(jnp.concatenate([z, sr, z, sc], axis=-1), (1, reps))
    return cos, sin_a, sin_b


def _lane_mask(width, lo, n):
    lane = lax.broadcasted_iota(jnp.int32, (1, width), 1)
    return (lane >= lo) & (lane < lo + n)


def _softmax_parts(parts, extra_logit=None):
    m = parts[0].max(axis=-1, keepdims=True)
    for s in parts[1:]:
        m = jnp.maximum(m, s.max(axis=-1, keepdims=True))
    if extra_logit is not None:
        m = jnp.maximum(m, extra_logit)
    ps = [jnp.exp2(s - m) for s in parts]
    l = ps[0].sum(axis=-1, keepdims=True)
    for p in ps[1:]:
        l = l + p.sum(axis=-1, keepdims=True)
    if extra_logit is not None:
        l = l + jnp.exp2(extra_logit - m)
    return ps, l


def _na_kernel(q_ref, k_ref, v_ref, kc_ref, vc_ref, *rest, n_rows):
    bias_refs, o_ref = rest[:-1], rest[-1]
    slab = NA_SLAB_ROWS * GRID_W
    kc = kc_ref[0].astype(BF16)
    vc = vc_ref[0].astype(BF16)
    for sub, bias_ref in enumerate(bias_refs):
        tile = pl.program_id(1) * len(bias_refs) + sub
        rows = slice(sub * ATTN_Q, (sub + 1) * ATTN_Q)
        row0 = jnp.clip(NA_Q_ROWS * tile - NA_WIN_R // 2, 0, n_rows - NA_SLAB_ROWS)
        start = pl.multiple_of(row0 * GRID_W, GRID_W)
        ks = k_ref[0, pl.ds(start, slab), :].astype(BF16)
        vs = v_ref[0, pl.ds(start, slab), :].astype(BF16)
        q = q_ref[0, rows, :].astype(BF16)
        out = jnp.zeros(q.shape, F32)
        for h in range(N_HEADS):
            hm = _lane_mask(BRANCH_W, h * HEAD_DIM, HEAD_DIM)
            qh = jnp.where(hm, q, 0.0)
            s_win = _dot_nt(qh, ks) + bias_ref[0, 0, h]
            s_ctx = _dot_nt(qh, kc)
            (p_win, p_ctx), l = _softmax_parts([s_win, s_ctx])
            o = _dot(p_win.astype(BF16), vs) + _dot(p_ctx.astype(BF16), vc)
            out = out + jnp.where(hm, o / l, 0.0)
        o_ref[0, rows, :] = out


NA_DROWS = 2 * NA_WIN_R - 1
NA_DCOLS = 2 * NA_WIN_C - 1


def _na_tile_geometry(n_rows):
    n_tiles = n_rows // NA_Q_ROWS

    def geometry(tile):
        slab0 = min(max(NA_Q_ROWS * tile - NA_WIN_R // 2, 0), n_rows - NA_SLAB_ROWS)
        rows = []
        for j in range(NA_Q_ROWS):
            qr = NA_Q_ROWS * tile + j
            r0 = min(max(qr - NA_WIN_R // 2, 0), n_rows - NA_WIN_R)
            rows.append([slab0 + m - qr + NA_WIN_R - 1 if r0 <= slab0 + m < r0 + NA_WIN_R else None
                         for m in range(NA_SLAB_ROWS)])
        return rows

    kinds = [geometry(0), geometry(1), geometry(n_tiles - 1)]
    assert all(geometry(t) == kinds[1] for t in range(1, n_tiles - 1))
    return kinds


def _na_bias_kernel(rpb_ref, o_ref, band_ref, *, kinds):
    base = (pl.program_id(0) * N_HEADS + pl.program_id(1)) * (NA_DROWS * NA_DCOLS)
    qc = lax.broadcasted_iota(jnp.int32, (GRID_W, LANES), 0)
    lane = lax.broadcasted_iota(jnp.int32, (GRID_W, LANES), 1)
    kc = lane % GRID_W
    d_col = kc - qc + (NA_WIN_C - 1)
    c0 = jnp.clip(qc - NA_WIN_C // 2, 0, GRID_W - NA_WIN_C)
    col_ok = (kc >= c0) & (kc < c0 + NA_WIN_C)
    for d in range(NA_DROWS):
        t = jnp.full((GRID_W, LANES), NEG, F32)
        for e in range(NA_DCOLS):
            t = jnp.where(d_col == e, rpb_ref[base + d * NA_DCOLS + e] * LOG2E, t)
        band_ref[d] = jnp.where(col_ok, t, NEG)
    masked = jnp.full((GRID_W, LANES), NEG, F32)
    left = lane < GRID_W
    for kind, rows in enumerate(kinds):
        for j, drow in enumerate(rows):
            for m in range(0, NA_SLAB_ROWS, 2):
                a = masked if drow[m] is None else band_ref[drow[m]]
                b = masked if drow[m + 1] is None else band_ref[drow[m + 1]]
                o_ref[0, kind, 0, j * GRID_W:(j + 1) * GRID_W, m * GRID_W:(m + 2) * GRID_W] = jnp.where(left, a, b)


def na_bias_tables(rpb, n_rows):
    slab = NA_SLAB_ROWS * GRID_W
    return pl.pallas_call(
        functools.partial(_na_bias_kernel, kinds=_na_tile_geometry(n_rows)),
        grid=(DEPTH, N_HEADS),
        in_specs=[pl.BlockSpec(memory_space=pltpu.SMEM)],
        out_specs=pl.BlockSpec((1, 3, 1, ATTN_Q, slab), lambda l, h: (l, 0, h, 0, 0)),
        out_shape=jax.ShapeDtypeStruct((DEPTH, 3, N_HEADS, ATTN_Q, slab), F32),
        scratch_shapes=[pltpu.VMEM((NA_DROWS, GRID_W, LANES), F32)],
        compiler_params=_cparams(32),
        name="na_bias_tables",
    )(rpb.astype(F32).reshape(-1))


def neighbourhood_attention(na, kc, vc, bias, layer, B, L):
    n_rows = L // GRID_W
    n_tiles = L // ATTN_Q
    na3 = na.reshape(B, L, NA_W)
    P = kc.shape[1]
    slab = NA_SLAB_ROWS * GRID_W

    def kind(sub):
        def index(b, i):
            tile = i * ATTN_TILES + sub
            return (layer, jnp.where(tile == 0, 0, jnp.where(tile == n_tiles - 1, 2, 1)), 0, 0, 0)
        return index

    tq = ATTN_TILES * ATTN_Q
    return pl.pallas_call(
        functools.partial(_na_kernel, n_rows=n_rows),
        grid=(B, L // tq),
        in_specs=[pl.BlockSpec((1, tq, BRANCH_W), lambda b, i: (b, i, 0)),
                  pl.BlockSpec((1, L, BRANCH_W), lambda b, i: (b, 0, 1)),
                  pl.BlockSpec((1, L, BRANCH_W), lambda b, i: (b, 0, 2)),
                  pl.BlockSpec((1, P, BRANCH_W), lambda b, i: (b, 0, 0)),
                  pl.BlockSpec((1, P, BRANCH_W), lambda b, i: (b, 0, 0))]
                 + [pl.BlockSpec((1, 1, N_HEADS, ATTN_Q, slab), kind(sub)) for sub in range(ATTN_TILES)],
        out_specs=pl.BlockSpec((1, tq, BRANCH_W), lambda b, i: (b, i, 0)),
        out_shape=jax.ShapeDtypeStruct((B, L, BRANCH_W), F32),
        compiler_params=_cparams(56),
        name="neighbourhood_attention",
    )(na3, na3, na3, kc, vc, *([bias] * ATTN_TILES)).reshape(B * L, BRANCH_W)


def _attn_kernel(*refs, L, window, gqa, has_ctx, has_sink):
    refs = list(refs)
    q_ref, k_ref, v_ref = refs[:3]
    pos = 3
    if has_ctx:
        kc_ref, vc_ref = refs[pos:pos + 2]
        pos += 2
    if has_sink:
        sink_ref = refs[pos]
        pos += 1
    o_ref = refs[pos]
    if has_ctx:
        kc = kc_ref[0].astype(BF16)
        vc = vc_ref[0].astype(BF16)
    n_sub = q_ref.shape[1] // ATTN_Q
    for sub in range(n_sub):
        rows = slice(sub * ATTN_Q, (sub + 1) * ATTN_Q)
        if window:
            slab = SWA_SLAB
            q0 = (pl.program_id(1) * n_sub + sub) * ATTN_Q
            start = pl.multiple_of(jnp.clip(q0 - SWA_WINDOW, 0, L - slab), SWA_WINDOW)
            q_pos = q0 + lax.broadcasted_iota(jnp.int32, (ATTN_Q, 1), 0)
            k_pos = start + lax.broadcasted_iota(jnp.int32, (1, slab), 1)
            ok = jnp.abs(k_pos - q_pos) <= SWA_WINDOW
            ks = k_ref[0, pl.ds(start, slab), :].astype(BF16)
            vs = v_ref[0, pl.ds(start, slab), :].astype(BF16)
        else:
            ks = k_ref[0].astype(BF16)
            vs = v_ref[0].astype(BF16)
        q = q_ref[0, rows, :].astype(F32)
        kv_w = ks.shape[1]
        halves = [jnp.zeros((ATTN_Q, LANES), F32), jnp.zeros((ATTN_Q, LANES), F32)]
        out = jnp.zeros(q.shape, F32)
        for h in range(N_HEADS):
            if gqa:
                kvh, slot = h // 2, h % 2
                qh = q[:, kvh * LANES:(kvh + 1) * LANES]
                if slot != kvh:
                    qh = pltpu.roll(qh, HEAD_DIM, 1)
                hm = _lane_mask(kv_w, kvh * HEAD_DIM, HEAD_DIM)
            else:
                qh = q
                hm = _lane_mask(kv_w, h * HEAD_DIM, HEAD_DIM)
            qh = jnp.where(hm, qh, 0.0).astype(BF16)
            s = _dot_nt(qh, ks)
            if window:
                s = jnp.where(ok, s, NEG)
            parts = [s]
            if has_ctx:
                parts.append(_dot_nt(qh, kc))
            ps, l = _softmax_parts(parts, sink_ref[h] * LOG2E if has_sink else None)
            o = _dot(ps[0].astype(BF16), vs)
            if has_ctx:
                o = o + _dot(ps[1].astype(BF16), vc)
            o = jnp.where(hm, o / l, 0.0)
            if gqa:
                if slot != kvh:
                    o = pltpu.roll(o, HEAD_DIM, 1)
                halves[kvh] = halves[kvh] + o
            else:
                out = out + o
        if gqa:
            o_ref[0, rows, 0:LANES] = halves[0]
            o_ref[0, rows, LANES:2 * LANES] = halves[1]
        else:
            o_ref[0, rows, :] = out


def dense_attention(src, cols, B, L, *, window=False, gqa=False, ctx=None, sink=None):
    W = src.shape[1]
    src3 = src.reshape(B, L, W)
    kv_w = SWA_KVW if gqa else BRANCH_W
    qc, kcol, vcol = cols
    tq = min(ATTN_TILES * ATTN_Q, L)
    in_specs = [pl.BlockSpec((1, tq, BRANCH_W), lambda b, i: (b, i, qc)),
                pl.BlockSpec((1, L, kv_w), lambda b, i: (b, 0, kcol)),
                pl.BlockSpec((1, L, kv_w), lambda b, i: (b, 0, vcol))]
    args = [src3, src3, src3]
    if ctx is not None:
        P = ctx[0].shape[1]
        in_specs += [pl.BlockSpec((1, P, kv_w), lambda b, i: (b, 0, 0))] * 2
        args += list(ctx)
    if sink is not None:
        in_specs.append(pl.BlockSpec(memory_space=pltpu.SMEM))
        args.append(sink)
    return pl.pallas_call(
        functools.partial(_attn_kernel, L=L, window=window, gqa=gqa,
                          has_ctx=ctx is not None, has_sink=sink is not None),
        grid=(B, L // tq),
        in_specs=in_specs,
        out_specs=pl.BlockSpec((1, tq, BRANCH_W), lambda b, i: (b, i, 0)),
        out_shape=jax.ShapeDtypeStruct((B, L, BRANCH_W), F32),
        compiler_params=_cparams(48),
        name="window_attention" if window else "dense_attention",
    )(*args).reshape(B * L, BRANCH_W)


def _diff_kernel(*refs, lam_init, has_ctx):
    refs = list(refs)
    lam_ref, gain_ref, ones_ref, q_ref, k_ref, v_ref = refs[:6]
    if has_ctx:
        kc_ref, vc_ref, o_ref = refs[6:]
    else:
        (o_ref,) = refs[6:]
    lv = lam_ref[...]
    lam = (jnp.exp(jnp.sum(lv[0:1] * lv[1:2], keepdims=True))
           - jnp.exp(jnp.sum(lv[2:3] * lv[3:4], keepdims=True)) + lam_init)
    kl = k_ref[0].astype(BF16)
    vl = v_ref[0].astype(BF16)
    if has_ctx:
        kc = kc_ref[0].astype(BF16)
        vc = vc_ref[0].astype(BF16)
    q = q_ref[0].astype(BF16)
    out = jnp.zeros(q.shape, F32)
    for h in range(N_HEADS):
        o = None
        for mp in range(2):
            mm = _lane_mask(BRANCH_W, h * HEAD_DIM + mp * DIFF_HD, DIFF_HD)
            qm = jnp.where(mm, q, 0.0)
            parts = [_dot_nt(qm, kl)]
            if has_ctx:
                parts.append(_dot_nt(qm, kc))
            ps, l = _softmax_parts(parts)
            pv = _dot(ps[0].astype(BF16), vl)
            if has_ctx:
                pv = pv + _dot(ps[1].astype(BF16), vc)
            o = pv / l if mp == 0 else o - pv * (lam / l)
        out = out + jnp.where(_lane_mask(BRANCH_W, h * HEAD_DIM, HEAD_DIM), o, 0.0)
    ms = _dot_exact(out * out, ones_ref[...]) * (1.0 / HEAD_DIM)
    o_ref[0] = out * lax.rsqrt(ms + EPS) * gain_ref[...] * (1.0 - lam_init)


def diff_attention(dif, lam_vec, subln, layer, B, L, ctx=None):
    lam_init = 0.8 - 0.6 * math.exp(-0.3 * layer)
    dif3 = dif.reshape(B, L, DIFF_W)
    head = jnp.arange(BRANCH_W) // HEAD_DIM
    ones = (head[:, None] == head[None, :]).astype(F32)
    gain = jnp.tile(subln.astype(F32), N_HEADS).reshape(1, BRANCH_W)
    tq = min(DIFF_Q, L)
    in_specs = [_resident((4, DIFF_HD)), _resident((1, BRANCH_W)), _resident((BRANCH_W, BRANCH_W)),
                pl.BlockSpec((1, tq, BRANCH_W), lambda b, i: (b, i, 0)),
                pl.BlockSpec((1, L, BRANCH_W), lambda b, i: (b, 0, 1)),
                pl.BlockSpec((1, L, BRANCH_W), lambda b, i: (b, 0, 2))]
    args = [lam_vec, gain, ones, dif3, dif3, dif3]
    if ctx is not None:
        P = ctx[0].shape[1]
        in_specs += [pl.BlockSpec((1, P, BRANCH_W), lambda b, i: (b, 0, 0))] * 2
        args += list(ctx)
    return pl.pallas_call(
        functools.partial(_diff_kernel, lam_init=lam_init, has_ctx=ctx is not None),
        grid=(B, L // tq),
        in_specs=in_specs,
        out_specs=pl.BlockSpec((1, tq, BRANCH_W), lambda b, i: (b, i, 0)),
        out_shape=jax.ShapeDtypeStruct((B, L, BRANCH_W), F32),
        compiler_params=_cparams(56),
        name="diff_attention",
    )(*args).reshape(B * L, BRANCH_W)


def _filter_kernel(z_ref, w1_ref, b1_ref, w2_ref, b2_ref, w3_ref, fr_ref, ld_ref, o_ref):
    z = z_ref[...]
    tn = z[:, 0:1]
    fr = fr_ref[0]
    g = jnp.sin(fr * (_dot_exact(z, w1_ref[0]) + b1_ref[0]))
    g = jnp.sin(fr * (_dot_exact(g, w2_ref[0]) + b2_ref[0]))
    hf = _dot_exact(g, w3_ref[0]) * jnp.exp(-jnp.exp(ld_ref[0]) * tn)
    row = lax.broadcasted_iota(jnp.int32, (z.shape[0], 1), 0)
    for o in range(HY_ORDER):
        pos = hf[:, (2 * o) * HY_WIDTH:(2 * o + 1) * HY_WIDTH]
        neg = jnp.where(row == 0, 0.0, hf[:, (2 * o + 1) * HY_WIDTH:(2 * o + 2) * HY_WIDTH])
        norm = (jnp.sum(jnp.abs(pos), axis=0, keepdims=True)
                + jnp.sum(jnp.abs(neg), axis=0, keepdims=True) + EPS)
        o_ref[0, o, 0] = (neg + pos) / norm
        o_ref[0, o, 1] = (neg - pos) / norm


def hyena_filters(L, p):
    tn = jnp.arange(L, dtype=F32) / L
    ang = 2.0 * math.pi * tn[:, None] * jnp.arange(1, HY_FREQS + 1, dtype=F32)[None, :]
    z = jnp.concatenate([tn[:, None], jnp.cos(ang), jnp.sin(ang)], axis=-1)
    z = jnp.pad(z, ((0, 0), (0, HY_HIDDEN - HY_EMB)))
    w1 = jnp.pad(p['hy_w1'], ((0, 0), (0, HY_HIDDEN - HY_EMB), (0, 0)))
    fw = HY_ORDER * 2 * HY_WIDTH
    per_layer = lambda *shape: pl.BlockSpec((1,) + shape, lambda l: (l,) + (0,) * len(shape))
    return pl.pallas_call(
        _filter_kernel,
        grid=(DEPTH,),
        in_specs=[pl.BlockSpec((L, HY_HIDDEN), lambda l: (0, 0)),
                  per_layer(HY_HIDDEN, HY_HIDDEN), per_layer(1, HY_HIDDEN),
                  per_layer(HY_HIDDEN, HY_HIDDEN), per_layer(1, HY_HIDDEN),
                  per_layer(HY_HIDDEN, fw), per_layer(1, HY_HIDDEN), per_layer(1, fw)],
        out_specs=per_layer(HY_ORDER, 2, L, HY_WIDTH),
        out_shape=jax.ShapeDtypeStruct((DEPTH, HY_ORDER, 2, L, HY_WIDTH), F32),
        compiler_params=_cparams(56),
        name="hyena_filters",
    )(z, w1, p['hy_b1'].reshape(DEPTH, 1, HY_HIDDEN), p['hy_w2'], p['hy_b2'].reshape(DEPTH, 1, HY_HIDDEN),
      p['hy_w3'], p['hy_sin_freq'].reshape(DEPTH, 1, HY_HIDDEN), p['hy_log_decay'].reshape(DEPTH, 1, fw))


def dft_table(L, tk, half_sample):
    step = 2 * GRID_W
    k = 2 * jnp.arange(L, dtype=jnp.int32)[:, None] + 1

    def grid(m):
        ang = ((k * m[None, :]) % (8 * L)).astype(F32) * (math.pi / (4 * L))
        return jnp.cos(ang)[:, :, None], jnp.sin(ang)[:, :, None]

    (c_hi, s_hi) = grid(step * jnp.arange(2 * L // step, dtype=jnp.int32))
    lo = 2 * jnp.arange(step // 2, dtype=jnp.int32) + (1 if half_sample else 0)
    (c_lo, s_lo) = (t.reshape(L, 1, step // 2) for t in grid(lo))
    c = (c_hi * c_lo - s_hi * s_lo).reshape(L, L)
    s = (s_hi * c_lo + c_hi * s_lo).reshape(L, L)
    return jnp.concatenate([c.reshape(L // tk, tk, L), s.reshape(L // tk, tk, L)], axis=1).astype(BF16)


def _spectrum_kernel(t_ref, f_ref, o_ref):
    tk = o_ref.shape[2]
    o_ref[0, 0] = _dot(t_ref[0, :tk, :], f_ref[0, 0].astype(BF16))
    o_ref[0, 1] = _dot(t_ref[0, tk:, :], f_ref[0, 1].astype(BF16))


def filter_spectra(filt, fwd):
    G, _, L, C = filt.shape
    nkt, tk2, _ = fwd.shape
    tk = tk2 // 2
    return pl.pallas_call(
        _spectrum_kernel,
        grid=(nkt, G),
        in_specs=[pl.BlockSpec((1, tk2, L), lambda i, g: (i, 0, 0)),
                  pl.BlockSpec((1, 2, L, C), lambda i, g: (g, 0, 0, 0))],
        out_specs=pl.BlockSpec((1, 2, tk, C), lambda i, g: (g, 0, i, 0)),
        out_shape=jax.ShapeDtypeStruct((G, 2, L, C), F32),
        compiler_params=_cparams(48),
        name="filter_spectra",
    )(fwd, filt)


def _hyena_kernel(t_ref, h_ref, u_ref, w_ref, b_ref, skip_ref, o_ref, z_ref, y_ref):
    nkt, tk2, L = t_ref.shape
    tk = tk2 // 2
    C = HY_WIDTH
    row = lax.broadcasted_iota(jnp.int32, (L, 1), 0)

    def short_conv(col):
        sl = slice(col * C, (col + 1) * C)
        u = u_ref[0, :, sl]
        prev = jnp.where(row == 0, 0.0, pltpu.roll(u, 1, 0))
        nxt = jnp.where(row == L - 1, 0.0, pltpu.roll(u, L - 1, 0))
        return prev * w_ref[0:1, sl] + u * w_ref[1:2, sl] + nxt * w_ref[2:3, sl] + b_ref[:, sl]

    y_ref[...] = short_conv(0)
    for o in range(HY_ORDER):
        y = y_ref[...].astype(BF16)
        for i in range(nkt):
            rows = slice(i * tk, (i + 1) * tk)
            acc = _dot(t_ref[i], y)
            yc, ys = acc[:tk], acc[tk:]
            hr = h_ref[0, o, 0, rows, :]
            hi = h_ref[0, o, 1, rows, :]
            z_ref[0, rows, :] = (yc * hr + ys * hi).astype(BF16)
            z_ref[1, rows, :] = (yc * hi - ys * hr).astype(BF16)
        gate = short_conv(1 + o)
        zr, zi = z_ref[0], z_ref[1]
        for i in range(nkt):
            rows = slice(i * tk, (i + 1) * tk)
            conv = (_dot(t_ref[i, :tk, :], zr) - _dot(t_ref[i, tk:, :], zi)) * (1.0 / L)
            new = gate[rows] * (conv + skip_ref[o:o + 1, :] * y_ref[rows, :])
            if o == HY_ORDER - 1:
                o_ref[0, rows, :] = new
            else:
                y_ref[rows, :] = new


def hyena(hy, lp, layer, spec, table, B, L):
    C = HY_WIDTH
    return pl.pallas_call(
        _hyena_kernel,
        grid=(B,),
        in_specs=[_resident(table.shape),
                  pl.BlockSpec((1, HY_ORDER, 2, L, C), lambda b: (layer, 0, 0, 0, 0), pipeline_mode=pl.Buffered(1)),
                  pl.BlockSpec((1, L, HY_IN_W), lambda b: (b, 0, 0)),
                  _resident((3, HY_IN_W)), _resident((1, HY_IN_W)), _resident((HY_ORDER, C))],
        out_specs=pl.BlockSpec((1, L, C), lambda b: (b, 0, 0)),
        out_shape=jax.ShapeDtypeStruct((B, L, C), F32),
        scratch_shapes=[pltpu.VMEM((2, L, C), BF16), pltpu.VMEM((L, C), F32)],
        compiler_params=_cparams(56),
        name="hyena",
    )(table, spec, hy.reshape(B, L, HY_IN_W), lp['hy_short_w'], lp['hy_short_b'].reshape(1, HY_IN_W),
      lp['hy_skip']).reshape(B * L, C)


def _merge_kernel(a_ref, b_ref, c_ref, d_ref, x_ref, mod_ref, nw_ref, wg_ref, wb_ref, wo_ref, o_ref):
    x = x_ref[...]
    h = _norm_mod(x, nw_ref[...], mod_ref[0, 3:4, :], mod_ref[0, 4:5, :]).astype(BF16)
    merged = None
    for i, br in enumerate((a_ref, b_ref, c_ref, d_ref)):
        gate = _sigmoid(_dot(h, wg_ref[:, MIX_W + i * D_MODEL:MIX_W + (i + 1) * D_MODEL]))
        t = gate * _dot(br[...].astype(BF16), wb_ref[i])
        merged = t if merged is None else merged + t
    o_ref[...] = x + mod_ref[0, 5:6, :] * _dot(merged.astype(BF16), wo_ref[...])


def merge_block(branches, x, mod, nw, wg, wb, wo, layer, rows_per_b):
    T = x.shape[0]
    tm = MERGE_ROWS
    nb = mod.shape[0]
    row = lambda w: pl.BlockSpec((tm, w), lambda i: (i, 0))
    return pl.pallas_call(
        _merge_kernel,
        grid=(T // tm,),
        in_specs=[row(BRANCH_W)] * N_BRANCH + [row(D_MODEL), _mod_spec(nb, rows_per_b // tm),
                                               _resident((1, D_MODEL)),
                                               _resident_layer(wg, layer),
                                               _resident_layer(wb, layer),
                                               _resident_layer(wo, layer)],
        out_specs=row(D_MODEL),
        out_shape=jax.ShapeDtypeStruct((T, D_MODEL), F32),
        compiler_params=_cparams(56),
        name="merge_block",
    )(*branches, x, mod, nw.reshape(1, D_MODEL), wg, wb, wo)


def _heads_in(t):
    B, H, P, d = t.shape
    return t.transpose(0, 2, 1, 3).reshape(B, P, H * d)


def _run_pass(x, mod_all, p, wts, final_norm, B, L, spec, tabs, caches):
    ctx_pass = caches is None
    collected = []
    if not ctx_pass:
        rope = rope_tables(L, HEAD_DIM, SWA_QW + SWA_KVW) + rope_tables(L, DIFF_HD, 2 * BRANCH_W)
        bias_all = na_bias_tables(p['na_rpb'], L // GRID_W)
    for l in range(DEPTH):
        lp = {k: v[l] for k, v in p.items()}
        w = wts
        mod = mod_all[l]
        x = ffn_block(x, mod, lp['norm_ffn1'], w['ffn1_w1'], w['ffn1_w3'], w['ffn1_w2'], l, 0, L)
        na, swa, hy, dif = in_projection(x, mod, lp['norm_mix'], w['w_in'], l, L,
                                         None if ctx_pass else rope, F32 if ctx_pass else BF16)
        if ctx_pass:
            a_o = dense_attention(na, (0, 1, 2), B, L)
            b_o = dense_attention(swa, (0, 2, 3), B, L, gqa=True, sink=lp['swa_sink'])
            d_o = diff_attention(dif, lp['diff_lambda'], lp['diff_subln'], l, B, L)
            collected.append((na, swa, dif))
        else:
            ck_na, cv_na, ck_swa, cv_swa, ck_d, cv_d = (_heads_in(t[:, l]).astype(BF16) for t in caches)
            a_o = neighbourhood_attention(na, ck_na, cv_na, bias_all, l, B, L)
            b_o = dense_attention(swa, (0, 2, 3), B, L, window=True, gqa=True,
                                  ctx=(ck_swa, cv_swa), sink=lp['swa_sink'])
            d_o = diff_attention(dif, lp['diff_lambda'], lp['diff_subln'], l, B, L, ctx=(ck_d, cv_d))
        c_o = hyena(hy, lp, l, spec, tabs, B, L)
        x = merge_block((a_o, b_o, c_o, d_o), x, mod, lp['norm_mix'], w['w_in'], w['w_branch'], w['w_out'], l, L)
        x = ffn_block(x, mod, lp['norm_ffn2'], w['ffn2_w1'], w['ffn2_w3'], w['ffn2_w2'], l, 6, L,
                      final_w=final_norm if l == DEPTH - 1 else None)
    return x, collected


_CACHE_SLOTS = ((0, N_HEADS, N_HEADS), (0, 2 * N_HEADS, N_HEADS),
                (1, N_HEADS, SWA_KV_HEADS), (1, N_HEADS + SWA_KV_HEADS, SWA_KV_HEADS),
                (2, N_HEADS, N_HEADS), (2, 2 * N_HEADS, N_HEADS))


def _cache_kernel(*refs):
    srcs, outs = refs[:3 * DEPTH], refs[3 * DEPTH:]
    for l in range(DEPTH):
        for o_ref, (src, slot0, n) in zip(outs, _CACHE_SLOTS):
            x_ref = srcs[3 * l + src]
            for h in range(n):
                o_ref[0, l, h] = x_ref[:, (slot0 + h) * HEAD_DIM:(slot0 + h + 1) * HEAD_DIM]


def _new_caches(collected, B, L):
    srcs = [t for layer in collected for t in layer]
    return pl.pallas_call(
        _cache_kernel,
        grid=(B,),
        in_specs=[pl.BlockSpec((L, t.shape[1]), lambda b: (b, 0)) for t in srcs],
        out_specs=[pl.BlockSpec((1, DEPTH, n, L, HEAD_DIM), lambda b: (b, 0, 0, 0, 0)) for _, _, n in _CACHE_SLOTS],
        out_shape=[jax.ShapeDtypeStruct((B, DEPTH, n, L, HEAD_DIM), F32) for _, _, n in _CACHE_SLOTS],
        compiler_params=_cparams(48),
        name="cache_outputs",
    )(*srcs)


def _hyena_setup(L, p, tk):
    filt = hyena_filters(L, p)
    spec = filter_spectra(filt.reshape(DEPTH * HY_ORDER, 2, L, HY_WIDTH), dft_table(L, tk, half_sample=False))
    return spec.reshape(DEPTH, HY_ORDER, 2, L, HY_WIDTH), dft_table(L, tk, half_sample=True)


def kernel(x_prompt, x_sample, cache_na_k, cache_na_v, cache_swa_k, cache_swa_v, cache_diff_k, cache_diff_v, c, c_ctx, w_ada, b_ada, norm_ffn1, norm_mix, norm_ffn2, final_norm, ffn1_w1, ffn1_w3, ffn1_w2, ffn2_w1, ffn2_w3, ffn2_w2, w_in, w_branch, w_out, na_rpb, swa_sink, hy_short_w, hy_short_b, hy_w1, hy_b1, hy_w2, hy_b2, hy_w3, hy_sin_freq, hy_log_decay, hy_skip, diff_lambda, diff_subln):
    B_ctx, L_ctx, _ = x_prompt.shape
    B_den, L_den, _ = x_sample.shape
    p = {
        'norm_ffn1': norm_ffn1, 'norm_mix': norm_mix, 'norm_ffn2': norm_ffn2,
        'na_rpb': na_rpb, 'swa_sink': swa_sink, 'hy_short_w': hy_short_w, 'hy_short_b': hy_short_b,
        'hy_w1': hy_w1, 'hy_b1': hy_b1, 'hy_w2': hy_w2, 'hy_b2': hy_b2, 'hy_w3': hy_w3,
        'hy_sin_freq': hy_sin_freq, 'hy_log_decay': hy_log_decay, 'hy_skip': hy_skip,
        'diff_lambda': diff_lambda, 'diff_subln': diff_subln,
    }
    big = {'ffn1_w1': ffn1_w1, 'ffn1_w3': ffn1_w3, 'ffn1_w2': ffn1_w2, 'ffn2_w1': ffn2_w1,
           'ffn2_w3': ffn2_w3, 'ffn2_w2': ffn2_w2, 'w_in': w_in, 'w_branch': w_branch, 'w_out': w_out}
    wts = {k: v.astype(BF16) for k, v in big.items()}

    cond = jnp.concatenate([c, c_ctx[None, :]], axis=0)
    rows = 8 * ((cond.shape[0] + 7) // 8)
    cond = jnp.pad(cond, ((0, rows - cond.shape[0]), (0, 0)))
    mod = ada_modulation(cond, w_ada, b_ada).reshape(DEPTH, rows, N_MOD, D_MODEL)
    mod_den = mod[:, :B_den]
    mod_ctx = mod[:, B_den:B_den + 1]

    spec_c, tabs_c = _hyena_setup(L_ctx, p, min(L_ctx, HY_FREQ_TILE))
    y_ctx, collected = _run_pass(x_prompt.reshape(B_ctx * L_ctx, D_MODEL), mod_ctx, p, wts, final_norm,
                                 B_ctx, L_ctx, spec_c, tabs_c, None)
    new_caches = _new_caches(collected, B_ctx, L_ctx)

    spec_d, tabs_d = _hyena_setup(L_den, p, min(L_den, HY_FREQ_TILE))
    caches = (cache_na_k, cache_na_v, cache_swa_k, cache_swa_v, cache_diff_k, cache_diff_v)
    y_den, _ = _run_pass(x_sample.reshape(B_den * L_den, D_MODEL), mod_den, p, wts, final_norm,
                         B_den, L_den, spec_d, tabs_d, caches)
    return (y_ctx.reshape(B_ctx, L_ctx, D_MODEL), y_den.reshape(B_den, L_den, D_MODEL), *new_caches)
```

```python
import functools
import math

import jax
import jax.numpy as jnp
from jax import lax
from jax.experimental import pallas as pl
from jax.experimental.pallas import tpu as pltpu

F32 = jnp.float32
BF16 = jnp.bfloat16

D_MODEL = 1024
DEPTH = 4
GRID_W = 64
N_BRANCH = 4
BRANCH_W = D_MODEL // 4
HEAD_DIM = 64
N_HEADS = BRANCH_W // HEAD_DIM
NA_WIN_R = 8
NA_WIN_C = 16
SWA_KV_HEADS = N_HEADS // 2
SWA_WINDOW = 128
HY_WIDTH = BRANCH_W
HY_ORDER = 2
HY_FREQS = 16
HY_EMB = 1 + 2 * HY_FREQS
HY_HIDDEN = 64
DIFF_HD = 32
D_FF = 128 * ((8 * D_MODEL // 3 + 127) // 128)
ROPE_BASE = 10000.0
EPS = 1e-6
NEG = -1e30
N_MOD = 9
NA_W = 3 * BRANCH_W
SWA_QW = BRANCH_W
SWA_KVW = SWA_KV_HEADS * HEAD_DIM
SWA_W = SWA_QW + 2 * SWA_KVW
HY_IN_W = 3 * HY_WIDTH
DIFF_W = 3 * BRANCH_W
GATE_W = N_BRANCH * D_MODEL
MIX_W = NA_W + SWA_W + HY_IN_W + DIFF_W

LOG2E = math.log2(math.e)
QK_SCALE = HEAD_DIM ** -0.5 * LOG2E
DIFF_QK_SCALE = DIFF_HD ** -0.5 * LOG2E

LANES = 128
MXU_DIM = 256
MIB = 1024 * 1024

FFN_ROWS = 1024
FFN_SPLIT = 2
MERGE_ROWS = 512
MERGE_SPLIT = 2
PROJ_ROWS = 512
PROJ_SPLIT = 2
FFN_CHUNK = 768
ATTN_Q = 256
ATTN_TILES = 2
NA_TILES = 4
HY_FREQ_TILE = 512
DIFF_Q = 512
NA_Q_ROWS = ATTN_Q // GRID_W
NA_SLAB_ROWS = NA_Q_ROWS + NA_WIN_R
SWA_SLAB = ATTN_Q + 2 * SWA_WINDOW


def _cparams(vmem_mib):
    return pltpu.CompilerParams(vmem_limit_bytes=vmem_mib * MIB)


def _resident(shape):
    nd = len(shape)
    return pl.BlockSpec(shape, lambda *_: (0,) * nd, pipeline_mode=pl.Buffered(1))


def _resident_layer(stacked, layer):
    shape = stacked.shape[1:]
    return pl.BlockSpec((None,) + shape, lambda *_: (layer,) + (0,) * len(shape), pipeline_mode=pl.Buffered(1))


def _dot(a, b):
    return jnp.dot(a, b, preferred_element_type=F32)


def _dot_nt(a, b):
    return lax.dot_general(a, b, (((1,), (1,)), ((), ())), preferred_element_type=F32)


def _dot_exact(a, b):
    return jnp.dot(a, b, preferred_element_type=F32, precision=lax.Precision.HIGHEST)


def _sigmoid(x):
    return 1.0 / (1.0 + jnp.exp(-x))


def _norm_mod(x, nw, shift, scale):
    return x * lax.rsqrt(jnp.mean(x * x, axis=-1, keepdims=True) + EPS) * (nw * (1.0 + scale)) + shift


def _mod_spec(nb, tiles_per_b):
    if nb == 1:
        return pl.BlockSpec((1, N_MOD, D_MODEL), lambda i: (0, 0, 0))
    return pl.BlockSpec((1, N_MOD, D_MODEL), lambda i: (i // tiles_per_b, 0, 0))


def _ada_kernel(c_ref, w_ref, b_ref, o_ref):
    c = c_ref[...]
    s = (c * _sigmoid(c)).astype(BF16)
    o_ref[0] = _dot(s, w_ref[0].astype(BF16)) + b_ref[0]


def ada_modulation(cond, w_ada, b_ada):
    rows = cond.shape[0]
    width = N_MOD * D_MODEL
    tn = 9 * LANES
    return pl.pallas_call(
        _ada_kernel,
        grid=(DEPTH, width // tn),
        in_specs=[pl.BlockSpec((rows, D_MODEL), lambda l, j: (0, 0)),
                  pl.BlockSpec((1, D_MODEL, tn), lambda l, j: (l, 0, j)),
                  pl.BlockSpec((1, 1, tn), lambda l, j: (l, 0, j))],
        out_specs=pl.BlockSpec((1, rows, tn), lambda l, j: (l, 0, j)),
        out_shape=jax.ShapeDtypeStruct((DEPTH, rows, width), F32),
        compiler_params=_cparams(32),
        name="ada_modulation",
    )(cond, w_ada, b_ada.reshape(DEPTH, 1, width))


def _ffn_kernel(x_ref, mod_ref, nw_ref, w1_ref, w3_ref, w2_ref, *rest, mod_base, final):
    if final:
        fw_ref, o_ref = rest
    else:
        (o_ref,) = rest
    shift = mod_ref[0, mod_base:mod_base + 1, :]
    scale = mod_ref[0, mod_base + 1:mod_base + 2, :]
    gate = mod_ref[0, mod_base + 2:mod_base + 3, :]
    rows = x_ref.shape[0] // FFN_SPLIT
    groups = [slice(g * rows, (g + 1) * rows) for g in range(FFN_SPLIT)]
    hs = [_norm_mod(x_ref[g, :], nw_ref[...], shift, scale).astype(BF16) for g in groups]
    for g, h in zip(groups, hs):
        acc = jnp.zeros((rows, D_MODEL), F32)
        for lo in range(0, D_FF, FFN_CHUNK):
            hi = min(lo + FFN_CHUNK, D_FF)
            a = _dot(h, w1_ref[:, lo:hi])
            b = _dot(h, w3_ref[:, lo:hi])
            u = (a * _sigmoid(a) * b).astype(BF16)
            acc = acc + _dot(u, w2_ref[lo:hi, :])
        y = x_ref[g, :] + 0.5 * gate * acc
        if final:
            y = y * lax.rsqrt(jnp.mean(y * y, axis=-1, keepdims=True) + EPS) * fw_ref[...]
        o_ref[g, :] = y


def ffn_block(x, mod, nw, w1, w3, w2, layer, mod_base, rows_per_b, final_w=None):
    T = x.shape[0]
    tm = FFN_ROWS
    nb = mod.shape[0]
    final = final_w is not None
    in_specs = [pl.BlockSpec((tm, D_MODEL), lambda i: (i, 0)),
                _mod_spec(nb, rows_per_b // tm),
                _resident((1, D_MODEL)),
                _resident_layer(w1, layer), _resident_layer(w3, layer), _resident_layer(w2, layer)]
    args = [x, mod, nw.reshape(1, D_MODEL), w1, w3, w2]
    if final:
        in_specs.append(_resident((1, D_MODEL)))
        args.append(final_w.reshape(1, D_MODEL))
    return pl.pallas_call(
        functools.partial(_ffn_kernel, mod_base=mod_base, final=final),
        grid=(T // tm,),
        in_specs=in_specs,
        out_specs=pl.BlockSpec((tm, D_MODEL), lambda i: (i, 0)),
        out_shape=jax.ShapeDtypeStruct((T, D_MODEL), F32),
        compiler_params=_cparams(56),
        name="ffn_block",
    )(*args)


def _rope_chunk(x, cos, sin_a, sin_b, dist):
    return x * cos + pltpu.roll(x, LANES - dist, 1) * sin_a + pltpu.roll(x, dist, 1) * sin_b


def _proj_kernel(x_ref, mod_ref, nw_ref, w_ref, *rest, rope):
    if rope:
        (cs_ref, sa_ref, sb_ref, cd_ref, da_ref, db_ref,
         na_ref, swa_ref, hy_ref, dif_ref) = rest
    else:
        na_ref, swa_ref, hy_ref, dif_ref = rest
    rows = x_ref.shape[0] // PROJ_SPLIT
    groups = [slice(g * rows, (g + 1) * rows) for g in range(PROJ_SPLIT)]
    hs = [_norm_mod(x_ref[g, :], nw_ref[...], mod_ref[0, 3:4, :], mod_ref[0, 4:5, :]).astype(BF16) for g in groups]
    q_chunks = BRANCH_W // LANES
    for g, h in zip(groups, hs):
        o = 0
        s = _dot(h, w_ref[:, o:o + NA_W])
        o += NA_W
        na_ref[g, :BRANCH_W] = (s[:, :BRANCH_W] * QK_SCALE).astype(na_ref.dtype)
        na_ref[g, BRANCH_W:] = s[:, BRANCH_W:].astype(na_ref.dtype)
        s = _dot(h, w_ref[:, o:o + SWA_W])
        o += SWA_W
        n_rot = (SWA_QW + SWA_KVW) // LANES
        for c in range(SWA_W // LANES):
            sl = slice(c * LANES, (c + 1) * LANES)
            chunk = s[:, sl]
            if rope and c < n_rot:
                chunk = _rope_chunk(chunk, cs_ref[g, sl], sa_ref[g, sl], sb_ref[g, sl], HEAD_DIM // 4)
            if c < q_chunks:
                chunk = chunk * QK_SCALE
            swa_ref[g, sl] = chunk.astype(swa_ref.dtype)
        hy_ref[g, :] = _dot(h, w_ref[:, o:o + HY_IN_W])
        o += HY_IN_W
        s = _dot(h, w_ref[:, o:o + DIFF_W])
        o += DIFF_W
        n_rot = 2 * BRANCH_W // LANES
        for c in range(DIFF_W // LANES):
            sl = slice(c * LANES, (c + 1) * LANES)
            chunk = s[:, sl]
            if rope and c < n_rot:
                chunk = _rope_chunk(chunk, cd_ref[g, sl], da_ref[g, sl], db_ref[g, sl], DIFF_HD // 4)
            if c < q_chunks:
                chunk = chunk * DIFF_QK_SCALE
            dif_ref[g, sl] = chunk.astype(dif_ref.dtype)


def in_projection(x, mod, nw, w_in, layer, rows_per_b, rope_tabs, qkv_dtype):
    T = x.shape[0]
    tm = PROJ_ROWS
    nb = mod.shape[0]
    rope = rope_tabs is not None
    if rope:
        pos_tiles = rows_per_b // tm
        grid = (pos_tiles, T // rows_per_b)
        row_tile = lambda p, b: (b * pos_tiles + p, 0)
        mod_spec = pl.BlockSpec((1, N_MOD, D_MODEL), lambda p, b: (b, 0, 0))
    else:
        grid = (T // tm,)
        row_tile = lambda i: (i, 0)
        mod_spec = _mod_spec(nb, rows_per_b // tm)
    in_specs = [pl.BlockSpec((tm, D_MODEL), row_tile),
                mod_spec,
                _resident((1, D_MODEL)),
                pl.BlockSpec((None, D_MODEL, MIX_W), lambda *_: (layer, 0, 0), pipeline_mode=pl.Buffered(1))]
    args = [x, mod, nw.reshape(1, D_MODEL), w_in]
    if rope:
        for t in rope_tabs:
            in_specs.append(pl.BlockSpec((tm, t.shape[1]), lambda p, b: (p, 0)))
            args.append(t)
    widths = (NA_W, SWA_W, HY_IN_W, DIFF_W)
    dtypes = (qkv_dtype, qkv_dtype, F32, qkv_dtype)
    return pl.pallas_call(
        functools.partial(_proj_kernel, rope=rope),
        grid=grid,
        in_specs=in_specs,
        out_specs=[pl.BlockSpec((tm, w), row_tile) for w in widths],
        out_shape=[jax.ShapeDtypeStruct((T, w), dt) for w, dt in zip(widths, dtypes)],
        compiler_params=_cparams(56),
        name="in_projection",
    )(*args)


def rope_tables(L, dh, width):
    nf = dh // 4
    t = jnp.arange(L)
    freqs = ROPE_BASE ** (-jnp.arange(nf, dtype=F32) / nf)
    ang_r = (t // GRID_W).astype(F32)[:, None] * freqs
    ang_c = (t % GRID_W).astype(F32)[:, None] * freqs
    cr, sr, cc, sc = jnp.cos(ang_r), jnp.sin(ang_r), jnp.cos(ang_c), jnp.sin(ang_c)
    z = jnp.zeros_like(sr)
    reps = width // dh
    cos = jnp.tile(jnp.concatenate([cr, cr, cc, cc], axis=-1), (1, reps))
    sin_a = jnp.tile(jnp.concatenate([-sr, z, -sc, z], axis=-1), (1, reps))
    sin_b = jnp.tile(jnp.concatenate([z, sr, z, sc], axis=-1), (1, reps))
    return cos, sin_a, sin_b


def _lane_mask(width, lo, n):
    lane = lax.broadcasted_iota(jnp.int32, (1, width), 1)
    return (lane >= lo) & (lane < lo + n)


def _softmax_parts(parts, extra_logit=None):
    m = parts[0].max(axis=-1, keepdims=True)
    for s in parts[1:]:
        m = jnp.maximum(m, s.max(axis=-1, keepdims=True))
    if extra_logit is not None:
        m = jnp.maximum(m, extra_logit)
    ps = [jnp.exp2(s - m) for s in parts]
    l = ps[0].sum(axis=-1, keepdims=True)
    for p in ps[1:]:
        l = l + p.sum(axis=-1, keepdims=True)
    if extra_logit is not None:
        l = l + jnp.exp2(extra_logit - m)
    return ps, l


def _na_kernel(q_ref, k_ref, v_ref, kc_ref, vc_ref, *rest, n_rows):
    bias_refs, o_ref = rest[:-1], rest[-1]
    slab = NA_SLAB_ROWS * GRID_W
    kc = kc_ref[0].astype(BF16)
    vc = vc_ref[0].astype(BF16)
    for sub, bias_ref in enumerate(bias_refs):
        tile = pl.program_id(1) * len(bias_refs) + sub
        rows = slice(sub * ATTN_Q, (sub + 1) * ATTN_Q)
        row0 = jnp.clip(NA_Q_ROWS * tile - NA_WIN_R // 2, 0, n_rows - NA_SLAB_ROWS)
        start = pl.multiple_of(row0 * GRID_W, GRID_W)
        ks = k_ref[0, pl.ds(start, slab), :].astype(BF16)
        vs = v_ref[0, pl.ds(start, slab), :].astype(BF16)
        q = q_ref[0, rows, :].astype(BF16)
        out = jnp.zeros(q.shape, F32)
        for h in range(N_HEADS):
            hm = _lane_mask(BRANCH_W, h * HEAD_DIM, HEAD_DIM)
            qh = jnp.where(hm, q, 0.0)
            s_win = _dot_nt(qh, ks) + bias_ref[0, 0, h]
            s_ctx = _dot_nt(qh, kc)
            (p_win, p_ctx), l = _softmax_parts([s_win, s_ctx])
            o = _dot(p_win.astype(BF16), vs) + _dot(p_ctx.astype(BF16), vc)
            out = out + jnp.where(hm, o / l, 0.0)
        o_ref[0, rows, :] = out


NA_DROWS = 2 * NA_WIN_R - 1
NA_DCOLS = 2 * NA_WIN_C - 1


def _na_tile_geometry(n_rows):
    n_tiles = n_rows // NA_Q_ROWS

    def geometry(tile):
        slab0 = min(max(NA_Q_ROWS * tile - NA_WIN_R // 2, 0), n_rows - NA_SLAB_ROWS)
        rows = []
        for j in range(NA_Q_ROWS):
            qr = NA_Q_ROWS * tile + j
            r0 = min(max(qr - NA_WIN_R // 2, 0), n_rows - NA_WIN_R)
            rows.append([slab0 + m - qr + NA_WIN_R - 1 if r0 <= slab0 + m < r0 + NA_WIN_R else None
                         for m in range(NA_SLAB_ROWS)])
        return rows

    kinds = [geometry(0), geometry(1), geometry(n_tiles - 1)]
    assert all(geometry(t) == kinds[1] for t in range(1, n_tiles - 1))
    return kinds


def _na_bias_kernel(rpb_ref, o_ref, band_ref, *, kinds):
    base = (pl.program_id(0) * N_HEADS + pl.program_id(1)) * (NA_DROWS * NA_DCOLS)
    qc = lax.broadcasted_iota(jnp.int32, (GRID_W, LANES), 0)
    lane = lax.broadcasted_iota(jnp.int32, (GRID_W, LANES), 1)
    kc = lane % GRID_W
    d_col = kc - qc + (NA_WIN_C - 1)
    c0 = jnp.clip(qc - NA_WIN_C // 2, 0, GRID_W - NA_WIN_C)
    col_ok = (kc >= c0) & (kc < c0 + NA_WIN_C)
    for d in range(NA_DROWS):
        t = jnp.full((GRID_W, LANES), NEG, F32)
        for e in range(NA_DCOLS):
            t = jnp.where(d_col == e, rpb_ref[base + d * NA_DCOLS + e] * LOG2E, t)
        band_ref[d] = jnp.where(col_ok, t, NEG)
    masked = jnp.full((GRID_W, LANES), NEG, F32)
    left = lane < GRID_W
    for kind, rows in enumerate(kinds):
        for j, drow in enumerate(rows):
            for m in range(0, NA_SLAB_ROWS, 2):
                a = masked if drow[m] is None else band_ref[drow[m]]
                b = masked if drow[m + 1] is None else band_ref[drow[m + 1]]
                o_ref[0, kind, 0, j * GRID_W:(j + 1) * GRID_W, m * GRID_W:(m + 2) * GRID_W] = jnp.where(left, a, b)


def na_bias_tables(rpb, n_rows):
    slab = NA_SLAB_ROWS * GRID_W
    return pl.pallas_call(
        functools.partial(_na_bias_kernel, kinds=_na_tile_geometry(n_rows)),
        grid=(DEPTH, N_HEADS),
        in_specs=[pl.BlockSpec(memory_space=pltpu.SMEM)],
        out_specs=pl.BlockSpec((1, 3, 1, ATTN_Q, slab), lambda l, h: (l, 0, h, 0, 0)),
        out_shape=jax.ShapeDtypeStruct((DEPTH, 3, N_HEADS, ATTN_Q, slab), F32),
        scratch_shapes=[pltpu.VMEM((NA_DROWS, GRID_W, LANES), F32)],
        compiler_params=_cparams(32),
        name="na_bias_tables",
    )(rpb.astype(F32).reshape(-1))


def neighbourhood_attention(na, kc, vc, bias, layer, B, L):
    n_rows = L // GRID_W
    n_tiles = L // ATTN_Q
    na3 = na.reshape(B, L, NA_W)
    P = kc.shape[1]
    slab = NA_SLAB_ROWS * GRID_W

    def kind(sub):
        def index(b, i):
            tile = i * NA_TILES + sub
            return (layer, jnp.where(tile == 0, 0, jnp.where(tile == n_tiles - 1, 2, 1)), 0, 0, 0)
        return index

    tq = NA_TILES * ATTN_Q
    return pl.pallas_call(
        functools.partial(_na_kernel, n_rows=n_rows),
        grid=(B, L // tq),
        in_specs=[pl.BlockSpec((1, tq, BRANCH_W), lambda b, i: (b, i, 0)),
                  pl.BlockSpec((1, L, BRANCH_W), lambda b, i: (b, 0, 1)),
                  pl.BlockSpec((1, L, BRANCH_W), lambda b, i: (b, 0, 2)),
                  pl.BlockSpec((1, P, BRANCH_W), lambda b, i: (b, 0, 0)),
                  pl.BlockSpec((1, P, BRANCH_W), lambda b, i: (b, 0, 0))]
                 + [pl.BlockSpec((1, 1, N_HEADS, ATTN_Q, slab), kind(sub)) for sub in range(NA_TILES)],
        out_specs=pl.BlockSpec((1, tq, BRANCH_W), lambda b, i: (b, i, 0)),
        out_shape=jax.ShapeDtypeStruct((B, L, BRANCH_W), F32),
        compiler_params=_cparams(56),
        name="neighbourhood_attention",
    )(na3, na3, na3, kc, vc, *([bias] * NA_TILES)).reshape(B * L, BRANCH_W)


def _attn_kernel(*refs, L, window, gqa, has_ctx, has_sink):
    refs = list(refs)
    q_ref, k_ref, v_ref = refs[:3]
    pos = 3
    if has_ctx:
        kc_ref, vc_ref = refs[pos:pos + 2]
        pos += 2
    if has_sink:
        sink_ref = refs[pos]
        pos += 1
    o_ref = refs[pos]
    if has_ctx:
        kc = kc_ref[0].astype(BF16)
        vc = vc_ref[0].astype(BF16)
    n_sub = q_ref.shape[1] // ATTN_Q
    for sub in range(n_sub):
        rows = slice(sub * ATTN_Q, (sub + 1) * ATTN_Q)
        if window:
            slab = SWA_SLAB
            q0 = (pl.program_id(1) * n_sub + sub) * ATTN_Q
            start = pl.multiple_of(jnp.clip(q0 - SWA_WINDOW, 0, L - slab), SWA_WINDOW)
            q_pos = q0 + lax.broadcasted_iota(jnp.int32, (ATTN_Q, 1), 0)
            k_pos = start + lax.broadcasted_iota(jnp.int32, (1, slab), 1)
            ok = jnp.abs(k_pos - q_pos) <= SWA_WINDOW
            ks = k_ref[0, pl.ds(start, slab), :].astype(BF16)
            vs = v_ref[0, pl.ds(start, slab), :].astype(BF16)
        else:
            ks = k_ref[0].astype(BF16)
            vs = v_ref[0].astype(BF16)
        q = q_ref[0, rows, :].astype(F32)
        kv_w = ks.shape[1]
        halves = [jnp.zeros((ATTN_Q, LANES), F32), jnp.zeros((ATTN_Q, LANES), F32)]
        out = jnp.zeros(q.shape, F32)
        for h in range(N_HEADS):
            if gqa:
                kvh, slot = h // 2, h % 2
                qh = q[:, kvh * LANES:(kvh + 1) * LANES]
                if slot != kvh:
                    qh = pltpu.roll(qh, HEAD_DIM, 1)
                hm = _lane_mask(kv_w, kvh * HEAD_DIM, HEAD_DIM)
            else:
                qh = q
                hm = _lane_mask(kv_w, h * HEAD_DIM, HEAD_DIM)
            qh = jnp.where(hm, qh, 0.0).astype(BF16)
            s = _dot_nt(qh, ks)
            if window:
                s = jnp.where(ok, s, NEG)
            parts = [s]
            if has_ctx:
                parts.append(_dot_nt(qh, kc))
            ps, l = _softmax_parts(parts, sink_ref[h] * LOG2E if has_sink else None)
            o = _dot(ps[0].astype(BF16), vs)
            if has_ctx:
                o = o + _dot(ps[1].astype(BF16), vc)
            o = jnp.where(hm, o / l, 0.0)
            if gqa:
                if slot != kvh:
                    o = pltpu.roll(o, HEAD_DIM, 1)
                halves[kvh] = halves[kvh] + o
            else:
                out = out + o
        if gqa:
            o_ref[0, rows, 0:LANES] = halves[0]
            o_ref[0, rows, LANES:2 * LANES] = halves[1]
        else:
            o_ref[0, rows, :] = out


def dense_attention(src, cols, B, L, *, window=False, gqa=False, ctx=None, sink=None):
    W = src.shape[1]
    src3 = src.reshape(B, L, W)
    kv_w = SWA_KVW if gqa else BRANCH_W
    qc, kcol, vcol = cols
    tq = min(ATTN_TILES * ATTN_Q, L)
    in_specs = [pl.BlockSpec((1, tq, BRANCH_W), lambda b, i: (b, i, qc)),
                pl.BlockSpec((1, L, kv_w), lambda b, i: (b, 0, kcol)),
                pl.BlockSpec((1, L, kv_w), lambda b, i: (b, 0, vcol))]
    args = [src3, src3, src3]
    if ctx is not None:
        P = ctx[0].shape[1]
        in_specs += [pl.BlockSpec((1, P, kv_w), lambda b, i: (b, 0, 0))] * 2
        args += list(ctx)
    if sink is not None:
        in_specs.append(pl.BlockSpec(memory_space=pltpu.SMEM))
        args.append(sink)
    return pl.pallas_call(
        functools.partial(_attn_kernel, L=L, window=window, gqa=gqa,
                          has_ctx=ctx is not None, has_sink=sink is not None),
        grid=(B, L // tq),
        in_specs=in_specs,
        out_specs=pl.BlockSpec((1, tq, BRANCH_W), lambda b, i: (b, i, 0)),
        out_shape=jax.ShapeDtypeStruct((B, L, BRANCH_W), F32),
        compiler_params=_cparams(48),
        name="window_attention" if window else "dense_attention",
    )(*args).reshape(B * L, BRANCH_W)


def _diff_kernel(*refs, lam_init, has_ctx):
    refs = list(refs)
    lam_ref, gain_ref, ones_ref, q_ref, k_ref, v_ref = refs[:6]
    if has_ctx:
        kc_ref, vc_ref, o_ref = refs[6:]
    else:
        (o_ref,) = refs[6:]
    lv = lam_ref[...]
    lam = (jnp.exp(jnp.sum(lv[0:1] * lv[1:2], keepdims=True))
           - jnp.exp(jnp.sum(lv[2:3] * lv[3:4], keepdims=True)) + lam_init)
    kl = k_ref[0].astype(BF16)
    vl = v_ref[0].astype(BF16)
    if has_ctx:
        kc = kc_ref[0].astype(BF16)
        vc = vc_ref[0].astype(BF16)
    q = q_ref[0].astype(BF16)
    out = jnp.zeros(q.shape, F32)
    for h in range(N_HEADS):
        o = None
        for mp in range(2):
            mm = _lane_mask(BRANCH_W, h * HEAD_DIM + mp * DIFF_HD, DIFF_HD)
            qm = jnp.where(mm, q, 0.0)
            parts = [_dot_nt(qm, kl)]
            if has_ctx:
                parts.append(_dot_nt(qm, kc))
            ps, l = _softmax_parts(parts)
            pv = _dot(ps[0].astype(BF16), vl)
            if has_ctx:
                pv = pv + _dot(ps[1].astype(BF16), vc)
            o = pv / l if mp == 0 else o - pv * (lam / l)
        out = out + jnp.where(_lane_mask(BRANCH_W, h * HEAD_DIM, HEAD_DIM), o, 0.0)
    ms = _dot_exact(out * out, ones_ref[...]) * (1.0 / HEAD_DIM)
    o_ref[0] = out * lax.rsqrt(ms + EPS) * gain_ref[...] * (1.0 - lam_init)


def diff_attention(dif, lam_vec, subln, layer, B, L, ctx=None):
    lam_init = 0.8 - 0.6 * math.exp(-0.3 * layer)
    dif3 = dif.reshape(B, L, DIFF_W)
    head = jnp.arange(BRANCH_W) // HEAD_DIM
    ones = (head[:, None] == head[None, :]).astype(F32)
    gain = jnp.tile(subln.astype(F32), N_HEADS).reshape(1, BRANCH_W)
    tq = min(DIFF_Q, L)
    in_specs = [_resident((4, DIFF_HD)), _resident((1, BRANCH_W)), _resident((BRANCH_W, BRANCH_W)),
                pl.BlockSpec((1, tq, BRANCH_W), lambda b, i: (b, i, 0)),
                pl.BlockSpec((1, L, BRANCH_W), lambda b, i: (b, 0, 1)),
                pl.BlockSpec((1, L, BRANCH_W), lambda b, i: (b, 0, 2))]
    args = [lam_vec, gain, ones, dif3, dif3, dif3]
    if ctx is not None:
        P = ctx[0].shape[1]
        in_specs += [pl.BlockSpec((1, P, BRANCH_W), lambda b, i: (b, 0, 0))] * 2
        args += list(ctx)
    return pl.pallas_call(
        functools.partial(_diff_kernel, lam_init=lam_init, has_ctx=ctx is not None),
        grid=(B, L // tq),
        in_specs=in_specs,
        out_specs=pl.BlockSpec((1, tq, BRANCH_W), lambda b, i: (b, i, 0)),
        out_shape=jax.ShapeDtypeStruct((B, L, BRANCH_W), F32),
        compiler_params=_cparams(56),
        name="diff_attention",
    )(*args).reshape(B * L, BRANCH_W)


def _filter_kernel(z_ref, w1_ref, b1_ref, w2_ref, b2_ref, w3_ref, fr_ref, ld_ref, o_ref):
    z = z_ref[...]
    tn = z[:, 0:1]
    fr = fr_ref[0]
    g = jnp.sin(fr * (_dot_exact(z, w1_ref[0]) + b1_ref[0]))
    g = jnp.sin(fr * (_dot_exact(g, w2_ref[0]) + b2_ref[0]))
    hf = _dot_exact(g, w3_ref[0]) * jnp.exp(-jnp.exp(ld_ref[0]) * tn)
    row = lax.broadcasted_iota(jnp.int32, (z.shape[0], 1), 0)
    for o in range(HY_ORDER):
        pos = hf[:, (2 * o) * HY_WIDTH:(2 * o + 1) * HY_WIDTH]
        neg = jnp.where(row == 0, 0.0, hf[:, (2 * o + 1) * HY_WIDTH:(2 * o + 2) * HY_WIDTH])
        norm = (jnp.sum(jnp.abs(pos), axis=0, keepdims=True)
                + jnp.sum(jnp.abs(neg), axis=0, keepdims=True) + EPS)
        o_ref[0, o, 0] = (neg + pos) / norm
        o_ref[0, o, 1] = (neg - pos) / norm


def hyena_filters(L, p):
    tn = jnp.arange(L, dtype=F32) / L
    ang = 2.0 * math.pi * tn[:, None] * jnp.arange(1, HY_FREQS + 1, dtype=F32)[None, :]
    z = jnp.concatenate([tn[:, None], jnp.cos(ang), jnp.sin(ang)], axis=-1)
    z = jnp.pad(z, ((0, 0), (0, HY_HIDDEN - HY_EMB)))
    w1 = jnp.pad(p['hy_w1'], ((0, 0), (0, HY_HIDDEN - HY_EMB), (0, 0)))
    fw = HY_ORDER * 2 * HY_WIDTH
    per_layer = lambda *shape: pl.BlockSpec((1,) + shape, lambda l: (l,) + (0,) * len(shape))
    return pl.pallas_call(
        _filter_kernel,
        grid=(DEPTH,),
        in_specs=[pl.BlockSpec((L, HY_HIDDEN), lambda l: (0, 0)),
                  per_layer(HY_HIDDEN, HY_HIDDEN), per_layer(1, HY_HIDDEN),
                  per_layer(HY_HIDDEN, HY_HIDDEN), per_layer(1, HY_HIDDEN),
                  per_layer(HY_HIDDEN, fw), per_layer(1, HY_HIDDEN), per_layer(1, fw)],
        out_specs=per_layer(HY_ORDER, 2, L, HY_WIDTH),
        out_shape=jax.ShapeDtypeStruct((DEPTH, HY_ORDER, 2, L, HY_WIDTH), F32),
        compiler_params=_cparams(56),
        name="hyena_filters",
    )(z, w1, p['hy_b1'].reshape(DEPTH, 1, HY_HIDDEN), p['hy_w2'], p['hy_b2'].reshape(DEPTH, 1, HY_HIDDEN),
      p['hy_w3'], p['hy_sin_freq'].reshape(DEPTH, 1, HY_HIDDEN), p['hy_log_decay'].reshape(DEPTH, 1, fw))


def dft_table(L, tk, half_sample):
    step = 2 * GRID_W
    k = 2 * jnp.arange(L, dtype=jnp.int32)[:, None] + 1

    def grid(m):
        ang = ((k * m[None, :]) % (8 * L)).astype(F32) * (math.pi / (4 * L))
        return jnp.cos(ang)[:, :, None], jnp.sin(ang)[:, :, None]

    (c_hi, s_hi) = grid(step * jnp.arange(2 * L // step, dtype=jnp.int32))
    lo = 2 * jnp.arange(step // 2, dtype=jnp.int32) + (1 if half_sample else 0)
    (c_lo, s_lo) = (t.reshape(L, 1, step // 2) for t in grid(lo))
    c = (c_hi * c_lo - s_hi * s_lo).reshape(L, L)
    s = (s_hi * c_lo + c_hi * s_lo).reshape(L, L)
    return jnp.concatenate([c.reshape(L // tk, tk, L), s.reshape(L // tk, tk, L)], axis=1).astype(BF16)


def _spectrum_kernel(t_ref, f_ref, o_ref):
    tk = o_ref.shape[2]
    o_ref[0, 0] = _dot(t_ref[0, :tk, :], f_ref[0, 0].astype(BF16))
    o_ref[0, 1] = _dot(t_ref[0, tk:, :], f_ref[0, 1].astype(BF16))


def filter_spectra(filt, fwd):
    G, _, L, C = filt.shape
    nkt, tk2, _ = fwd.shape
    tk = tk2 // 2
    return pl.pallas_call(
        _spectrum_kernel,
        grid=(nkt, G),
        in_specs=[pl.BlockSpec((1, tk2, L), lambda i, g: (i, 0, 0)),
                  pl.BlockSpec((1, 2, L, C), lambda i, g: (g, 0, 0, 0))],
        out_specs=pl.BlockSpec((1, 2, tk, C), lambda i, g: (g, 0, i, 0)),
        out_shape=jax.ShapeDtypeStruct((G, 2, L, C), F32),
        compiler_params=_cparams(48),
        name="filter_spectra",
    )(fwd, filt)


def _hyena_kernel(t_ref, h_ref, u_ref, w_ref, b_ref, skip_ref, o_ref, z_ref, y_ref):
    nkt, tk2, L = t_ref.shape
    tk = tk2 // 2
    C = HY_WIDTH
    row = lax.broadcasted_iota(jnp.int32, (L, 1), 0)

    def short_conv(col):
        sl = slice(col * C, (col + 1) * C)
        u = u_ref[0, :, sl]
        prev = jnp.where(row == 0, 0.0, pltpu.roll(u, 1, 0))
        nxt = jnp.where(row == L - 1, 0.0, pltpu.roll(u, L - 1, 0))
        return prev * w_ref[0:1, sl] + u * w_ref[1:2, sl] + nxt * w_ref[2:3, sl] + b_ref[:, sl]

    y_ref[...] = short_conv(0)
    for o in range(HY_ORDER):
        y = y_ref[...].astype(BF16)
        for i in range(nkt):
            rows = slice(i * tk, (i + 1) * tk)
            acc = _dot(t_ref[i], y)
            yc, ys = acc[:tk], acc[tk:]
            hr = h_ref[0, o, 0, rows, :]
            hi = h_ref[0, o, 1, rows, :]
            z_ref[0, rows, :] = (yc * hr + ys * hi).astype(BF16)
            z_ref[1, rows, :] = (yc * hi - ys * hr).astype(BF16)
        gate = short_conv(1 + o)
        zr, zi = z_ref[0], z_ref[1]
        for i in range(nkt):
            rows = slice(i * tk, (i + 1) * tk)
            conv = (_dot(t_ref[i, :tk, :], zr) - _dot(t_ref[i, tk:, :], zi)) * (1.0 / L)
            new = gate[rows] * (conv + skip_ref[o:o + 1, :] * y_ref[rows, :])
            if o == HY_ORDER - 1:
                o_ref[0, rows, :] = new
            else:
                y_ref[rows, :] = new


def hyena(hy, lp, layer, spec, table, B, L):
    C = HY_WIDTH
    return pl.pallas_call(
        _hyena_kernel,
        grid=(B,),
        in_specs=[_resident(table.shape),
                  pl.BlockSpec((1, HY_ORDER, 2, L, C), lambda b: (layer, 0, 0, 0, 0), pipeline_mode=pl.Buffered(1)),
                  pl.BlockSpec((1, L, HY_IN_W), lambda b: (b, 0, 0)),
                  _resident((3, HY_IN_W)), _resident((1, HY_IN_W)), _resident((HY_ORDER, C))],
        out_specs=pl.BlockSpec((1, L, C), lambda b: (b, 0, 0)),
        out_shape=jax.ShapeDtypeStruct((B, L, C), F32),
        scratch_shapes=[pltpu.VMEM((2, L, C), BF16), pltpu.VMEM((L, C), F32)],
        compiler_params=_cparams(56),
        name="hyena",
    )(table, spec, hy.reshape(B, L, HY_IN_W), lp['hy_short_w'], lp['hy_short_b'].reshape(1, HY_IN_W),
      lp['hy_skip']).reshape(B * L, C)


def _merge_kernel(a_ref, b_ref, c_ref, d_ref, x_ref, mod_ref, nw_ref, wg_ref, wb_ref, wo_ref, o_ref):
    rows = x_ref.shape[0] // MERGE_SPLIT
    groups = [slice(g * rows, (g + 1) * rows) for g in range(MERGE_SPLIT)]
    hs = [_norm_mod(x_ref[g, :], nw_ref[...], mod_ref[0, 3:4, :], mod_ref[0, 4:5, :]).astype(BF16) for g in groups]
    for g, h in zip(groups, hs):
        merged = None
        for i, br in enumerate((a_ref, b_ref, c_ref, d_ref)):
            gate = _sigmoid(_dot(h, wg_ref[:, MIX_W + i * D_MODEL:MIX_W + (i + 1) * D_MODEL]))
            t = gate * _dot(br[g, :].astype(BF16), wb_ref[i])
            merged = t if merged is None else merged + t
        o_ref[g, :] = x_ref[g, :] + mod_ref[0, 5:6, :] * _dot(merged.astype(BF16), wo_ref[...])


def merge_block(branches, x, mod, nw, wg, wb, wo, layer, rows_per_b):
    T = x.shape[0]
    tm = MERGE_ROWS
    nb = mod.shape[0]
    row = lambda w: pl.BlockSpec((tm, w), lambda i: (i, 0))
    return pl.pallas_call(
        _merge_kernel,
        grid=(T // tm,),
        in_specs=[row(BRANCH_W)] * N_BRANCH + [row(D_MODEL), _mod_spec(nb, rows_per_b // tm),
                                               _resident((1, D_MODEL)),
                                               _resident_layer(wg, layer),
                                               _resident_layer(wb, layer),
                                               _resident_layer(wo, layer)],
        out_specs=row(D_MODEL),
        out_shape=jax.ShapeDtypeStruct((T, D_MODEL), F32),
        compiler_params=_cparams(56),
        name="merge_block",
    )(*branches, x, mod, nw.reshape(1, D_MODEL), wg, wb, wo)


def _heads_in(t):
    B, H, P, d = t.shape
    return t.transpose(0, 2, 1, 3).reshape(B, P, H * d)


def _run_pass(x, mod_all, p, wts, final_norm, B, L, spec, tabs, caches):
    ctx_pass = caches is None
    collected = []
    if not ctx_pass:
        rope = rope_tables(L, HEAD_DIM, SWA_QW + SWA_KVW) + rope_tables(L, DIFF_HD, 2 * BRANCH_W)
        bias_all = na_bias_tables(p['na_rpb'], L // GRID_W)
    for l in range(DEPTH):
        lp = {k: v[l] for k, v in p.items()}
        w = wts
        mod = mod_all[l]
        x = ffn_block(x, mod, lp['norm_ffn1'], w['ffn1_w1'], w['ffn1_w3'], w['ffn1_w2'], l, 0, L)
        na, swa, hy, dif = in_projection(x, mod, lp['norm_mix'], w['w_in'], l, L,
                                         None if ctx_pass else rope, F32 if ctx_pass else BF16)
        if ctx_pass:
            a_o = dense_attention(na, (0, 1, 2), B, L)
            b_o = dense_attention(swa, (0, 2, 3), B, L, gqa=True, sink=lp['swa_sink'])
            d_o = diff_attention(dif, lp['diff_lambda'], lp['diff_subln'], l, B, L)
            collected.append((na, swa, dif))
        else:
            ck_na, cv_na, ck_swa, cv_swa, ck_d, cv_d = (_heads_in(t[:, l]).astype(BF16) for t in caches)
            a_o = neighbourhood_attention(na, ck_na, cv_na, bias_all, l, B, L)
            b_o = dense_attention(swa, (0, 2, 3), B, L, window=True, gqa=True,
                                  ctx=(ck_swa, cv_swa), sink=lp['swa_sink'])
            d_o = diff_attention(dif, lp['diff_lambda'], lp['diff_subln'], l, B, L, ctx=(ck_d, cv_d))
        c_o = hyena(hy, lp, l, spec, tabs, B, L)
        x = merge_block((a_o, b_o, c_o, d_o), x, mod, lp['norm_mix'], w['w_in'], w['w_branch'], w['w_out'], l, L)
        x = ffn_block(x, mod, lp['norm_ffn2'], w['ffn2_w1'], w['ffn2_w3'], w['ffn2_w2'], l, 6, L,
                      final_w=final_norm if l == DEPTH - 1 else None)
    return x, collected


_CACHE_SLOTS = ((0, N_HEADS, N_HEADS), (0, 2 * N_HEADS, N_HEADS),
                (1, N_HEADS, SWA_KV_HEADS), (1, N_HEADS + SWA_KV_HEADS, SWA_KV_HEADS),
                (2, N_HEADS, N_HEADS), (2, 2 * N_HEADS, N_HEADS))


def _cache_kernel(*refs):
    srcs, outs = refs[:3 * DEPTH], refs[3 * DEPTH:]
    for l in range(DEPTH):
        for o_ref, (src, slot0, n) in zip(outs, _CACHE_SLOTS):
            x_ref = srcs[3 * l + src]
            for h in range(n):
                o_ref[0, l, h] = x_ref[:, (slot0 + h) * HEAD_DIM:(slot0 + h + 1) * HEAD_DIM]


def _new_caches(collected, B, L):
    srcs = [t for layer in collected for t in layer]
    return pl.pallas_call(
        _cache_kernel,
        grid=(B,),
        in_specs=[pl.BlockSpec((L, t.shape[1]), lambda b: (b, 0)) for t in srcs],
        out_specs=[pl.BlockSpec((1, DEPTH, n, L, HEAD_DIM), lambda b: (b, 0, 0, 0, 0)) for _, _, n in _CACHE_SLOTS],
        out_shape=[jax.ShapeDtypeStruct((B, DEPTH, n, L, HEAD_DIM), F32) for _, _, n in _CACHE_SLOTS],
        compiler_params=_cparams(48),
        name="cache_outputs",
    )(*srcs)


def _hyena_setup(L, p, tk):
    filt = hyena_filters(L, p)
    spec = filter_spectra(filt.reshape(DEPTH * HY_ORDER, 2, L, HY_WIDTH), dft_table(L, tk, half_sample=False))
    return spec.reshape(DEPTH, HY_ORDER, 2, L, HY_WIDTH), dft_table(L, tk, half_sample=True)


def kernel(x_prompt, x_sample, cache_na_k, cache_na_v, cache_swa_k, cache_swa_v, cache_diff_k, cache_diff_v, c, c_ctx, w_ada, b_ada, norm_ffn1, norm_mix, norm_ffn2, final_norm, ffn1_w1, ffn1_w3, ffn1_w2, ffn2_w1, ffn2_w3, ffn2_w2, w_in, w_branch, w_out, na_rpb, swa_sink, hy_short_w, hy_short_b, hy_w1, hy_b1, hy_w2, hy_b2, hy_w3, hy_sin_freq, hy_log_decay, hy_skip, diff_lambda, diff_subln):
    B_ctx, L_ctx, _ = x_prompt.shape
    B_den, L_den, _ = x_sample.shape
    p = {
        'norm_ffn1': norm_ffn1, 'norm_mix': norm_mix, 'norm_ffn2': norm_ffn2,
        'na_rpb': na_rpb, 'swa_sink': swa_sink, 'hy_short_w': hy_short_w, 'hy_short_b': hy_short_b,
        'hy_w1': hy_w1, 'hy_b1': hy_b1, 'hy_w2': hy_w2, 'hy_b2': hy_b2, 'hy_w3': hy_w3,
        'hy_sin_freq': hy_sin_freq, 'hy_log_decay': hy_log_decay, 'hy_skip': hy_skip,
        'diff_lambda': diff_lambda, 'diff_subln': diff_subln,
    }
    big = {'ffn1_w1': ffn1_w1, 'ffn1_w3': ffn1_w3, 'ffn1_w2': ffn1_w2, 'ffn2_w1': ffn2_w1,
           'ffn2_w3': ffn2_w3, 'ffn2_w2': ffn2_w2, 'w_in': w_in, 'w_branch': w_branch, 'w_out': w_out}
    wts = {k: v.astype(BF16) for k, v in big.items()}

    cond = jnp.concatenate([c, c_ctx[None, :]], axis=0)
    rows = 8 * ((cond.shape[0] + 7) // 8)
    cond = jnp.pad(cond, ((0, rows - cond.shape[0]), (0, 0)))
    mod = ada_modulation(cond, w_ada, b_ada).reshape(DEPTH, rows, N_MOD, D_MODEL)
    mod_den = mod[:, :B_den]
    mod_ctx = mod[:, B_den:B_den + 1]

    spec_c, tabs_c = _hyena_setup(L_ctx, p, min(L_ctx, HY_FREQ_TILE))
    y_ctx, collected = _run_pass(x_prompt.reshape(B_ctx * L_ctx, D_MODEL), mod_ctx, p, wts, final_norm,
                                 B_ctx, L_ctx, spec_c, tabs_c, None)
    new_caches = _new_caches(collected, B_ctx, L_ctx)

    spec_d, tabs_d = _hyena_setup(L_den, p, min(L_den, HY_FREQ_TILE))
    caches = (cache_na_k, cache_na_v, cache_swa_k, cache_swa_v, cache_diff_k, cache_diff_v)
    y_den, _ = _run_pass(x_sample.reshape(B_den * L_den, D_MODEL), mod_den, p, wts, final_norm,
                         B_den, L_den, spec_d, tabs_d, caches)
    return (y_ctx.reshape(B_ctx, L_ctx, D_MODEL), y_den.reshape(B_den, L_den, D_MODEL), *new_caches)
```

```python
import functools
import math

import jax
import jax.numpy as jnp
from jax import lax
from jax.experimental import pallas as pl
from jax.experimental.pallas import tpu as pltpu

F32 = jnp.float32
BF16 = jnp.bfloat16

D_MODEL = 1024
DEPTH = 4
GRID_W = 64
N_BRANCH = 4
BRANCH_W = D_MODEL // 4
HEAD_DIM = 64
N_HEADS = BRANCH_W // HEAD_DIM
NA_WIN_R = 8
NA_WIN_C = 16
SWA_KV_HEADS = N_HEADS // 2
SWA_WINDOW = 128
HY_WIDTH = BRANCH_W
HY_ORDER = 2
HY_FREQS = 16
HY_EMB = 1 + 2 * HY_FREQS
HY_HIDDEN = 64
DIFF_HD = 32
D_FF = 128 * ((8 * D_MODEL // 3 + 127) // 128)
ROPE_BASE = 10000.0
EPS = 1e-6
NEG = -1e30
N_MOD = 9
NA_W = 3 * BRANCH_W
SWA_QW = BRANCH_W
SWA_KVW = SWA_KV_HEADS * HEAD_DIM
SWA_W = SWA_QW + 2 * SWA_KVW
HY_IN_W = 3 * HY_WIDTH
DIFF_W = 3 * BRANCH_W
GATE_W = N_BRANCH * D_MODEL
MIX_W = NA_W + SWA_W + HY_IN_W + DIFF_W

LOG2E = math.log2(math.e)
QK_SCALE = HEAD_DIM ** -0.5 * LOG2E
DIFF_QK_SCALE = DIFF_HD ** -0.5 * LOG2E

LANES = 128
MXU_DIM = 256
MIB = 1024 * 1024

FFN_ROWS = 1024
FFN_SPLIT = 2
MERGE_ROWS = 512
MERGE_SPLIT = 2
PROJ_ROWS = 512
PROJ_SPLIT = 2
FFN_CHUNK = 768
ATTN_Q = 256
ATTN_TILES = 4
NA_TILES = 4
HY_FREQ_TILE = 512
DIFF_Q = 512
DIFF_TILES = 2
NA_Q_ROWS = ATTN_Q // GRID_W
NA_SLAB_ROWS = NA_Q_ROWS + NA_WIN_R
SWA_SLAB = ATTN_Q + 2 * SWA_WINDOW


def _cparams(vmem_mib):
    return pltpu.CompilerParams(vmem_limit_bytes=vmem_mib * MIB)


def _resident(shape):
    nd = len(shape)
    return pl.BlockSpec(shape, lambda *_: (0,) * nd, pipeline_mode=pl.Buffered(1))


def _resident_layer(stacked, layer):
    shape = stacked.shape[1:]
    return pl.BlockSpec((None,) + shape, lambda *_: (layer,) + (0,) * len(shape), pipeline_mode=pl.Buffered(1))


def _dot(a, b):
    return jnp.dot(a, b, preferred_element_type=F32)


def _dot_nt(a, b):
    return lax.dot_general(a, b, (((1,), (1,)), ((), ())), preferred_element_type=F32)


def _dot_exact(a, b):
    return jnp.dot(a, b, preferred_element_type=F32, precision=lax.Precision.HIGHEST)


def _sigmoid(x):
    return 1.0 / (1.0 + jnp.exp(-x))


def _norm_mod(x, nw, shift, scale):
    return x * lax.rsqrt(jnp.mean(x * x, axis=-1, keepdims=True) + EPS) * (nw * (1.0 + scale)) + shift


def _mod_spec(nb, tiles_per_b):
    if nb == 1:
        return pl.BlockSpec((1, N_MOD, D_MODEL), lambda i: (0, 0, 0))
    return pl.BlockSpec((1, N_MOD, D_MODEL), lambda i: (i // tiles_per_b, 0, 0))


def _ada_kernel(c_ref, w_ref, b_ref, o_ref):
    c = c_ref[...]
    s = (c * _sigmoid(c)).astype(BF16)
    o_ref[0] = _dot(s, w_ref[0].astype(BF16)) + b_ref[0]


def ada_modulation(cond, w_ada, b_ada):
    rows = cond.shape[0]
    width = N_MOD * D_MODEL
    tn = 9 * LANES
    return pl.pallas_call(
        _ada_kernel,
        grid=(DEPTH, width // tn),
        in_specs=[pl.BlockSpec((rows, D_MODEL), lambda l, j: (0, 0)),
                  pl.BlockSpec((1, D_MODEL, tn), lambda l, j: (l, 0, j)),
                  pl.BlockSpec((1, 1, tn), lambda l, j: (l, 0, j))],
        out_specs=pl.BlockSpec((1, rows, tn), lambda l, j: (l, 0, j)),
        out_shape=jax.ShapeDtypeStruct((DEPTH, rows, width), F32),
        compiler_params=_cparams(32),
        name="ada_modulation",
    )(cond, w_ada, b_ada.reshape(DEPTH, 1, width))


def _ffn_kernel(x_ref, mod_ref, nw_ref, w1_ref, w3_ref, w2_ref, *rest, mod_base, final):
    if final:
        fw_ref, o_ref = rest
    else:
        (o_ref,) = rest
    shift = mod_ref[0, mod_base:mod_base + 1, :]
    scale = mod_ref[0, mod_base + 1:mod_base + 2, :]
    gate = mod_ref[0, mod_base + 2:mod_base + 3, :]
    rows = x_ref.shape[0] // FFN_SPLIT
    groups = [slice(g * rows, (g + 1) * rows) for g in range(FFN_SPLIT)]
    hs = [_norm_mod(x_ref[g, :], nw_ref[...], shift, scale).astype(BF16) for g in groups]
    for g, h in zip(groups, hs):
        acc = jnp.zeros((rows, D_MODEL), F32)
        for lo in range(0, D_FF, FFN_CHUNK):
            hi = min(lo + FFN_CHUNK, D_FF)
            a = _dot(h, w1_ref[:, lo:hi])
            b = _dot(h, w3_ref[:, lo:hi])
            u = (a * _sigmoid(a) * b).astype(BF16)
            acc = acc + _dot(u, w2_ref[lo:hi, :])
        y = x_ref[g, :] + 0.5 * gate * acc
        if final:
            y = y * lax.rsqrt(jnp.mean(y * y, axis=-1, keepdims=True) + EPS) * fw_ref[...]
        o_ref[g, :] = y


def ffn_block(x, mod, nw, w1, w3, w2, layer, mod_base, rows_per_b, final_w=None):
    T = x.shape[0]
    tm = FFN_ROWS
    nb = mod.shape[0]
    final = final_w is not None
    in_specs = [pl.BlockSpec((tm, D_MODEL), lambda i: (i, 0)),
                _mod_spec(nb, rows_per_b // tm),
                _resident((1, D_MODEL)),
                _resident_layer(w1, layer), _resident_layer(w3, layer), _resident_layer(w2, layer)]
    args = [x, mod, nw.reshape(1, D_MODEL), w1, w3, w2]
    if final:
        in_specs.append(_resident((1, D_MODEL)))
        args.append(final_w.reshape(1, D_MODEL))
    return pl.pallas_call(
        functools.partial(_ffn_kernel, mod_base=mod_base, final=final),
        grid=(T // tm,),
        in_specs=in_specs,
        out_specs=pl.BlockSpec((tm, D_MODEL), lambda i: (i, 0)),
        out_shape=jax.ShapeDtypeStruct((T, D_MODEL), F32),
        compiler_params=_cparams(56),
        name="ffn_block",
    )(*args)


def _rope_chunk(x, cos, sin_a, sin_b, dist):
    return x * cos + pltpu.roll(x, LANES - dist, 1) * sin_a + pltpu.roll(x, dist, 1) * sin_b


def _proj_kernel(x_ref, mod_ref, nw_ref, w_ref, *rest, rope):
    if rope:
        (cs_ref, sa_ref, sb_ref, cd_ref, da_ref, db_ref,
         na_ref, swa_ref, hy_ref, dif_ref) = rest
    else:
        na_ref, swa_ref, hy_ref, dif_ref = rest
    rows = x_ref.shape[0] // PROJ_SPLIT
    groups = [slice(g * rows, (g + 1) * rows) for g in range(PROJ_SPLIT)]
    hs = [_norm_mod(x_ref[g, :], nw_ref[...], mod_ref[0, 3:4, :], mod_ref[0, 4:5, :]).astype(BF16) for g in groups]
    q_chunks = BRANCH_W // LANES
    for g, h in zip(groups, hs):
        o = 0
        s = _dot(h, w_ref[:, o:o + NA_W])
        o += NA_W
        na_ref[g, :BRANCH_W] = (s[:, :BRANCH_W] * QK_SCALE).astype(na_ref.dtype)
        na_ref[g, BRANCH_W:] = s[:, BRANCH_W:].astype(na_ref.dtype)
        s = _dot(h, w_ref[:, o:o + SWA_W])
        o += SWA_W
        n_rot = (SWA_QW + SWA_KVW) // LANES
        for c in range(SWA_W // LANES):
            sl = slice(c * LANES, (c + 1) * LANES)
            chunk = s[:, sl]
            if rope and c < n_rot:
                chunk = _rope_chunk(chunk, cs_ref[g, sl], sa_ref[g, sl], sb_ref[g, sl], HEAD_DIM // 4)
            if c < q_chunks:
                chunk = chunk * QK_SCALE
            swa_ref[g, sl] = chunk.astype(swa_ref.dtype)
        hy_ref[g, :] = _dot(h, w_ref[:, o:o + HY_IN_W])
        o += HY_IN_W
        s = _dot(h, w_ref[:, o:o + DIFF_W])
        o += DIFF_W
        n_rot = 2 * BRANCH_W // LANES
        for c in range(DIFF_W // LANES):
            sl = slice(c * LANES, (c + 1) * LANES)
            chunk = s[:, sl]
            if rope and c < n_rot:
                chunk = _rope_chunk(chunk, cd_ref[g, sl], da_ref[g, sl], db_ref[g, sl], DIFF_HD // 4)
            if c < q_chunks:
                chunk = chunk * DIFF_QK_SCALE
            dif_ref[g, sl] = chunk.astype(dif_ref.dtype)


def in_projection(x, mod, nw, w_in, layer, rows_per_b, rope_tabs, qkv_dtype):
    T = x.shape[0]
    tm = PROJ_ROWS
    nb = mod.shape[0]
    rope = rope_tabs is not None
    if rope:
        pos_tiles = rows_per_b // tm
        grid = (pos_tiles, T // rows_per_b)
        row_tile = lambda p, b: (b * pos_tiles + p, 0)
        mod_spec = pl.BlockSpec((1, N_MOD, D_MODEL), lambda p, b: (b, 0, 0))
    else:
        grid = (T // tm,)
        row_tile = lambda i: (i, 0)
        mod_spec = _mod_spec(nb, rows_per_b // tm)
    in_specs = [pl.BlockSpec((tm, D_MODEL), row_tile),
                mod_spec,
                _resident((1, D_MODEL)),
                pl.BlockSpec((None, D_MODEL, MIX_W), lambda *_: (layer, 0, 0), pipeline_mode=pl.Buffered(1))]
    args = [x, mod, nw.reshape(1, D_MODEL), w_in]
    if rope:
        for t in rope_tabs:
            in_specs.append(pl.BlockSpec((tm, t.shape[1]), lambda p, b: (p, 0)))
            args.append(t)
    widths = (NA_W, SWA_W, HY_IN_W, DIFF_W)
    dtypes = (qkv_dtype, qkv_dtype, F32, qkv_dtype)
    return pl.pallas_call(
        functools.partial(_proj_kernel, rope=rope),
        grid=grid,
        in_specs=in_specs,
        out_specs=[pl.BlockSpec((tm, w), row_tile) for w in widths],
        out_shape=[jax.ShapeDtypeStruct((T, w), dt) for w, dt in zip(widths, dtypes)],
        compiler_params=_cparams(56),
        name="in_projection",
    )(*args)


def rope_tables(L, dh, width):
    nf = dh // 4
    t = jnp.arange(L)
    freqs = ROPE_BASE ** (-jnp.arange(nf, dtype=F32) / nf)
    ang_r = (t // GRID_W).astype(F32)[:, None] * freqs
    ang_c = (t % GRID_W).astype(F32)[:, None] * freqs
    cr, sr, cc, sc = jnp.cos(ang_r), jnp.sin(ang_r), jnp.cos(ang_c), jnp.sin(ang_c)
    z = jnp.zeros_like(sr)
    reps = width // dh
    cos = jnp.tile(jnp.concatenate([cr, cr, cc, cc], axis=-1), (1, reps))
    sin_a = jnp.tile(jnp.concatenate([-sr, z, -sc, z], axis=-1), (1, reps))
    sin_b = jnp.tile(jnp.concatenate([z, sr, z, sc], axis=-1), (1, reps))
    return cos, sin_a, sin_b


def _lane_mask(width, lo, n):
    lane = lax.broadcasted_iota(jnp.int32, (1, width), 1)
    return (lane >= lo) & (lane < lo + n)


def _softmax_parts(parts, extra_logit=None):
    m = parts[0].max(axis=-1, keepdims=True)
    for s in parts[1:]:
        m = jnp.maximum(m, s.max(axis=-1, keepdims=True))
    if extra_logit is not None:
        m = jnp.maximum(m, extra_logit)
    ps = [jnp.exp2(s - m) for s in parts]
    l = ps[0].sum(axis=-1, keepdims=True)
    for p in ps[1:]:
        l = l + p.sum(axis=-1, keepdims=True)
    if extra_logit is not None:
        l = l + jnp.exp2(extra_logit - m)
    return ps, l


def _na_kernel(q_ref, k_ref, v_ref, kc_ref, vc_ref, *rest, n_rows):
    bias_refs, o_ref = rest[:-1], rest[-1]
    slab = NA_SLAB_ROWS * GRID_W
    kc = kc_ref[0].astype(BF16)
    vc = vc_ref[0].astype(BF16)
    for sub, bias_ref in enumerate(bias_refs):
        tile = pl.program_id(1) * len(bias_refs) + sub
        rows = slice(sub * ATTN_Q, (sub + 1) * ATTN_Q)
        row0 = jnp.clip(NA_Q_ROWS * tile - NA_WIN_R // 2, 0, n_rows - NA_SLAB_ROWS)
        start = pl.multiple_of(row0 * GRID_W, GRID_W)
        ks = k_ref[0, pl.ds(start, slab), :].astype(BF16)
        vs = v_ref[0, pl.ds(start, slab), :].astype(BF16)
        q = q_ref[0, rows, :].astype(BF16)
        out = jnp.zeros(q.shape, F32)
        for h in range(N_HEADS):
            hm = _lane_mask(BRANCH_W, h * HEAD_DIM, HEAD_DIM)
            qh = jnp.where(hm, q, 0.0)
            s_win = _dot_nt(qh, ks) + bias_ref[0, 0, h]
            s_ctx = _dot_nt(qh, kc)
            (p_win, p_ctx), l = _softmax_parts([s_win, s_ctx])
            o = _dot(p_win.astype(BF16), vs) + _dot(p_ctx.astype(BF16), vc)
            out = out + jnp.where(hm, o / l, 0.0)
        o_ref[0, rows, :] = out


NA_DROWS = 2 * NA_WIN_R - 1
NA_DCOLS = 2 * NA_WIN_C - 1


def _na_tile_geometry(n_rows):
    n_tiles = n_rows // NA_Q_ROWS

    def geometry(tile):
        slab0 = min(max(NA_Q_ROWS * tile - NA_WIN_R // 2, 0), n_rows - NA_SLAB_ROWS)
        rows = []
        for j in range(NA_Q_ROWS):
            qr = NA_Q_ROWS * tile + j
            r0 = min(max(qr - NA_WIN_R // 2, 0), n_rows - NA_WIN_R)
            rows.append([slab0 + m - qr + NA_WIN_R - 1 if r0 <= slab0 + m < r0 + NA_WIN_R else None
                         for m in range(NA_SLAB_ROWS)])
        return rows

    kinds = [geometry(0), geometry(1), geometry(n_tiles - 1)]
    assert all(geometry(t) == kinds[1] for t in range(1, n_tiles - 1))
    return kinds


def _na_bias_kernel(rpb_ref, o_ref, band_ref, *, kinds):
    base = (pl.program_id(0) * N_HEADS + pl.program_id(1)) * (NA_DROWS * NA_DCOLS)
    qc = lax.broadcasted_iota(jnp.int32, (GRID_W, LANES), 0)
    lane = lax.broadcasted_iota(jnp.int32, (GRID_W, LANES), 1)
    kc = lane % GRID_W
    d_col = kc - qc + (NA_WIN_C - 1)
    c0 = jnp.clip(qc - NA_WIN_C // 2, 0, GRID_W - NA_WIN_C)
    col_ok = (kc >= c0) & (kc < c0 + NA_WIN_C)
    for d in range(NA_DROWS):
        t = jnp.full((GRID_W, LANES), NEG, F32)
        for e in range(NA_DCOLS):
            t = jnp.where(d_col == e, rpb_ref[base + d * NA_DCOLS + e] * LOG2E, t)
        band_ref[d] = jnp.where(col_ok, t, NEG)
    masked = jnp.full((GRID_W, LANES), NEG, F32)
    left = lane < GRID_W
    for kind, rows in enumerate(kinds):
        for j, drow in enumerate(rows):
            for m in range(0, NA_SLAB_ROWS, 2):
                a = masked if drow[m] is None else band_ref[drow[m]]
                b = masked if drow[m + 1] is None else band_ref[drow[m + 1]]
                o_ref[0, kind, 0, j * GRID_W:(j + 1) * GRID_W, m * GRID_W:(m + 2) * GRID_W] = jnp.where(left, a, b)


def na_bias_tables(rpb, n_rows):
    slab = NA_SLAB_ROWS * GRID_W
    return pl.pallas_call(
        functools.partial(_na_bias_kernel, kinds=_na_tile_geometry(n_rows)),
        grid=(DEPTH, N_HEADS),
        in_specs=[pl.BlockSpec(memory_space=pltpu.SMEM)],
        out_specs=pl.BlockSpec((1, 3, 1, ATTN_Q, slab), lambda l, h: (l, 0, h, 0, 0)),
        out_shape=jax.ShapeDtypeStruct((DEPTH, 3, N_HEADS, ATTN_Q, slab), F32),
        scratch_shapes=[pltpu.VMEM((NA_DROWS, GRID_W, LANES), F32)],
        compiler_params=_cparams(32),
        name="na_bias_tables",
    )(rpb.astype(F32).reshape(-1))


def neighbourhood_attention(na, kc, vc, bias, layer, B, L):
    n_rows = L // GRID_W
    n_tiles = L // ATTN_Q
    na3 = na.reshape(B, L, NA_W)
    P = kc.shape[1]
    slab = NA_SLAB_ROWS * GRID_W

    def kind(sub):
        def index(b, i):
            tile = i * NA_TILES + sub
            return (layer, jnp.where(tile == 0, 0, jnp.where(tile == n_tiles - 1, 2, 1)), 0, 0, 0)
        return index

    tq = NA_TILES * ATTN_Q
    return pl.pallas_call(
        functools.partial(_na_kernel, n_rows=n_rows),
        grid=(B, L // tq),
        in_specs=[pl.BlockSpec((1, tq, BRANCH_W), lambda b, i: (b, i, 0)),
                  pl.BlockSpec((1, L, BRANCH_W), lambda b, i: (b, 0, 1)),
                  pl.BlockSpec((1, L, BRANCH_W), lambda b, i: (b, 0, 2)),
                  pl.BlockSpec((1, P, BRANCH_W), lambda b, i: (b, 0, 0)),
                  pl.BlockSpec((1, P, BRANCH_W), lambda b, i: (b, 0, 0))]
                 + [pl.BlockSpec((1, 1, N_HEADS, ATTN_Q, slab), kind(sub)) for sub in range(NA_TILES)],
        out_specs=pl.BlockSpec((1, tq, BRANCH_W), lambda b, i: (b, i, 0)),
        out_shape=jax.ShapeDtypeStruct((B, L, BRANCH_W), F32),
        compiler_params=_cparams(56),
        name="neighbourhood_attention",
    )(na3, na3, na3, kc, vc, *([bias] * NA_TILES)).reshape(B * L, BRANCH_W)


def _attn_kernel(*refs, L, window, gqa, has_ctx, has_sink):
    refs = list(refs)
    q_ref, k_ref, v_ref = refs[:3]
    pos = 3
    if has_ctx:
        kc_ref, vc_ref = refs[pos:pos + 2]
        pos += 2
    if has_sink:
        sink_ref = refs[pos]
        pos += 1
    o_ref = refs[pos]
    if has_ctx:
        kc = kc_ref[0].astype(BF16)
        vc = vc_ref[0].astype(BF16)
    n_sub = q_ref.shape[1] // ATTN_Q
    for sub in range(n_sub):
        rows = slice(sub * ATTN_Q, (sub + 1) * ATTN_Q)
        if window:
            slab = SWA_SLAB
            q0 = (pl.program_id(1) * n_sub + sub) * ATTN_Q
            start = pl.multiple_of(jnp.clip(q0 - SWA_WINDOW, 0, L - slab), SWA_WINDOW)
            q_pos = q0 + lax.broadcasted_iota(jnp.int32, (ATTN_Q, 1), 0)
            k_pos = start + lax.broadcasted_iota(jnp.int32, (1, slab), 1)
            ok = jnp.abs(k_pos - q_pos) <= SWA_WINDOW
            ks = k_ref[0, pl.ds(start, slab), :].astype(BF16)
            vs = v_ref[0, pl.ds(start, slab), :].astype(BF16)
        else:
            ks = k_ref[0].astype(BF16)
            vs = v_ref[0].astype(BF16)
        q = q_ref[0, rows, :].astype(F32)
        kv_w = ks.shape[1]
        halves = [jnp.zeros((ATTN_Q, LANES), F32), jnp.zeros((ATTN_Q, LANES), F32)]
        out = jnp.zeros(q.shape, F32)
        for h in range(N_HEADS):
            if gqa:
                kvh, slot = h // 2, h % 2
                qh = q[:, kvh * LANES:(kvh + 1) * LANES]
                if slot != kvh:
                    qh = pltpu.roll(qh, HEAD_DIM, 1)
                hm = _lane_mask(kv_w, kvh * HEAD_DIM, HEAD_DIM)
            else:
                qh = q
                hm = _lane_mask(kv_w, h * HEAD_DIM, HEAD_DIM)
            qh = jnp.where(hm, qh, 0.0).astype(BF16)
            s = _dot_nt(qh, ks)
            if window:
                s = jnp.where(ok, s, NEG)
            parts = [s]
            if has_ctx:
                parts.append(_dot_nt(qh, kc))
            ps, l = _softmax_parts(parts, sink_ref[h] * LOG2E if has_sink else None)
            o = _dot(ps[0].astype(BF16), vs)
            if has_ctx:
                o = o + _dot(ps[1].astype(BF16), vc)
            o = jnp.where(hm, o / l, 0.0)
            if gqa:
                if slot != kvh:
                    o = pltpu.roll(o, HEAD_DIM, 1)
                halves[kvh] = halves[kvh] + o
            else:
                out = out + o
        if gqa:
            o_ref[0, rows, 0:LANES] = halves[0]
            o_ref[0, rows, LANES:2 * LANES] = halves[1]
        else:
            o_ref[0, rows, :] = out


def dense_attention(src, cols, B, L, *, window=False, gqa=False, ctx=None, sink=None):
    W = src.shape[1]
    src3 = src.reshape(B, L, W)
    kv_w = SWA_KVW if gqa else BRANCH_W
    qc, kcol, vcol = cols
    tq = min(ATTN_TILES * ATTN_Q, L)
    in_specs = [pl.BlockSpec((1, tq, BRANCH_W), lambda b, i: (b, i, qc)),
                pl.BlockSpec((1, L, kv_w), lambda b, i: (b, 0, kcol)),
                pl.BlockSpec((1, L, kv_w), lambda b, i: (b, 0, vcol))]
    args = [src3, src3, src3]
    if ctx is not None:
        P = ctx[0].shape[1]
        in_specs += [pl.BlockSpec((1, P, kv_w), lambda b, i: (b, 0, 0))] * 2
        args += list(ctx)
    if sink is not None:
        in_specs.append(pl.BlockSpec(memory_space=pltpu.SMEM))
        args.append(sink)
    return pl.pallas_call(
        functools.partial(_attn_kernel, L=L, window=window, gqa=gqa,
                          has_ctx=ctx is not None, has_sink=sink is not None),
        grid=(B, L // tq),
        in_specs=in_specs,
        out_specs=pl.BlockSpec((1, tq, BRANCH_W), lambda b, i: (b, i, 0)),
        out_shape=jax.ShapeDtypeStruct((B, L, BRANCH_W), F32),
        compiler_params=_cparams(48),
        name="window_attention" if window else "dense_attention",
    )(*args).reshape(B * L, BRANCH_W)


def _diff_kernel(*refs, lam_init, has_ctx):
    refs = list(refs)
    lam_ref, gain_ref, ones_ref, q_ref, k_ref, v_ref = refs[:6]
    if has_ctx:
        kc_ref, vc_ref, o_ref = refs[6:]
    else:
        (o_ref,) = refs[6:]
    lv = lam_ref[...]
    lam = (jnp.exp(jnp.sum(lv[0:1] * lv[1:2], keepdims=True))
           - jnp.exp(jnp.sum(lv[2:3] * lv[3:4], keepdims=True)) + lam_init)
    kl = k_ref[0].astype(BF16)
    vl = v_ref[0].astype(BF16)
    if has_ctx:
        kc = kc_ref[0].astype(BF16)
        vc = vc_ref[0].astype(BF16)
    tq = min(DIFF_Q, q_ref.shape[1])
    for sub in range(q_ref.shape[1] // tq):
        rows = slice(sub * tq, (sub + 1) * tq)
        q = q_ref[0, rows, :].astype(BF16)
        out = jnp.zeros(q.shape, F32)
        for h in range(N_HEADS):
            o = None
            for mp in range(2):
                mm = _lane_mask(BRANCH_W, h * HEAD_DIM + mp * DIFF_HD, DIFF_HD)
                qm = jnp.where(mm, q, 0.0)
                parts = [_dot_nt(qm, kl)]
                if has_ctx:
                    parts.append(_dot_nt(qm, kc))
                ps, l = _softmax_parts(parts)
                pv = _dot(ps[0].astype(BF16), vl)
                if has_ctx:
                    pv = pv + _dot(ps[1].astype(BF16), vc)
                o = pv / l if mp == 0 else o - pv * (lam / l)
            out = out + jnp.where(_lane_mask(BRANCH_W, h * HEAD_DIM, HEAD_DIM), o, 0.0)
        ms = _dot_exact(out * out, ones_ref[...]) * (1.0 / HEAD_DIM)
        o_ref[0, rows, :] = out * lax.rsqrt(ms + EPS) * gain_ref[...] * (1.0 - lam_init)


def diff_attention(dif, lam_vec, subln, layer, B, L, ctx=None):
    lam_init = 0.8 - 0.6 * math.exp(-0.3 * layer)
    dif3 = dif.reshape(B, L, DIFF_W)
    head = jnp.arange(BRANCH_W) // HEAD_DIM
    ones = (head[:, None] == head[None, :]).astype(F32)
    gain = jnp.tile(subln.astype(F32), N_HEADS).reshape(1, BRANCH_W)
    tq = min(DIFF_TILES * DIFF_Q, L)
    in_specs = [_resident((4, DIFF_HD)), _resident((1, BRANCH_W)), _resident((BRANCH_W, BRANCH_W)),
                pl.BlockSpec((1, tq, BRANCH_W), lambda b, i: (b, i, 0)),
                pl.BlockSpec((1, L, BRANCH_W), lambda b, i: (b, 0, 1)),
                pl.BlockSpec((1, L, BRANCH_W), lambda b, i: (b, 0, 2))]
    args = [lam_vec, gain, ones, dif3, dif3, dif3]
    if ctx is not None:
        P = ctx[0].shape[1]
        in_specs += [pl.BlockSpec((1, P, BRANCH_W), lambda b, i: (b, 0, 0))] * 2
        args += list(ctx)
    return pl.pallas_call(
        functools.partial(_diff_kernel, lam_init=lam_init, has_ctx=ctx is not None),
        grid=(B, L // tq),
        in_specs=in_specs,
        out_specs=pl.BlockSpec((1, tq, BRANCH_W), lambda b, i: (b, i, 0)),
        out_shape=jax.ShapeDtypeStruct((B, L, BRANCH_W), F32),
        compiler_params=_cparams(56),
        name="diff_attention",
    )(*args).reshape(B * L, BRANCH_W)


def _filter_kernel(z_ref, w1_ref, b1_ref, w2_ref, b2_ref, w3_ref, fr_ref, ld_ref, o_ref):
    z = z_ref[...]
    tn = z[:, 0:1]
    fr = fr_ref[0]
    g = jnp.sin(fr * (_dot_exact(z, w1_ref[0]) + b1_ref[0]))
    g = jnp.sin(fr * (_dot_exact(g, w2_ref[0]) + b2_ref[0]))
    hf = _dot_exact(g, w3_ref[0]) * jnp.exp(-jnp.exp(ld_ref[0]) * tn)
    row = lax.broadcasted_iota(jnp.int32, (z.shape[0], 1), 0)
    for o in range(HY_ORDER):
        pos = hf[:, (2 * o) * HY_WIDTH:(2 * o + 1) * HY_WIDTH]
        neg = jnp.where(row == 0, 0.0, hf[:, (2 * o + 1) * HY_WIDTH:(2 * o + 2) * HY_WIDTH])
        norm = (jnp.sum(jnp.abs(pos), axis=0, keepdims=True)
                + jnp.sum(jnp.abs(neg), axis=0, keepdims=True) + EPS)
        o_ref[0, o, 0] = (neg + pos) / norm
        o_ref[0, o, 1] = (neg - pos) / norm


def hyena_filters(L, p):
    tn = jnp.arange(L, dtype=F32) / L
    ang = 2.0 * math.pi * tn[:, None] * jnp.arange(1, HY_FREQS + 1, dtype=F32)[None, :]
    z = jnp.concatenate([tn[:, None], jnp.cos(ang), jnp.sin(ang)], axis=-1)
    z = jnp.pad(z, ((0, 0), (0, HY_HIDDEN - HY_EMB)))
    w1 = jnp.pad(p['hy_w1'], ((0, 0), (0, HY_HIDDEN - HY_EMB), (0, 0)))
    fw = HY_ORDER * 2 * HY_WIDTH
    per_layer = lambda *shape: pl.BlockSpec((1,) + shape, lambda l: (l,) + (0,) * len(shape))
    return pl.pallas_call(
        _filter_kernel,
        grid=(DEPTH,),
        in_specs=[pl.BlockSpec((L, HY_HIDDEN), lambda l: (0, 0)),
                  per_layer(HY_HIDDEN, HY_HIDDEN), per_layer(1, HY_HIDDEN),
                  per_layer(HY_HIDDEN, HY_HIDDEN), per_layer(1, HY_HIDDEN),
                  per_layer(HY_HIDDEN, fw), per_layer(1, HY_HIDDEN), per_layer(1, fw)],
        out_specs=per_layer(HY_ORDER, 2, L, HY_WIDTH),
        out_shape=jax.ShapeDtypeStruct((DEPTH, HY_ORDER, 2, L, HY_WIDTH), F32),
        compiler_params=_cparams(56),
        name="hyena_filters",
    )(z, w1, p['hy_b1'].reshape(DEPTH, 1, HY_HIDDEN), p['hy_w2'], p['hy_b2'].reshape(DEPTH, 1, HY_HIDDEN),
      p['hy_w3'], p['hy_sin_freq'].reshape(DEPTH, 1, HY_HIDDEN), p['hy_log_decay'].reshape(DEPTH, 1, fw))


def dft_table(L, tk, half_sample):
    assert L % LANES == 0 and L & (L - 1) == 0 and L // DFT_LO <= LANES
    return pl.pallas_call(
        functools.partial(_dft_table_kernel, half_sample=half_sample),
        grid=(L // tk,),
        out_specs=pl.BlockSpec((1, 2 * tk, L), lambda i: (i, 0, 0)),
        out_shape=jax.ShapeDtypeStruct((L // tk, 2 * tk, L), BF16),
        compiler_params=_cparams(48),
        name="dft_table",
    )()


DFT_LO = 64


def _dft_table_kernel(o_ref, *, half_sample):
    _, tk2, L = o_ref.shape
    tk = tk2 // 2
    k2 = 2 * (pl.program_id(0) * tk + lax.broadcasted_iota(jnp.int32, (tk, LANES), 0)) + 1
    lane = lax.broadcasted_iota(jnp.int32, (tk, LANES), 1)

    def cos_sin(m):
        ang = ((k2 * m) & (8 * L - 1)).astype(F32) * (math.pi / (4 * L))
        return jnp.cos(ang), jnp.sin(ang)

    c_hi, s_hi = cos_sin(2 * DFT_LO * lane)
    c_lo, s_lo = cos_sin(2 * lane + (1 if half_sample else 0))
    n = lax.broadcasted_iota(jnp.int32, (LANES, L), 1)
    r = lax.broadcasted_iota(jnp.int32, (LANES, L), 0)
    pick_hi = (r == (n >> (DFT_LO.bit_length() - 1))).astype(F32)
    pick_lo = (r == (n & (DFT_LO - 1))).astype(F32)
    ch, sh = _dot_exact(c_hi, pick_hi), _dot_exact(s_hi, pick_hi)
    cl, sl = _dot_exact(c_lo, pick_lo), _dot_exact(s_lo, pick_lo)
    o_ref[0, :tk, :] = (ch * cl - sh * sl).astype(BF16)
    o_ref[0, tk:, :] = (sh * cl + ch * sl).astype(BF16)


def _spectrum_kernel(t_ref, f_ref, o_ref):
    tk = o_ref.shape[2]
    o_ref[0, 0] = _dot(t_ref[0, :tk, :], f_ref[0, 0].astype(BF16))
    o_ref[0, 1] = _dot(t_ref[0, tk:, :], f_ref[0, 1].astype(BF16))


def filter_spectra(filt, fwd):
    G, _, L, C = filt.shape
    nkt, tk2, _ = fwd.shape
    tk = tk2 // 2
    return pl.pallas_call(
        _spectrum_kernel,
        grid=(nkt, G),
        in_specs=[pl.BlockSpec((1, tk2, L), lambda i, g: (i, 0, 0)),
                  pl.BlockSpec((1, 2, L, C), lambda i, g: (g, 0, 0, 0))],
        out_specs=pl.BlockSpec((1, 2, tk, C), lambda i, g: (g, 0, i, 0)),
        out_shape=jax.ShapeDtypeStruct((G, 2, L, C), F32),
        compiler_params=_cparams(48),
        name="filter_spectra",
    )(fwd, filt)


def _hyena_kernel(t_ref, h_ref, u_ref, w_ref, b_ref, skip_ref, o_ref, z_ref, y_ref):
    nkt, tk2, L = t_ref.shape
    tk = tk2 // 2
    C = HY_WIDTH
    row = lax.broadcasted_iota(jnp.int32, (L, 1), 0)

    def short_conv(col):
        sl = slice(col * C, (col + 1) * C)
        u = u_ref[0, :, sl]
        prev = jnp.where(row == 0, 0.0, pltpu.roll(u, 1, 0))
        nxt = jnp.where(row == L - 1, 0.0, pltpu.roll(u, L - 1, 0))
        return prev * w_ref[0:1, sl] + u * w_ref[1:2, sl] + nxt * w_ref[2:3, sl] + b_ref[:, sl]

    y_ref[...] = short_conv(0)
    for o in range(HY_ORDER):
        y = y_ref[...].astype(BF16)
        for i in range(nkt):
            rows = slice(i * tk, (i + 1) * tk)
            acc = _dot(t_ref[i], y)
            yc, ys = acc[:tk], acc[tk:]
            hr = h_ref[0, o, 0, rows, :]
            hi = h_ref[0, o, 1, rows, :]
            z_ref[0, rows, :] = (yc * hr + ys * hi).astype(BF16)
            z_ref[1, rows, :] = (yc * hi - ys * hr).astype(BF16)
        gate = short_conv(1 + o)
        zr, zi = z_ref[0], z_ref[1]
        for i in range(nkt):
            rows = slice(i * tk, (i + 1) * tk)
            conv = (_dot(t_ref[i, :tk, :], zr) - _dot(t_ref[i, tk:, :], zi)) * (1.0 / L)
            new = gate[rows] * (conv + skip_ref[o:o + 1, :] * y_ref[rows, :])
            if o == HY_ORDER - 1:
                o_ref[0, rows, :] = new
            else:
                y_ref[rows, :] = new


def hyena(hy, lp, layer, spec, table, B, L):
    C = HY_WIDTH
    return pl.pallas_call(
        _hyena_kernel,
        grid=(B,),
        in_specs=[_resident(table.shape),
                  pl.BlockSpec((1, HY_ORDER, 2, L, C), lambda b: (layer, 0, 0, 0, 0), pipeline_mode=pl.Buffered(1)),
                  pl.BlockSpec((1, L, HY_IN_W), lambda b: (b, 0, 0)),
                  _resident((3, HY_IN_W)), _resident((1, HY_IN_W)), _resident((HY_ORDER, C))],
        out_specs=pl.BlockSpec((1, L, C), lambda b: (b, 0, 0)),
        out_shape=jax.ShapeDtypeStruct((B, L, C), F32),
        scratch_shapes=[pltpu.VMEM((2, L, C), BF16), pltpu.VMEM((L, C), F32)],
        compiler_params=_cparams(56),
        name="hyena",
    )(table, spec, hy.reshape(B, L, HY_IN_W), lp['hy_short_w'], lp['hy_short_b'].reshape(1, HY_IN_W),
      lp['hy_skip']).reshape(B * L, C)


def _merge_kernel(a_ref, b_ref, c_ref, d_ref, x_ref, mod_ref, nw_ref, wg_ref, wb_ref, wo_ref, o_ref):
    rows = x_ref.shape[0] // MERGE_SPLIT
    groups = [slice(g * rows, (g + 1) * rows) for g in range(MERGE_SPLIT)]
    hs = [_norm_mod(x_ref[g, :], nw_ref[...], mod_ref[0, 3:4, :], mod_ref[0, 4:5, :]).astype(BF16) for g in groups]
    for g, h in zip(groups, hs):
        merged = None
        for i, br in enumerate((a_ref, b_ref, c_ref, d_ref)):
            gate = _sigmoid(_dot(h, wg_ref[:, MIX_W + i * D_MODEL:MIX_W + (i + 1) * D_MODEL]))
            t = gate * _dot(br[g, :].astype(BF16), wb_ref[i])
            merged = t if merged is None else merged + t
        o_ref[g, :] = x_ref[g, :] + mod_ref[0, 5:6, :] * _dot(merged.astype(BF16), wo_ref[...])


def merge_block(branches, x, mod, nw, wg, wb, wo, layer, rows_per_b):
    T = x.shape[0]
    tm = MERGE_ROWS
    nb = mod.shape[0]
    row = lambda w: pl.BlockSpec((tm, w), lambda i: (i, 0))
    return pl.pallas_call(
        _merge_kernel,
        grid=(T // tm,),
        in_specs=[row(BRANCH_W)] * N_BRANCH + [row(D_MODEL), _mod_spec(nb, rows_per_b // tm),
                                               _resident((1, D_MODEL)),
                                               _resident_layer(wg, layer),
                                               _resident_layer(wb, layer),
                                               _resident_layer(wo, layer)],
        out_specs=row(D_MODEL),
        out_shape=jax.ShapeDtypeStruct((T, D_MODEL), F32),
        compiler_params=_cparams(56),
        name="merge_block",
    )(*branches, x, mod, nw.reshape(1, D_MODEL), wg, wb, wo)


def _heads_in(t):
    B, H, P, d = t.shape
    return t.transpose(0, 2, 1, 3).reshape(B, P, H * d)


def _run_pass(x, mod_all, p, wts, final_norm, B, L, spec, tabs, caches):
    ctx_pass = caches is None
    collected = []
    if not ctx_pass:
        rope = rope_tables(L, HEAD_DIM, SWA_QW + SWA_KVW) + rope_tables(L, DIFF_HD, 2 * BRANCH_W)
        bias_all = na_bias_tables(p['na_rpb'], L // GRID_W)
    for l in range(DEPTH):
        lp = {k: v[l] for k, v in p.items()}
        w = wts
        mod = mod_all[l]
        x = ffn_block(x, mod, lp['norm_ffn1'], w['ffn1_w1'], w['ffn1_w3'], w['ffn1_w2'], l, 0, L)
        na, swa, hy, dif = in_projection(x, mod, lp['norm_mix'], w['w_in'], l, L,
                                         None if ctx_pass else rope, F32 if ctx_pass else BF16)
        if ctx_pass:
            a_o = dense_attention(na, (0, 1, 2), B, L)
            b_o = dense_attention(swa, (0, 2, 3), B, L, gqa=True, sink=lp['swa_sink'])
            d_o = diff_attention(dif, lp['diff_lambda'], lp['diff_subln'], l, B, L)
            collected.append((na, swa, dif))
        else:
            ck_na, cv_na, ck_swa, cv_swa, ck_d, cv_d = (_heads_in(t[:, l]).astype(BF16) for t in caches)
            a_o = neighbourhood_attention(na, ck_na, cv_na, bias_all, l, B, L)
            b_o = dense_attention(swa, (0, 2, 3), B, L, window=True, gqa=True,
                                  ctx=(ck_swa, cv_swa), sink=lp['swa_sink'])
            d_o = diff_attention(dif, lp['diff_lambda'], lp['diff_subln'], l, B, L, ctx=(ck_d, cv_d))
        c_o = hyena(hy, lp, l, spec, tabs, B, L)
        x = merge_block((a_o, b_o, c_o, d_o), x, mod, lp['norm_mix'], w['w_in'], w['w_branch'], w['w_out'], l, L)
        x = ffn_block(x, mod, lp['norm_ffn2'], w['ffn2_w1'], w['ffn2_w3'], w['ffn2_w2'], l, 6, L,
                      final_w=final_norm if l == DEPTH - 1 else None)
    return x, collected


_CACHE_SLOTS = ((0, N_HEADS, N_HEADS), (0, 2 * N_HEADS, N_HEADS),
                (1, N_HEADS, SWA_KV_HEADS), (1, N_HEADS + SWA_KV_HEADS, SWA_KV_HEADS),
                (2, N_HEADS, N_HEADS), (2, 2 * N_HEADS, N_HEADS))


def _cache_kernel(*refs):
    srcs, outs = refs[:3 * DEPTH], refs[3 * DEPTH:]
    for l in range(DEPTH):
        for o_ref, (src, slot0, n) in zip(outs, _CACHE_SLOTS):
            x_ref = srcs[3 * l + src]
            for h in range(n):
                o_ref[0, l, h] = x_ref[:, (slot0 + h) * HEAD_DIM:(slot0 + h + 1) * HEAD_DIM]


def _new_caches(collected, B, L):
    srcs = [t for layer in collected for t in layer]
    return pl.pallas_call(
        _cache_kernel,
        grid=(B,),
        in_specs=[pl.BlockSpec((L, t.shape[1]), lambda b: (b, 0)) for t in srcs],
        out_specs=[pl.BlockSpec((1, DEPTH, n, L, HEAD_DIM), lambda b: (b, 0, 0, 0, 0)) for _, _, n in _CACHE_SLOTS],
        out_shape=[jax.ShapeDtypeStruct((B, DEPTH, n, L, HEAD_DIM), F32) for _, _, n in _CACHE_SLOTS],
        compiler_params=_cparams(48),
        name="cache_outputs",
    )(*srcs)


def _hyena_setup(L, p, tk):
    filt = hyena_filters(L, p)
    spec = filter_spectra(filt.reshape(DEPTH * HY_ORDER, 2, L, HY_WIDTH), dft_table(L, tk, half_sample=False))
    return spec.reshape(DEPTH, HY_ORDER, 2, L, HY_WIDTH), dft_table(L, tk, half_sample=True)


def kernel(x_prompt, x_sample, cache_na_k, cache_na_v, cache_swa_k, cache_swa_v, cache_diff_k, cache_diff_v, c, c_ctx, w_ada, b_ada, norm_ffn1, norm_mix, norm_ffn2, final_norm, ffn1_w1, ffn1_w3, ffn1_w2, ffn2_w1, ffn2_w3, ffn2_w2, w_in, w_branch, w_out, na_rpb, swa_sink, hy_short_w, hy_short_b, hy_w1, hy_b1, hy_w2, hy_b2, hy_w3, hy_sin_freq, hy_log_decay, hy_skip, diff_lambda, diff_subln):
    B_ctx, L_ctx, _ = x_prompt.shape
    B_den, L_den, _ = x_sample.shape
    p = {
        'norm_ffn1': norm_ffn1, 'norm_mix': norm_mix, 'norm_ffn2': norm_ffn2,
        'na_rpb': na_rpb, 'swa_sink': swa_sink, 'hy_short_w': hy_short_w, 'hy_short_b': hy_short_b,
        'hy_w1': hy_w1, 'hy_b1': hy_b1, 'hy_w2': hy_w2, 'hy_b2': hy_b2, 'hy_w3': hy_w3,
        'hy_sin_freq': hy_sin_freq, 'hy_log_decay': hy_log_decay, 'hy_skip': hy_skip,
        'diff_lambda': diff_lambda, 'diff_subln': diff_subln,
    }
    big = {'ffn1_w1': ffn1_w1, 'ffn1_w3': ffn1_w3, 'ffn1_w2': ffn1_w2, 'ffn2_w1': ffn2_w1,
           'ffn2_w3': ffn2_w3, 'ffn2_w2': ffn2_w2, 'w_in': w_in, 'w_branch': w_branch, 'w_out': w_out}
    wts = {k: v.astype(BF16) for k, v in big.items()}

    cond = jnp.concatenate([c, c_ctx[None, :]], axis=0)
    rows = 8 * ((cond.shape[0] + 7) // 8)
    cond = jnp.pad(cond, ((0, rows - cond.shape[0]), (0, 0)))
    mod = ada_modulation(cond, w_ada, b_ada).reshape(DEPTH, rows, N_MOD, D_MODEL)
    mod_den = mod[:, :B_den]
    mod_ctx = mod[:, B_den:B_den + 1]

    spec_c, tabs_c = _hyena_setup(L_ctx, p, min(L_ctx, HY_FREQ_TILE))
    y_ctx, collected = _run_pass(x_prompt.reshape(B_ctx * L_ctx, D_MODEL), mod_ctx, p, wts, final_norm,
                                 B_ctx, L_ctx, spec_c, tabs_c, None)
    new_caches = _new_caches(collected, B_ctx, L_ctx)

    spec_d, tabs_d = _hyena_setup(L_den, p, min(L_den, HY_FREQ_TILE))
    caches = (cache_na_k, cache_na_v, cache_swa_k, cache_swa_v, cache_diff_k, cache_diff_v)
    y_den, _ = _run_pass(x_sample.reshape(B_den * L_den, D_MODEL), mod_den, p, wts, final_norm,
                         B_den, L_den, spec_d, tabs_d, caches)
    return (y_ctx.reshape(B_ctx, L_ctx, D_MODEL), y_den.reshape(B_den, L_den, D_MODEL), *new_caches)
```

```python
import functools
import math

import jax
import jax.numpy as jnp
from jax import lax
from jax.experimental import pallas as pl
from jax.experimental.pallas import tpu as pltpu

F32 = jnp.float32
BF16 = jnp.bfloat16

D_MODEL = 1024
DEPTH = 4
GRID_W = 64
N_BRANCH = 4
BRANCH_W = D_MODEL // 4
HEAD_DIM = 64
N_HEADS = BRANCH_W // HEAD_DIM
NA_WIN_R = 8
NA_WIN_C = 16
SWA_KV_HEADS = N_HEADS // 2
SWA_WINDOW = 128
HY_WIDTH = BRANCH_W
HY_ORDER = 2
HY_FREQS = 16
HY_EMB = 1 + 2 * HY_FREQS
HY_HIDDEN = 64
DIFF_HD = 32
D_FF = 128 * ((8 * D_MODEL // 3 + 127) // 128)
ROPE_BASE = 10000.0
EPS = 1e-6
NEG = -1e30
N_MOD = 9
NA_W = 3 * BRANCH_W
SWA_QW = BRANCH_W
SWA_KVW = SWA_KV_HEADS * HEAD_DIM
SWA_W = SWA_QW + 2 * SWA_KVW
HY_IN_W = 3 * HY_WIDTH
DIFF_W = 3 * BRANCH_W
GATE_W = N_BRANCH * D_MODEL
MIX_W = NA_W + SWA_W + HY_IN_W + DIFF_W

LOG2E = math.log2(math.e)
QK_SCALE = HEAD_DIM ** -0.5 * LOG2E
DIFF_QK_SCALE = DIFF_HD ** -0.5 * LOG2E

LANES = 128
MXU_DIM = 256
MIB = 1024 * 1024

FFN_ROWS = 1024
FFN_SPLIT = 4
MERGE_ROWS = 512
MERGE_SPLIT = 2
PROJ_ROWS = 512
PROJ_SPLIT = 2
FFN_CHUNK = 768
ATTN_Q = 256
ATTN_TILES = 4
NA_TILES = 4
HY_FREQ_TILE = 512
DIFF_Q = 512
NA_Q_ROWS = ATTN_Q // GRID_W
NA_SLAB_ROWS = NA_Q_ROWS + NA_WIN_R
SWA_SLAB = ATTN_Q + 2 * SWA_WINDOW


def _cparams(vmem_mib):
    return pltpu.CompilerParams(vmem_limit_bytes=vmem_mib * MIB)


def _resident(shape):
    nd = len(shape)
    return pl.BlockSpec(shape, lambda *_: (0,) * nd, pipeline_mode=pl.Buffered(1))


def _resident_layer(stacked, layer):
    shape = stacked.shape[1:]
    return pl.BlockSpec((None,) + shape, lambda *_: (layer,) + (0,) * len(shape), pipeline_mode=pl.Buffered(1))


def _dot(a, b):
    return jnp.dot(a, b, preferred_element_type=F32)


def _dot_nt(a, b):
    return lax.dot_general(a, b, (((1,), (1,)), ((), ())), preferred_element_type=F32)


def _dot_exact(a, b):
    return jnp.dot(a, b, preferred_element_type=F32, precision=lax.Precision.HIGHEST)


def _sigmoid(x):
    return 1.0 / (1.0 + jnp.exp(-x))


def _norm_mod(x, nw, shift, scale):
    return x * lax.rsqrt(jnp.mean(x * x, axis=-1, keepdims=True) + EPS) * (nw * (1.0 + scale)) + shift


def _mod_spec(nb, tiles_per_b):
    if nb == 1:
        return pl.BlockSpec((1, N_MOD, D_MODEL), lambda i: (0, 0, 0))
    return pl.BlockSpec((1, N_MOD, D_MODEL), lambda i: (i // tiles_per_b, 0, 0))


def _ada_kernel(c_ref, w_ref, b_ref, o_ref):
    c = c_ref[...]
    s = (c * _sigmoid(c)).astype(BF16)
    o_ref[0] = _dot(s, w_ref[0].astype(BF16)) + b_ref[0]


def ada_modulation(cond, w_ada, b_ada):
    rows = cond.shape[0]
    width = N_MOD * D_MODEL
    tn = 9 * LANES
    return pl.pallas_call(
        _ada_kernel,
        grid=(DEPTH, width // tn),
        in_specs=[pl.BlockSpec((rows, D_MODEL), lambda l, j: (0, 0)),
                  pl.BlockSpec((1, D_MODEL, tn), lambda l, j: (l, 0, j)),
                  pl.BlockSpec((1, 1, tn), lambda l, j: (l, 0, j))],
        out_specs=pl.BlockSpec((1, rows, tn), lambda l, j: (l, 0, j)),
        out_shape=jax.ShapeDtypeStruct((DEPTH, rows, width), F32),
        compiler_params=_cparams(32),
        name="ada_modulation",
    )(cond, w_ada, b_ada.reshape(DEPTH, 1, width))


def _ffn_kernel(x_ref, mod_ref, nw_ref, w1_ref, w3_ref, w2_ref, *rest, mod_base, final):
    if final:
        fw_ref, o_ref = rest
    else:
        (o_ref,) = rest
    shift = mod_ref[0, mod_base:mod_base + 1, :]
    scale = mod_ref[0, mod_base + 1:mod_base + 2, :]
    gate = mod_ref[0, mod_base + 2:mod_base + 3, :]
    rows = x_ref.shape[0] // FFN_SPLIT
    groups = [slice(g * rows, (g + 1) * rows) for g in range(FFN_SPLIT)]
    hs = [_norm_mod(x_ref[g, :], nw_ref[...], shift, scale).astype(BF16) for g in groups]
    for g, h in zip(groups, hs):
        acc = jnp.zeros((rows, D_MODEL), F32)
        for lo in range(0, D_FF, FFN_CHUNK):
            hi = min(lo + FFN_CHUNK, D_FF)
            a = _dot(h, w1_ref[:, lo:hi])
            b = _dot(h, w3_ref[:, lo:hi])
            u = (a * _sigmoid(a) * b).astype(BF16)
            acc = acc + _dot(u, w2_ref[lo:hi, :])
        y = x_ref[g, :] + 0.5 * gate * acc
        if final:
            y = y * lax.rsqrt(jnp.mean(y * y, axis=-1, keepdims=True) + EPS) * fw_ref[...]
        o_ref[g, :] = y


def ffn_block(x, mod, nw, w1, w3, w2, layer, mod_base, rows_per_b, final_w=None):
    T = x.shape[0]
    tm = FFN_ROWS
    nb = mod.shape[0]
    final = final_w is not None
    in_specs = [pl.BlockSpec((tm, D_MODEL), lambda i: (i, 0)),
                _mod_spec(nb, rows_per_b // tm),
                _resident((1, D_MODEL)),
                _resident_layer(w1, layer), _resident_layer(w3, layer), _resident_layer(w2, layer)]
    args = [x, mod, nw.reshape(1, D_MODEL), w1, w3, w2]
    if final:
        in_specs.append(_resident((1, D_MODEL)))
        args.append(final_w.reshape(1, D_MODEL))
    return pl.pallas_call(
        functools.partial(_ffn_kernel, mod_base=mod_base, final=final),
        grid=(T // tm,),
        in_specs=in_specs,
        out_specs=pl.BlockSpec((tm, D_MODEL), lambda i: (i, 0)),
        out_shape=jax.ShapeDtypeStruct((T, D_MODEL), F32),
        compiler_params=_cparams(56),
        name="ffn_block",
    )(*args)


def _rope_chunk(x, cos, sin_a, sin_b, dist):
    return x * cos + pltpu.roll(x, LANES - dist, 1) * sin_a + pltpu.roll(x, dist, 1) * sin_b


def _proj_kernel(x_ref, mod_ref, nw_ref, w_ref, *rest, rope):
    if rope:
        (cs_ref, sa_ref, sb_ref, cd_ref, da_ref, db_ref,
         na_ref, swa_ref, hy_ref, dif_ref) = rest
    else:
        na_ref, swa_ref, hy_ref, dif_ref = rest
    rows = x_ref.shape[0] // PROJ_SPLIT
    groups = [slice(g * rows, (g + 1) * rows) for g in range(PROJ_SPLIT)]
    hs = [_norm_mod(x_ref[g, :], nw_ref[...], mod_ref[0, 3:4, :], mod_ref[0, 4:5, :]).astype(BF16) for g in groups]
    q_chunks = BRANCH_W // LANES
    for g, h in zip(groups, hs):
        o = 0
        s = _dot(h, w_ref[:, o:o + NA_W])
        o += NA_W
        na_ref[g, :BRANCH_W] = (s[:, :BRANCH_W] * QK_SCALE).astype(na_ref.dtype)
        na_ref[g, BRANCH_W:] = s[:, BRANCH_W:].astype(na_ref.dtype)
        s = _dot(h, w_ref[:, o:o + SWA_W])
        o += SWA_W
        n_rot = (SWA_QW + SWA_KVW) // LANES
        for c in range(SWA_W // LANES):
            sl = slice(c * LANES, (c + 1) * LANES)
            chunk = s[:, sl]
            if rope and c < n_rot:
                chunk = _rope_chunk(chunk, cs_ref[g, sl], sa_ref[g, sl], sb_ref[g, sl], HEAD_DIM // 4)
            if c < q_chunks:
                chunk = chunk * QK_SCALE
            swa_ref[g, sl] = chunk.astype(swa_ref.dtype)
        hy_ref[g, :] = _dot(h, w_ref[:, o:o + HY_IN_W])
        o += HY_IN_W
        s = _dot(h, w_ref[:, o:o + DIFF_W])
        o += DIFF_W
        n_rot = 2 * BRANCH_W // LANES
        for c in range(DIFF_W // LANES):
            sl = slice(c * LANES, (c + 1) * LANES)
            chunk = s[:, sl]
            if rope and c < n_rot:
                chunk = _rope_chunk(chunk, cd_ref[g, sl], da_ref[g, sl], db_ref[g, sl], DIFF_HD // 4)
            if c < q_chunks:
                chunk = chunk * DIFF_QK_SCALE
            dif_ref[g, sl] = chunk.astype(dif_ref.dtype)


def in_projection(x, mod, nw, w_in, layer, rows_per_b, rope_tabs, qkv_dtype):
    T = x.shape[0]
    tm = PROJ_ROWS
    nb = mod.shape[0]
    rope = rope_tabs is not None
    if rope:
        pos_tiles = rows_per_b // tm
        grid = (pos_tiles, T // rows_per_b)
        row_tile = lambda p, b: (b * pos_tiles + p, 0)
        mod_spec = pl.BlockSpec((1, N_MOD, D_MODEL), lambda p, b: (b, 0, 0))
    else:
        grid = (T // tm,)
        row_tile = lambda i: (i, 0)
        mod_spec = _mod_spec(nb, rows_per_b // tm)
    in_specs = [pl.BlockSpec((tm, D_MODEL), row_tile),
                mod_spec,
                _resident((1, D_MODEL)),
                pl.BlockSpec((None, D_MODEL, MIX_W), lambda *_: (layer, 0, 0), pipeline_mode=pl.Buffered(1))]
    args = [x, mod, nw.reshape(1, D_MODEL), w_in]
    if rope:
        for t in rope_tabs:
            in_specs.append(pl.BlockSpec((tm, t.shape[1]), lambda p, b: (p, 0)))
            args.append(t)
    widths = (NA_W, SWA_W, HY_IN_W, DIFF_W)
    dtypes = (qkv_dtype, qkv_dtype, F32, qkv_dtype)
    return pl.pallas_call(
        functools.partial(_proj_kernel, rope=rope),
        grid=grid,
        in_specs=in_specs,
        out_specs=[pl.BlockSpec((tm, w), row_tile) for w in widths],
        out_shape=[jax.ShapeDtypeStruct((T, w), dt) for w, dt in zip(widths, dtypes)],
        compiler_params=_cparams(56),
        name="in_projection",
    )(*args)


def rope_tables(L, dh, width):
    nf = dh // 4
    t = jnp.arange(L)
    freqs = ROPE_BASE ** (-jnp.arange(nf, dtype=F32) / nf)
    ang_r = (t // GRID_W).astype(F32)[:, None] * freqs
    ang_c = (t % GRID_W).astype(F32)[:, None] * freqs
    cr, sr, cc, sc = jnp.cos(ang_r), jnp.sin(ang_r), jnp.cos(ang_c), jnp.sin(ang_c)
    z = jnp.zeros_like(sr)
    reps = width // dh
    cos = jnp.tile(jnp.concatenate([cr, cr, cc, cc], axis=-1), (1, reps))
    sin_a = jnp.tile(jnp.concatenate([-sr, z, -sc, z], axis=-1), (1, reps))
    sin_b = jnp.tile(jnp.concatenate([z, sr, z, sc], axis=-1), (1, reps))
    return cos, sin_a, sin_b


def _lane_mask(width, lo, n):
    lane = lax.broadcasted_iota(jnp.int32, (1, width), 1)
    return (lane >= lo) & (lane < lo + n)


def _softmax_parts(parts, extra_logit=None):
    m = parts[0].max(axis=-1, keepdims=True)
    for s in parts[1:]:
        m = jnp.maximum(m, s.max(axis=-1, keepdims=True))
    if extra_logit is not None:
        m = jnp.maximum(m, extra_logit)
    ps = [jnp.exp2(s - m) for s in parts]
    l = ps[0].sum(axis=-1, keepdims=True)
    for p in ps[1:]:
        l = l + p.sum(axis=-1, keepdims=True)
    if extra_logit is not None:
        l = l + jnp.exp2(extra_logit - m)
    return ps, l


def _na_kernel(q_ref, k_ref, v_ref, kc_ref, vc_ref, *rest, n_rows):
    bias_refs, o_ref = rest[:-1], rest[-1]
    slab = NA_SLAB_ROWS * GRID_W
    kc = kc_ref[0].astype(BF16)
    vc = vc_ref[0].astype(BF16)
    for sub, bias_ref in enumerate(bias_refs):
        tile = pl.program_id(1) * len(bias_refs) + sub
        rows = slice(sub * ATTN_Q, (sub + 1) * ATTN_Q)
        row0 = jnp.clip(NA_Q_ROWS * tile - NA_WIN_R // 2, 0, n_rows - NA_SLAB_ROWS)
        start = pl.multiple_of(row0 * GRID_W, GRID_W)
        ks = k_ref[0, pl.ds(start, slab), :].astype(BF16)
        vs = v_ref[0, pl.ds(start, slab), :].astype(BF16)
        q = q_ref[0, rows, :].astype(BF16)
        out = jnp.zeros(q.shape, F32)
        for h in range(N_HEADS):
            hm = _lane_mask(BRANCH_W, h * HEAD_DIM, HEAD_DIM)
            qh = jnp.where(hm, q, 0.0)
            s_win = _dot_nt(qh, ks) + bias_ref[0, 0, h]
            s_ctx = _dot_nt(qh, kc)
            (p_win, p_ctx), l = _softmax_parts([s_win, s_ctx])
            o = _dot(p_win.astype(BF16), vs) + _dot(p_ctx.astype(BF16), vc)
            out = out + jnp.where(hm, o / l, 0.0)
        o_ref[0, rows, :] = out


NA_DROWS = 2 * NA_WIN_R - 1
NA_DCOLS = 2 * NA_WIN_C - 1


def _na_tile_geometry(n_rows):
    n_tiles = n_rows // NA_Q_ROWS

    def geometry(tile):
        slab0 = min(max(NA_Q_ROWS * tile - NA_WIN_R // 2, 0), n_rows - NA_SLAB_ROWS)
        rows = []
        for j in range(NA_Q_ROWS):
            qr = NA_Q_ROWS * tile + j
            r0 = min(max(qr - NA_WIN_R // 2, 0), n_rows - NA_WIN_R)
            rows.append([slab0 + m - qr + NA_WIN_R - 1 if r0 <= slab0 + m < r0 + NA_WIN_R else None
                         for m in range(NA_SLAB_ROWS)])
        return rows

    kinds = [geometry(0), geometry(1), geometry(n_tiles - 1)]
    assert all(geometry(t) == kinds[1] for t in range(1, n_tiles - 1))
    return kinds


def _na_bias_kernel(rpb_ref, o_ref, band_ref, *, kinds):
    base = (pl.program_id(0) * N_HEADS + pl.program_id(1)) * (NA_DROWS * NA_DCOLS)
    qc = lax.broadcasted_iota(jnp.int32, (GRID_W, LANES), 0)
    lane = lax.broadcasted_iota(jnp.int32, (GRID_W, LANES), 1)
    kc = lane % GRID_W
    d_col = kc - qc + (NA_WIN_C - 1)
    c0 = jnp.clip(qc - NA_WIN_C // 2, 0, GRID_W - NA_WIN_C)
    col_ok = (kc >= c0) & (kc < c0 + NA_WIN_C)
    for d in range(NA_DROWS):
        t = jnp.full((GRID_W, LANES), NEG, F32)
        for e in range(NA_DCOLS):
            t = jnp.where(d_col == e, rpb_ref[base + d * NA_DCOLS + e] * LOG2E, t)
        band_ref[d] = jnp.where(col_ok, t, NEG)
    masked = jnp.full((GRID_W, LANES), NEG, F32)
    left = lane < GRID_W
    for kind, rows in enumerate(kinds):
        for j, drow in enumerate(rows):
            for m in range(0, NA_SLAB_ROWS, 2):
                a = masked if drow[m] is None else band_ref[drow[m]]
                b = masked if drow[m + 1] is None else band_ref[drow[m + 1]]
                o_ref[0, kind, 0, j * GRID_W:(j + 1) * GRID_W, m * GRID_W:(m + 2) * GRID_W] = jnp.where(left, a, b)


def na_bias_tables(rpb, n_rows):
    slab = NA_SLAB_ROWS * GRID_W
    return pl.pallas_call(
        functools.partial(_na_bias_kernel, kinds=_na_tile_geometry(n_rows)),
        grid=(DEPTH, N_HEADS),
        in_specs=[pl.BlockSpec(memory_space=pltpu.SMEM)],
        out_specs=pl.BlockSpec((1, 3, 1, ATTN_Q, slab), lambda l, h: (l, 0, h, 0, 0)),
        out_shape=jax.ShapeDtypeStruct((DEPTH, 3, N_HEADS, ATTN_Q, slab), F32),
        scratch_shapes=[pltpu.VMEM((NA_DROWS, GRID_W, LANES), F32)],
        compiler_params=_cparams(32),
        name="na_bias_tables",
    )(rpb.astype(F32).reshape(-1))


def neighbourhood_attention(na, kc, vc, bias, layer, B, L):
    n_rows = L // GRID_W
    n_tiles = L // ATTN_Q
    na3 = na.reshape(B, L, NA_W)
    P = kc.shape[1]
    slab = NA_SLAB_ROWS * GRID_W

    def kind(sub):
        def index(b, i):
            tile = i * NA_TILES + sub
            return (layer, jnp.where(tile == 0, 0, jnp.where(tile == n_tiles - 1, 2, 1)), 0, 0, 0)
        return index

    tq = NA_TILES * ATTN_Q
    return pl.pallas_call(
        functools.partial(_na_kernel, n_rows=n_rows),
        grid=(B, L // tq),
        in_specs=[pl.BlockSpec((1, tq, BRANCH_W), lambda b, i: (b, i, 0)),
                  pl.BlockSpec((1, L, BRANCH_W), lambda b, i: (b, 0, 1)),
                  pl.BlockSpec((1, L, BRANCH_W), lambda b, i: (b, 0, 2)),
                  pl.BlockSpec((1, P, BRANCH_W), lambda b, i: (b, 0, 0)),
                  pl.BlockSpec((1, P, BRANCH_W), lambda b, i: (b, 0, 0))]
                 + [pl.BlockSpec((1, 1, N_HEADS, ATTN_Q, slab), kind(sub)) for sub in range(NA_TILES)],
        out_specs=pl.BlockSpec((1, tq, BRANCH_W), lambda b, i: (b, i, 0)),
        out_shape=jax.ShapeDtypeStruct((B, L, BRANCH_W), F32),
        compiler_params=_cparams(56),
        name="neighbourhood_attention",
    )(na3, na3, na3, kc, vc, *([bias] * NA_TILES)).reshape(B * L, BRANCH_W)


def _attn_kernel(*refs, L, window, gqa, has_ctx, has_sink):
    refs = list(refs)
    q_ref, k_ref, v_ref = refs[:3]
    pos = 3
    if has_ctx:
        kc_ref, vc_ref = refs[pos:pos + 2]
        pos += 2
    if has_sink:
        sink_ref = refs[pos]
        pos += 1
    o_ref = refs[pos]
    n_seq = q_ref.shape[0]
    n_sub = q_ref.shape[1] // ATTN_Q
    for seq, sub in [(a, b) for a in range(n_seq) for b in range(n_sub)]:
        rows = slice(sub * ATTN_Q, (sub + 1) * ATTN_Q)
        if has_ctx:
            kc = kc_ref[seq].astype(BF16)
            vc = vc_ref[seq].astype(BF16)
        if window:
            slab = SWA_SLAB
            q0 = (pl.program_id(1) * n_sub + sub) * ATTN_Q
            start = pl.multiple_of(jnp.clip(q0 - SWA_WINDOW, 0, L - slab), SWA_WINDOW)
            q_pos = q0 + lax.broadcasted_iota(jnp.int32, (ATTN_Q, 1), 0)
            k_pos = start + lax.broadcasted_iota(jnp.int32, (1, slab), 1)
            ok = jnp.abs(k_pos - q_pos) <= SWA_WINDOW
            ks = k_ref[seq, pl.ds(start, slab), :].astype(BF16)
            vs = v_ref[seq, pl.ds(start, slab), :].astype(BF16)
        else:
            ks = k_ref[seq].astype(BF16)
            vs = v_ref[seq].astype(BF16)
        q = q_ref[seq, rows, :].astype(F32)
        kv_w = ks.shape[1]
        halves = [jnp.zeros((ATTN_Q, LANES), F32), jnp.zeros((ATTN_Q, LANES), F32)]
        out = jnp.zeros(q.shape, F32)
        for h in range(N_HEADS):
            if gqa:
                kvh, slot = h // 2, h % 2
                qh = q[:, kvh * LANES:(kvh + 1) * LANES]
                if slot != kvh:
                    qh = pltpu.roll(qh, HEAD_DIM, 1)
                hm = _lane_mask(kv_w, kvh * HEAD_DIM, HEAD_DIM)
            else:
                qh = q
                hm = _lane_mask(kv_w, h * HEAD_DIM, HEAD_DIM)
            qh = jnp.where(hm, qh, 0.0).astype(BF16)
            s = _dot_nt(qh, ks)
            if window:
                s = jnp.where(ok, s, NEG)
            parts = [s]
            if has_ctx:
                parts.append(_dot_nt(qh, kc))
            ps, l = _softmax_parts(parts, sink_ref[h] * LOG2E if has_sink else None)
            o = _dot(ps[0].astype(BF16), vs)
            if has_ctx:
                o = o + _dot(ps[1].astype(BF16), vc)
            o = jnp.where(hm, o / l, 0.0)
            if gqa:
                if slot != kvh:
                    o = pltpu.roll(o, HEAD_DIM, 1)
                halves[kvh] = halves[kvh] + o
            else:
                out = out + o
        if gqa:
            o_ref[seq, rows, 0:LANES] = halves[0]
            o_ref[seq, rows, LANES:2 * LANES] = halves[1]
        else:
            o_ref[seq, rows, :] = out


def _seqs_per_step(B, L, queries):
    ns = max(1, queries // L)
    return ns if B % ns == 0 else 1


def dense_attention(src, cols, B, L, *, window=False, gqa=False, ctx=None, sink=None):
    W = src.shape[1]
    src3 = src.reshape(B, L, W)
    kv_w = SWA_KVW if gqa else BRANCH_W
    qc, kcol, vcol = cols
    tq = min(ATTN_TILES * ATTN_Q, L)
    ns = _seqs_per_step(B, L, ATTN_TILES * ATTN_Q)
    in_specs = [pl.BlockSpec((ns, tq, BRANCH_W), lambda b, i: (b, i, qc)),
                pl.BlockSpec((ns, L, kv_w), lambda b, i: (b, 0, kcol)),
                pl.BlockSpec((ns, L, kv_w), lambda b, i: (b, 0, vcol))]
    args = [src3, src3, src3]
    if ctx is not None:
        P = ctx[0].shape[1]
        in_specs += [pl.BlockSpec((ns, P, kv_w), lambda b, i: (b, 0, 0))] * 2
        args += list(ctx)
    if sink is not None:
        in_specs.append(pl.BlockSpec(memory_space=pltpu.SMEM))
        args.append(sink)
    return pl.pallas_call(
        functools.partial(_attn_kernel, L=L, window=window, gqa=gqa,
                          has_ctx=ctx is not None, has_sink=sink is not None),
        grid=(B // ns, L // tq),
        in_specs=in_specs,
        out_specs=pl.BlockSpec((ns, tq, BRANCH_W), lambda b, i: (b, i, 0)),
        out_shape=jax.ShapeDtypeStruct((B, L, BRANCH_W), F32),
        compiler_params=_cparams(48),
        name="window_attention" if window else "dense_attention",
    )(*args).reshape(B * L, BRANCH_W)


def _diff_kernel(*refs, lam_init, has_ctx):
    refs = list(refs)
    lam_ref, gain_ref, ones_ref, q_ref, k_ref, v_ref = refs[:6]
    if has_ctx:
        kc_ref, vc_ref, o_ref = refs[6:]
    else:
        (o_ref,) = refs[6:]
    lv = lam_ref[...]
    lam = (jnp.exp(jnp.sum(lv[0:1] * lv[1:2], keepdims=True))
           - jnp.exp(jnp.sum(lv[2:3] * lv[3:4], keepdims=True)) + lam_init)
    for seq in range(q_ref.shape[0]):
        kl = k_ref[seq].astype(BF16)
        vl = v_ref[seq].astype(BF16)
        if has_ctx:
            kc = kc_ref[seq].astype(BF16)
            vc = vc_ref[seq].astype(BF16)
        q = q_ref[seq].astype(BF16)
        out = jnp.zeros(q.shape, F32)
        for h in range(N_HEADS):
            o = None
            for mp in range(2):
                mm = _lane_mask(BRANCH_W, h * HEAD_DIM + mp * DIFF_HD, DIFF_HD)
                qm = jnp.where(mm, q, 0.0)
                parts = [_dot_nt(qm, kl)]
                if has_ctx:
                    parts.append(_dot_nt(qm, kc))
                ps, l = _softmax_parts(parts)
                pv = _dot(ps[0].astype(BF16), vl)
                if has_ctx:
                    pv = pv + _dot(ps[1].astype(BF16), vc)
                o = pv / l if mp == 0 else o - pv * (lam / l)
            out = out + jnp.where(_lane_mask(BRANCH_W, h * HEAD_DIM, HEAD_DIM), o, 0.0)
        ms = _dot_exact(out * out, ones_ref[...]) * (1.0 / HEAD_DIM)
        o_ref[seq] = out * lax.rsqrt(ms + EPS) * gain_ref[...] * (1.0 - lam_init)


def diff_attention(dif, lam_vec, subln, layer, B, L, ctx=None):
    lam_init = 0.8 - 0.6 * math.exp(-0.3 * layer)
    dif3 = dif.reshape(B, L, DIFF_W)
    head = jnp.arange(BRANCH_W) // HEAD_DIM
    ones = (head[:, None] == head[None, :]).astype(F32)
    gain = jnp.tile(subln.astype(F32), N_HEADS).reshape(1, BRANCH_W)
    tq = min(DIFF_Q, L)
    ns = _seqs_per_step(B, L, DIFF_Q)
    in_specs = [_resident((4, DIFF_HD)), _resident((1, BRANCH_W)), _resident((BRANCH_W, BRANCH_W)),
                pl.BlockSpec((ns, tq, BRANCH_W), lambda b, i: (b, i, 0)),
                pl.BlockSpec((ns, L, BRANCH_W), lambda b, i: (b, 0, 1)),
                pl.BlockSpec((ns, L, BRANCH_W), lambda b, i: (b, 0, 2))]
    args = [lam_vec, gain, ones, dif3, dif3, dif3]
    if ctx is not None:
        P = ctx[0].shape[1]
        in_specs += [pl.BlockSpec((ns, P, BRANCH_W), lambda b, i: (b, 0, 0))] * 2
        args += list(ctx)
    return pl.pallas_call(
        functools.partial(_diff_kernel, lam_init=lam_init, has_ctx=ctx is not None),
        grid=(B // ns, L // tq),
        in_specs=in_specs,
        out_specs=pl.BlockSpec((ns, tq, BRANCH_W), lambda b, i: (b, i, 0)),
        out_shape=jax.ShapeDtypeStruct((B, L, BRANCH_W), F32),
        compiler_params=_cparams(56),
        name="diff_attention",
    )(*args).reshape(B * L, BRANCH_W)


def _filter_kernel(z_ref, w1_ref, b1_ref, w2_ref, b2_ref, w3_ref, fr_ref, ld_ref, o_ref):
    z = z_ref[...]
    tn = z[:, 0:1]
    fr = fr_ref[0]
    g = jnp.sin(fr * (_dot_exact(z, w1_ref[0]) + b1_ref[0]))
    g = jnp.sin(fr * (_dot_exact(g, w2_ref[0]) + b2_ref[0]))
    hf = _dot_exact(g, w3_ref[0]) * jnp.exp(-jnp.exp(ld_ref[0]) * tn)
    row = lax.broadcasted_iota(jnp.int32, (z.shape[0], 1), 0)
    for o in range(HY_ORDER):
        pos = hf[:, (2 * o) * HY_WIDTH:(2 * o + 1) * HY_WIDTH]
        neg = jnp.where(row == 0, 0.0, hf[:, (2 * o + 1) * HY_WIDTH:(2 * o + 2) * HY_WIDTH])
        norm = (jnp.sum(jnp.abs(pos), axis=0, keepdims=True)
                + jnp.sum(jnp.abs(neg), axis=0, keepdims=True) + EPS)
        o_ref[0, o, 0] = (neg + pos) / norm
        o_ref[0, o, 1] = (neg - pos) / norm


def hyena_filters(L, p):
    tn = jnp.arange(L, dtype=F32) / L
    ang = 2.0 * math.pi * tn[:, None] * jnp.arange(1, HY_FREQS + 1, dtype=F32)[None, :]
    z = jnp.concatenate([tn[:, None], jnp.cos(ang), jnp.sin(ang)], axis=-1)
    z = jnp.pad(z, ((0, 0), (0, HY_HIDDEN - HY_EMB)))
    w1 = jnp.pad(p['hy_w1'], ((0, 0), (0, HY_HIDDEN - HY_EMB), (0, 0)))
    fw = HY_ORDER * 2 * HY_WIDTH
    per_layer = lambda *shape: pl.BlockSpec((1,) + shape, lambda l: (l,) + (0,) * len(shape))
    return pl.pallas_call(
        _filter_kernel,
        grid=(DEPTH,),
        in_specs=[pl.BlockSpec((L, HY_HIDDEN), lambda l: (0, 0)),
                  per_layer(HY_HIDDEN, HY_HIDDEN), per_layer(1, HY_HIDDEN),
                  per_layer(HY_HIDDEN, HY_HIDDEN), per_layer(1, HY_HIDDEN),
                  per_layer(HY_HIDDEN, fw), per_layer(1, HY_HIDDEN), per_layer(1, fw)],
        out_specs=per_layer(HY_ORDER, 2, L, HY_WIDTH),
        out_shape=jax.ShapeDtypeStruct((DEPTH, HY_ORDER, 2, L, HY_WIDTH), F32),
        compiler_params=_cparams(56),
        name="hyena_filters",
    )(z, w1, p['hy_b1'].reshape(DEPTH, 1, HY_HIDDEN), p['hy_w2'], p['hy_b2'].reshape(DEPTH, 1, HY_HIDDEN),
      p['hy_w3'], p['hy_sin_freq'].reshape(DEPTH, 1, HY_HIDDEN), p['hy_log_decay'].reshape(DEPTH, 1, fw))


def dft_table(L, tk, half_sample):
    assert L % LANES == 0 and L & (L - 1) == 0 and L // DFT_LO <= LANES
    return pl.pallas_call(
        functools.partial(_dft_table_kernel, half_sample=half_sample),
        grid=(L // tk,),
        out_specs=pl.BlockSpec((1, 2 * tk, L), lambda i: (i, 0, 0)),
        out_shape=jax.ShapeDtypeStruct((L // tk, 2 * tk, L), BF16),
        compiler_params=_cparams(48),
        name="dft_table",
    )()


DFT_LO = 64


def _dft_table_kernel(o_ref, *, half_sample):
    _, tk2, L = o_ref.shape
    tk = tk2 // 2
    k2 = 2 * (pl.program_id(0) * tk + lax.broadcasted_iota(jnp.int32, (tk, LANES), 0)) + 1
    lane = lax.broadcasted_iota(jnp.int32, (tk, LANES), 1)

    def cos_sin(m):
        ang = ((k2 * m) & (8 * L - 1)).astype(F32) * (math.pi / (4 * L))
        return jnp.cos(ang), jnp.sin(ang)

    c_hi, s_hi = cos_sin(2 * DFT_LO * lane)
    c_lo, s_lo = cos_sin(2 * lane + (1 if half_sample else 0))
    n = lax.broadcasted_iota(jnp.int32, (LANES, L), 1)
    r = lax.broadcasted_iota(jnp.int32, (LANES, L), 0)
    pick_hi = jnp.where(r == (n >> (DFT_LO.bit_length() - 1)), 1.0, 0.0).astype(BF16)
    pick_lo = jnp.where(r == (n & (DFT_LO - 1)), 1.0, 0.0).astype(BF16)

    def copy(t, pick):
        head = t.astype(BF16)
        return _dot(head, pick) + _dot((t - head.astype(F32)).astype(BF16), pick)

    ch, sh = copy(c_hi, pick_hi), copy(s_hi, pick_hi)
    cl, sl = copy(c_lo, pick_lo), copy(s_lo, pick_lo)
    o_ref[0, :tk, :] = (ch * cl - sh * sl).astype(BF16)
    o_ref[0, tk:, :] = (sh * cl + ch * sl).astype(BF16)


def _spectrum_kernel(t_ref, f_ref, o_ref):
    tk = o_ref.shape[2]
    o_ref[0, 0] = _dot(t_ref[0, :tk, :], f_ref[0, 0].astype(BF16))
    o_ref[0, 1] = _dot(t_ref[0, tk:, :], f_ref[0, 1].astype(BF16))


def filter_spectra(filt, fwd):
    G, _, L, C = filt.shape
    nkt, tk2, _ = fwd.shape
    tk = tk2 // 2
    return pl.pallas_call(
        _spectrum_kernel,
        grid=(nkt, G),
        in_specs=[pl.BlockSpec((1, tk2, L), lambda i, g: (i, 0, 0)),
                  pl.BlockSpec((1, 2, L, C), lambda i, g: (g, 0, 0, 0))],
        out_specs=pl.BlockSpec((1, 2, tk, C), lambda i, g: (g, 0, i, 0)),
        out_shape=jax.ShapeDtypeStruct((G, 2, L, C), F32),
        compiler_params=_cparams(48),
        name="filter_spectra",
    )(fwd, filt)


def _hyena_kernel(t_ref, h_ref, u_ref, w_ref, b_ref, skip_ref, o_ref, z_ref, y_ref):
    nkt, tk2, L = t_ref.shape
    tk = tk2 // 2
    C = HY_WIDTH
    row = lax.broadcasted_iota(jnp.int32, (L, 1), 0)

    def short_conv(col):
        sl = slice(col * C, (col + 1) * C)
        u = u_ref[0, :, sl]
        prev = jnp.where(row == 0, 0.0, pltpu.roll(u, 1, 0))
        nxt = jnp.where(row == L - 1, 0.0, pltpu.roll(u, L - 1, 0))
        return prev * w_ref[0:1, sl] + u * w_ref[1:2, sl] + nxt * w_ref[2:3, sl] + b_ref[:, sl]

    y_ref[...] = short_conv(0)
    for o in range(HY_ORDER):
        y = y_ref[...].astype(BF16)
        for i in range(nkt):
            rows = slice(i * tk, (i + 1) * tk)
            acc = _dot(t_ref[i], y)
            yc, ys = acc[:tk], acc[tk:]
            hr = h_ref[0, o, 0, rows, :]
            hi = h_ref[0, o, 1, rows, :]
            z_ref[0, rows, :] = (yc * hr + ys * hi).astype(BF16)
            z_ref[1, rows, :] = (yc * hi - ys * hr).astype(BF16)
        gate = short_conv(1 + o)
        zr, zi = z_ref[0], z_ref[1]
        for i in range(nkt):
            rows = slice(i * tk, (i + 1) * tk)
            conv = (_dot(t_ref[i, :tk, :], zr) - _dot(t_ref[i, tk:, :], zi)) * (1.0 / L)
            new = gate[rows] * (conv + skip_ref[o:o + 1, :] * y_ref[rows, :])
            if o == HY_ORDER - 1:
                o_ref[0, rows, :] = new
            else:
                y_ref[rows, :] = new


def hyena(hy, lp, layer, spec, table, B, L):
    C = HY_WIDTH
    return pl.pallas_call(
        _hyena_kernel,
        grid=(B,),
        in_specs=[_resident(table.shape),
                  pl.BlockSpec((1, HY_ORDER, 2, L, C), lambda b: (layer, 0, 0, 0, 0), pipeline_mode=pl.Buffered(1)),
                  pl.BlockSpec((1, L, HY_IN_W), lambda b: (b, 0, 0)),
                  _resident((3, HY_IN_W)), _resident((1, HY_IN_W)), _resident((HY_ORDER, C))],
        out_specs=pl.BlockSpec((1, L, C), lambda b: (b, 0, 0)),
        out_shape=jax.ShapeDtypeStruct((B, L, C), F32),
        scratch_shapes=[pltpu.VMEM((2, L, C), BF16), pltpu.VMEM((L, C), F32)],
        compiler_params=_cparams(56),
        name="hyena",
    )(table, spec, hy.reshape(B, L, HY_IN_W), lp['hy_short_w'], lp['hy_short_b'].reshape(1, HY_IN_W),
      lp['hy_skip']).reshape(B * L, C)


def _merge_kernel(a_ref, b_ref, c_ref, d_ref, x_ref, mod_ref, nw_ref, wg_ref, wb_ref, wo_ref, o_ref):
    rows = x_ref.shape[0] // MERGE_SPLIT
    groups = [slice(g * rows, (g + 1) * rows) for g in range(MERGE_SPLIT)]
    hs = [_norm_mod(x_ref[g, :], nw_ref[...], mod_ref[0, 3:4, :], mod_ref[0, 4:5, :]).astype(BF16) for g in groups]
    for g, h in zip(groups, hs):
        merged = None
        for i, br in enumerate((a_ref, b_ref, c_ref, d_ref)):
            gate = _sigmoid(_dot(h, wg_ref[:, MIX_W + i * D_MODEL:MIX_W + (i + 1) * D_MODEL]))
            t = gate * _dot(br[g, :].astype(BF16), wb_ref[i])
            merged = t if merged is None else merged + t
        o_ref[g, :] = x_ref[g, :] + mod_ref[0, 5:6, :] * _dot(merged.astype(BF16), wo_ref[...])


def merge_block(branches, x, mod, nw, wg, wb, wo, layer, rows_per_b):
    T = x.shape[0]
    tm = MERGE_ROWS
    nb = mod.shape[0]
    row = lambda w: pl.BlockSpec((tm, w), lambda i: (i, 0))
    return pl.pallas_call(
        _merge_kernel,
        grid=(T // tm,),
        in_specs=[row(BRANCH_W)] * N_BRANCH + [row(D_MODEL), _mod_spec(nb, rows_per_b // tm),
                                               _resident((1, D_MODEL)),
                                               _resident_layer(wg, layer),
                                               _resident_layer(wb, layer),
                                               _resident_layer(wo, layer)],
        out_specs=row(D_MODEL),
        out_shape=jax.ShapeDtypeStruct((T, D_MODEL), F32),
        compiler_params=_cparams(56),
        name="merge_block",
    )(*branches, x, mod, nw.reshape(1, D_MODEL), wg, wb, wo)


def _heads_in(t):
    B, H, P, d = t.shape
    return t.transpose(0, 2, 1, 3).reshape(B, P, H * d)


def _run_pass(x, mod_all, p, wts, final_norm, B, L, spec, tabs, caches):
    ctx_pass = caches is None
    collected = []
    if not ctx_pass:
        rope = rope_tables(L, HEAD_DIM, SWA_QW + SWA_KVW) + rope_tables(L, DIFF_HD, 2 * BRANCH_W)
        bias_all = na_bias_tables(p['na_rpb'], L // GRID_W)
    for l in range(DEPTH):
        lp = {k: v[l] for k, v in p.items()}
        w = wts
        mod = mod_all[l]
        x = ffn_block(x, mod, lp['norm_ffn1'], w['ffn1_w1'], w['ffn1_w3'], w['ffn1_w2'], l, 0, L)
        na, swa, hy, dif = in_projection(x, mod, lp['norm_mix'], w['w_in'], l, L,
                                         None if ctx_pass else rope, F32 if ctx_pass else BF16)
        if ctx_pass:
            a_o = dense_attention(na, (0, 1, 2), B, L)
            b_o = dense_attention(swa, (0, 2, 3), B, L, gqa=True, sink=lp['swa_sink'])
            d_o = diff_attention(dif, lp['diff_lambda'], lp['diff_subln'], l, B, L)
            collected.append((na, swa, dif))
        else:
            ck_na, cv_na, ck_swa, cv_swa, ck_d, cv_d = (_heads_in(t[:, l]).astype(BF16) for t in caches)
            a_o = neighbourhood_attention(na, ck_na, cv_na, bias_all, l, B, L)
            b_o = dense_attention(swa, (0, 2, 3), B, L, window=True, gqa=True,
                                  ctx=(ck_swa, cv_swa), sink=lp['swa_sink'])
            d_o = diff_attention(dif, lp['diff_lambda'], lp['diff_subln'], l, B, L, ctx=(ck_d, cv_d))
        c_o = hyena(hy, lp, l, spec, tabs, B, L)
        x = merge_block((a_o, b_o, c_o, d_o), x, mod, lp['norm_mix'], w['w_in'], w['w_branch'], w['w_out'], l, L)
        x = ffn_block(x, mod, lp['norm_ffn2'], w['ffn2_w1'], w['ffn2_w3'], w['ffn2_w2'], l, 6, L,
                      final_w=final_norm if l == DEPTH - 1 else None)
    return x, collected


_CACHE_SLOTS = ((0, N_HEADS, N_HEADS), (0, 2 * N_HEADS, N_HEADS),
                (1, N_HEADS, SWA_KV_HEADS), (1, N_HEADS + SWA_KV_HEADS, SWA_KV_HEADS),
                (2, N_HEADS, N_HEADS), (2, 2 * N_HEADS, N_HEADS))


def _cache_kernel(*refs):
    srcs, outs = refs[:3 * DEPTH], refs[3 * DEPTH:]
    for l in range(DEPTH):
        for o_ref, (src, slot0, n) in zip(outs, _CACHE_SLOTS):
            x_ref = srcs[3 * l + src]
            for h in range(n):
                o_ref[0, l, h] = x_ref[:, (slot0 + h) * HEAD_DIM:(slot0 + h + 1) * HEAD_DIM]


def _new_caches(collected, B, L):
    srcs = [t for layer in collected for t in layer]
    return pl.pallas_call(
        _cache_kernel,
        grid=(B,),
        in_specs=[pl.BlockSpec((L, t.shape[1]), lambda b: (b, 0)) for t in srcs],
        out_specs=[pl.BlockSpec((1, DEPTH, n, L, HEAD_DIM), lambda b: (b, 0, 0, 0, 0)) for _, _, n in _CACHE_SLOTS],
        out_shape=[jax.ShapeDtypeStruct((B, DEPTH, n, L, HEAD_DIM), F32) for _, _, n in _CACHE_SLOTS],
        compiler_params=_cparams(48),
        name="cache_outputs",
    )(*srcs)


def _hyena_setup(L, p, tk):
    filt = hyena_filters(L, p)
    spec = filter_spectra(filt.reshape(DEPTH * HY_ORDER, 2, L, HY_WIDTH), dft_table(L, tk, half_sample=False))
    return spec.reshape(DEPTH, HY_ORDER, 2, L, HY_WIDTH), dft_table(L, tk, half_sample=True)


def kernel(x_prompt, x_sample, cache_na_k, cache_na_v, cache_swa_k, cache_swa_v, cache_diff_k, cache_diff_v, c, c_ctx, w_ada, b_ada, norm_ffn1, norm_mix, norm_ffn2, final_norm, ffn1_w1, ffn1_w3, ffn1_w2, ffn2_w1, ffn2_w3, ffn2_w2, w_in, w_branch, w_out, na_rpb, swa_sink, hy_short_w, hy_short_b, hy_w1, hy_b1, hy_w2, hy_b2, hy_w3, hy_sin_freq, hy_log_decay, hy_skip, diff_lambda, diff_subln):
    B_ctx, L_ctx, _ = x_prompt.shape
    B_den, L_den, _ = x_sample.shape
    p = {
        'norm_ffn1': norm_ffn1, 'norm_mix': norm_mix, 'norm_ffn2': norm_ffn2,
        'na_rpb': na_rpb, 'swa_sink': swa_sink, 'hy_short_w': hy_short_w, 'hy_short_b': hy_short_b,
        'hy_w1': hy_w1, 'hy_b1': hy_b1, 'hy_w2': hy_w2, 'hy_b2': hy_b2, 'hy_w3': hy_w3,
        'hy_sin_freq': hy_sin_freq, 'hy_log_decay': hy_log_decay, 'hy_skip': hy_skip,
        'diff_lambda': diff_lambda, 'diff_subln': diff_subln,
    }
    big = {'ffn1_w1': ffn1_w1, 'ffn1_w3': ffn1_w3, 'ffn1_w2': ffn1_w2, 'ffn2_w1': ffn2_w1,
           'ffn2_w3': ffn2_w3, 'ffn2_w2': ffn2_w2, 'w_in': w_in, 'w_branch': w_branch, 'w_out': w_out}
    wts = {k: v.astype(BF16) for k, v in big.items()}

    cond = jnp.concatenate([c, c_ctx[None, :]], axis=0)
    rows = 8 * ((cond.shape[0] + 7) // 8)
    cond = jnp.pad(cond, ((0, rows - cond.shape[0]), (0, 0)))
    mod = ada_modulation(cond, w_ada, b_ada).reshape(DEPTH, rows, N_MOD, D_MODEL)
    mod_den = mod[:, :B_den]
    mod_ctx = mod[:, B_den:B_den + 1]

    spec_c, tabs_c = _hyena_setup(L_ctx, p, min(L_ctx, HY_FREQ_TILE))
    y_ctx, collected = _run_pass(x_prompt.reshape(B_ctx * L_ctx, D_MODEL), mod_ctx, p, wts, final_norm,
                                 B_ctx, L_ctx, spec_c, tabs_c, None)
    new_caches = _new_caches(collected, B_ctx, L_ctx)

    spec_d, tabs_d = _hyena_setup(L_den, p, min(L_den, HY_FREQ_TILE))
    caches = (cache_na_k, cache_na_v, cache_swa_k, cache_swa_v, cache_diff_k, cache_diff_v)
    y_den, _ = _run_pass(x_sample.reshape(B_den * L_den, D_MODEL), mod_den, p, wts, final_norm,
                         B_den, L_den, spec_d, tabs_d, caches)
    return (y_ctx.reshape(B_ctx, L_ctx, D_MODEL), y_den.reshape(B_den, L_den, D_MODEL), *new_caches)
```

```python
import functools
import math

import jax
import jax.numpy as jnp
from jax import lax
from jax.experimental import pallas as pl
from jax.experimental.pallas import tpu as pltpu

F32 = jnp.float32
BF16 = jnp.bfloat16

D_MODEL = 1024
DEPTH = 4
GRID_W = 64
N_BRANCH = 4
BRANCH_W = D_MODEL // 4
HEAD_DIM = 64
N_HEADS = BRANCH_W // HEAD_DIM
NA_WIN_R = 8
NA_WIN_C = 16
SWA_KV_HEADS = N_HEADS // 2
SWA_WINDOW = 128
HY_WIDTH = BRANCH_W
HY_ORDER = 2
HY_FREQS = 16
HY_EMB = 1 + 2 * HY_FREQS
HY_HIDDEN = 64
DIFF_HD = 32
D_FF = 128 * ((8 * D_MODEL // 3 + 127) // 128)
ROPE_BASE = 10000.0
EPS = 1e-6
NEG = -1e30
N_MOD = 9
NA_W = 3 * BRANCH_W
SWA_QW = BRANCH_W
SWA_KVW = SWA_KV_HEADS * HEAD_DIM
SWA_W = SWA_QW + 2 * SWA_KVW
HY_IN_W = 3 * HY_WIDTH
DIFF_W = 3 * BRANCH_W
GATE_W = N_BRANCH * D_MODEL
MIX_W = NA_W + SWA_W + HY_IN_W + DIFF_W

LOG2E = math.log2(math.e)
QK_SCALE = HEAD_DIM ** -0.5 * LOG2E
DIFF_QK_SCALE = DIFF_HD ** -0.5 * LOG2E

LANES = 128
MXU_DIM = 256
MIB = 1024 * 1024

FFN_ROWS = 1024
FFN_SPLIT = 2
MERGE_ROWS = 512
MERGE_SPLIT = 2
PROJ_ROWS = 512
PROJ_SPLIT = 2
FFN_CHUNK = 768
ATTN_Q = 256
ATTN_TILES = 4
NA_TILES = 4
HY_FREQ_TILE = 512
DIFF_Q = 512
NA_Q_ROWS = ATTN_Q // GRID_W
NA_SLAB_ROWS = NA_Q_ROWS + NA_WIN_R
SWA_SLAB = ATTN_Q + 2 * SWA_WINDOW


def _cparams(vmem_mib):
    return pltpu.CompilerParams(vmem_limit_bytes=vmem_mib * MIB)


def _resident(shape):
    nd = len(shape)
    return pl.BlockSpec(shape, lambda *_: (0,) * nd, pipeline_mode=pl.Buffered(1))


def _resident_layer(stacked, layer):
    shape = stacked.shape[1:]
    return pl.BlockSpec((None,) + shape, lambda *_: (layer,) + (0,) * len(shape), pipeline_mode=pl.Buffered(1))


def _dot(a, b):
    return jnp.dot(a, b, preferred_element_type=F32)


def _dot_nt(a, b):
    return lax.dot_general(a, b, (((1,), (1,)), ((), ())), preferred_element_type=F32)


def _dot_exact(a, b):
    return jnp.dot(a, b, preferred_element_type=F32, precision=lax.Precision.HIGHEST)


def _sigmoid(x):
    return 1.0 / (1.0 + jnp.exp(-x))


def _norm_mod(x, nw, shift, scale):
    return x * lax.rsqrt(jnp.mean(x * x, axis=-1, keepdims=True) + EPS) * (nw * (1.0 + scale)) + shift


def _mod_spec(nb, tiles_per_b):
    if nb == 1:
        return pl.BlockSpec((1, N_MOD, D_MODEL), lambda i: (0, 0, 0))
    return pl.BlockSpec((1, N_MOD, D_MODEL), lambda i: (i // tiles_per_b, 0, 0))


def _ada_kernel(c_ref, w_ref, b_ref, o_ref):
    c = c_ref[...]
    s = (c * _sigmoid(c)).astype(BF16)
    o_ref[0] = _dot(s, w_ref[0].astype(BF16)) + b_ref[0]


def ada_modulation(cond, w_ada, b_ada):
    rows = cond.shape[0]
    width = N_MOD * D_MODEL
    tn = 9 * LANES
    return pl.pallas_call(
        _ada_kernel,
        grid=(DEPTH, width // tn),
        in_specs=[pl.BlockSpec((rows, D_MODEL), lambda l, j: (0, 0)),
                  pl.BlockSpec((1, D_MODEL, tn), lambda l, j: (l, 0, j)),
                  pl.BlockSpec((1, 1, tn), lambda l, j: (l, 0, j))],
        out_specs=pl.BlockSpec((1, rows, tn), lambda l, j: (l, 0, j)),
        out_shape=jax.ShapeDtypeStruct((DEPTH, rows, width), F32),
        compiler_params=_cparams(32),
        name="ada_modulation",
    )(cond, w_ada, b_ada.reshape(DEPTH, 1, width))


def _ffn_kernel(x_ref, mod_ref, nw_ref, w1_ref, w3_ref, w2_ref, *rest, mod_base, final):
    if final:
        fw_ref, o_ref = rest
    else:
        (o_ref,) = rest
    shift = mod_ref[0, mod_base:mod_base + 1, :]
    scale = mod_ref[0, mod_base + 1:mod_base + 2, :]
    gate = mod_ref[0, mod_base + 2:mod_base + 3, :]
    rows = x_ref.shape[0] // FFN_SPLIT
    groups = [slice(g * rows, (g + 1) * rows) for g in range(FFN_SPLIT)]
    hs = [_norm_mod(x_ref[g, :], nw_ref[...], shift, scale).astype(BF16) for g in groups]
    for g, h in zip(groups, hs):
        acc = jnp.zeros((rows, D_MODEL), F32)
        for lo in range(0, D_FF, FFN_CHUNK):
            hi = min(lo + FFN_CHUNK, D_FF)
            a = _dot(h, w1_ref[:, lo:hi])
            b = _dot(h, w3_ref[:, lo:hi])
            u = (a * _sigmoid(a) * b).astype(BF16)
            acc = acc + _dot(u, w2_ref[lo:hi, :])
        y = x_ref[g, :] + 0.5 * gate * acc
        if final:
            y = y * lax.rsqrt(jnp.mean(y * y, axis=-1, keepdims=True) + EPS) * fw_ref[...]
        o_ref[g, :] = y


def ffn_block(x, mod, nw, w1, w3, w2, layer, mod_base, rows_per_b, final_w=None):
    T = x.shape[0]
    tm = FFN_ROWS
    nb = mod.shape[0]
    final = final_w is not None
    in_specs = [pl.BlockSpec((tm, D_MODEL), lambda i: (i, 0)),
                _mod_spec(nb, rows_per_b // tm),
                _resident((1, D_MODEL)),
                _resident_layer(w1, layer), _resident_layer(w3, layer), _resident_layer(w2, layer)]
    args = [x, mod, nw.reshape(1, D_MODEL), w1, w3, w2]
    if final:
        in_specs.append(_resident((1, D_MODEL)))
        args.append(final_w.reshape(1, D_MODEL))
    return pl.pallas_call(
        functools.partial(_ffn_kernel, mod_base=mod_base, final=final),
        grid=(T // tm,),
        in_specs=in_specs,
        out_specs=pl.BlockSpec((tm, D_MODEL), lambda i: (i, 0)),
        out_shape=jax.ShapeDtypeStruct((T, D_MODEL), F32),
        compiler_params=_cparams(56),
        name="ffn_block",
    )(*args)


def _rope_chunk(x, cos, sin_a, sin_b, dist):
    return x * cos + pltpu.roll(x, LANES - dist, 1) * sin_a + pltpu.roll(x, dist, 1) * sin_b


def _proj_kernel(x_ref, mod_ref, nw_ref, w_ref, *rest, rope):
    if rope:
        (cs_ref, sa_ref, sb_ref, cd_ref, da_ref, db_ref,
         na_ref, swa_ref, hy_ref, dif_ref) = rest
    else:
        na_ref, swa_ref, hy_ref, dif_ref = rest
    rows = x_ref.shape[0] // PROJ_SPLIT
    groups = [slice(g * rows, (g + 1) * rows) for g in range(PROJ_SPLIT)]
    hs = [_norm_mod(x_ref[g, :], nw_ref[...], mod_ref[0, 3:4, :], mod_ref[0, 4:5, :]).astype(BF16) for g in groups]
    q_chunks = BRANCH_W // LANES
    for g, h in zip(groups, hs):
        o = 0
        s = _dot(h, w_ref[:, o:o + NA_W])
        o += NA_W
        na_ref[g, :BRANCH_W] = (s[:, :BRANCH_W] * QK_SCALE).astype(na_ref.dtype)
        na_ref[g, BRANCH_W:] = s[:, BRANCH_W:].astype(na_ref.dtype)
        s = _dot(h, w_ref[:, o:o + SWA_W])
        o += SWA_W
        n_rot = (SWA_QW + SWA_KVW) // LANES
        for c in range(SWA_W // LANES):
            sl = slice(c * LANES, (c + 1) * LANES)
            chunk = s[:, sl]
            if rope and c < n_rot:
                chunk = _rope_chunk(chunk, cs_ref[g, sl], sa_ref[g, sl], sb_ref[g, sl], HEAD_DIM // 4)
            if c < q_chunks:
                chunk = chunk * QK_SCALE
            swa_ref[g, sl] = chunk.astype(swa_ref.dtype)
        hy_ref[g, :] = _dot(h, w_ref[:, o:o + HY_IN_W])
        o += HY_IN_W
        s = _dot(h, w_ref[:, o:o + DIFF_W])
        o += DIFF_W
        n_rot = 2 * BRANCH_W // LANES
        for c in range(DIFF_W // LANES):
            sl = slice(c * LANES, (c + 1) * LANES)
            chunk = s[:, sl]
            if rope and c < n_rot:
                chunk = _rope_chunk(chunk, cd_ref[g, sl], da_ref[g, sl], db_ref[g, sl], DIFF_HD // 4)
            if c < q_chunks:
                chunk = chunk * DIFF_QK_SCALE
            dif_ref[g, sl] = chunk.astype(dif_ref.dtype)


def in_projection(x, mod, nw, w_in, layer, rows_per_b, rope_tabs, qkv_dtype):
    T = x.shape[0]
    tm = PROJ_ROWS
    nb = mod.shape[0]
    rope = rope_tabs is not None
    if rope:
        pos_tiles = rows_per_b // tm
        grid = (pos_tiles, T // rows_per_b)
        row_tile = lambda p, b: (b * pos_tiles + p, 0)
        mod_spec = pl.BlockSpec((1, N_MOD, D_MODEL), lambda p, b: (b, 0, 0))
    else:
        grid = (T // tm,)
        row_tile = lambda i: (i, 0)
        mod_spec = _mod_spec(nb, rows_per_b // tm)
    in_specs = [pl.BlockSpec((tm, D_MODEL), row_tile),
                mod_spec,
                _resident((1, D_MODEL)),
                pl.BlockSpec((None, D_MODEL, MIX_W), lambda *_: (layer, 0, 0), pipeline_mode=pl.Buffered(1))]
    args = [x, mod, nw.reshape(1, D_MODEL), w_in]
    if rope:
        for t in rope_tabs:
            in_specs.append(pl.BlockSpec((tm, t.shape[1]), lambda p, b: (p, 0)))
            args.append(t)
    widths = (NA_W, SWA_W, HY_IN_W, DIFF_W)
    dtypes = (qkv_dtype, qkv_dtype, F32, qkv_dtype)
    return pl.pallas_call(
        functools.partial(_proj_kernel, rope=rope),
        grid=grid,
        in_specs=in_specs,
        out_specs=[pl.BlockSpec((tm, w), row_tile) for w in widths],
        out_shape=[jax.ShapeDtypeStruct((T, w), dt) for w, dt in zip(widths, dtypes)],
        compiler_params=_cparams(56),
        name="in_projection",
    )(*args)


def rope_tables(L, dh, width):
    nf = dh // 4
    t = jnp.arange(L)
    freqs = ROPE_BASE ** (-jnp.arange(nf, dtype=F32) / nf)
    ang_r = (t // GRID_W).astype(F32)[:, None] * freqs
    ang_c = (t % GRID_W).astype(F32)[:, None] * freqs
    cr, sr, cc, sc = jnp.cos(ang_r), jnp.sin(ang_r), jnp.cos(ang_c), jnp.sin(ang_c)
    z = jnp.zeros_like(sr)
    reps = width // dh
    cos = jnp.tile(jnp.concatenate([cr, cr, cc, cc], axis=-1), (1, reps))
    sin_a = jnp.tile(jnp.concatenate([-sr, z, -sc, z], axis=-1), (1, reps))
    sin_b = jnp.tile(jnp.concatenate([z, sr, z, sc], axis=-1), (1, reps))
    return cos, sin_a, sin_b


def _lane_mask(width, lo, n):
    lane = lax.broadcasted_iota(jnp.int32, (1, width), 1)
    return (lane >= lo) & (lane < lo + n)


def _softmax_parts(parts, extra_logit=None):
    m = parts[0].max(axis=-1, keepdims=True)
    for s in parts[1:]:
        m = jnp.maximum(m, s.max(axis=-1, keepdims=True))
    if extra_logit is not None:
        m = jnp.maximum(m, extra_logit)
    ps = [jnp.exp2(s - m) for s in parts]
    l = ps[0].sum(axis=-1, keepdims=True)
    for p in ps[1:]:
        l = l + p.sum(axis=-1, keepdims=True)
    if extra_logit is not None:
        l = l + jnp.exp2(extra_logit - m)
    return ps, l


def _na_kernel(q_ref, k_ref, v_ref, kc_ref, vc_ref, *rest, n_rows):
    bias_refs, o_ref = rest[:-1], rest[-1]
    slab = NA_SLAB_ROWS * GRID_W
    kc = kc_ref[0].astype(BF16)
    vc = vc_ref[0].astype(BF16)
    for sub, bias_ref in enumerate(bias_refs):
        tile = pl.program_id(1) * len(bias_refs) + sub
        rows = slice(sub * ATTN_Q, (sub + 1) * ATTN_Q)
        row0 = jnp.clip(NA_Q_ROWS * tile - NA_WIN_R // 2, 0, n_rows - NA_SLAB_ROWS)
        start = pl.multiple_of(row0 * GRID_W, GRID_W)
        ks = k_ref[0, pl.ds(start, slab), :].astype(BF16)
        vs = v_ref[0, pl.ds(start, slab), :].astype(BF16)
        q = q_ref[0, rows, :].astype(BF16)
        out = jnp.zeros(q.shape, F32)
        for h in range(N_HEADS):
            hm = _lane_mask(BRANCH_W, h * HEAD_DIM, HEAD_DIM)
            qh = jnp.where(hm, q, 0.0)
            s_win = _dot_nt(qh, ks) + bias_ref[0, 0, h]
            s_ctx = _dot_nt(qh, kc)
            (p_win, p_ctx), l = _softmax_parts([s_win, s_ctx])
            o = _dot(p_win.astype(BF16), vs) + _dot(p_ctx.astype(BF16), vc)
            out = out + jnp.where(hm, o / l, 0.0)
        o_ref[0, rows, :] = out


NA_DROWS = 2 * NA_WIN_R - 1
NA_DCOLS = 2 * NA_WIN_C - 1


def _na_tile_geometry(n_rows):
    n_tiles = n_rows // NA_Q_ROWS

    def geometry(tile):
        slab0 = min(max(NA_Q_ROWS * tile - NA_WIN_R // 2, 0), n_rows - NA_SLAB_ROWS)
        rows = []
        for j in range(NA_Q_ROWS):
            qr = NA_Q_ROWS * tile + j
            r0 = min(max(qr - NA_WIN_R // 2, 0), n_rows - NA_WIN_R)
            rows.append([slab0 + m - qr + NA_WIN_R - 1 if r0 <= slab0 + m < r0 + NA_WIN_R else None
                         for m in range(NA_SLAB_ROWS)])
        return rows

    kinds = [geometry(0), geometry(1), geometry(n_tiles - 1)]
    assert all(geometry(t) == kinds[1] for t in range(1, n_tiles - 1))
    return kinds


def _na_bias_kernel(rpb_ref, o_ref, band_ref, *, kinds):
    base = (pl.program_id(0) * N_HEADS + pl.program_id(1)) * (NA_DROWS * NA_DCOLS)
    qc = lax.broadcasted_iota(jnp.int32, (GRID_W, LANES), 0)
    lane = lax.broadcasted_iota(jnp.int32, (GRID_W, LANES), 1)
    kc = lane % GRID_W
    d_col = kc - qc + (NA_WIN_C - 1)
    c0 = jnp.clip(qc - NA_WIN_C // 2, 0, GRID_W - NA_WIN_C)
    col_ok = (kc >= c0) & (kc < c0 + NA_WIN_C)
    for d in range(NA_DROWS):
        t = jnp.full((GRID_W, LANES), NEG, F32)
        for e in range(NA_DCOLS):
            t = jnp.where(d_col == e, rpb_ref[base + d * NA_DCOLS + e] * LOG2E, t)
        band_ref[d] = jnp.where(col_ok, t, NEG)
    masked = jnp.full((GRID_W, LANES), NEG, F32)
    left = lane < GRID_W
    for kind, rows in enumerate(kinds):
        for j, drow in enumerate(rows):
            for m in range(0, NA_SLAB_ROWS, 2):
                a = masked if drow[m] is None else band_ref[drow[m]]
                b = masked if drow[m + 1] is None else band_ref[drow[m + 1]]
                o_ref[0, kind, 0, j * GRID_W:(j + 1) * GRID_W, m * GRID_W:(m + 2) * GRID_W] = jnp.where(left, a, b)


def na_bias_tables(rpb, n_rows):
    slab = NA_SLAB_ROWS * GRID_W
    return pl.pallas_call(
        functools.partial(_na_bias_kernel, kinds=_na_tile_geometry(n_rows)),
        grid=(DEPTH, N_HEADS),
        in_specs=[pl.BlockSpec(memory_space=pltpu.SMEM)],
        out_specs=pl.BlockSpec((1, 3, 1, ATTN_Q, slab), lambda l, h: (l, 0, h, 0, 0)),
        out_shape=jax.ShapeDtypeStruct((DEPTH, 3, N_HEADS, ATTN_Q, slab), F32),
        scratch_shapes=[pltpu.VMEM((NA_DROWS, GRID_W, LANES), F32)],
        compiler_params=_cparams(32),
        name="na_bias_tables",
    )(rpb.astype(F32).reshape(-1))


def neighbourhood_attention(na, kc, vc, bias, layer, B, L):
    n_rows = L // GRID_W
    n_tiles = L // ATTN_Q
    na3 = na.reshape(B, L, NA_W)
    P = kc.shape[1]
    slab = NA_SLAB_ROWS * GRID_W

    def kind(sub):
        def index(b, i):
            tile = i * NA_TILES + sub
            return (layer, jnp.where(tile == 0, 0, jnp.where(tile == n_tiles - 1, 2, 1)), 0, 0, 0)
        return index

    tq = NA_TILES * ATTN_Q
    return pl.pallas_call(
        functools.partial(_na_kernel, n_rows=n_rows),
        grid=(B, L // tq),
        in_specs=[pl.BlockSpec((1, tq, BRANCH_W), lambda b, i: (b, i, 0)),
                  pl.BlockSpec((1, L, BRANCH_W), lambda b, i: (b, 0, 1)),
                  pl.BlockSpec((1, L, BRANCH_W), lambda b, i: (b, 0, 2)),
                  pl.BlockSpec((1, P, BRANCH_W), lambda b, i: (b, 0, 0)),
                  pl.BlockSpec((1, P, BRANCH_W), lambda b, i: (b, 0, 0))]
                 + [pl.BlockSpec((1, 1, N_HEADS, ATTN_Q, slab), kind(sub)) for sub in range(NA_TILES)],
        out_specs=pl.BlockSpec((1, tq, BRANCH_W), lambda b, i: (b, i, 0)),
        out_shape=jax.ShapeDtypeStruct((B, L, BRANCH_W), F32),
        compiler_params=_cparams(56),
        name="neighbourhood_attention",
    )(na3, na3, na3, kc, vc, *([bias] * NA_TILES)).reshape(B * L, BRANCH_W)


def _attn_kernel(*refs, L, window, gqa, has_ctx, has_sink):
    refs = list(refs)
    q_ref, k_ref, v_ref = refs[:3]
    pos = 3
    if has_ctx:
        kc_ref, vc_ref = refs[pos:pos + 2]
        pos += 2
    if has_sink:
        sink_ref = refs[pos]
        pos += 1
    o_ref = refs[pos]
    n_seq = q_ref.shape[0]
    n_sub = q_ref.shape[1] // ATTN_Q
    for seq, sub in [(a, b) for a in range(n_seq) for b in range(n_sub)]:
        rows = slice(sub * ATTN_Q, (sub + 1) * ATTN_Q)
        if has_ctx:
            kc = kc_ref[seq].astype(BF16)
            vc = vc_ref[seq].astype(BF16)
        if window:
            slab = SWA_SLAB
            q0 = (pl.program_id(1) * n_sub + sub) * ATTN_Q
            start = pl.multiple_of(jnp.clip(q0 - SWA_WINDOW, 0, L - slab), SWA_WINDOW)
            q_pos = q0 + lax.broadcasted_iota(jnp.int32, (ATTN_Q, 1), 0)
            k_pos = start + lax.broadcasted_iota(jnp.int32, (1, slab), 1)
            ok = jnp.abs(k_pos - q_pos) <= SWA_WINDOW
            ks = k_ref[seq, pl.ds(start, slab), :].astype(BF16)
            vs = v_ref[seq, pl.ds(start, slab), :].astype(BF16)
        else:
            ks = k_ref[seq].astype(BF16)
            vs = v_ref[seq].astype(BF16)
        q = q_ref[seq, rows, :].astype(F32)
        kv_w = ks.shape[1]
        halves = [jnp.zeros((ATTN_Q, LANES), F32), jnp.zeros((ATTN_Q, LANES), F32)]
        out = jnp.zeros(q.shape, F32)
        for h in range(N_HEADS):
            if gqa:
                kvh, slot = h // 2, h % 2
                qh = q[:, kvh * LANES:(kvh + 1) * LANES]
                if slot != kvh:
                    qh = pltpu.roll(qh, HEAD_DIM, 1)
                hm = _lane_mask(kv_w, kvh * HEAD_DIM, HEAD_DIM)
            else:
                qh = q
                hm = _lane_mask(kv_w, h * HEAD_DIM, HEAD_DIM)
            qh = jnp.where(hm, qh, 0.0).astype(BF16)
            s = _dot_nt(qh, ks)
            if window:
                s = jnp.where(ok, s, NEG)
            parts = [s]
            if has_ctx:
                parts.append(_dot_nt(qh, kc))
            ps, l = _softmax_parts(parts, sink_ref[h] * LOG2E if has_sink else None)
            o = _dot(ps[0].astype(BF16), vs)
            if has_ctx:
                o = o + _dot(ps[1].astype(BF16), vc)
            o = jnp.where(hm, o / l, 0.0)
            if gqa:
                if slot != kvh:
                    o = pltpu.roll(o, HEAD_DIM, 1)
                halves[kvh] = halves[kvh] + o
            else:
                out = out + o
        if gqa:
            o_ref[seq, rows, 0:LANES] = halves[0]
            o_ref[seq, rows, LANES:2 * LANES] = halves[1]
        else:
            o_ref[seq, rows, :] = out


def _seqs_per_step(B, L, queries):
    ns = max(1, queries // L)
    return ns if B % ns == 0 else 1


def dense_attention(src, cols, B, L, *, window=False, gqa=False, ctx=None, sink=None):
    W = src.shape[1]
    src3 = src.reshape(B, L, W)
    kv_w = SWA_KVW if gqa else BRANCH_W
    qc, kcol, vcol = cols
    tq = min(ATTN_TILES * ATTN_Q, L)
    ns = _seqs_per_step(B, L, ATTN_TILES * ATTN_Q)
    in_specs = [pl.BlockSpec((ns, tq, BRANCH_W), lambda b, i: (b, i, qc)),
                pl.BlockSpec((ns, L, kv_w), lambda b, i: (b, 0, kcol)),
                pl.BlockSpec((ns, L, kv_w), lambda b, i: (b, 0, vcol))]
    args = [src3, src3, src3]
    if ctx is not None:
        P = ctx[0].shape[1]
        in_specs += [pl.BlockSpec((ns, P, kv_w), lambda b, i: (b, 0, 0))] * 2
        args += list(ctx)
    if sink is not None:
        in_specs.append(pl.BlockSpec(memory_space=pltpu.SMEM))
        args.append(sink)
    return pl.pallas_call(
        functools.partial(_attn_kernel, L=L, window=window, gqa=gqa,
                          has_ctx=ctx is not None, has_sink=sink is not None),
        grid=(B // ns, L // tq),
        in_specs=in_specs,
        out_specs=pl.BlockSpec((ns, tq, BRANCH_W), lambda b, i: (b, i, 0)),
        out_shape=jax.ShapeDtypeStruct((B, L, BRANCH_W), F32),
        compiler_params=_cparams(48),
        name="window_attention" if window else "dense_attention",
    )(*args).reshape(B * L, BRANCH_W)


def _diff_kernel(*refs, lam_init, has_ctx):
    refs = list(refs)
    lam_ref, gain_ref, ones_ref, q_ref, k_ref, v_ref = refs[:6]
    if has_ctx:
        kc_ref, vc_ref, o_ref = refs[6:]
    else:
        (o_ref,) = refs[6:]
    lv = lam_ref[...]
    lam = (jnp.exp(jnp.sum(lv[0:1] * lv[1:2], keepdims=True))
           - jnp.exp(jnp.sum(lv[2:3] * lv[3:4], keepdims=True)) + lam_init)
    for seq in range(q_ref.shape[0]):
        kl = k_ref[seq].astype(BF16)
        vl = v_ref[seq].astype(BF16)
        if has_ctx:
            kc = kc_ref[seq].astype(BF16)
            vc = vc_ref[seq].astype(BF16)
        q = q_ref[seq].astype(BF16)
        out = jnp.zeros(q.shape, F32)
        for h in range(N_HEADS):
            o = None
            for mp in range(2):
                mm = _lane_mask(BRANCH_W, h * HEAD_DIM + mp * DIFF_HD, DIFF_HD)
                qm = jnp.where(mm, q, 0.0)
                parts = [_dot_nt(qm, kl)]
                if has_ctx:
                    parts.append(_dot_nt(qm, kc))
                ps, l = _softmax_parts(parts)
                pv = _dot(ps[0].astype(BF16), vl)
                if has_ctx:
                    pv = pv + _dot(ps[1].astype(BF16), vc)
                o = pv / l if mp == 0 else o - pv * (lam / l)
            out = out + jnp.where(_lane_mask(BRANCH_W, h * HEAD_DIM, HEAD_DIM), o, 0.0)
        ms = _dot_exact(out * out, ones_ref[...]) * (1.0 / HEAD_DIM)
        o_ref[seq] = out * lax.rsqrt(ms + EPS) * gain_ref[...] * (1.0 - lam_init)


def diff_attention(dif, lam_vec, subln, layer, B, L, ctx=None):
    lam_init = 0.8 - 0.6 * math.exp(-0.3 * layer)
    dif3 = dif.reshape(B, L, DIFF_W)
    head = jnp.arange(BRANCH_W) // HEAD_DIM
    ones = (head[:, None] == head[None, :]).astype(F32)
    gain = jnp.tile(subln.astype(F32), N_HEADS).reshape(1, BRANCH_W)
    tq = min(DIFF_Q, L)
    ns = _seqs_per_step(B, L, DIFF_Q)
    in_specs = [_resident((4, DIFF_HD)), _resident((1, BRANCH_W)), _resident((BRANCH_W, BRANCH_W)),
                pl.BlockSpec((ns, tq, BRANCH_W), lambda b, i: (b, i, 0)),
                pl.BlockSpec((ns, L, BRANCH_W), lambda b, i: (b, 0, 1)),
                pl.BlockSpec((ns, L, BRANCH_W), lambda b, i: (b, 0, 2))]
    args = [lam_vec, gain, ones, dif3, dif3, dif3]
    if ctx is not None:
        P = ctx[0].shape[1]
        in_specs += [pl.BlockSpec((ns, P, BRANCH_W), lambda b, i: (b, 0, 0))] * 2
        args += list(ctx)
    return pl.pallas_call(
        functools.partial(_diff_kernel, lam_init=lam_init, has_ctx=ctx is not None),
        grid=(B // ns, L // tq),
        in_specs=in_specs,
        out_specs=pl.BlockSpec((ns, tq, BRANCH_W), lambda b, i: (b, i, 0)),
        out_shape=jax.ShapeDtypeStruct((B, L, BRANCH_W), F32),
        compiler_params=_cparams(56),
        name="diff_attention",
    )(*args).reshape(B * L, BRANCH_W)


def _filter_kernel(z_ref, w1_ref, b1_ref, w2_ref, b2_ref, w3_ref, fr_ref, ld_ref, o_ref):
    z = z_ref[...]
    tn = z[:, 0:1]
    fr = fr_ref[0]
    g = jnp.sin(fr * (_dot_exact(z, w1_ref[0]) + b1_ref[0]))
    g = jnp.sin(fr * (_dot_exact(g, w2_ref[0]) + b2_ref[0]))
    hf = _dot_exact(g, w3_ref[0]) * jnp.exp(-jnp.exp(ld_ref[0]) * tn)
    row = lax.broadcasted_iota(jnp.int32, (z.shape[0], 1), 0)
    for o in range(HY_ORDER):
        pos = hf[:, (2 * o) * HY_WIDTH:(2 * o + 1) * HY_WIDTH]
        neg = jnp.where(row == 0, 0.0, hf[:, (2 * o + 1) * HY_WIDTH:(2 * o + 2) * HY_WIDTH])
        norm = (jnp.sum(jnp.abs(pos), axis=0, keepdims=True)
                + jnp.sum(jnp.abs(neg), axis=0, keepdims=True) + EPS)
        inv = 1.0 / norm
        o_ref[0, o, 0] = (neg + pos) * inv
        o_ref[0, o, 1] = (neg - pos) * inv


def hyena_filters(L, p):
    tn = jnp.arange(L, dtype=F32) / L
    ang = 2.0 * math.pi * tn[:, None] * jnp.arange(1, HY_FREQS + 1, dtype=F32)[None, :]
    z = jnp.concatenate([tn[:, None], jnp.cos(ang), jnp.sin(ang)], axis=-1)
    z = jnp.pad(z, ((0, 0), (0, HY_HIDDEN - HY_EMB)))
    w1 = jnp.pad(p['hy_w1'], ((0, 0), (0, HY_HIDDEN - HY_EMB), (0, 0)))
    fw = HY_ORDER * 2 * HY_WIDTH
    per_layer = lambda *shape: pl.BlockSpec((1,) + shape, lambda l: (l,) + (0,) * len(shape))
    return pl.pallas_call(
        _filter_kernel,
        grid=(DEPTH,),
        in_specs=[pl.BlockSpec((L, HY_HIDDEN), lambda l: (0, 0)),
                  per_layer(HY_HIDDEN, HY_HIDDEN), per_layer(1, HY_HIDDEN),
                  per_layer(HY_HIDDEN, HY_HIDDEN), per_layer(1, HY_HIDDEN),
                  per_layer(HY_HIDDEN, fw), per_layer(1, HY_HIDDEN), per_layer(1, fw)],
        out_specs=per_layer(HY_ORDER, 2, L, HY_WIDTH),
        out_shape=jax.ShapeDtypeStruct((DEPTH, HY_ORDER, 2, L, HY_WIDTH), F32),
        compiler_params=_cparams(56),
        name="hyena_filters",
    )(z, w1, p['hy_b1'].reshape(DEPTH, 1, HY_HIDDEN), p['hy_w2'], p['hy_b2'].reshape(DEPTH, 1, HY_HIDDEN),
      p['hy_w3'], p['hy_sin_freq'].reshape(DEPTH, 1, HY_HIDDEN), p['hy_log_decay'].reshape(DEPTH, 1, fw))


def dft_table(L, tk, half_sample):
    assert L % LANES == 0 and L & (L - 1) == 0 and L // DFT_LO <= LANES
    return pl.pallas_call(
        functools.partial(_dft_table_kernel, half_sample=half_sample),
        grid=(L // tk,),
        out_specs=pl.BlockSpec((1, 2 * tk, L), lambda i: (i, 0, 0)),
        out_shape=jax.ShapeDtypeStruct((L // tk, 2 * tk, L), BF16),
        compiler_params=_cparams(48),
        name="dft_table",
    )()


DFT_LO = 64


def _dft_table_kernel(o_ref, *, half_sample):
    _, tk2, L = o_ref.shape
    tk = tk2 // 2
    k2 = 2 * (pl.program_id(0) * tk + lax.broadcasted_iota(jnp.int32, (tk, LANES), 0)) + 1
    lane = lax.broadcasted_iota(jnp.int32, (tk, LANES), 1)

    def cos_sin(m):
        ang = ((k2 * m) & (8 * L - 1)).astype(F32) * (math.pi / (4 * L))
        return jnp.cos(ang), jnp.sin(ang)

    c_hi, s_hi = cos_sin(2 * DFT_LO * lane)
    c_lo, s_lo = cos_sin(2 * lane + (1 if half_sample else 0))
    n = lax.broadcasted_iota(jnp.int32, (LANES, L), 1)
    r = lax.broadcasted_iota(jnp.int32, (LANES, L), 0)
    pick_hi = jnp.where(r == (n >> (DFT_LO.bit_length() - 1)), 1.0, 0.0).astype(BF16)
    pick_lo = jnp.where(r == (n & (DFT_LO - 1)), 1.0, 0.0).astype(BF16)

    def copy(t, pick):
        head = t.astype(BF16)
        return _dot(head, pick) + _dot((t - head.astype(F32)).astype(BF16), pick)

    ch, sh = copy(c_hi, pick_hi), copy(s_hi, pick_hi)
    cl, sl = copy(c_lo, pick_lo), copy(s_lo, pick_lo)
    o_ref[0, :tk, :] = (ch * cl - sh * sl).astype(BF16)
    o_ref[0, tk:, :] = (sh * cl + ch * sl).astype(BF16)


def _spectrum_kernel(t_ref, f_ref, o_ref):
    nkt, tk2, _ = t_ref.shape
    tk = tk2 // 2
    sums, diffs = f_ref[0, 0].astype(BF16), f_ref[0, 1].astype(BF16)
    for i in range(nkt):
        rows = slice(i * tk, (i + 1) * tk)
        o_ref[0, 0, rows, :] = _dot(t_ref[i, :tk, :], sums)
        o_ref[0, 1, rows, :] = _dot(t_ref[i, tk:, :], diffs)


def filter_spectra(filt, fwd):
    G, _, L, C = filt.shape
    return pl.pallas_call(
        _spectrum_kernel,
        grid=(G,),
        in_specs=[_resident(fwd.shape), pl.BlockSpec((1, 2, L, C), lambda g: (g, 0, 0, 0))],
        out_specs=pl.BlockSpec((1, 2, L, C), lambda g: (g, 0, 0, 0)),
        out_shape=jax.ShapeDtypeStruct((G, 2, L, C), F32),
        compiler_params=_cparams(48),
        name="filter_spectra",
    )(fwd, filt)


def _hyena_kernel(t_ref, h_ref, u_ref, w_ref, b_ref, skip_ref, o_ref, z_ref, y_ref):
    nkt, tk2, L = t_ref.shape
    tk = tk2 // 2
    C = HY_WIDTH
    row = lax.broadcasted_iota(jnp.int32, (L, 1), 0)

    def short_conv(col):
        sl = slice(col * C, (col + 1) * C)
        u = u_ref[0, :, sl]
        prev = jnp.where(row == 0, 0.0, pltpu.roll(u, 1, 0))
        nxt = jnp.where(row == L - 1, 0.0, pltpu.roll(u, L - 1, 0))
        return prev * w_ref[0:1, sl] + u * w_ref[1:2, sl] + nxt * w_ref[2:3, sl] + b_ref[:, sl]

    y_ref[...] = short_conv(0)
    for o in range(HY_ORDER):
        y = y_ref[...].astype(BF16)
        for i in range(nkt):
            rows = slice(i * tk, (i + 1) * tk)
            acc = _dot(t_ref[i], y)
            yc, ys = acc[:tk], acc[tk:]
            hr = h_ref[0, o, 0, rows, :]
            hi = h_ref[0, o, 1, rows, :]
            z_ref[0, rows, :] = (yc * hr + ys * hi).astype(BF16)
            z_ref[1, rows, :] = (yc * hi - ys * hr).astype(BF16)
        gate = short_conv(1 + o)
        zr, zi = z_ref[0], z_ref[1]
        for i in range(nkt):
            rows = slice(i * tk, (i + 1) * tk)
            conv = (_dot(t_ref[i, :tk, :], zr) - _dot(t_ref[i, tk:, :], zi)) * (1.0 / L)
            new = gate[rows] * (conv + skip_ref[o:o + 1, :] * y_ref[rows, :])
            if o == HY_ORDER - 1:
                o_ref[0, rows, :] = new
            else:
                y_ref[rows, :] = new


def hyena(hy, lp, layer, spec, table, B, L):
    C = HY_WIDTH
    return pl.pallas_call(
        _hyena_kernel,
        grid=(B,),
        in_specs=[_resident(table.shape),
                  pl.BlockSpec((1, HY_ORDER, 2, L, C), lambda b: (layer, 0, 0, 0, 0), pipeline_mode=pl.Buffered(1)),
                  pl.BlockSpec((1, L, HY_IN_W), lambda b: (b, 0, 0)),
                  _resident((3, HY_IN_W)), _resident((1, HY_IN_W)), _resident((HY_ORDER, C))],
        out_specs=pl.BlockSpec((1, L, C), lambda b: (b, 0, 0)),
        out_shape=jax.ShapeDtypeStruct((B, L, C), F32),
        scratch_shapes=[pltpu.VMEM((2, L, C), BF16), pltpu.VMEM((L, C), F32)],
        compiler_params=_cparams(56),
        name="hyena",
    )(table, spec, hy.reshape(B, L, HY_IN_W), lp['hy_short_w'], lp['hy_short_b'].reshape(1, HY_IN_W),
      lp['hy_skip']).reshape(B * L, C)


def _merge_kernel(a_ref, b_ref, c_ref, d_ref, x_ref, mod_ref, nw_ref, wg_ref, wb_ref, wo_ref, o_ref):
    rows = x_ref.shape[0] // MERGE_SPLIT
    groups = [slice(g * rows, (g + 1) * rows) for g in range(MERGE_SPLIT)]
    hs = [_norm_mod(x_ref[g, :], nw_ref[...], mod_ref[0, 3:4, :], mod_ref[0, 4:5, :]).astype(BF16) for g in groups]
    for g, h in zip(groups, hs):
        merged = None
        for i, br in enumerate((a_ref, b_ref, c_ref, d_ref)):
            gate = _sigmoid(_dot(h, wg_ref[:, MIX_W + i * D_MODEL:MIX_W + (i + 1) * D_MODEL]))
            t = gate * _dot(br[g, :].astype(BF16), wb_ref[i])
            merged = t if merged is None else merged + t
        o_ref[g, :] = x_ref[g, :] + mod_ref[0, 5:6, :] * _dot(merged.astype(BF16), wo_ref[...])


def merge_block(branches, x, mod, nw, wg, wb, wo, layer, rows_per_b):
    T = x.shape[0]
    tm = MERGE_ROWS
    nb = mod.shape[0]
    row = lambda w: pl.BlockSpec((tm, w), lambda i: (i, 0))
    return pl.pallas_call(
        _merge_kernel,
        grid=(T // tm,),
        in_specs=[row(BRANCH_W)] * N_BRANCH + [row(D_MODEL), _mod_spec(nb, rows_per_b // tm),
                                               _resident((1, D_MODEL)),
                                               _resident_layer(wg, layer),
                                               _resident_layer(wb, layer),
                                               _resident_layer(wo, layer)],
        out_specs=row(D_MODEL),
        out_shape=jax.ShapeDtypeStruct((T, D_MODEL), F32),
        compiler_params=_cparams(56),
        name="merge_block",
    )(*branches, x, mod, nw.reshape(1, D_MODEL), wg, wb, wo)


def _heads_in(t):
    B, H, P, d = t.shape
    return t.transpose(0, 2, 1, 3).reshape(B, P, H * d)


def _run_pass(x, mod_all, p, wts, final_norm, B, L, spec, tabs, caches):
    ctx_pass = caches is None
    collected = []
    if not ctx_pass:
        rope = rope_tables(L, HEAD_DIM, SWA_QW + SWA_KVW) + rope_tables(L, DIFF_HD, 2 * BRANCH_W)
        bias_all = na_bias_tables(p['na_rpb'], L // GRID_W)
    for l in range(DEPTH):
        lp = {k: v[l] for k, v in p.items()}
        w = wts
        mod = mod_all[l]
        x = ffn_block(x, mod, lp['norm_ffn1'], w['ffn1_w1'], w['ffn1_w3'], w['ffn1_w2'], l, 0, L)
        na, swa, hy, dif = in_projection(x, mod, lp['norm_mix'], w['w_in'], l, L,
                                         None if ctx_pass else rope, F32 if ctx_pass else BF16)
        if ctx_pass:
            a_o = dense_attention(na, (0, 1, 2), B, L)
            b_o = dense_attention(swa, (0, 2, 3), B, L, gqa=True, sink=lp['swa_sink'])
            d_o = diff_attention(dif, lp['diff_lambda'], lp['diff_subln'], l, B, L)
            collected.append((na, swa, dif))
        else:
            ck_na, cv_na, ck_swa, cv_swa, ck_d, cv_d = (_heads_in(t[:, l]).astype(BF16) for t in caches)
            a_o = neighbourhood_attention(na, ck_na, cv_na, bias_all, l, B, L)
            b_o = dense_attention(swa, (0, 2, 3), B, L, window=True, gqa=True,
                                  ctx=(ck_swa, cv_swa), sink=lp['swa_sink'])
            d_o = diff_attention(dif, lp['diff_lambda'], lp['diff_subln'], l, B, L, ctx=(ck_d, cv_d))
        c_o = hyena(hy, lp, l, spec, tabs, B, L)
        x = merge_block((a_o, b_o, c_o, d_o), x, mod, lp['norm_mix'], w['w_in'], w['w_branch'], w['w_out'], l, L)
        x = ffn_block(x, mod, lp['norm_ffn2'], w['ffn2_w1'], w['ffn2_w3'], w['ffn2_w2'], l, 6, L,
                      final_w=final_norm if l == DEPTH - 1 else None)
    return x, collected


_CACHE_SLOTS = ((0, N_HEADS, N_HEADS), (0, 2 * N_HEADS, N_HEADS),
                (1, N_HEADS, SWA_KV_HEADS), (1, N_HEADS + SWA_KV_HEADS, SWA_KV_HEADS),
                (2, N_HEADS, N_HEADS), (2, 2 * N_HEADS, N_HEADS))


def _cache_kernel(*refs):
    srcs, outs = refs[:3 * DEPTH], refs[3 * DEPTH:]
    for l in range(DEPTH):
        for o_ref, (src, slot0, n) in zip(outs, _CACHE_SLOTS):
            x_ref = srcs[3 * l + src]
            for h in range(n):
                o_ref[0, l, h] = x_ref[:, (slot0 + h) * HEAD_DIM:(slot0 + h + 1) * HEAD_DIM]


def _new_caches(collected, B, L):
    srcs = [t for layer in collected for t in layer]
    return pl.pallas_call(
        _cache_kernel,
        grid=(B,),
        in_specs=[pl.BlockSpec((L, t.shape[1]), lambda b: (b, 0)) for t in srcs],
        out_specs=[pl.BlockSpec((1, DEPTH, n, L, HEAD_DIM), lambda b: (b, 0, 0, 0, 0)) for _, _, n in _CACHE_SLOTS],
        out_shape=[jax.ShapeDtypeStruct((B, DEPTH, n, L, HEAD_DIM), F32) for _, _, n in _CACHE_SLOTS],
        compiler_params=_cparams(48),
        name="cache_outputs",
    )(*srcs)


def _hyena_setup(L, p, tk):
    filt = hyena_filters(L, p)
    spec = filter_spectra(filt.reshape(DEPTH * HY_ORDER, 2, L, HY_WIDTH), dft_table(L, tk, half_sample=False))
    return spec.reshape(DEPTH, HY_ORDER, 2, L, HY_WIDTH), dft_table(L, tk, half_sample=True)


def kernel(x_prompt, x_sample, cache_na_k, cache_na_v, cache_swa_k, cache_swa_v, cache_diff_k, cache_diff_v, c, c_ctx, w_ada, b_ada, norm_ffn1, norm_mix, norm_ffn2, final_norm, ffn1_w1, ffn1_w3, ffn1_w2, ffn2_w1, ffn2_w3, ffn2_w2, w_in, w_branch, w_out, na_rpb, swa_sink, hy_short_w, hy_short_b, hy_w1, hy_b1, hy_w2, hy_b2, hy_w3, hy_sin_freq, hy_log_decay, hy_skip, diff_lambda, diff_subln):
    B_ctx, L_ctx, _ = x_prompt.shape
    B_den, L_den, _ = x_sample.shape
    p = {
        'norm_ffn1': norm_ffn1, 'norm_mix': norm_mix, 'norm_ffn2': norm_ffn2,
        'na_rpb': na_rpb, 'swa_sink': swa_sink, 'hy_short_w': hy_short_w, 'hy_short_b': hy_short_b,
        'hy_w1': hy_w1, 'hy_b1': hy_b1, 'hy_w2': hy_w2, 'hy_b2': hy_b2, 'hy_w3': hy_w3,
        'hy_sin_freq': hy_sin_freq, 'hy_log_decay': hy_log_decay, 'hy_skip': hy_skip,
        'diff_lambda': diff_lambda, 'diff_subln': diff_subln,
    }
    big = {'ffn1_w1': ffn1_w1, 'ffn1_w3': ffn1_w3, 'ffn1_w2': ffn1_w2, 'ffn2_w1': ffn2_w1,
           'ffn2_w3': ffn2_w3, 'ffn2_w2': ffn2_w2, 'w_in': w_in, 'w_branch': w_branch, 'w_out': w_out}
    wts = {k: v.astype(BF16) for k, v in big.items()}

    cond = jnp.concatenate([c, c_ctx[None, :]], axis=0)
    rows = 8 * ((cond.shape[0] + 7) // 8)
    cond = jnp.pad(cond, ((0, rows - cond.shape[0]), (0, 0)))
    mod = ada_modulation(cond, w_ada, b_ada).reshape(DEPTH, rows, N_MOD, D_MODEL)
    mod_den = mod[:, :B_den]
    mod_ctx = mod[:, B_den:B_den + 1]

    spec_c, tabs_c = _hyena_setup(L_ctx, p, min(L_ctx, HY_FREQ_TILE))
    y_ctx, collected = _run_pass(x_prompt.reshape(B_ctx * L_ctx, D_MODEL), mod_ctx, p, wts, final_norm,
                                 B_ctx, L_ctx, spec_c, tabs_c, None)
    new_caches = _new_caches(collected, B_ctx, L_ctx)

    spec_d, tabs_d = _hyena_setup(L_den, p, min(L_den, HY_FREQ_TILE))
    caches = (cache_na_k, cache_na_v, cache_swa_k, cache_swa_v, cache_diff_k, cache_diff_v)
    y_den, _ = _run_pass(x_sample.reshape(B_den * L_den, D_MODEL), mod_den, p, wts, final_norm,
                         B_den, L_den, spec_d, tabs_d, caches)
    return (y_ctx.reshape(B_ctx, L_ctx, D_MODEL), y_den.reshape(B_den, L_den, D_MODEL), *new_caches)
```

```python
import functools
import math

import jax
import jax.numpy as jnp
from jax import lax
from jax.experimental import pallas as pl
from jax.experimental.pallas import tpu as pltpu

F32 = jnp.float32
BF16 = jnp.bfloat16

D_MODEL = 1024
DEPTH = 4
GRID_W = 64
N_BRANCH = 4
BRANCH_W = D_MODEL // 4
HEAD_DIM = 64
N_HEADS = BRANCH_W // HEAD_DIM
NA_WIN_R = 8
NA_WIN_C = 16
SWA_KV_HEADS = N_HEADS // 2
SWA_WINDOW = 128
HY_WIDTH = BRANCH_W
HY_ORDER = 2
HY_FREQS = 16
HY_EMB = 1 + 2 * HY_FREQS
HY_HIDDEN = 64
DIFF_HD = 32
D_FF = 128 * ((8 * D_MODEL // 3 + 127) // 128)
ROPE_BASE = 10000.0
EPS = 1e-6
NEG = -1e30
N_MOD = 9
NA_W = 3 * BRANCH_W
SWA_QW = BRANCH_W
SWA_KVW = SWA_KV_HEADS * HEAD_DIM
SWA_W = SWA_QW + 2 * SWA_KVW
HY_IN_W = 3 * HY_WIDTH
DIFF_W = 3 * BRANCH_W
GATE_W = N_BRANCH * D_MODEL
MIX_W = NA_W + SWA_W + HY_IN_W + DIFF_W

LOG2E = math.log2(math.e)
QK_SCALE = HEAD_DIM ** -0.5 * LOG2E
DIFF_QK_SCALE = DIFF_HD ** -0.5 * LOG2E

LANES = 128
MXU_DIM = 256
MIB = 1024 * 1024

FFN_ROWS = 1024
FFN_CAST_ROWS = 512
FFN_SPLIT = 2
MERGE_ROWS = 512
MERGE_SPLIT = 2
PROJ_ROWS = 512
PROJ_SPLIT = 2
FFN_CHUNK = 768
ATTN_Q = 256
ATTN_TILES = 4
NA_TILES = 4
HY_FREQ_TILE = 512
DIFF_Q = 512
NA_Q_ROWS = ATTN_Q // GRID_W
NA_SLAB_ROWS = NA_Q_ROWS + NA_WIN_R
SWA_SLAB = ATTN_Q + 2 * SWA_WINDOW


def _cparams(vmem_mib):
    return pltpu.CompilerParams(vmem_limit_bytes=vmem_mib * MIB)


def _resident(shape):
    nd = len(shape)
    return pl.BlockSpec(shape, lambda *_: (0,) * nd, pipeline_mode=pl.Buffered(1))


def _resident_layer(stacked, layer):
    shape = stacked.shape[1:]
    return pl.BlockSpec((None,) + shape, lambda *_: (layer,) + (0,) * len(shape), pipeline_mode=pl.Buffered(1))


def _dot(a, b):
    return jnp.dot(a, b, preferred_element_type=F32)


def _dot_nt(a, b):
    return lax.dot_general(a, b, (((1,), (1,)), ((), ())), preferred_element_type=F32)


def _dot_exact(a, b):
    return jnp.dot(a, b, preferred_element_type=F32, precision=lax.Precision.HIGHEST)


def _sigmoid(x):
    return 1.0 / (1.0 + jnp.exp(-x))


def _norm_mod(x, nw, shift, scale):
    return x * lax.rsqrt(jnp.mean(x * x, axis=-1, keepdims=True) + EPS) * (nw * (1.0 + scale)) + shift


def _mod_spec(nb, tiles_per_b):
    if nb == 1:
        return pl.BlockSpec((1, N_MOD, D_MODEL), lambda i: (0, 0, 0))
    return pl.BlockSpec((1, N_MOD, D_MODEL), lambda i: (i // tiles_per_b, 0, 0))


def _ada_kernel(c_ref, w_ref, b_ref, o_ref):
    c = c_ref[...]
    s = (c * _sigmoid(c)).astype(BF16)
    o_ref[0] = _dot(s, w_ref[0].astype(BF16)) + b_ref[0]


def ada_modulation(cond, w_ada, b_ada):
    rows = cond.shape[0]
    width = N_MOD * D_MODEL
    tn = 9 * LANES
    return pl.pallas_call(
        _ada_kernel,
        grid=(DEPTH, width // tn),
        in_specs=[pl.BlockSpec((rows, D_MODEL), lambda l, j: (0, 0)),
                  pl.BlockSpec((1, D_MODEL, tn), lambda l, j: (l, 0, j)),
                  pl.BlockSpec((1, 1, tn), lambda l, j: (l, 0, j))],
        out_specs=pl.BlockSpec((1, rows, tn), lambda l, j: (l, 0, j)),
        out_shape=jax.ShapeDtypeStruct((DEPTH, rows, width), F32),
        compiler_params=_cparams(32),
        name="ada_modulation",
    )(cond, w_ada, b_ada.reshape(DEPTH, 1, width))


def _ffn_kernel(x_ref, mod_ref, nw_ref, w1_ref, w3_ref, w2_ref, *rest, mod_base, final, n_cast):
    rest = list(rest)
    fw_ref = rest.pop(0) if final else None
    cast_in, o_ref, cast_out = rest[:n_cast], rest[n_cast], rest[n_cast + 1:]
    for src, dst in zip(cast_in, cast_out):
        dst[...] = src[...].astype(BF16)
    shift = mod_ref[0, mod_base:mod_base + 1, :]
    scale = mod_ref[0, mod_base + 1:mod_base + 2, :]
    gate = mod_ref[0, mod_base + 2:mod_base + 3, :]
    rows = x_ref.shape[0] // FFN_SPLIT
    groups = [slice(g * rows, (g + 1) * rows) for g in range(FFN_SPLIT)]
    hs = [_norm_mod(x_ref[g, :], nw_ref[...], shift, scale).astype(BF16) for g in groups]
    for g, h in zip(groups, hs):
        acc = jnp.zeros((rows, D_MODEL), F32)
        for lo in range(0, D_FF, FFN_CHUNK):
            hi = min(lo + FFN_CHUNK, D_FF)
            a = _dot(h, w1_ref[:, lo:hi])
            b = _dot(h, w3_ref[:, lo:hi])
            u = (a * _sigmoid(a) * b).astype(BF16)
            acc = acc + _dot(u, w2_ref[lo:hi, :])
        y = x_ref[g, :] + 0.5 * gate * acc
        if final:
            y = y * lax.rsqrt(jnp.mean(y * y, axis=-1, keepdims=True) + EPS) * fw_ref[...]
        o_ref[g, :] = y


def ffn_block(x, mod, nw, w1, w3, w2, mod_base, rows_per_b, final_w=None, cast=None):
    T = x.shape[0]
    tm = FFN_ROWS if cast is None else FFN_CAST_ROWS
    steps = T // tm
    nb = mod.shape[0]
    final = final_w is not None
    in_specs = [pl.BlockSpec((tm, D_MODEL), lambda i: (i, 0)),
                _mod_spec(nb, rows_per_b // tm),
                _resident((1, D_MODEL)),
                _resident(w1.shape), _resident(w3.shape), _resident(w2.shape)]
    args = [x, mod, nw.reshape(1, D_MODEL), w1, w3, w2]
    if final:
        in_specs.append(_resident((1, D_MODEL)))
        args.append(final_w.reshape(1, D_MODEL))
    out_specs = [pl.BlockSpec((tm, D_MODEL), lambda i: (i, 0))]
    out_shape = [jax.ShapeDtypeStruct((T, D_MODEL), F32)]
    n_cast = 0
    if cast is not None:
        layer, sources = cast
        n_cast = len(sources)
        for src in sources:
            _, rows, cols = src.shape
            in_specs.append(pl.BlockSpec((None, rows // steps, cols), lambda i: (layer, i, 0)))
            out_specs.append(pl.BlockSpec((rows // steps, cols), lambda i: (i, 0)))
            out_shape.append(jax.ShapeDtypeStruct((rows, cols), BF16))
        args += list(sources)
    outs = pl.pallas_call(
        functools.partial(_ffn_kernel, mod_base=mod_base, final=final, n_cast=n_cast),
        grid=(steps,),
        in_specs=in_specs,
        out_specs=out_specs,
        out_shape=out_shape,
        compiler_params=_cparams(56),
        name="ffn_block",
    )(*args)
    return outs[0], list(outs[1:])


def _rope_chunk(x, cos, sin_a, sin_b, dist):
    return x * cos + pltpu.roll(x, LANES - dist, 1) * sin_a + pltpu.roll(x, dist, 1) * sin_b


def _proj_kernel(x_ref, mod_ref, nw_ref, w_ref, *rest, rope):
    if rope:
        (cs_ref, sa_ref, sb_ref, cd_ref, da_ref, db_ref,
         na_ref, swa_ref, hy_ref, dif_ref) = rest
    else:
        na_ref, swa_ref, hy_ref, dif_ref = rest
    rows = x_ref.shape[0] // PROJ_SPLIT
    groups = [slice(g * rows, (g + 1) * rows) for g in range(PROJ_SPLIT)]
    hs = [_norm_mod(x_ref[g, :], nw_ref[...], mod_ref[0, 3:4, :], mod_ref[0, 4:5, :]).astype(BF16) for g in groups]
    q_chunks = BRANCH_W // LANES
    for g, h in zip(groups, hs):
        o = 0
        s = _dot(h, w_ref[:, o:o + NA_W])
        o += NA_W
        na_ref[g, :BRANCH_W] = (s[:, :BRANCH_W] * QK_SCALE).astype(na_ref.dtype)
        na_ref[g, BRANCH_W:] = s[:, BRANCH_W:].astype(na_ref.dtype)
        s = _dot(h, w_ref[:, o:o + SWA_W])
        o += SWA_W
        n_rot = (SWA_QW + SWA_KVW) // LANES
        for c in range(SWA_W // LANES):
            sl = slice(c * LANES, (c + 1) * LANES)
            chunk = s[:, sl]
            if rope and c < n_rot:
                chunk = _rope_chunk(chunk, cs_ref[g, sl], sa_ref[g, sl], sb_ref[g, sl], HEAD_DIM // 4)
            if c < q_chunks:
                chunk = chunk * QK_SCALE
            swa_ref[g, sl] = chunk.astype(swa_ref.dtype)
        hy_ref[g, :] = _dot(h, w_ref[:, o:o + HY_IN_W])
        o += HY_IN_W
        s = _dot(h, w_ref[:, o:o + DIFF_W])
        o += DIFF_W
        n_rot = 2 * BRANCH_W // LANES
        for c in range(DIFF_W // LANES):
            sl = slice(c * LANES, (c + 1) * LANES)
            chunk = s[:, sl]
            if rope and c < n_rot:
                chunk = _rope_chunk(chunk, cd_ref[g, sl], da_ref[g, sl], db_ref[g, sl], DIFF_HD // 4)
            if c < q_chunks:
                chunk = chunk * DIFF_QK_SCALE
            dif_ref[g, sl] = chunk.astype(dif_ref.dtype)


def in_projection(x, mod, nw, w_in, layer, rows_per_b, rope_tabs, qkv_dtype):
    T = x.shape[0]
    tm = PROJ_ROWS
    nb = mod.shape[0]
    rope = rope_tabs is not None
    if rope:
        pos_tiles = rows_per_b // tm
        grid = (pos_tiles, T // rows_per_b)
        row_tile = lambda p, b: (b * pos_tiles + p, 0)
        mod_spec = pl.BlockSpec((1, N_MOD, D_MODEL), lambda p, b: (b, 0, 0))
    else:
        grid = (T // tm,)
        row_tile = lambda i: (i, 0)
        mod_spec = _mod_spec(nb, rows_per_b // tm)
    in_specs = [pl.BlockSpec((tm, D_MODEL), row_tile),
                mod_spec,
                _resident((1, D_MODEL)),
                pl.BlockSpec((None, D_MODEL, MIX_W), lambda *_: (layer, 0, 0), pipeline_mode=pl.Buffered(1))]
    args = [x, mod, nw.reshape(1, D_MODEL), w_in]
    if rope:
        for t in rope_tabs:
            in_specs.append(pl.BlockSpec((tm, t.shape[1]), lambda p, b: (p, 0)))
            args.append(t)
    widths = (NA_W, SWA_W, HY_IN_W, DIFF_W)
    dtypes = (qkv_dtype, qkv_dtype, F32, qkv_dtype)
    return pl.pallas_call(
        functools.partial(_proj_kernel, rope=rope),
        grid=grid,
        in_specs=in_specs,
        out_specs=[pl.BlockSpec((tm, w), row_tile) for w in widths],
        out_shape=[jax.ShapeDtypeStruct((T, w), dt) for w, dt in zip(widths, dtypes)],
        compiler_params=_cparams(56),
        name="in_projection",
    )(*args)


def rope_tables(L, dh, width):
    nf = dh // 4
    t = jnp.arange(L)
    freqs = ROPE_BASE ** (-jnp.arange(nf, dtype=F32) / nf)
    ang_r = (t // GRID_W).astype(F32)[:, None] * freqs
    ang_c = (t % GRID_W).astype(F32)[:, None] * freqs
    cr, sr, cc, sc = jnp.cos(ang_r), jnp.sin(ang_r), jnp.cos(ang_c), jnp.sin(ang_c)
    z = jnp.zeros_like(sr)
    reps = width // dh
    cos = jnp.tile(jnp.concatenate([cr, cr, cc, cc], axis=-1), (1, reps))
    sin_a = jnp.tile(jnp.concatenate([-sr, z, -sc, z], axis=-1), (1, reps))
    sin_b = jnp.tile(jnp.concatenate([z, sr, z, sc], axis=-1), (1, reps))
    return cos, sin_a, sin_b


def _lane_mask(width, lo, n):
    lane = lax.broadcasted_iota(jnp.int32, (1, width), 1)
    return (lane >= lo) & (lane < lo + n)


def _softmax_parts(parts, extra_logit=None):
    m = parts[0].max(axis=-1, keepdims=True)
    for s in parts[1:]:
        m = jnp.maximum(m, s.max(axis=-1, keepdims=True))
    if extra_logit is not None:
        m = jnp.maximum(m, extra_logit)
    ps = [jnp.exp2(s - m) for s in parts]
    l = ps[0].sum(axis=-1, keepdims=True)
    for p in ps[1:]:
        l = l + p.sum(axis=-1, keepdims=True)
    if extra_logit is not None:
        l = l + jnp.exp2(extra_logit - m)
    return ps, l


def _na_kernel(q_ref, k_ref, v_ref, kc_ref, vc_ref, *rest, n_rows):
    bias_refs, o_ref = rest[:-1], rest[-1]
    slab = NA_SLAB_ROWS * GRID_W
    kc = kc_ref[0].astype(BF16)
    vc = vc_ref[0].astype(BF16)
    for sub, bias_ref in enumerate(bias_refs):
        tile = pl.program_id(1) * len(bias_refs) + sub
        rows = slice(sub * ATTN_Q, (sub + 1) * ATTN_Q)
        row0 = jnp.clip(NA_Q_ROWS * tile - NA_WIN_R // 2, 0, n_rows - NA_SLAB_ROWS)
        start = pl.multiple_of(row0 * GRID_W, GRID_W)
        ks = k_ref[0, pl.ds(start, slab), :].astype(BF16)
        vs = v_ref[0, pl.ds(start, slab), :].astype(BF16)
        q = q_ref[0, rows, :].astype(BF16)
        out = jnp.zeros(q.shape, F32)
        for h in range(N_HEADS):
            hm = _lane_mask(BRANCH_W, h * HEAD_DIM, HEAD_DIM)
            qh = jnp.where(hm, q, 0.0)
            s_win = _dot_nt(qh, ks) + bias_ref[0, 0, h]
            s_ctx = _dot_nt(qh, kc)
            (p_win, p_ctx), l = _softmax_parts([s_win, s_ctx])
            o = _dot(p_win.astype(BF16), vs) + _dot(p_ctx.astype(BF16), vc)
            out = out + jnp.where(hm, o / l, 0.0)
        o_ref[0, rows, :] = out


NA_DROWS = 2 * NA_WIN_R - 1
NA_DCOLS = 2 * NA_WIN_C - 1


def _na_tile_geometry(n_rows):
    n_tiles = n_rows // NA_Q_ROWS

    def geometry(tile):
        slab0 = min(max(NA_Q_ROWS * tile - NA_WIN_R // 2, 0), n_rows - NA_SLAB_ROWS)
        rows = []
        for j in range(NA_Q_ROWS):
            qr = NA_Q_ROWS * tile + j
            r0 = min(max(qr - NA_WIN_R // 2, 0), n_rows - NA_WIN_R)
            rows.append([slab0 + m - qr + NA_WIN_R - 1 if r0 <= slab0 + m < r0 + NA_WIN_R else None
                         for m in range(NA_SLAB_ROWS)])
        return rows

    kinds = [geometry(0), geometry(1), geometry(n_tiles - 1)]
    assert all(geometry(t) == kinds[1] for t in range(1, n_tiles - 1))
    return kinds


def _na_bias_kernel(rpb_ref, o_ref, band_ref, *, kinds):
    base = (pl.program_id(0) * N_HEADS + pl.program_id(1)) * (NA_DROWS * NA_DCOLS)
    qc = lax.broadcasted_iota(jnp.int32, (GRID_W, LANES), 0)
    lane = lax.broadcasted_iota(jnp.int32, (GRID_W, LANES), 1)
    kc = lane % GRID_W
    d_col = kc - qc + (NA_WIN_C - 1)
    c0 = jnp.clip(qc - NA_WIN_C // 2, 0, GRID_W - NA_WIN_C)
    col_ok = (kc >= c0) & (kc < c0 + NA_WIN_C)
    for d in range(NA_DROWS):
        t = jnp.full((GRID_W, LANES), NEG, F32)
        for e in range(NA_DCOLS):
            t = jnp.where(d_col == e, rpb_ref[base + d * NA_DCOLS + e] * LOG2E, t)
        band_ref[d] = jnp.where(col_ok, t, NEG)
    masked = jnp.full((GRID_W, LANES), NEG, F32)
    left = lane < GRID_W
    for kind, rows in enumerate(kinds):
        for j, drow in enumerate(rows):
            for m in range(0, NA_SLAB_ROWS, 2):
                a = masked if drow[m] is None else band_ref[drow[m]]
                b = masked if drow[m + 1] is None else band_ref[drow[m + 1]]
                o_ref[0, kind, 0, j * GRID_W:(j + 1) * GRID_W, m * GRID_W:(m + 2) * GRID_W] = jnp.where(left, a, b)


def na_bias_tables(rpb, n_rows):
    slab = NA_SLAB_ROWS * GRID_W
    return pl.pallas_call(
        functools.partial(_na_bias_kernel, kinds=_na_tile_geometry(n_rows)),
        grid=(DEPTH, N_HEADS),
        in_specs=[pl.BlockSpec(memory_space=pltpu.SMEM)],
        out_specs=pl.BlockSpec((1, 3, 1, ATTN_Q, slab), lambda l, h: (l, 0, h, 0, 0)),
        out_shape=jax.ShapeDtypeStruct((DEPTH, 3, N_HEADS, ATTN_Q, slab), F32),
        scratch_shapes=[pltpu.VMEM((NA_DROWS, GRID_W, LANES), F32)],
        compiler_params=_cparams(32),
        name="na_bias_tables",
    )(rpb.astype(F32).reshape(-1))


def neighbourhood_attention(na, kc, vc, bias, layer, B, L):
    n_rows = L // GRID_W
    n_tiles = L // ATTN_Q
    na3 = na.reshape(B, L, NA_W)
    P = kc.shape[1]
    slab = NA_SLAB_ROWS * GRID_W

    def kind(sub):
        def index(b, i):
            tile = i * NA_TILES + sub
            return (layer, jnp.where(tile == 0, 0, jnp.where(tile == n_tiles - 1, 2, 1)), 0, 0, 0)
        return index

    tq = NA_TILES * ATTN_Q
    return pl.pallas_call(
        functools.partial(_na_kernel, n_rows=n_rows),
        grid=(B, L // tq),
        in_specs=[pl.BlockSpec((1, tq, BRANCH_W), lambda b, i: (b, i, 0)),
                  pl.BlockSpec((1, L, BRANCH_W), lambda b, i: (b, 0, 1)),
                  pl.BlockSpec((1, L, BRANCH_W), lambda b, i: (b, 0, 2)),
                  pl.BlockSpec((1, P, BRANCH_W), lambda b, i: (b, 0, 0)),
                  pl.BlockSpec((1, P, BRANCH_W), lambda b, i: (b, 0, 0))]
                 + [pl.BlockSpec((1, 1, N_HEADS, ATTN_Q, slab), kind(sub)) for sub in range(NA_TILES)],
        out_specs=pl.BlockSpec((1, tq, BRANCH_W), lambda b, i: (b, i, 0)),
        out_shape=jax.ShapeDtypeStruct((B, L, BRANCH_W), F32),
        compiler_params=_cparams(56),
        name="neighbourhood_attention",
    )(na3, na3, na3, kc, vc, *([bias] * NA_TILES)).reshape(B * L, BRANCH_W)


def _attn_kernel(*refs, L, window, gqa, has_ctx, has_sink):
    refs = list(refs)
    q_ref, k_ref, v_ref = refs[:3]
    pos = 3
    if has_ctx:
        kc_ref, vc_ref = refs[pos:pos + 2]
        pos += 2
    if has_sink:
        sink_ref = refs[pos]
        pos += 1
    o_ref = refs[pos]
    n_seq = q_ref.shape[0]
    n_sub = q_ref.shape[1] // ATTN_Q
    for seq, sub in [(a, b) for a in range(n_seq) for b in range(n_sub)]:
        rows = slice(sub * ATTN_Q, (sub + 1) * ATTN_Q)
        if has_ctx:
            kc = kc_ref[seq].astype(BF16)
            vc = vc_ref[seq].astype(BF16)
        if window:
            slab = SWA_SLAB
            q0 = (pl.program_id(1) * n_sub + sub) * ATTN_Q
            start = pl.multiple_of(jnp.clip(q0 - SWA_WINDOW, 0, L - slab), SWA_WINDOW)
            q_pos = q0 + lax.broadcasted_iota(jnp.int32, (ATTN_Q, 1), 0)
            k_pos = start + lax.broadcasted_iota(jnp.int32, (1, slab), 1)
            ok = jnp.abs(k_pos - q_pos) <= SWA_WINDOW
            ks = k_ref[seq, pl.ds(start, slab), :].astype(BF16)
            vs = v_ref[seq, pl.ds(start, slab), :].astype(BF16)
        else:
            ks = k_ref[seq].astype(BF16)
            vs = v_ref[seq].astype(BF16)
        q = q_ref[seq, rows, :].astype(F32)
        kv_w = ks.shape[1]
        halves = [jnp.zeros((ATTN_Q, LANES), F32), jnp.zeros((ATTN_Q, LANES), F32)]
        out = jnp.zeros(q.shape, F32)
        for h in range(N_HEADS):
            if gqa:
                kvh, slot = h // 2, h % 2
                qh = q[:, kvh * LANES:(kvh + 1) * LANES]
                if slot != kvh:
                    qh = pltpu.roll(qh, HEAD_DIM, 1)
                hm = _lane_mask(kv_w, kvh * HEAD_DIM, HEAD_DIM)
            else:
                qh = q
                hm = _lane_mask(kv_w, h * HEAD_DIM, HEAD_DIM)
            qh = jnp.where(hm, qh, 0.0).astype(BF16)
            s = _dot_nt(qh, ks)
            if window:
                s = jnp.where(ok, s, NEG)
            parts = [s]
            if has_ctx:
                parts.append(_dot_nt(qh, kc))
            ps, l = _softmax_parts(parts, sink_ref[h] * LOG2E if has_sink else None)
            o = _dot(ps[0].astype(BF16), vs)
            if has_ctx:
                o = o + _dot(ps[1].astype(BF16), vc)
            o = jnp.where(hm, o / l, 0.0)
            if gqa:
                if slot != kvh:
                    o = pltpu.roll(o, HEAD_DIM, 1)
                halves[kvh] = halves[kvh] + o
            else:
                out = out + o
        if gqa:
            o_ref[seq, rows, 0:LANES] = halves[0]
            o_ref[seq, rows, LANES:2 * LANES] = halves[1]
        else:
            o_ref[seq, rows, :] = out


def _seqs_per_step(B, L, queries):
    ns = max(1, queries // L)
    return ns if B % ns == 0 else 1


def dense_attention(src, cols, B, L, *, window=False, gqa=False, ctx=None, sink=None):
    W = src.shape[1]
    src3 = src.reshape(B, L, W)
    kv_w = SWA_KVW if gqa else BRANCH_W
    qc, kcol, vcol = cols
    tq = min(ATTN_TILES * ATTN_Q, L)
    ns = _seqs_per_step(B, L, ATTN_TILES * ATTN_Q)
    in_specs = [pl.BlockSpec((ns, tq, BRANCH_W), lambda b, i: (b, i, qc)),
                pl.BlockSpec((ns, L, kv_w), lambda b, i: (b, 0, kcol)),
                pl.BlockSpec((ns, L, kv_w), lambda b, i: (b, 0, vcol))]
    args = [src3, src3, src3]
    if ctx is not None:
        P = ctx[0].shape[1]
        in_specs += [pl.BlockSpec((ns, P, kv_w), lambda b, i: (b, 0, 0))] * 2
        args += list(ctx)
    if sink is not None:
        in_specs.append(pl.BlockSpec(memory_space=pltpu.SMEM))
        args.append(sink)
    return pl.pallas_call(
        functools.partial(_attn_kernel, L=L, window=window, gqa=gqa,
                          has_ctx=ctx is not None, has_sink=sink is not None),
        grid=(B // ns, L // tq),
        in_specs=in_specs,
        out_specs=pl.BlockSpec((ns, tq, BRANCH_W), lambda b, i: (b, i, 0)),
        out_shape=jax.ShapeDtypeStruct((B, L, BRANCH_W), F32),
        compiler_params=_cparams(48),
        name="window_attention" if window else "dense_attention",
    )(*args).reshape(B * L, BRANCH_W)


def _diff_kernel(*refs, lam_init, has_ctx):
    refs = list(refs)
    lam_ref, gain_ref, ones_ref, q_ref, k_ref, v_ref = refs[:6]
    if has_ctx:
        kc_ref, vc_ref, o_ref = refs[6:]
    else:
        (o_ref,) = refs[6:]
    lv = lam_ref[...]
    lam = (jnp.exp(jnp.sum(lv[0:1] * lv[1:2], keepdims=True))
           - jnp.exp(jnp.sum(lv[2:3] * lv[3:4], keepdims=True)) + lam_init)
    for seq in range(q_ref.shape[0]):
        kl = k_ref[seq].astype(BF16)
        vl = v_ref[seq].astype(BF16)
        if has_ctx:
            kc = kc_ref[seq].astype(BF16)
            vc = vc_ref[seq].astype(BF16)
        q = q_ref[seq].astype(BF16)
        out = jnp.zeros(q.shape, F32)
        for h in range(N_HEADS):
            o = None
            for mp in range(2):
                mm = _lane_mask(BRANCH_W, h * HEAD_DIM + mp * DIFF_HD, DIFF_HD)
                qm = jnp.where(mm, q, 0.0)
                parts = [_dot_nt(qm, kl)]
                if has_ctx:
                    parts.append(_dot_nt(qm, kc))
                ps, l = _softmax_parts(parts)
                pv = _dot(ps[0].astype(BF16), vl)
                if has_ctx:
                    pv = pv + _dot(ps[1].astype(BF16), vc)
                o = pv / l if mp == 0 else o - pv * (lam / l)
            out = out + jnp.where(_lane_mask(BRANCH_W, h * HEAD_DIM, HEAD_DIM), o, 0.0)
        ms = _dot_exact(out * out, ones_ref[...]) * (1.0 / HEAD_DIM)
        o_ref[seq] = out * lax.rsqrt(ms + EPS) * gain_ref[...] * (1.0 - lam_init)


def diff_attention(dif, lam_vec, subln, layer, B, L, ctx=None):
    lam_init = 0.8 - 0.6 * math.exp(-0.3 * layer)
    dif3 = dif.reshape(B, L, DIFF_W)
    head = jnp.arange(BRANCH_W) // HEAD_DIM
    ones = (head[:, None] == head[None, :]).astype(F32)
    gain = jnp.tile(subln.astype(F32), N_HEADS).reshape(1, BRANCH_W)
    tq = min(DIFF_Q, L)
    ns = _seqs_per_step(B, L, DIFF_Q)
    in_specs = [_resident((4, DIFF_HD)), _resident((1, BRANCH_W)), _resident((BRANCH_W, BRANCH_W)),
                pl.BlockSpec((ns, tq, BRANCH_W), lambda b, i: (b, i, 0)),
                pl.BlockSpec((ns, L, BRANCH_W), lambda b, i: (b, 0, 1)),
                pl.BlockSpec((ns, L, BRANCH_W), lambda b, i: (b, 0, 2))]
    args = [lam_vec, gain, ones, dif3, dif3, dif3]
    if ctx is not None:
        P = ctx[0].shape[1]
        in_specs += [pl.BlockSpec((ns, P, BRANCH_W), lambda b, i: (b, 0, 0))] * 2
        args += list(ctx)
    return pl.pallas_call(
        functools.partial(_diff_kernel, lam_init=lam_init, has_ctx=ctx is not None),
        grid=(B // ns, L // tq),
        in_specs=in_specs,
        out_specs=pl.BlockSpec((ns, tq, BRANCH_W), lambda b, i: (b, i, 0)),
        out_shape=jax.ShapeDtypeStruct((B, L, BRANCH_W), F32),
        compiler_params=_cparams(56),
        name="diff_attention",
    )(*args).reshape(B * L, BRANCH_W)


def _filter_kernel(z_ref, w1_ref, b1_ref, w2_ref, b2_ref, w3_ref, fr_ref, ld_ref, o_ref):
    z = z_ref[...]
    tn = z[:, 0:1]
    fr = fr_ref[0]
    g = jnp.sin(fr * (_dot_exact(z, w1_ref[0]) + b1_ref[0]))
    g = jnp.sin(fr * (_dot_exact(g, w2_ref[0]) + b2_ref[0]))
    hf = _dot_exact(g, w3_ref[0]) * jnp.exp(-jnp.exp(ld_ref[0]) * tn)
    row = lax.broadcasted_iota(jnp.int32, (z.shape[0], 1), 0)
    for o in range(HY_ORDER):
        pos = hf[:, (2 * o) * HY_WIDTH:(2 * o + 1) * HY_WIDTH]
        neg = jnp.where(row == 0, 0.0, hf[:, (2 * o + 1) * HY_WIDTH:(2 * o + 2) * HY_WIDTH])
        norm = (jnp.sum(jnp.abs(pos), axis=0, keepdims=True)
                + jnp.sum(jnp.abs(neg), axis=0, keepdims=True) + EPS)
        inv = 1.0 / norm
        o_ref[0, o, 0] = (neg + pos) * inv
        o_ref[0, o, 1] = (neg - pos) * inv


def hyena_filters(L, p):
    tn = jnp.arange(L, dtype=F32) / L
    ang = 2.0 * math.pi * tn[:, None] * jnp.arange(1, HY_FREQS + 1, dtype=F32)[None, :]
    z = jnp.concatenate([tn[:, None], jnp.cos(ang), jnp.sin(ang)], axis=-1)
    z = jnp.pad(z, ((0, 0), (0, HY_HIDDEN - HY_EMB)))
    w1 = jnp.pad(p['hy_w1'], ((0, 0), (0, HY_HIDDEN - HY_EMB), (0, 0)))
    fw = HY_ORDER * 2 * HY_WIDTH
    per_layer = lambda *shape: pl.BlockSpec((1,) + shape, lambda l: (l,) + (0,) * len(shape))
    return pl.pallas_call(
        _filter_kernel,
        grid=(DEPTH,),
        in_specs=[pl.BlockSpec((L, HY_HIDDEN), lambda l: (0, 0)),
                  per_layer(HY_HIDDEN, HY_HIDDEN), per_layer(1, HY_HIDDEN),
                  per_layer(HY_HIDDEN, HY_HIDDEN), per_layer(1, HY_HIDDEN),
                  per_layer(HY_HIDDEN, fw), per_layer(1, HY_HIDDEN), per_layer(1, fw)],
        out_specs=per_layer(HY_ORDER, 2, L, HY_WIDTH),
        out_shape=jax.ShapeDtypeStruct((DEPTH, HY_ORDER, 2, L, HY_WIDTH), F32),
        compiler_params=_cparams(56),
        name="hyena_filters",
    )(z, w1, p['hy_b1'].reshape(DEPTH, 1, HY_HIDDEN), p['hy_w2'], p['hy_b2'].reshape(DEPTH, 1, HY_HIDDEN),
      p['hy_w3'], p['hy_sin_freq'].reshape(DEPTH, 1, HY_HIDDEN), p['hy_log_decay'].reshape(DEPTH, 1, fw))


def dft_table(L, tk, half_sample):
    assert L % LANES == 0 and L & (L - 1) == 0 and L // DFT_LO <= LANES
    return pl.pallas_call(
        functools.partial(_dft_table_kernel, half_sample=half_sample),
        grid=(L // tk,),
        out_specs=pl.BlockSpec((1, 2 * tk, L), lambda i: (i, 0, 0)),
        out_shape=jax.ShapeDtypeStruct((L // tk, 2 * tk, L), BF16),
        compiler_params=_cparams(48),
        name="dft_table",
    )()


DFT_LO = 64


def _dft_table_kernel(o_ref, *, half_sample):
    _, tk2, L = o_ref.shape
    tk = tk2 // 2
    k2 = 2 * (pl.program_id(0) * tk + lax.broadcasted_iota(jnp.int32, (tk, LANES), 0)) + 1
    lane = lax.broadcasted_iota(jnp.int32, (tk, LANES), 1)

    def cos_sin(m):
        ang = ((k2 * m) & (8 * L - 1)).astype(F32) * (math.pi / (4 * L))
        return jnp.cos(ang), jnp.sin(ang)

    c_hi, s_hi = cos_sin(2 * DFT_LO * lane)
    c_lo, s_lo = cos_sin(2 * lane + (1 if half_sample else 0))
    n = lax.broadcasted_iota(jnp.int32, (LANES, L), 1)
    r = lax.broadcasted_iota(jnp.int32, (LANES, L), 0)
    pick_hi = jnp.where(r == (n >> (DFT_LO.bit_length() - 1)), 1.0, 0.0).astype(BF16)
    pick_lo = jnp.where(r == (n & (DFT_LO - 1)), 1.0, 0.0).astype(BF16)

    def copy(t, pick):
        head = t.astype(BF16)
        return _dot(head, pick) + _dot((t - head.astype(F32)).astype(BF16), pick)

    ch, sh = copy(c_hi, pick_hi), copy(s_hi, pick_hi)
    cl, sl = copy(c_lo, pick_lo), copy(s_lo, pick_lo)
    o_ref[0, :tk, :] = (ch * cl - sh * sl).astype(BF16)
    o_ref[0, tk:, :] = (sh * cl + ch * sl).astype(BF16)


def _spectrum_kernel(t_ref, f_ref, o_ref):
    nkt, tk2, _ = t_ref.shape
    tk = tk2 // 2
    sums, diffs = f_ref[0, 0].astype(BF16), f_ref[0, 1].astype(BF16)
    for i in range(nkt):
        rows = slice(i * tk, (i + 1) * tk)
        o_ref[0, 0, rows, :] = _dot(t_ref[i, :tk, :], sums)
        o_ref[0, 1, rows, :] = _dot(t_ref[i, tk:, :], diffs)


def filter_spectra(filt, fwd):
    G, _, L, C = filt.shape
    return pl.pallas_call(
        _spectrum_kernel,
        grid=(G,),
        in_specs=[_resident(fwd.shape), pl.BlockSpec((1, 2, L, C), lambda g: (g, 0, 0, 0))],
        out_specs=pl.BlockSpec((1, 2, L, C), lambda g: (g, 0, 0, 0)),
        out_shape=jax.ShapeDtypeStruct((G, 2, L, C), F32),
        compiler_params=_cparams(48),
        name="filter_spectra",
    )(fwd, filt)


def _hyena_kernel(t_ref, h_ref, u_ref, w_ref, b_ref, skip_ref, o_ref, z_ref, y_ref):
    nkt, tk2, L = t_ref.shape
    tk = tk2 // 2
    C = HY_WIDTH
    row = lax.broadcasted_iota(jnp.int32, (L, 1), 0)

    def short_conv(col):
        sl = slice(col * C, (col + 1) * C)
        u = u_ref[0, :, sl]
        prev = jnp.where(row == 0, 0.0, pltpu.roll(u, 1, 0))
        nxt = jnp.where(row == L - 1, 0.0, pltpu.roll(u, L - 1, 0))
        return prev * w_ref[0:1, sl] + u * w_ref[1:2, sl] + nxt * w_ref[2:3, sl] + b_ref[:, sl]

    y_ref[...] = short_conv(0)
    for o in range(HY_ORDER):
        y = y_ref[...].astype(BF16)
        for i in range(nkt):
            rows = slice(i * tk, (i + 1) * tk)
            acc = _dot(t_ref[i], y)
            yc, ys = acc[:tk], acc[tk:]
            hr = h_ref[0, o, 0, rows, :]
            hi = h_ref[0, o, 1, rows, :]
            z_ref[0, rows, :] = (yc * hr + ys * hi).astype(BF16)
            z_ref[1, rows, :] = (yc * hi - ys * hr).astype(BF16)
        gate = short_conv(1 + o)
        zr, zi = z_ref[0], z_ref[1]
        for i in range(nkt):
            rows = slice(i * tk, (i + 1) * tk)
            conv = (_dot(t_ref[i, :tk, :], zr) - _dot(t_ref[i, tk:, :], zi)) * (1.0 / L)
            new = gate[rows] * (conv + skip_ref[o:o + 1, :] * y_ref[rows, :])
            if o == HY_ORDER - 1:
                o_ref[0, rows, :] = new
            else:
                y_ref[rows, :] = new


def hyena(hy, lp, layer, spec, table, B, L):
    C = HY_WIDTH
    return pl.pallas_call(
        _hyena_kernel,
        grid=(B,),
        in_specs=[_resident(table.shape),
                  pl.BlockSpec((1, HY_ORDER, 2, L, C), lambda b: (layer, 0, 0, 0, 0), pipeline_mode=pl.Buffered(1)),
                  pl.BlockSpec((1, L, HY_IN_W), lambda b: (b, 0, 0)),
                  _resident((3, HY_IN_W)), _resident((1, HY_IN_W)), _resident((HY_ORDER, C))],
        out_specs=pl.BlockSpec((1, L, C), lambda b: (b, 0, 0)),
        out_shape=jax.ShapeDtypeStruct((B, L, C), F32),
        scratch_shapes=[pltpu.VMEM((2, L, C), BF16), pltpu.VMEM((L, C), F32)],
        compiler_params=_cparams(56),
        name="hyena",
    )(table, spec, hy.reshape(B, L, HY_IN_W), lp['hy_short_w'], lp['hy_short_b'].reshape(1, HY_IN_W),
      lp['hy_skip']).reshape(B * L, C)


def _merge_kernel(a_ref, b_ref, c_ref, d_ref, x_ref, mod_ref, nw_ref, wg_ref, wb_ref, wo_ref, o_ref):
    rows = x_ref.shape[0] // MERGE_SPLIT
    groups = [slice(g * rows, (g + 1) * rows) for g in range(MERGE_SPLIT)]
    hs = [_norm_mod(x_ref[g, :], nw_ref[...], mod_ref[0, 3:4, :], mod_ref[0, 4:5, :]).astype(BF16) for g in groups]
    for g, h in zip(groups, hs):
        merged = None
        for i, br in enumerate((a_ref, b_ref, c_ref, d_ref)):
            gate = _sigmoid(_dot(h, wg_ref[:, MIX_W + i * D_MODEL:MIX_W + (i + 1) * D_MODEL]))
            t = gate * _dot(br[g, :].astype(BF16), wb_ref[i])
            merged = t if merged is None else merged + t
        o_ref[g, :] = x_ref[g, :] + mod_ref[0, 5:6, :] * _dot(merged.astype(BF16), wo_ref[...])


def merge_block(branches, x, mod, nw, wg, wb, wo, layer, rows_per_b):
    T = x.shape[0]
    tm = MERGE_ROWS
    nb = mod.shape[0]
    row = lambda w: pl.BlockSpec((tm, w), lambda i: (i, 0))
    return pl.pallas_call(
        _merge_kernel,
        grid=(T // tm,),
        in_specs=[row(BRANCH_W)] * N_BRANCH + [row(D_MODEL), _mod_spec(nb, rows_per_b // tm),
                                               _resident((1, D_MODEL)),
                                               _resident_layer(wg, layer),
                                               _resident_layer(wb, layer),
                                               _resident_layer(wo, layer)],
        out_specs=row(D_MODEL),
        out_shape=jax.ShapeDtypeStruct((T, D_MODEL), F32),
        compiler_params=_cparams(56),
        name="merge_block",
    )(*branches, x, mod, nw.reshape(1, D_MODEL), wg, wb, wo)


def _heads_in(t):
    B, H, P, d = t.shape
    return t.transpose(0, 2, 1, 3).reshape(B, P, H * d)


def _run_pass(x, mod_all, p, wts, ffn_w, ffn_f32, final_norm, B, L, spec, tabs, caches):
    ctx_pass = caches is None
    collected = []
    if not ctx_pass:
        rope = rope_tables(L, HEAD_DIM, SWA_QW + SWA_KVW) + rope_tables(L, DIFF_HD, 2 * BRANCH_W)
        bias_all = na_bias_tables(p['na_rpb'], L // GRID_W)
    for l in range(DEPTH):
        lp = {k: v[l] for k, v in p.items()}
        w = wts
        mod = mod_all[l]
        cast_next = ctx_pass and l + 1 < DEPTH
        x, next_ffn1 = ffn_block(x, mod, lp['norm_ffn1'], *ffn_w[l][0], 0, L,
                                 cast=(l + 1, ffn_f32[0]) if cast_next else None)
        na, swa, hy, dif = in_projection(x, mod, lp['norm_mix'], w['w_in'], l, L,
                                         None if ctx_pass else rope, F32 if ctx_pass else BF16)
        if ctx_pass:
            a_o = dense_attention(na, (0, 1, 2), B, L)
            b_o = dense_attention(swa, (0, 2, 3), B, L, gqa=True, sink=lp['swa_sink'])
            d_o = diff_attention(dif, lp['diff_lambda'], lp['diff_subln'], l, B, L)
            collected.append((na, swa, dif))
        else:
            ck_na, cv_na, ck_swa, cv_swa, ck_d, cv_d = (_heads_in(t[:, l]).astype(BF16) for t in caches)
            a_o = neighbourhood_attention(na, ck_na, cv_na, bias_all, l, B, L)
            b_o = dense_attention(swa, (0, 2, 3), B, L, window=True, gqa=True,
                                  ctx=(ck_swa, cv_swa), sink=lp['swa_sink'])
            d_o = diff_attention(dif, lp['diff_lambda'], lp['diff_subln'], l, B, L, ctx=(ck_d, cv_d))
        c_o = hyena(hy, lp, l, spec, tabs, B, L)
        x = merge_block((a_o, b_o, c_o, d_o), x, mod, lp['norm_mix'], w['w_in'], w['w_branch'], w['w_out'], l, L)
        x, next_ffn2 = ffn_block(x, mod, lp['norm_ffn2'], *ffn_w[l][1], 6, L,
                                 final_w=final_norm if l == DEPTH - 1 else None,
                                 cast=(l + 1, ffn_f32[1]) if cast_next else None)
        if cast_next:
            ffn_w.append((next_ffn1, next_ffn2))
    return x, collected


_CACHE_SLOTS = ((0, N_HEADS, N_HEADS), (0, 2 * N_HEADS, N_HEADS),
                (1, N_HEADS, SWA_KV_HEADS), (1, N_HEADS + SWA_KV_HEADS, SWA_KV_HEADS),
                (2, N_HEADS, N_HEADS), (2, 2 * N_HEADS, N_HEADS))


def _cache_kernel(*refs):
    srcs, outs = refs[:3 * DEPTH], refs[3 * DEPTH:]
    for l in range(DEPTH):
        for o_ref, (src, slot0, n) in zip(outs, _CACHE_SLOTS):
            x_ref = srcs[3 * l + src]
            for h in range(n):
                o_ref[0, l, h] = x_ref[:, (slot0 + h) * HEAD_DIM:(slot0 + h + 1) * HEAD_DIM]


def _new_caches(collected, B, L):
    srcs = [t for layer in collected for t in layer]
    return pl.pallas_call(
        _cache_kernel,
        grid=(B,),
        in_specs=[pl.BlockSpec((L, t.shape[1]), lambda b: (b, 0)) for t in srcs],
        out_specs=[pl.BlockSpec((1, DEPTH, n, L, HEAD_DIM), lambda b: (b, 0, 0, 0, 0)) for _, _, n in _CACHE_SLOTS],
        out_shape=[jax.ShapeDtypeStruct((B, DEPTH, n, L, HEAD_DIM), F32) for _, _, n in _CACHE_SLOTS],
        compiler_params=_cparams(48),
        name="cache_outputs",
    )(*srcs)


def _hyena_setup(L, p, tk):
    filt = hyena_filters(L, p)
    spec = filter_spectra(filt.reshape(DEPTH * HY_ORDER, 2, L, HY_WIDTH), dft_table(L, tk, half_sample=False))
    return spec.reshape(DEPTH, HY_ORDER, 2, L, HY_WIDTH), dft_table(L, tk, half_sample=True)


def kernel(x_prompt, x_sample, cache_na_k, cache_na_v, cache_swa_k, cache_swa_v, cache_diff_k, cache_diff_v, c, c_ctx, w_ada, b_ada, norm_ffn1, norm_mix, norm_ffn2, final_norm, ffn1_w1, ffn1_w3, ffn1_w2, ffn2_w1, ffn2_w3, ffn2_w2, w_in, w_branch, w_out, na_rpb, swa_sink, hy_short_w, hy_short_b, hy_w1, hy_b1, hy_w2, hy_b2, hy_w3, hy_sin_freq, hy_log_decay, hy_skip, diff_lambda, diff_subln):
    B_ctx, L_ctx, _ = x_prompt.shape
    B_den, L_den, _ = x_sample.shape
    p = {
        'norm_ffn1': norm_ffn1, 'norm_mix': norm_mix, 'norm_ffn2': norm_ffn2,
        'na_rpb': na_rpb, 'swa_sink': swa_sink, 'hy_short_w': hy_short_w, 'hy_short_b': hy_short_b,
        'hy_w1': hy_w1, 'hy_b1': hy_b1, 'hy_w2': hy_w2, 'hy_b2': hy_b2, 'hy_w3': hy_w3,
        'hy_sin_freq': hy_sin_freq, 'hy_log_decay': hy_log_decay, 'hy_skip': hy_skip,
        'diff_lambda': diff_lambda, 'diff_subln': diff_subln,
    }
    wts = {'w_in': w_in.astype(BF16), 'w_branch': w_branch.astype(BF16), 'w_out': w_out.astype(BF16)}
    ffn_f32 = ([ffn1_w1, ffn1_w3, ffn1_w2], [ffn2_w1, ffn2_w3, ffn2_w2])
    ffn_w = [tuple([w[0].astype(BF16) for w in group] for group in ffn_f32)]

    cond = jnp.concatenate([c, c_ctx[None, :]], axis=0)
    rows = 8 * ((cond.shape[0] + 7) // 8)
    cond = jnp.pad(cond, ((0, rows - cond.shape[0]), (0, 0)))
    mod = ada_modulation(cond, w_ada, b_ada).reshape(DEPTH, rows, N_MOD, D_MODEL)
    mod_den = mod[:, :B_den]
    mod_ctx = mod[:, B_den:B_den + 1]

    spec_c, tabs_c = _hyena_setup(L_ctx, p, min(L_ctx, HY_FREQ_TILE))
    y_ctx, collected = _run_pass(x_prompt.reshape(B_ctx * L_ctx, D_MODEL), mod_ctx, p, wts, ffn_w, ffn_f32,
                                 final_norm, B_ctx, L_ctx, spec_c, tabs_c, None)
    new_caches = _new_caches(collected, B_ctx, L_ctx)

    spec_d, tabs_d = _hyena_setup(L_den, p, min(L_den, HY_FREQ_TILE))
    caches = (cache_na_k, cache_na_v, cache_swa_k, cache_swa_v, cache_diff_k, cache_diff_v)
    y_den, _ = _run_pass(x_sample.reshape(B_den * L_den, D_MODEL), mod_den, p, wts, ffn_w, ffn_f32,
                         final_norm, B_den, L_den, spec_d, tabs_d, caches)
    return (y_ctx.reshape(B_ctx, L_ctx, D_MODEL), y_den.reshape(B_den, L_den, D_MODEL), *new_caches)
```

```python
import functools
import math

import jax
import jax.numpy as jnp
from jax import lax
from jax.experimental import pallas as pl
from jax.experimental.pallas import tpu as pltpu

F32 = jnp.float32
BF16 = jnp.bfloat16

D_MODEL = 1024
DEPTH = 4
GRID_W = 64
N_BRANCH = 4
BRANCH_W = D_MODEL // 4
HEAD_DIM = 64
N_HEADS = BRANCH_W // HEAD_DIM
NA_WIN_R = 8
NA_WIN_C = 16
SWA_KV_HEADS = N_HEADS // 2
SWA_WINDOW = 128
HY_WIDTH = BRANCH_W
HY_ORDER = 2
HY_FREQS = 16
HY_EMB = 1 + 2 * HY_FREQS
HY_HIDDEN = 64
DIFF_HD = 32
D_FF = 128 * ((8 * D_MODEL // 3 + 127) // 128)
ROPE_BASE = 10000.0
EPS = 1e-6
NEG = -1e30
N_MOD = 9
NA_W = 3 * BRANCH_W
SWA_QW = BRANCH_W
SWA_KVW = SWA_KV_HEADS * HEAD_DIM
SWA_W = SWA_QW + 2 * SWA_KVW
HY_IN_W = 3 * HY_WIDTH
DIFF_W = 3 * BRANCH_W
GATE_W = N_BRANCH * D_MODEL
MIX_W = NA_W + SWA_W + HY_IN_W + DIFF_W

LOG2E = math.log2(math.e)
QK_SCALE = HEAD_DIM ** -0.5 * LOG2E
DIFF_QK_SCALE = DIFF_HD ** -0.5 * LOG2E

LANES = 128
MXU_DIM = 256
MIB = 1024 * 1024

FFN_ROWS = 1024
FFN_CAST_ROWS = 512
FFN_SPLIT = 2
MERGE_ROWS = 512
MERGE_SPLIT = 2
PROJ_ROWS = 512
PROJ_SPLIT = 2
FFN_CHUNK = 768
ATTN_Q = 256
ATTN_TILES = 4
NA_TILES = 4
HY_FREQ_TILE = 512
DIFF_Q = 512
NA_Q_ROWS = ATTN_Q // GRID_W
NA_SLAB_ROWS = NA_Q_ROWS + NA_WIN_R
SWA_SLAB = ATTN_Q + 2 * SWA_WINDOW


def _cparams(vmem_mib):
    return pltpu.CompilerParams(vmem_limit_bytes=vmem_mib * MIB)


def _resident(shape):
    nd = len(shape)
    return pl.BlockSpec(shape, lambda *_: (0,) * nd, pipeline_mode=pl.Buffered(1))


def _dot(a, b):
    return jnp.dot(a, b, preferred_element_type=F32)


def _dot_nt(a, b):
    return lax.dot_general(a, b, (((1,), (1,)), ((), ())), preferred_element_type=F32)


def _dot_exact(a, b):
    return jnp.dot(a, b, preferred_element_type=F32, precision=lax.Precision.HIGHEST)


def _sigmoid(x):
    return 1.0 / (1.0 + jnp.exp(-x))


def _norm_mod(x, nw, shift, scale):
    return x * lax.rsqrt(jnp.mean(x * x, axis=-1, keepdims=True) + EPS) * (nw * (1.0 + scale)) + shift


def _mod_spec(nb, tiles_per_b):
    if nb == 1:
        return pl.BlockSpec((1, N_MOD, D_MODEL), lambda i: (0, 0, 0))
    return pl.BlockSpec((1, N_MOD, D_MODEL), lambda i: (i // tiles_per_b, 0, 0))


def _ada_kernel(c_ref, w_ref, b_ref, o_ref):
    c = c_ref[...]
    s = (c * _sigmoid(c)).astype(BF16)
    o_ref[0] = _dot(s, w_ref[0].astype(BF16)) + b_ref[0]


def ada_modulation(cond, w_ada, b_ada):
    rows = cond.shape[0]
    width = N_MOD * D_MODEL
    tn = 9 * LANES
    return pl.pallas_call(
        _ada_kernel,
        grid=(DEPTH, width // tn),
        in_specs=[pl.BlockSpec((rows, D_MODEL), lambda l, j: (0, 0)),
                  pl.BlockSpec((1, D_MODEL, tn), lambda l, j: (l, 0, j)),
                  pl.BlockSpec((1, 1, tn), lambda l, j: (l, 0, j))],
        out_specs=pl.BlockSpec((1, rows, tn), lambda l, j: (l, 0, j)),
        out_shape=jax.ShapeDtypeStruct((DEPTH, rows, width), F32),
        compiler_params=_cparams(32),
        name="ada_modulation",
    )(cond, w_ada, b_ada.reshape(DEPTH, 1, width))


def _cast_specs(cast, steps):
    in_specs, out_specs, out_shape, args = [], [], [], []
    if cast is not None:
        layer, sources = cast
        for src in sources:
            _, rows, cols = src.shape
            in_specs.append(pl.BlockSpec((None, rows // steps, cols), lambda i: (layer, i, 0)))
            out_specs.append(pl.BlockSpec((rows // steps, cols), lambda i: (i, 0)))
            out_shape.append(jax.ShapeDtypeStruct((rows, cols), BF16))
            args.append(src)
    return in_specs, out_specs, out_shape, args


def _cast_blocks(cast_in, cast_out):
    for src, dst in zip(cast_in, cast_out):
        dst[...] = src[...].astype(BF16)


def _ffn_kernel(x_ref, mod_ref, nw_ref, w1_ref, w3_ref, w2_ref, *rest, mod_base, final, n_cast):
    rest = list(rest)
    fw_ref = rest.pop(0) if final else None
    cast_in, o_ref, cast_out = rest[:n_cast], rest[n_cast], rest[n_cast + 1:]
    _cast_blocks(cast_in, cast_out)
    shift = mod_ref[0, mod_base:mod_base + 1, :]
    scale = mod_ref[0, mod_base + 1:mod_base + 2, :]
    gate = mod_ref[0, mod_base + 2:mod_base + 3, :]
    rows = x_ref.shape[0] // FFN_SPLIT
    groups = [slice(g * rows, (g + 1) * rows) for g in range(FFN_SPLIT)]
    hs = [_norm_mod(x_ref[g, :], nw_ref[...], shift, scale).astype(BF16) for g in groups]
    for g, h in zip(groups, hs):
        acc = jnp.zeros((rows, D_MODEL), F32)
        for lo in range(0, D_FF, FFN_CHUNK):
            hi = min(lo + FFN_CHUNK, D_FF)
            a = _dot(h, w1_ref[:, lo:hi])
            b = _dot(h, w3_ref[:, lo:hi])
            u = (a * _sigmoid(a) * b).astype(BF16)
            acc = acc + _dot(u, w2_ref[lo:hi, :])
        y = x_ref[g, :] + 0.5 * gate * acc
        if final:
            y = y * lax.rsqrt(jnp.mean(y * y, axis=-1, keepdims=True) + EPS) * fw_ref[...]
        o_ref[g, :] = y


def ffn_block(x, mod, nw, w1, w3, w2, mod_base, rows_per_b, final_w=None, cast=None):
    T = x.shape[0]
    tm = FFN_ROWS if cast is None else FFN_CAST_ROWS
    steps = T // tm
    nb = mod.shape[0]
    final = final_w is not None
    in_specs = [pl.BlockSpec((tm, D_MODEL), lambda i: (i, 0)),
                _mod_spec(nb, rows_per_b // tm),
                _resident((1, D_MODEL)),
                _resident(w1.shape), _resident(w3.shape), _resident(w2.shape)]
    args = [x, mod, nw.reshape(1, D_MODEL), w1, w3, w2]
    if final:
        in_specs.append(_resident((1, D_MODEL)))
        args.append(final_w.reshape(1, D_MODEL))
    c_in, c_out, c_shape, c_args = _cast_specs(cast, steps)
    outs = pl.pallas_call(
        functools.partial(_ffn_kernel, mod_base=mod_base, final=final, n_cast=len(c_args)),
        grid=(steps,),
        in_specs=in_specs + c_in,
        out_specs=[pl.BlockSpec((tm, D_MODEL), lambda i: (i, 0))] + c_out,
        out_shape=[jax.ShapeDtypeStruct((T, D_MODEL), F32)] + c_shape,
        compiler_params=_cparams(56),
        name="ffn_block",
    )(*args, *c_args)
    return outs[0], list(outs[1:])


def _rope_chunk(x, cos, sin_a, sin_b, dist):
    return x * cos + pltpu.roll(x, LANES - dist, 1) * sin_a + pltpu.roll(x, dist, 1) * sin_b


def _proj_kernel(x_ref, mod_ref, nw_ref, w_ref, *rest, rope, n_cast):
    rest = list(rest)
    if rope:
        cs_ref, sa_ref, sb_ref, cd_ref, da_ref, db_ref = rest[:6]
        rest = rest[6:]
    cast_in, (na_ref, swa_ref, hy_ref, dif_ref), cast_out = rest[:n_cast], rest[n_cast:n_cast + 4], rest[n_cast + 4:]
    _cast_blocks(cast_in, cast_out)
    rows = x_ref.shape[0] // PROJ_SPLIT
    groups = [slice(g * rows, (g + 1) * rows) for g in range(PROJ_SPLIT)]
    hs = [_norm_mod(x_ref[g, :], nw_ref[...], mod_ref[0, 3:4, :], mod_ref[0, 4:5, :]).astype(BF16) for g in groups]
    q_chunks = BRANCH_W // LANES
    for g, h in zip(groups, hs):
        o = 0
        s = _dot(h, w_ref[:, o:o + NA_W])
        o += NA_W
        na_ref[g, :BRANCH_W] = (s[:, :BRANCH_W] * QK_SCALE).astype(na_ref.dtype)
        na_ref[g, BRANCH_W:] = s[:, BRANCH_W:].astype(na_ref.dtype)
        s = _dot(h, w_ref[:, o:o + SWA_W])
        o += SWA_W
        n_rot = (SWA_QW + SWA_KVW) // LANES
        for c in range(SWA_W // LANES):
            sl = slice(c * LANES, (c + 1) * LANES)
            chunk = s[:, sl]
            if rope and c < n_rot:
                chunk = _rope_chunk(chunk, cs_ref[g, sl], sa_ref[g, sl], sb_ref[g, sl], HEAD_DIM // 4)
            if c < q_chunks:
                chunk = chunk * QK_SCALE
            swa_ref[g, sl] = chunk.astype(swa_ref.dtype)
        hy_ref[g, :] = _dot(h, w_ref[:, o:o + HY_IN_W])
        o += HY_IN_W
        s = _dot(h, w_ref[:, o:o + DIFF_W])
        o += DIFF_W
        n_rot = 2 * BRANCH_W // LANES
        for c in range(DIFF_W // LANES):
            sl = slice(c * LANES, (c + 1) * LANES)
            chunk = s[:, sl]
            if rope and c < n_rot:
                chunk = _rope_chunk(chunk, cd_ref[g, sl], da_ref[g, sl], db_ref[g, sl], DIFF_HD // 4)
            if c < q_chunks:
                chunk = chunk * DIFF_QK_SCALE
            dif_ref[g, sl] = chunk.astype(dif_ref.dtype)


def in_projection(x, mod, nw, w_in, rows_per_b, rope_tabs, qkv_dtype, cast=None):
    T = x.shape[0]
    tm = PROJ_ROWS
    nb = mod.shape[0]
    rope = rope_tabs is not None
    if rope:
        pos_tiles = rows_per_b // tm
        grid = (pos_tiles, T // rows_per_b)
        row_tile = lambda p, b: (b * pos_tiles + p, 0)
        mod_spec = pl.BlockSpec((1, N_MOD, D_MODEL), lambda p, b: (b, 0, 0))
    else:
        grid = (T // tm,)
        row_tile = lambda i: (i, 0)
        mod_spec = _mod_spec(nb, rows_per_b // tm)
    in_specs = [pl.BlockSpec((tm, D_MODEL), row_tile),
                mod_spec,
                _resident((1, D_MODEL)),
                pl.BlockSpec((D_MODEL, MIX_W), lambda *_: (0, 0), pipeline_mode=pl.Buffered(1))]
    args = [x, mod, nw.reshape(1, D_MODEL), w_in]
    if rope:
        for t in rope_tabs:
            in_specs.append(pl.BlockSpec((tm, t.shape[1]), lambda p, b: (p, 0)))
            args.append(t)
    assert cast is None or not rope
    c_in, c_out, c_shape, c_args = _cast_specs(cast, T // tm)
    widths = (NA_W, SWA_W, HY_IN_W, DIFF_W)
    dtypes = (qkv_dtype, qkv_dtype, F32, qkv_dtype)
    outs = pl.pallas_call(
        functools.partial(_proj_kernel, rope=rope, n_cast=len(c_args)),
        grid=grid,
        in_specs=in_specs + c_in,
        out_specs=[pl.BlockSpec((tm, w), row_tile) for w in widths] + c_out,
        out_shape=[jax.ShapeDtypeStruct((T, w), dt) for w, dt in zip(widths, dtypes)] + c_shape,
        compiler_params=_cparams(56),
        name="in_projection",
    )(*args, *c_args)
    return outs[:4], list(outs[4:])


def rope_tables(L, dh, width):
    nf = dh // 4
    t = jnp.arange(L)
    freqs = ROPE_BASE ** (-jnp.arange(nf, dtype=F32) / nf)
    ang_r = (t // GRID_W).astype(F32)[:, None] * freqs
    ang_c = (t % GRID_W).astype(F32)[:, None] * freqs
    cr, sr, cc, sc = jnp.cos(ang_r), jnp.sin(ang_r), jnp.cos(ang_c), jnp.sin(ang_c)
    z = jnp.zeros_like(sr)
    reps = width // dh
    cos = jnp.tile(jnp.concatenate([cr, cr, cc, cc], axis=-1), (1, reps))
    sin_a = jnp.tile(jnp.concatenate([-sr, z, -sc, z], axis=-1), (1, reps))
    sin_b = jnp.tile(jnp.concatenate([z, sr, z, sc], axis=-1), (1, reps))
    return cos, sin_a, sin_b


def _lane_mask(width, lo, n):
    lane = lax.broadcasted_iota(jnp.int32, (1, width), 1)
    return (lane >= lo) & (lane < lo + n)


def _softmax_parts(parts, extra_logit=None):
    m = parts[0].max(axis=-1, keepdims=True)
    for s in parts[1:]:
        m = jnp.maximum(m, s.max(axis=-1, keepdims=True))
    if extra_logit is not None:
        m = jnp.maximum(m, extra_logit)
    ps = [jnp.exp2(s - m) for s in parts]
    l = ps[0].sum(axis=-1, keepdims=True)
    for p in ps[1:]:
        l = l + p.sum(axis=-1, keepdims=True)
    if extra_logit is not None:
        l = l + jnp.exp2(extra_logit - m)
    return ps, l


def _na_kernel(q_ref, k_ref, v_ref, kc_ref, vc_ref, *rest, n_rows):
    bias_refs, o_ref = rest[:-1], rest[-1]
    slab = NA_SLAB_ROWS * GRID_W
    kc = kc_ref[0].astype(BF16)
    vc = vc_ref[0].astype(BF16)
    for sub, bias_ref in enumerate(bias_refs):
        tile = pl.program_id(1) * len(bias_refs) + sub
        rows = slice(sub * ATTN_Q, (sub + 1) * ATTN_Q)
        row0 = jnp.clip(NA_Q_ROWS * tile - NA_WIN_R // 2, 0, n_rows - NA_SLAB_ROWS)
        start = pl.multiple_of(row0 * GRID_W, GRID_W)
        ks = k_ref[0, pl.ds(start, slab), :].astype(BF16)
        vs = v_ref[0, pl.ds(start, slab), :].astype(BF16)
        q = q_ref[0, rows, :].astype(BF16)
        out = jnp.zeros(q.shape, F32)
        for h in range(N_HEADS):
            hm = _lane_mask(BRANCH_W, h * HEAD_DIM, HEAD_DIM)
            qh = jnp.where(hm, q, 0.0)
            s_win = _dot_nt(qh, ks) + bias_ref[0, 0, h]
            s_ctx = _dot_nt(qh, kc)
            (p_win, p_ctx), l = _softmax_parts([s_win, s_ctx])
            o = _dot(p_win.astype(BF16), vs) + _dot(p_ctx.astype(BF16), vc)
            out = out + jnp.where(hm, o / l, 0.0)
        o_ref[0, rows, :] = out


NA_DROWS = 2 * NA_WIN_R - 1
NA_DCOLS = 2 * NA_WIN_C - 1


def _na_tile_geometry(n_rows):
    n_tiles = n_rows // NA_Q_ROWS

    def geometry(tile):
        slab0 = min(max(NA_Q_ROWS * tile - NA_WIN_R // 2, 0), n_rows - NA_SLAB_ROWS)
        rows = []
        for j in range(NA_Q_ROWS):
            qr = NA_Q_ROWS * tile + j
            r0 = min(max(qr - NA_WIN_R // 2, 0), n_rows - NA_WIN_R)
            rows.append([slab0 + m - qr + NA_WIN_R - 1 if r0 <= slab0 + m < r0 + NA_WIN_R else None
                         for m in range(NA_SLAB_ROWS)])
        return rows

    kinds = [geometry(0), geometry(1), geometry(n_tiles - 1)]
    assert all(geometry(t) == kinds[1] for t in range(1, n_tiles - 1))
    return kinds


def _na_bias_kernel(rpb_ref, o_ref, band_ref, *, kinds):
    base = (pl.program_id(0) * N_HEADS + pl.program_id(1)) * (NA_DROWS * NA_DCOLS)
    qc = lax.broadcasted_iota(jnp.int32, (GRID_W, LANES), 0)
    lane = lax.broadcasted_iota(jnp.int32, (GRID_W, LANES), 1)
    kc = lane % GRID_W
    d_col = kc - qc + (NA_WIN_C - 1)
    c0 = jnp.clip(qc - NA_WIN_C // 2, 0, GRID_W - NA_WIN_C)
    col_ok = (kc >= c0) & (kc < c0 + NA_WIN_C)
    for d in range(NA_DROWS):
        t = jnp.full((GRID_W, LANES), NEG, F32)
        for e in range(NA_DCOLS):
            t = jnp.where(d_col == e, rpb_ref[base + d * NA_DCOLS + e] * LOG2E, t)
        band_ref[d] = jnp.where(col_ok, t, NEG)
    masked = jnp.full((GRID_W, LANES), NEG, F32)
    left = lane < GRID_W
    for kind, rows in enumerate(kinds):
        for j, drow in enumerate(rows):
            for m in range(0, NA_SLAB_ROWS, 2):
                a = masked if drow[m] is None else band_ref[drow[m]]
                b = masked if drow[m + 1] is None else band_ref[drow[m + 1]]
                o_ref[0, kind, 0, j * GRID_W:(j + 1) * GRID_W, m * GRID_W:(m + 2) * GRID_W] = jnp.where(left, a, b)


def na_bias_tables(rpb, n_rows):
    slab = NA_SLAB_ROWS * GRID_W
    return pl.pallas_call(
        functools.partial(_na_bias_kernel, kinds=_na_tile_geometry(n_rows)),
        grid=(DEPTH, N_HEADS),
        in_specs=[pl.BlockSpec(memory_space=pltpu.SMEM)],
        out_specs=pl.BlockSpec((1, 3, 1, ATTN_Q, slab), lambda l, h: (l, 0, h, 0, 0)),
        out_shape=jax.ShapeDtypeStruct((DEPTH, 3, N_HEADS, ATTN_Q, slab), F32),
        scratch_shapes=[pltpu.VMEM((NA_DROWS, GRID_W, LANES), F32)],
        compiler_params=_cparams(32),
        name="na_bias_tables",
    )(rpb.astype(F32).reshape(-1))


def neighbourhood_attention(na, kc, vc, bias, layer, B, L):
    n_rows = L // GRID_W
    n_tiles = L // ATTN_Q
    na3 = na.reshape(B, L, NA_W)
    P = kc.shape[1]
    slab = NA_SLAB_ROWS * GRID_W

    def kind(sub):
        def index(b, i):
            tile = i * NA_TILES + sub
            return (layer, jnp.where(tile == 0, 0, jnp.where(tile == n_tiles - 1, 2, 1)), 0, 0, 0)
        return index

    tq = NA_TILES * ATTN_Q
    return pl.pallas_call(
        functools.partial(_na_kernel, n_rows=n_rows),
        grid=(B, L // tq),
        in_specs=[pl.BlockSpec((1, tq, BRANCH_W), lambda b, i: (b, i, 0)),
                  pl.BlockSpec((1, L, BRANCH_W), lambda b, i: (b, 0, 1)),
                  pl.BlockSpec((1, L, BRANCH_W), lambda b, i: (b, 0, 2)),
                  pl.BlockSpec((1, P, BRANCH_W), lambda b, i: (b, 0, 0)),
                  pl.BlockSpec((1, P, BRANCH_W), lambda b, i: (b, 0, 0))]
                 + [pl.BlockSpec((1, 1, N_HEADS, ATTN_Q, slab), kind(sub)) for sub in range(NA_TILES)],
        out_specs=pl.BlockSpec((1, tq, BRANCH_W), lambda b, i: (b, i, 0)),
        out_shape=jax.ShapeDtypeStruct((B, L, BRANCH_W), F32),
        compiler_params=_cparams(56),
        name="neighbourhood_attention",
    )(na3, na3, na3, kc, vc, *([bias] * NA_TILES)).reshape(B * L, BRANCH_W)


def _attn_kernel(*refs, L, window, gqa, has_ctx, has_sink):
    refs = list(refs)
    q_ref, k_ref, v_ref = refs[:3]
    pos = 3
    if has_ctx:
        kc_ref, vc_ref = refs[pos:pos + 2]
        pos += 2
    if has_sink:
        sink_ref = refs[pos]
        pos += 1
    o_ref = refs[pos]
    n_seq = q_ref.shape[0]
    n_sub = q_ref.shape[1] // ATTN_Q
    for seq, sub in [(a, b) for a in range(n_seq) for b in range(n_sub)]:
        rows = slice(sub * ATTN_Q, (sub + 1) * ATTN_Q)
        if has_ctx:
            kc = kc_ref[seq].astype(BF16)
            vc = vc_ref[seq].astype(BF16)
        if window:
            slab = SWA_SLAB
            q0 = (pl.program_id(1) * n_sub + sub) * ATTN_Q
            start = pl.multiple_of(jnp.clip(q0 - SWA_WINDOW, 0, L - slab), SWA_WINDOW)
            q_pos = q0 + lax.broadcasted_iota(jnp.int32, (ATTN_Q, 1), 0)
            k_pos = start + lax.broadcasted_iota(jnp.int32, (1, slab), 1)
            ok = jnp.abs(k_pos - q_pos) <= SWA_WINDOW
            ks = k_ref[seq, pl.ds(start, slab), :].astype(BF16)
            vs = v_ref[seq, pl.ds(start, slab), :].astype(BF16)
        else:
            ks = k_ref[seq].astype(BF16)
            vs = v_ref[seq].astype(BF16)
        q = q_ref[seq, rows, :].astype(F32)
        kv_w = ks.shape[1]
        halves = [jnp.zeros((ATTN_Q, LANES), F32), jnp.zeros((ATTN_Q, LANES), F32)]
        out = jnp.zeros(q.shape, F32)
        for h in range(N_HEADS):
            if gqa:
                kvh, slot = h // 2, h % 2
                qh = q[:, kvh * LANES:(kvh + 1) * LANES]
                if slot != kvh:
                    qh = pltpu.roll(qh, HEAD_DIM, 1)
                hm = _lane_mask(kv_w, kvh * HEAD_DIM, HEAD_DIM)
            else:
                qh = q
                hm = _lane_mask(kv_w, h * HEAD_DIM, HEAD_DIM)
            qh = jnp.where(hm, qh, 0.0).astype(BF16)
            s = _dot_nt(qh, ks)
            if window:
                s = jnp.where(ok, s, NEG)
            parts = [s]
            if has_ctx:
                parts.append(_dot_nt(qh, kc))
            ps, l = _softmax_parts(parts, sink_ref[h] * LOG2E if has_sink else None)
            o = _dot(ps[0].astype(BF16), vs)
            if has_ctx:
                o = o + _dot(ps[1].astype(BF16), vc)
            o = jnp.where(hm, o / l, 0.0)
            if gqa:
                if slot != kvh:
                    o = pltpu.roll(o, HEAD_DIM, 1)
                halves[kvh] = halves[kvh] + o
            else:
                out = out + o
        if gqa:
            o_ref[seq, rows, 0:LANES] = halves[0]
            o_ref[seq, rows, LANES:2 * LANES] = halves[1]
        else:
            o_ref[seq, rows, :] = out


def _seqs_per_step(B, L, queries):
    ns = max(1, queries // L)
    return ns if B % ns == 0 else 1


def dense_attention(src, cols, B, L, *, window=False, gqa=False, ctx=None, sink=None):
    W = src.shape[1]
    src3 = src.reshape(B, L, W)
    kv_w = SWA_KVW if gqa else BRANCH_W
    qc, kcol, vcol = cols
    tq = min(ATTN_TILES * ATTN_Q, L)
    ns = _seqs_per_step(B, L, ATTN_TILES * ATTN_Q)
    in_specs = [pl.BlockSpec((ns, tq, BRANCH_W), lambda b, i: (b, i, qc)),
                pl.BlockSpec((ns, L, kv_w), lambda b, i: (b, 0, kcol)),
                pl.BlockSpec((ns, L, kv_w), lambda b, i: (b, 0, vcol))]
    args = [src3, src3, src3]
    if ctx is not None:
        P = ctx[0].shape[1]
        in_specs += [pl.BlockSpec((ns, P, kv_w), lambda b, i: (b, 0, 0))] * 2
        args += list(ctx)
    if sink is not None:
        in_specs.append(pl.BlockSpec(memory_space=pltpu.SMEM))
        args.append(sink)
    return pl.pallas_call(
        functools.partial(_attn_kernel, L=L, window=window, gqa=gqa,
                          has_ctx=ctx is not None, has_sink=sink is not None),
        grid=(B // ns, L // tq),
        in_specs=in_specs,
        out_specs=pl.BlockSpec((ns, tq, BRANCH_W), lambda b, i: (b, i, 0)),
        out_shape=jax.ShapeDtypeStruct((B, L, BRANCH_W), F32),
        compiler_params=_cparams(48),
        name="window_attention" if window else "dense_attention",
    )(*args).reshape(B * L, BRANCH_W)


def _diff_kernel(*refs, lam_init, has_ctx):
    refs = list(refs)
    lam_ref, gain_ref, ones_ref, q_ref, k_ref, v_ref = refs[:6]
    if has_ctx:
        kc_ref, vc_ref, o_ref = refs[6:]
    else:
        (o_ref,) = refs[6:]
    lv = lam_ref[...]
    lam = (jnp.exp(jnp.sum(lv[0:1] * lv[1:2], keepdims=True))
           - jnp.exp(jnp.sum(lv[2:3] * lv[3:4], keepdims=True)) + lam_init)
    for seq in range(q_ref.shape[0]):
        kl = k_ref[seq].astype(BF16)
        vl = v_ref[seq].astype(BF16)
        if has_ctx:
            kc = kc_ref[seq].astype(BF16)
            vc = vc_ref[seq].astype(BF16)
        q = q_ref[seq].astype(BF16)
        out = jnp.zeros(q.shape, F32)
        for h in range(N_HEADS):
            o = None
            for mp in range(2):
                mm = _lane_mask(BRANCH_W, h * HEAD_DIM + mp * DIFF_HD, DIFF_HD)
                qm = jnp.where(mm, q, 0.0)
                parts = [_dot_nt(qm, kl)]
                if has_ctx:
                    parts.append(_dot_nt(qm, kc))
                ps, l = _softmax_parts(parts)
                pv = _dot(ps[0].astype(BF16), vl)
                if has_ctx:
                    pv = pv + _dot(ps[1].astype(BF16), vc)
                o = pv / l if mp == 0 else o - pv * (lam / l)
            out = out + jnp.where(_lane_mask(BRANCH_W, h * HEAD_DIM, HEAD_DIM), o, 0.0)
        ms = _dot_exact(out * out, ones_ref[...]) * (1.0 / HEAD_DIM)
        o_ref[seq] = out * lax.rsqrt(ms + EPS) * gain_ref[...] * (1.0 - lam_init)


def diff_attention(dif, lam_vec, subln, layer, B, L, ctx=None):
    lam_init = 0.8 - 0.6 * math.exp(-0.3 * layer)
    dif3 = dif.reshape(B, L, DIFF_W)
    head = jnp.arange(BRANCH_W) // HEAD_DIM
    ones = (head[:, None] == head[None, :]).astype(F32)
    gain = jnp.tile(subln.astype(F32), N_HEADS).reshape(1, BRANCH_W)
    tq = min(DIFF_Q, L)
    ns = _seqs_per_step(B, L, DIFF_Q)
    in_specs = [_resident((4, DIFF_HD)), _resident((1, BRANCH_W)), _resident((BRANCH_W, BRANCH_W)),
                pl.BlockSpec((ns, tq, BRANCH_W), lambda b, i: (b, i, 0)),
                pl.BlockSpec((ns, L, BRANCH_W), lambda b, i: (b, 0, 1)),
                pl.BlockSpec((ns, L, BRANCH_W), lambda b, i: (b, 0, 2))]
    args = [lam_vec, gain, ones, dif3, dif3, dif3]
    if ctx is not None:
        P = ctx[0].shape[1]
        in_specs += [pl.BlockSpec((ns, P, BRANCH_W), lambda b, i: (b, 0, 0))] * 2
        args += list(ctx)
    return pl.pallas_call(
        functools.partial(_diff_kernel, lam_init=lam_init, has_ctx=ctx is not None),
        grid=(B // ns, L // tq),
        in_specs=in_specs,
        out_specs=pl.BlockSpec((ns, tq, BRANCH_W), lambda b, i: (b, i, 0)),
        out_shape=jax.ShapeDtypeStruct((B, L, BRANCH_W), F32),
        compiler_params=_cparams(56),
        name="diff_attention",
    )(*args).reshape(B * L, BRANCH_W)


def _filter_kernel(z_ref, w1_ref, b1_ref, w2_ref, b2_ref, w3_ref, fr_ref, ld_ref, o_ref):
    z = z_ref[...]
    tn = z[:, 0:1]
    fr = fr_ref[0]
    g = jnp.sin(fr * (_dot_exact(z, w1_ref[0]) + b1_ref[0]))
    g = jnp.sin(fr * (_dot_exact(g, w2_ref[0]) + b2_ref[0]))
    hf = _dot_exact(g, w3_ref[0]) * jnp.exp(-jnp.exp(ld_ref[0]) * tn)
    row = lax.broadcasted_iota(jnp.int32, (z.shape[0], 1), 0)
    for o in range(HY_ORDER):
        pos = hf[:, (2 * o) * HY_WIDTH:(2 * o + 1) * HY_WIDTH]
        neg = jnp.where(row == 0, 0.0, hf[:, (2 * o + 1) * HY_WIDTH:(2 * o + 2) * HY_WIDTH])
        norm = (jnp.sum(jnp.abs(pos), axis=0, keepdims=True)
                + jnp.sum(jnp.abs(neg), axis=0, keepdims=True) + EPS)
        inv = 1.0 / norm
        o_ref[0, o, 0] = (neg + pos) * inv
        o_ref[0, o, 1] = (neg - pos) * inv


def hyena_filters(L, p):
    tn = jnp.arange(L, dtype=F32) / L
    ang = 2.0 * math.pi * tn[:, None] * jnp.arange(1, HY_FREQS + 1, dtype=F32)[None, :]
    z = jnp.concatenate([tn[:, None], jnp.cos(ang), jnp.sin(ang)], axis=-1)
    z = jnp.pad(z, ((0, 0), (0, HY_HIDDEN - HY_EMB)))
    w1 = jnp.pad(p['hy_w1'], ((0, 0), (0, HY_HIDDEN - HY_EMB), (0, 0)))
    fw = HY_ORDER * 2 * HY_WIDTH
    per_layer = lambda *shape: pl.BlockSpec((1,) + shape, lambda l: (l,) + (0,) * len(shape))
    return pl.pallas_call(
        _filter_kernel,
        grid=(DEPTH,),
        in_specs=[pl.BlockSpec((L, HY_HIDDEN), lambda l: (0, 0)),
                  per_layer(HY_HIDDEN, HY_HIDDEN), per_layer(1, HY_HIDDEN),
                  per_layer(HY_HIDDEN, HY_HIDDEN), per_layer(1, HY_HIDDEN),
                  per_layer(HY_HIDDEN, fw), per_layer(1, HY_HIDDEN), per_layer(1, fw)],
        out_specs=per_layer(HY_ORDER, 2, L, HY_WIDTH),
        out_shape=jax.ShapeDtypeStruct((DEPTH, HY_ORDER, 2, L, HY_WIDTH), F32),
        compiler_params=_cparams(56),
        name="hyena_filters",
    )(z, w1, p['hy_b1'].reshape(DEPTH, 1, HY_HIDDEN), p['hy_w2'], p['hy_b2'].reshape(DEPTH, 1, HY_HIDDEN),
      p['hy_w3'], p['hy_sin_freq'].reshape(DEPTH, 1, HY_HIDDEN), p['hy_log_decay'].reshape(DEPTH, 1, fw))


def dft_table(L, tk, half_sample):
    assert L % LANES == 0 and L & (L - 1) == 0 and L // DFT_LO <= LANES
    return pl.pallas_call(
        functools.partial(_dft_table_kernel, half_sample=half_sample),
        grid=(L // tk,),
        out_specs=pl.BlockSpec((1, 2 * tk, L), lambda i: (i, 0, 0)),
        out_shape=jax.ShapeDtypeStruct((L // tk, 2 * tk, L), BF16),
        compiler_params=_cparams(48),
        name="dft_table",
    )()


DFT_LO = 64


def _dft_table_kernel(o_ref, *, half_sample):
    _, tk2, L = o_ref.shape
    tk = tk2 // 2
    k2 = 2 * (pl.program_id(0) * tk + lax.broadcasted_iota(jnp.int32, (tk, LANES), 0)) + 1
    lane = lax.broadcasted_iota(jnp.int32, (tk, LANES), 1)

    def cos_sin(m):
        ang = ((k2 * m) & (8 * L - 1)).astype(F32) * (math.pi / (4 * L))
        return jnp.cos(ang), jnp.sin(ang)

    c_hi, s_hi = cos_sin(2 * DFT_LO * lane)
    c_lo, s_lo = cos_sin(2 * lane + (1 if half_sample else 0))
    n = lax.broadcasted_iota(jnp.int32, (LANES, L), 1)
    r = lax.broadcasted_iota(jnp.int32, (LANES, L), 0)
    pick_hi = jnp.where(r == (n >> (DFT_LO.bit_length() - 1)), 1.0, 0.0).astype(BF16)
    pick_lo = jnp.where(r == (n & (DFT_LO - 1)), 1.0, 0.0).astype(BF16)

    def copy(t, pick):
        head = t.astype(BF16)
        return _dot(head, pick) + _dot((t - head.astype(F32)).astype(BF16), pick)

    ch, sh = copy(c_hi, pick_hi), copy(s_hi, pick_hi)
    cl, sl = copy(c_lo, pick_lo), copy(s_lo, pick_lo)
    o_ref[0, :tk, :] = (ch * cl - sh * sl).astype(BF16)
    o_ref[0, tk:, :] = (sh * cl + ch * sl).astype(BF16)


def _spectrum_kernel(t_ref, f_ref, o_ref):
    nkt, tk2, _ = t_ref.shape
    tk = tk2 // 2
    sums, diffs = f_ref[0, 0].astype(BF16), f_ref[0, 1].astype(BF16)
    for i in range(nkt):
        rows = slice(i * tk, (i + 1) * tk)
        o_ref[0, 0, rows, :] = _dot(t_ref[i, :tk, :], sums)
        o_ref[0, 1, rows, :] = _dot(t_ref[i, tk:, :], diffs)


def filter_spectra(filt, fwd):
    G, _, L, C = filt.shape
    return pl.pallas_call(
        _spectrum_kernel,
        grid=(G,),
        in_specs=[_resident(fwd.shape), pl.BlockSpec((1, 2, L, C), lambda g: (g, 0, 0, 0))],
        out_specs=pl.BlockSpec((1, 2, L, C), lambda g: (g, 0, 0, 0)),
        out_shape=jax.ShapeDtypeStruct((G, 2, L, C), F32),
        compiler_params=_cparams(48),
        name="filter_spectra",
    )(fwd, filt)


def _hyena_kernel(t_ref, h_ref, u_ref, w_ref, b_ref, skip_ref, o_ref, z_ref, y_ref):
    nkt, tk2, L = t_ref.shape
    tk = tk2 // 2
    C = HY_WIDTH
    row = lax.broadcasted_iota(jnp.int32, (L, 1), 0)

    def short_conv(col):
        sl = slice(col * C, (col + 1) * C)
        u = u_ref[0, :, sl]
        prev = jnp.where(row == 0, 0.0, pltpu.roll(u, 1, 0))
        nxt = jnp.where(row == L - 1, 0.0, pltpu.roll(u, L - 1, 0))
        return prev * w_ref[0:1, sl] + u * w_ref[1:2, sl] + nxt * w_ref[2:3, sl] + b_ref[:, sl]

    y_ref[...] = short_conv(0)
    for o in range(HY_ORDER):
        y = y_ref[...].astype(BF16)
        for i in range(nkt):
            rows = slice(i * tk, (i + 1) * tk)
            acc = _dot(t_ref[i], y)
            yc, ys = acc[:tk], acc[tk:]
            hr = h_ref[0, o, 0, rows, :]
            hi = h_ref[0, o, 1, rows, :]
            z_ref[0, rows, :] = (yc * hr + ys * hi).astype(BF16)
            z_ref[1, rows, :] = (yc * hi - ys * hr).astype(BF16)
        gate = short_conv(1 + o)
        zr, zi = z_ref[0], z_ref[1]
        for i in range(nkt):
            rows = slice(i * tk, (i + 1) * tk)
            conv = (_dot(t_ref[i, :tk, :], zr) - _dot(t_ref[i, tk:, :], zi)) * (1.0 / L)
            new = gate[rows] * (conv + skip_ref[o:o + 1, :] * y_ref[rows, :])
            if o == HY_ORDER - 1:
                o_ref[0, rows, :] = new
            else:
                y_ref[rows, :] = new


def hyena(hy, lp, layer, spec, table, B, L):
    C = HY_WIDTH
    return pl.pallas_call(
        _hyena_kernel,
        grid=(B,),
        in_specs=[_resident(table.shape),
                  pl.BlockSpec((1, HY_ORDER, 2, L, C), lambda b: (layer, 0, 0, 0, 0), pipeline_mode=pl.Buffered(1)),
                  pl.BlockSpec((1, L, HY_IN_W), lambda b: (b, 0, 0)),
                  _resident((3, HY_IN_W)), _resident((1, HY_IN_W)), _resident((HY_ORDER, C))],
        out_specs=pl.BlockSpec((1, L, C), lambda b: (b, 0, 0)),
        out_shape=jax.ShapeDtypeStruct((B, L, C), F32),
        scratch_shapes=[pltpu.VMEM((2, L, C), BF16), pltpu.VMEM((L, C), F32)],
        compiler_params=_cparams(56),
        name="hyena",
    )(table, spec, hy.reshape(B, L, HY_IN_W), lp['hy_short_w'], lp['hy_short_b'].reshape(1, HY_IN_W),
      lp['hy_skip']).reshape(B * L, C)


def _merge_kernel(a_ref, b_ref, c_ref, d_ref, x_ref, mod_ref, nw_ref, wg_ref, wb_ref, wo_ref, *rest, n_cast):
    cast_in, o_ref, cast_out = rest[:n_cast], rest[n_cast], rest[n_cast + 1:]
    _cast_blocks(cast_in, cast_out)
    rows = x_ref.shape[0] // MERGE_SPLIT
    groups = [slice(g * rows, (g + 1) * rows) for g in range(MERGE_SPLIT)]
    hs = [_norm_mod(x_ref[g, :], nw_ref[...], mod_ref[0, 3:4, :], mod_ref[0, 4:5, :]).astype(BF16) for g in groups]
    for g, h in zip(groups, hs):
        merged = None
        for i, br in enumerate((a_ref, b_ref, c_ref, d_ref)):
            gate = _sigmoid(_dot(h, wg_ref[:, MIX_W + i * D_MODEL:MIX_W + (i + 1) * D_MODEL]))
            t = gate * _dot(br[g, :].astype(BF16), wb_ref[i * BRANCH_W:(i + 1) * BRANCH_W, :])
            merged = t if merged is None else merged + t
        o_ref[g, :] = x_ref[g, :] + mod_ref[0, 5:6, :] * _dot(merged.astype(BF16), wo_ref[...])


def merge_block(branches, x, mod, nw, wg, wb, wo, rows_per_b, cast=None):
    T = x.shape[0]
    tm = MERGE_ROWS
    nb = mod.shape[0]
    row = lambda w: pl.BlockSpec((tm, w), lambda i: (i, 0))
    c_in, c_out, c_shape, c_args = _cast_specs(cast, T // tm)
    outs = pl.pallas_call(
        functools.partial(_merge_kernel, n_cast=len(c_args)),
        grid=(T // tm,),
        in_specs=[row(BRANCH_W)] * N_BRANCH + [row(D_MODEL), _mod_spec(nb, rows_per_b // tm),
                                               _resident((1, D_MODEL)),
                                               _resident(wg.shape), _resident(wb.shape), _resident(wo.shape)] + c_in,
        out_specs=[row(D_MODEL)] + c_out,
        out_shape=[jax.ShapeDtypeStruct((T, D_MODEL), F32)] + c_shape,
        compiler_params=_cparams(56),
        name="merge_block",
    )(*branches, x, mod, nw.reshape(1, D_MODEL), wg, wb, wo, *c_args)
    return outs[0], list(outs[1:])


def _heads_in(t):
    B, H, P, d = t.shape
    return t.transpose(0, 2, 1, 3).reshape(B, P, H * d)


_WEIGHT_GROUPS = ('ffn1', 'w_in', 'mix', 'ffn2')


def _run_pass(x, mod_all, p, layer_w, w_f32, final_norm, B, L, spec, tabs, caches):
    ctx_pass = caches is None
    collected = []
    if not ctx_pass:
        rope = rope_tables(L, HEAD_DIM, SWA_QW + SWA_KVW) + rope_tables(L, DIFF_HD, 2 * BRANCH_W)
        bias_all = na_bias_tables(p['na_rpb'], L // GRID_W)
    for l in range(DEPTH):
        lp = {k: v[l] for k, v in p.items()}
        w = layer_w[l]
        mod = mod_all[l]
        cast = {g: (l + 1, w_f32[g]) if ctx_pass and l + 1 < DEPTH else None for g in _WEIGHT_GROUPS}
        nxt = {}
        x, nxt['ffn1'] = ffn_block(x, mod, lp['norm_ffn1'], *w['ffn1'], 0, L, cast=cast['ffn1'])
        (na, swa, hy, dif), nxt['w_in'] = in_projection(x, mod, lp['norm_mix'], w['w_in'][0], L,
                                                        None if ctx_pass else rope, F32 if ctx_pass else BF16,
                                                        cast=cast['w_in'])
        if ctx_pass:
            a_o = dense_attention(na, (0, 1, 2), B, L)
            b_o = dense_attention(swa, (0, 2, 3), B, L, gqa=True, sink=lp['swa_sink'])
            d_o = diff_attention(dif, lp['diff_lambda'], lp['diff_subln'], l, B, L)
            collected.append((na, swa, dif))
        else:
            ck_na, cv_na, ck_swa, cv_swa, ck_d, cv_d = (_heads_in(t[:, l]).astype(BF16) for t in caches)
            a_o = neighbourhood_attention(na, ck_na, cv_na, bias_all, l, B, L)
            b_o = dense_attention(swa, (0, 2, 3), B, L, window=True, gqa=True,
                                  ctx=(ck_swa, cv_swa), sink=lp['swa_sink'])
            d_o = diff_attention(dif, lp['diff_lambda'], lp['diff_subln'], l, B, L, ctx=(ck_d, cv_d))
        c_o = hyena(hy, lp, l, spec, tabs, B, L)
        x, nxt['mix'] = merge_block((a_o, b_o, c_o, d_o), x, mod, lp['norm_mix'], w['w_in'][0], *w['mix'], L,
                                    cast=cast['mix'])
        x, nxt['ffn2'] = ffn_block(x, mod, lp['norm_ffn2'], *w['ffn2'], 6, L,
                                   final_w=final_norm if l == DEPTH - 1 else None, cast=cast['ffn2'])
        if ctx_pass and l + 1 < DEPTH:
            layer_w.append(nxt)
    return x, collected


_CACHE_SLOTS = ((0, N_HEADS, N_HEADS), (0, 2 * N_HEADS, N_HEADS),
                (1, N_HEADS, SWA_KV_HEADS), (1, N_HEADS + SWA_KV_HEADS, SWA_KV_HEADS),
                (2, N_HEADS, N_HEADS), (2, 2 * N_HEADS, N_HEADS))


def _cache_kernel(*refs):
    srcs, outs = refs[:3 * DEPTH], refs[3 * DEPTH:]
    for l in range(DEPTH):
        for o_ref, (src, slot0, n) in zip(outs, _CACHE_SLOTS):
            x_ref = srcs[3 * l + src]
            for h in range(n):
                o_ref[0, l, h] = x_ref[:, (slot0 + h) * HEAD_DIM:(slot0 + h + 1) * HEAD_DIM]


def _new_caches(collected, B, L):
    srcs = [t for layer in collected for t in layer]
    return pl.pallas_call(
        _cache_kernel,
        grid=(B,),
        in_specs=[pl.BlockSpec((L, t.shape[1]), lambda b: (b, 0)) for t in srcs],
        out_specs=[pl.BlockSpec((1, DEPTH, n, L, HEAD_DIM), lambda b: (b, 0, 0, 0, 0)) for _, _, n in _CACHE_SLOTS],
        out_shape=[jax.ShapeDtypeStruct((B, DEPTH, n, L, HEAD_DIM), F32) for _, _, n in _CACHE_SLOTS],
        compiler_params=_cparams(48),
        name="cache_outputs",
    )(*srcs)


def _hyena_setup(L, p, tk):
    filt = hyena_filters(L, p)
    spec = filter_spectra(filt.reshape(DEPTH * HY_ORDER, 2, L, HY_WIDTH), dft_table(L, tk, half_sample=False))
    return spec.reshape(DEPTH, HY_ORDER, 2, L, HY_WIDTH), dft_table(L, tk, half_sample=True)


def kernel(x_prompt, x_sample, cache_na_k, cache_na_v, cache_swa_k, cache_swa_v, cache_diff_k, cache_diff_v, c, c_ctx, w_ada, b_ada, norm_ffn1, norm_mix, norm_ffn2, final_norm, ffn1_w1, ffn1_w3, ffn1_w2, ffn2_w1, ffn2_w3, ffn2_w2, w_in, w_branch, w_out, na_rpb, swa_sink, hy_short_w, hy_short_b, hy_w1, hy_b1, hy_w2, hy_b2, hy_w3, hy_sin_freq, hy_log_decay, hy_skip, diff_lambda, diff_subln):
    B_ctx, L_ctx, _ = x_prompt.shape
    B_den, L_den, _ = x_sample.shape
    p = {
        'norm_ffn1': norm_ffn1, 'norm_mix': norm_mix, 'norm_ffn2': norm_ffn2,
        'na_rpb': na_rpb, 'swa_sink': swa_sink, 'hy_short_w': hy_short_w, 'hy_short_b': hy_short_b,
        'hy_w1': hy_w1, 'hy_b1': hy_b1, 'hy_w2': hy_w2, 'hy_b2': hy_b2, 'hy_w3': hy_w3,
        'hy_sin_freq': hy_sin_freq, 'hy_log_decay': hy_log_decay, 'hy_skip': hy_skip,
        'diff_lambda': diff_lambda, 'diff_subln': diff_subln,
    }
    w_f32 = {'ffn1': [ffn1_w1, ffn1_w3, ffn1_w2], 'ffn2': [ffn2_w1, ffn2_w3, ffn2_w2], 'w_in': [w_in],
             'mix': [w_branch.reshape(DEPTH, N_BRANCH * BRANCH_W, D_MODEL), w_out]}
    layer_w = [{g: [w[0].astype(BF16) for w in ws] for g, ws in w_f32.items()}]

    cond = jnp.concatenate([c, c_ctx[None, :]], axis=0)
    rows = 8 * ((cond.shape[0] + 7) // 8)
    cond = jnp.pad(cond, ((0, rows - cond.shape[0]), (0, 0)))
    mod = ada_modulation(cond, w_ada, b_ada).reshape(DEPTH, rows, N_MOD, D_MODEL)
    mod_den = mod[:, :B_den]
    mod_ctx = mod[:, B_den:B_den + 1]

    spec_c, tabs_c = _hyena_setup(L_ctx, p, min(L_ctx, HY_FREQ_TILE))
    y_ctx, collected = _run_pass(x_prompt.reshape(B_ctx * L_ctx, D_MODEL), mod_ctx, p, layer_w, w_f32,
                                 final_norm, B_ctx, L_ctx, spec_c, tabs_c, None)
    new_caches = _new_caches(collected, B_ctx, L_ctx)

    spec_d, tabs_d = _hyena_setup(L_den, p, min(L_den, HY_FREQ_TILE))
    caches = (cache_na_k, cache_na_v, cache_swa_k, cache_swa_v, cache_diff_k, cache_diff_v)
    y_den, _ = _run_pass(x_sample.reshape(B_den * L_den, D_MODEL), mod_den, p, layer_w, w_f32,
                         final_norm, B_den, L_den, spec_d, tabs_d, caches)
    return (y_ctx.reshape(B_ctx, L_ctx, D_MODEL), y_den.reshape(B_den, L_den, D_MODEL), *new_caches)
```

```python
import functools
import math

import jax
import jax.numpy as jnp
from jax import lax
from jax.experimental import pallas as pl
from jax.experimental.pallas import tpu as pltpu

F32 = jnp.float32
BF16 = jnp.bfloat16

D_MODEL = 1024
DEPTH = 4
GRID_W = 64
N_BRANCH = 4
BRANCH_W = D_MODEL // 4
HEAD_DIM = 64
N_HEADS = BRANCH_W // HEAD_DIM
NA_WIN_R = 8
NA_WIN_C = 16
SWA_KV_HEADS = N_HEADS // 2
SWA_WINDOW = 128
HY_WIDTH = BRANCH_W
HY_ORDER = 2
HY_FREQS = 16
HY_EMB = 1 + 2 * HY_FREQS
HY_HIDDEN = 64
DIFF_HD = 32
D_FF = 128 * ((8 * D_MODEL // 3 + 127) // 128)
ROPE_BASE = 10000.0
EPS = 1e-6
NEG = -1e30
N_MOD = 9
NA_W = 3 * BRANCH_W
SWA_QW = BRANCH_W
SWA_KVW = SWA_KV_HEADS * HEAD_DIM
SWA_W = SWA_QW + 2 * SWA_KVW
HY_IN_W = 3 * HY_WIDTH
DIFF_W = 3 * BRANCH_W
GATE_W = N_BRANCH * D_MODEL
MIX_W = NA_W + SWA_W + HY_IN_W + DIFF_W
IN_W = MIX_W + GATE_W

LOG2E = math.log2(math.e)
QK_SCALE = HEAD_DIM ** -0.5 * LOG2E
DIFF_QK_SCALE = DIFF_HD ** -0.5 * LOG2E

LANES = 128
MXU_DIM = 256
MIB = 1024 * 1024

FFN_ROWS = 1024
FFN_CAST_ROWS = 512
FFN_SPLIT = 2
MERGE_ROWS = 512
MERGE_SPLIT = 2
PROJ_ROWS = 512
PROJ_SPLIT = 2
FFN_CHUNK = 768
ATTN_Q = 256
ATTN_TILES = 4
NA_TILES = 4
HY_FREQ_TILE = 512
DIFF_Q = 512
NA_Q_ROWS = ATTN_Q // GRID_W
NA_SLAB_ROWS = NA_Q_ROWS + NA_WIN_R
SWA_SLAB = ATTN_Q + 2 * SWA_WINDOW
assert FFN_CHUNK % MXU_DIM == 0 and ATTN_Q % GRID_W == 0 and SWA_SLAB % SWA_WINDOW == 0


def _cparams(vmem_mib):
    return pltpu.CompilerParams(vmem_limit_bytes=vmem_mib * MIB)


def _resident(shape):
    nd = len(shape)
    return pl.BlockSpec(shape, lambda *_: (0,) * nd, pipeline_mode=pl.Buffered(1))


def _dot(a, b):
    return jnp.dot(a, b, preferred_element_type=F32)


def _dot_nt(a, b):
    return lax.dot_general(a, b, (((1,), (1,)), ((), ())), preferred_element_type=F32)


def _dot_exact(a, b):
    return jnp.dot(a, b, preferred_element_type=F32, precision=lax.Precision.HIGHEST)


def _sigmoid(x):
    return 1.0 / (1.0 + jnp.exp(-x))


def _norm_mod(x, nw, shift, scale):
    return x * lax.rsqrt(jnp.mean(x * x, axis=-1, keepdims=True) + EPS) * (nw * (1.0 + scale)) + shift


def _mod_spec(nb, tiles_per_b):
    if nb == 1:
        return pl.BlockSpec((1, N_MOD, D_MODEL), lambda i: (0, 0, 0))
    return pl.BlockSpec((1, N_MOD, D_MODEL), lambda i: (i // tiles_per_b, 0, 0))


def _ada_kernel(c_ref, w_ref, b_ref, o_ref):
    c = c_ref[...]
    s = (c * _sigmoid(c)).astype(BF16)
    o_ref[0] = _dot(s, w_ref[0].astype(BF16)) + b_ref[0]


def ada_modulation(cond, w_ada, b_ada):
    rows = cond.shape[0]
    width = N_MOD * D_MODEL
    tn = 9 * LANES
    return pl.pallas_call(
        _ada_kernel,
        grid=(DEPTH, width // tn),
        in_specs=[pl.BlockSpec((rows, D_MODEL), lambda l, j: (0, 0)),
                  pl.BlockSpec((1, D_MODEL, tn), lambda l, j: (l, 0, j)),
                  pl.BlockSpec((1, 1, tn), lambda l, j: (l, 0, j))],
        out_specs=pl.BlockSpec((1, rows, tn), lambda l, j: (l, 0, j)),
        out_shape=jax.ShapeDtypeStruct((DEPTH, rows, width), F32),
        compiler_params=_cparams(32),
        name="ada_modulation",
    )(cond, w_ada, b_ada.reshape(DEPTH, 1, width))


def _cast_specs(cast, steps):
    in_specs, out_specs, out_shape, args = [], [], [], []
    if cast is not None:
        layer, sources = cast
        for src in sources:
            _, rows, cols = src.shape
            in_specs.append(pl.BlockSpec((None, rows // steps, cols), lambda i: (layer, i, 0)))
            out_specs.append(pl.BlockSpec((rows // steps, cols), lambda i: (i, 0)))
            out_shape.append(jax.ShapeDtypeStruct((rows, cols), BF16))
            args.append(src)
    return in_specs, out_specs, out_shape, args


def _cast_blocks(cast_in, cast_out):
    for src, dst in zip(cast_in, cast_out):
        dst[...] = src[...].astype(BF16)


def _ffn_kernel(x_ref, mod_ref, nw_ref, w1_ref, w3_ref, w2_ref, *rest, mod_base, final, n_cast):
    rest = list(rest)
    fw_ref = rest.pop(0) if final else None
    cast_in, o_ref, cast_out = rest[:n_cast], rest[n_cast], rest[n_cast + 1:]
    _cast_blocks(cast_in, cast_out)
    shift = mod_ref[0, mod_base:mod_base + 1, :]
    scale = mod_ref[0, mod_base + 1:mod_base + 2, :]
    gate = mod_ref[0, mod_base + 2:mod_base + 3, :]
    rows = x_ref.shape[0] // FFN_SPLIT
    groups = [slice(g * rows, (g + 1) * rows) for g in range(FFN_SPLIT)]
    hs = [_norm_mod(x_ref[g, :], nw_ref[...], shift, scale).astype(BF16) for g in groups]
    for g, h in zip(groups, hs):
        acc = jnp.zeros((rows, D_MODEL), F32)
        for lo in range(0, D_FF, FFN_CHUNK):
            hi = min(lo + FFN_CHUNK, D_FF)
            a = _dot(h, w1_ref[:, lo:hi])
            b = _dot(h, w3_ref[:, lo:hi])
            u = (a * _sigmoid(a) * b).astype(BF16)
            acc = acc + _dot(u, w2_ref[lo:hi, :])
        y = x_ref[g, :] + 0.5 * gate * acc
        if final:
            y = y * lax.rsqrt(jnp.mean(y * y, axis=-1, keepdims=True) + EPS) * fw_ref[...]
        o_ref[g, :] = y


def ffn_block(x, mod, nw, w1, w3, w2, mod_base, rows_per_b, final_w=None, cast=None):
    T = x.shape[0]
    tm = FFN_ROWS if cast is None else FFN_CAST_ROWS
    steps = T // tm
    nb = mod.shape[0]
    final = final_w is not None
    in_specs = [pl.BlockSpec((tm, D_MODEL), lambda i: (i, 0)),
                _mod_spec(nb, rows_per_b // tm),
                _resident((1, D_MODEL)),
                _resident(w1.shape), _resident(w3.shape), _resident(w2.shape)]
    args = [x, mod, nw.reshape(1, D_MODEL), w1, w3, w2]
    if final:
        in_specs.append(_resident((1, D_MODEL)))
        args.append(final_w.reshape(1, D_MODEL))
    c_in, c_out, c_shape, c_args = _cast_specs(cast, steps)
    outs = pl.pallas_call(
        functools.partial(_ffn_kernel, mod_base=mod_base, final=final, n_cast=len(c_args)),
        grid=(steps,),
        in_specs=in_specs + c_in,
        out_specs=[pl.BlockSpec((tm, D_MODEL), lambda i: (i, 0))] + c_out,
        out_shape=[jax.ShapeDtypeStruct((T, D_MODEL), F32)] + c_shape,
        compiler_params=_cparams(56),
        name="ffn_block",
    )(*args, *c_args)
    return outs[0], list(outs[1:])


def _rope_chunk(x, cos, sin_a, sin_b, dist):
    return x * cos + pltpu.roll(x, LANES - dist, 1) * sin_a + pltpu.roll(x, dist, 1) * sin_b


def _proj_kernel(x_ref, mod_ref, nw_ref, w_ref, *rest, rope):
    if rope:
        (cs_ref, sa_ref, sb_ref, cd_ref, da_ref, db_ref,
         na_ref, swa_ref, hy_ref, dif_ref) = rest
    else:
        na_ref, swa_ref, hy_ref, dif_ref = rest
    rows = x_ref.shape[0] // PROJ_SPLIT
    groups = [slice(g * rows, (g + 1) * rows) for g in range(PROJ_SPLIT)]
    hs = [_norm_mod(x_ref[g, :], nw_ref[...], mod_ref[0, 3:4, :], mod_ref[0, 4:5, :]).astype(BF16) for g in groups]
    q_chunks = BRANCH_W // LANES
    for g, h in zip(groups, hs):
        o = 0
        s = _dot(h, w_ref[:, o:o + NA_W])
        o += NA_W
        na_ref[g, :BRANCH_W] = (s[:, :BRANCH_W] * QK_SCALE).astype(na_ref.dtype)
        na_ref[g, BRANCH_W:] = s[:, BRANCH_W:].astype(na_ref.dtype)
        s = _dot(h, w_ref[:, o:o + SWA_W])
        o += SWA_W
        n_rot = (SWA_QW + SWA_KVW) // LANES
        for c in range(SWA_W // LANES):
            sl = slice(c * LANES, (c + 1) * LANES)
            chunk = s[:, sl]
            if rope and c < n_rot:
                chunk = _rope_chunk(chunk, cs_ref[g, sl], sa_ref[g, sl], sb_ref[g, sl], HEAD_DIM // 4)
            if c < q_chunks:
                chunk = chunk * QK_SCALE
            swa_ref[g, sl] = chunk.astype(swa_ref.dtype)
        hy_ref[g, :] = _dot(h, w_ref[:, o:o + HY_IN_W])
        o += HY_IN_W
        s = _dot(h, w_ref[:, o:o + DIFF_W])
        o += DIFF_W
        n_rot = 2 * BRANCH_W // LANES
        for c in range(DIFF_W // LANES):
            sl = slice(c * LANES, (c + 1) * LANES)
            chunk = s[:, sl]
            if rope and c < n_rot:
                chunk = _rope_chunk(chunk, cd_ref[g, sl], da_ref[g, sl], db_ref[g, sl], DIFF_HD // 4)
            if c < q_chunks:
                chunk = chunk * DIFF_QK_SCALE
            dif_ref[g, sl] = chunk.astype(dif_ref.dtype)


def in_projection(x, mod, nw, w_in, rows_per_b, rope_tabs, qkv_dtype):
    T = x.shape[0]
    tm = PROJ_ROWS
    nb = mod.shape[0]
    rope = rope_tabs is not None
    if rope:
        pos_tiles = rows_per_b // tm
        grid = (pos_tiles, T // rows_per_b)
        row_tile = lambda p, b: (b * pos_tiles + p, 0)
        mod_spec = pl.BlockSpec((1, N_MOD, D_MODEL), lambda p, b: (b, 0, 0))
    else:
        grid = (T // tm,)
        row_tile = lambda i: (i, 0)
        mod_spec = _mod_spec(nb, rows_per_b // tm)
    in_specs = [pl.BlockSpec((tm, D_MODEL), row_tile),
                mod_spec,
                _resident((1, D_MODEL)),
                pl.BlockSpec((D_MODEL, MIX_W), lambda *_: (0, 0), pipeline_mode=pl.Buffered(1))]
    args = [x, mod, nw.reshape(1, D_MODEL), w_in]
    if rope:
        for t in rope_tabs:
            in_specs.append(pl.BlockSpec((tm, t.shape[1]), lambda p, b: (p, 0)))
            args.append(t)
    widths = (NA_W, SWA_W, HY_IN_W, DIFF_W)
    dtypes = (qkv_dtype, qkv_dtype, F32, qkv_dtype)
    return pl.pallas_call(
        functools.partial(_proj_kernel, rope=rope),
        grid=grid,
        in_specs=in_specs,
        out_specs=[pl.BlockSpec((tm, w), row_tile) for w in widths],
        out_shape=[jax.ShapeDtypeStruct((T, w), dt) for w, dt in zip(widths, dtypes)],
        compiler_params=_cparams(56),
        name="in_projection",
    )(*args)


def rope_tables(L, dh, width):
    nf = dh // 4
    t = jnp.arange(L)
    freqs = ROPE_BASE ** (-jnp.arange(nf, dtype=F32) / nf)
    ang_r = (t // GRID_W).astype(F32)[:, None] * freqs
    ang_c = (t % GRID_W).astype(F32)[:, None] * freqs
    cr, sr, cc, sc = jnp.cos(ang_r), jnp.sin(ang_r), jnp.cos(ang_c), jnp.sin(ang_c)
    z = jnp.zeros_like(sr)
    reps = width // dh
    cos = jnp.tile(jnp.concatenate([cr, cr, cc, cc], axis=-1), (1, reps))
    sin_a = jnp.tile(jnp.concatenate([-sr, z, -sc, z], axis=-1), (1, reps))
    sin_b = jnp.tile(jnp.concatenate([z, sr, z, sc], axis=-1), (1, reps))
    return cos, sin_a, sin_b


def _lane_mask(width, lo, n):
    lane = lax.broadcasted_iota(jnp.int32, (1, width), 1)
    return (lane >= lo) & (lane < lo + n)


def _softmax_parts(parts, extra_logit=None):
    m = parts[0].max(axis=-1, keepdims=True)
    for s in parts[1:]:
        m = jnp.maximum(m, s.max(axis=-1, keepdims=True))
    if extra_logit is not None:
        m = jnp.maximum(m, extra_logit)
    ps = [jnp.exp2(s - m) for s in parts]
    l = ps[0].sum(axis=-1, keepdims=True)
    for p in ps[1:]:
        l = l + p.sum(axis=-1, keepdims=True)
    if extra_logit is not None:
        l = l + jnp.exp2(extra_logit - m)
    return ps, l


def _na_kernel(q_ref, k_ref, v_ref, kc_ref, vc_ref, *rest, n_rows):
    bias_refs, o_ref = rest[:-1], rest[-1]
    slab = NA_SLAB_ROWS * GRID_W
    kc = kc_ref[0].astype(BF16)
    vc = vc_ref[0].astype(BF16)
    for sub, bias_ref in enumerate(bias_refs):
        tile = pl.program_id(1) * len(bias_refs) + sub
        rows = slice(sub * ATTN_Q, (sub + 1) * ATTN_Q)
        row0 = jnp.clip(NA_Q_ROWS * tile - NA_WIN_R // 2, 0, n_rows - NA_SLAB_ROWS)
        start = pl.multiple_of(row0 * GRID_W, GRID_W)
        ks = k_ref[0, pl.ds(start, slab), :].astype(BF16)
        vs = v_ref[0, pl.ds(start, slab), :].astype(BF16)
        q = q_ref[0, rows, :].astype(BF16)
        out = jnp.zeros(q.shape, F32)
        for h in range(N_HEADS):
            hm = _lane_mask(BRANCH_W, h * HEAD_DIM, HEAD_DIM)
            qh = jnp.where(hm, q, 0.0)
            s_win = _dot_nt(qh, ks) + bias_ref[0, 0, h]
            s_ctx = _dot_nt(qh, kc)
            (p_win, p_ctx), l = _softmax_parts([s_win, s_ctx])
            o = _dot(p_win.astype(BF16), vs) + _dot(p_ctx.astype(BF16), vc)
            out = out + jnp.where(hm, o / l, 0.0)
        o_ref[0, rows, :] = out


NA_DROWS = 2 * NA_WIN_R - 1
NA_DCOLS = 2 * NA_WIN_C - 1


def _na_tile_geometry(n_rows):
    n_tiles = n_rows // NA_Q_ROWS

    def geometry(tile):
        slab0 = min(max(NA_Q_ROWS * tile - NA_WIN_R // 2, 0), n_rows - NA_SLAB_ROWS)
        rows = []
        for j in range(NA_Q_ROWS):
            qr = NA_Q_ROWS * tile + j
            r0 = min(max(qr - NA_WIN_R // 2, 0), n_rows - NA_WIN_R)
            rows.append([slab0 + m - qr + NA_WIN_R - 1 if r0 <= slab0 + m < r0 + NA_WIN_R else None
                         for m in range(NA_SLAB_ROWS)])
        return rows

    kinds = [geometry(0), geometry(1), geometry(n_tiles - 1)]
    assert all(geometry(t) == kinds[1] for t in range(1, n_tiles - 1))
    return kinds


def _na_bias_kernel(rpb_ref, o_ref, band_ref, *, kinds):
    base = (pl.program_id(0) * N_HEADS + pl.program_id(1)) * (NA_DROWS * NA_DCOLS)
    qc = lax.broadcasted_iota(jnp.int32, (GRID_W, LANES), 0)
    lane = lax.broadcasted_iota(jnp.int32, (GRID_W, LANES), 1)
    kc = lane % GRID_W
    d_col = kc - qc + (NA_WIN_C - 1)
    c0 = jnp.clip(qc - NA_WIN_C // 2, 0, GRID_W - NA_WIN_C)
    col_ok = (kc >= c0) & (kc < c0 + NA_WIN_C)
    for d in range(NA_DROWS):
        t = jnp.full((GRID_W, LANES), NEG, F32)
        for e in range(NA_DCOLS):
            t = jnp.where(d_col == e, rpb_ref[base + d * NA_DCOLS + e] * LOG2E, t)
        band_ref[d] = jnp.where(col_ok, t, NEG)
    masked = jnp.full((GRID_W, LANES), NEG, F32)
    left = lane < GRID_W
    for kind, rows in enumerate(kinds):
        for j, drow in enumerate(rows):
            for m in range(0, NA_SLAB_ROWS, 2):
                a = masked if drow[m] is None else band_ref[drow[m]]
                b = masked if drow[m + 1] is None else band_ref[drow[m + 1]]
                o_ref[0, kind, 0, j * GRID_W:(j + 1) * GRID_W, m * GRID_W:(m + 2) * GRID_W] = jnp.where(left, a, b)


def na_bias_tables(rpb, n_rows):
    slab = NA_SLAB_ROWS * GRID_W
    return pl.pallas_call(
        functools.partial(_na_bias_kernel, kinds=_na_tile_geometry(n_rows)),
        grid=(DEPTH, N_HEADS),
        in_specs=[pl.BlockSpec(memory_space=pltpu.SMEM)],
        out_specs=pl.BlockSpec((1, 3, 1, ATTN_Q, slab), lambda l, h: (l, 0, h, 0, 0)),
        out_shape=jax.ShapeDtypeStruct((DEPTH, 3, N_HEADS, ATTN_Q, slab), F32),
        scratch_shapes=[pltpu.VMEM((NA_DROWS, GRID_W, LANES), F32)],
        compiler_params=_cparams(32),
        name="na_bias_tables",
    )(rpb.astype(F32).reshape(-1))


def neighbourhood_attention(na, kc, vc, bias, layer, B, L):
    n_rows = L // GRID_W
    n_tiles = L // ATTN_Q
    na3 = na.reshape(B, L, NA_W)
    P = kc.shape[1]
    slab = NA_SLAB_ROWS * GRID_W

    def kind(sub):
        def index(b, i):
            tile = i * NA_TILES + sub
            return (layer, jnp.where(tile == 0, 0, jnp.where(tile == n_tiles - 1, 2, 1)), 0, 0, 0)
        return index

    tq = NA_TILES * ATTN_Q
    return pl.pallas_call(
        functools.partial(_na_kernel, n_rows=n_rows),
        grid=(B, L // tq),
        in_specs=[pl.BlockSpec((1, tq, BRANCH_W), lambda b, i: (b, i, 0)),
                  pl.BlockSpec((1, L, BRANCH_W), lambda b, i: (b, 0, 1)),
                  pl.BlockSpec((1, L, BRANCH_W), lambda b, i: (b, 0, 2)),
                  pl.BlockSpec((1, P, BRANCH_W), lambda b, i: (b, 0, 0)),
                  pl.BlockSpec((1, P, BRANCH_W), lambda b, i: (b, 0, 0))]
                 + [pl.BlockSpec((1, 1, N_HEADS, ATTN_Q, slab), kind(sub)) for sub in range(NA_TILES)],
        out_specs=pl.BlockSpec((1, tq, BRANCH_W), lambda b, i: (b, i, 0)),
        out_shape=jax.ShapeDtypeStruct((B, L, BRANCH_W), F32),
        compiler_params=_cparams(56),
        name="neighbourhood_attention",
    )(na3, na3, na3, kc, vc, *([bias] * NA_TILES)).reshape(B * L, BRANCH_W)


def _attn_kernel(*refs, L, window, gqa, has_ctx, has_sink):
    refs = list(refs)
    q_ref, k_ref, v_ref = refs[:3]
    pos = 3
    if has_ctx:
        kc_ref, vc_ref = refs[pos:pos + 2]
        pos += 2
    if has_sink:
        sink_ref = refs[pos]
        pos += 1
    o_ref = refs[pos]
    n_seq = q_ref.shape[0]
    n_sub = q_ref.shape[1] // ATTN_Q
    for seq, sub in [(a, b) for a in range(n_seq) for b in range(n_sub)]:
        rows = slice(sub * ATTN_Q, (sub + 1) * ATTN_Q)
        if has_ctx:
            kc = kc_ref[seq].astype(BF16)
            vc = vc_ref[seq].astype(BF16)
        if window:
            slab = SWA_SLAB
            q0 = (pl.program_id(1) * n_sub + sub) * ATTN_Q
            start = pl.multiple_of(jnp.clip(q0 - SWA_WINDOW, 0, L - slab), SWA_WINDOW)
            q_pos = q0 + lax.broadcasted_iota(jnp.int32, (ATTN_Q, 1), 0)
            k_pos = start + lax.broadcasted_iota(jnp.int32, (1, slab), 1)
            ok = jnp.abs(k_pos - q_pos) <= SWA_WINDOW
            ks = k_ref[seq, pl.ds(start, slab), :].astype(BF16)
            vs = v_ref[seq, pl.ds(start, slab), :].astype(BF16)
        else:
            ks = k_ref[seq].astype(BF16)
            vs = v_ref[seq].astype(BF16)
        q = q_ref[seq, rows, :].astype(F32)
        kv_w = ks.shape[1]
        halves = [jnp.zeros((ATTN_Q, LANES), F32), jnp.zeros((ATTN_Q, LANES), F32)]
        out = jnp.zeros(q.shape, F32)
        for h in range(N_HEADS):
            if gqa:
                kvh, slot = h // 2, h % 2
                qh = q[:, kvh * LANES:(kvh + 1) * LANES]
                if slot != kvh:
                    qh = pltpu.roll(qh, HEAD_DIM, 1)
                hm = _lane_mask(kv_w, kvh * HEAD_DIM, HEAD_DIM)
            else:
                qh = q
                hm = _lane_mask(kv_w, h * HEAD_DIM, HEAD_DIM)
            qh = jnp.where(hm, qh, 0.0).astype(BF16)
            s = _dot_nt(qh, ks)
            if window:
                s = jnp.where(ok, s, NEG)
            parts = [s]
            if has_ctx:
                parts.append(_dot_nt(qh, kc))
            ps, l = _softmax_parts(parts, sink_ref[h] * LOG2E if has_sink else None)
            o = _dot(ps[0].astype(BF16), vs)
            if has_ctx:
                o = o + _dot(ps[1].astype(BF16), vc)
            o = jnp.where(hm, o / l, 0.0)
            if gqa:
                if slot != kvh:
                    o = pltpu.roll(o, HEAD_DIM, 1)
                halves[kvh] = halves[kvh] + o
            else:
                out = out + o
        if gqa:
            o_ref[seq, rows, 0:LANES] = halves[0]
            o_ref[seq, rows, LANES:2 * LANES] = halves[1]
        else:
            o_ref[seq, rows, :] = out


def _seqs_per_step(B, L, queries):
    ns = max(1, queries // L)
    return ns if B % ns == 0 else 1


def dense_attention(src, cols, B, L, *, window=False, gqa=False, ctx=None, sink=None):
    W = src.shape[1]
    src3 = src.reshape(B, L, W)
    kv_w = SWA_KVW if gqa else BRANCH_W
    qc, kcol, vcol = cols
    tq = min(ATTN_TILES * ATTN_Q, L)
    ns = _seqs_per_step(B, L, ATTN_TILES * ATTN_Q)
    in_specs = [pl.BlockSpec((ns, tq, BRANCH_W), lambda b, i: (b, i, qc)),
                pl.BlockSpec((ns, L, kv_w), lambda b, i: (b, 0, kcol)),
                pl.BlockSpec((ns, L, kv_w), lambda b, i: (b, 0, vcol))]
    args = [src3, src3, src3]
    if ctx is not None:
        P = ctx[0].shape[1]
        in_specs += [pl.BlockSpec((ns, P, kv_w), lambda b, i: (b, 0, 0))] * 2
        args += list(ctx)
    if sink is not None:
        in_specs.append(pl.BlockSpec(memory_space=pltpu.SMEM))
        args.append(sink)
    return pl.pallas_call(
        functools.partial(_attn_kernel, L=L, window=window, gqa=gqa,
                          has_ctx=ctx is not None, has_sink=sink is not None),
        grid=(B // ns, L // tq),
        in_specs=in_specs,
        out_specs=pl.BlockSpec((ns, tq, BRANCH_W), lambda b, i: (b, i, 0)),
        out_shape=jax.ShapeDtypeStruct((B, L, BRANCH_W), F32),
        compiler_params=_cparams(48),
        name="window_attention" if window else "dense_attention",
    )(*args).reshape(B * L, BRANCH_W)


def _diff_kernel(*refs, lam_init, has_ctx):
    refs = list(refs)
    lam_ref, gain_ref, ones_ref, q_ref, k_ref, v_ref = refs[:6]
    if has_ctx:
        kc_ref, vc_ref, o_ref = refs[6:]
    else:
        (o_ref,) = refs[6:]
    lv = lam_ref[...]
    lam = (jnp.exp(jnp.sum(lv[0:1] * lv[1:2], keepdims=True))
           - jnp.exp(jnp.sum(lv[2:3] * lv[3:4], keepdims=True)) + lam_init)
    for seq in range(q_ref.shape[0]):
        kl = k_ref[seq].astype(BF16)
        vl = v_ref[seq].astype(BF16)
        if has_ctx:
            kc = kc_ref[seq].astype(BF16)
            vc = vc_ref[seq].astype(BF16)
        q = q_ref[seq].astype(BF16)
        out = jnp.zeros(q.shape, F32)
        for h in range(N_HEADS):
            o = None
            for mp in range(2):
                mm = _lane_mask(BRANCH_W, h * HEAD_DIM + mp * DIFF_HD, DIFF_HD)
                qm = jnp.where(mm, q, 0.0)
                parts = [_dot_nt(qm, kl)]
                if has_ctx:
                    parts.append(_dot_nt(qm, kc))
                ps, l = _softmax_parts(parts)
                pv = _dot(ps[0].astype(BF16), vl)
                if has_ctx:
                    pv = pv + _dot(ps[1].astype(BF16), vc)
                o = pv / l if mp == 0 else o - pv * (lam / l)
            out = out + jnp.where(_lane_mask(BRANCH_W, h * HEAD_DIM, HEAD_DIM), o, 0.0)
        ms = _dot_exact(out * out, ones_ref[...]) * (1.0 / HEAD_DIM)
        o_ref[seq] = out * lax.rsqrt(ms + EPS) * gain_ref[...] * (1.0 - lam_init)


def diff_attention(dif, lam_vec, subln, layer, B, L, ctx=None):
    lam_init = 0.8 - 0.6 * math.exp(-0.3 * layer)
    dif3 = dif.reshape(B, L, DIFF_W)
    head = jnp.arange(BRANCH_W) // HEAD_DIM
    ones = (head[:, None] == head[None, :]).astype(F32)
    gain = jnp.tile(subln.astype(F32), N_HEADS).reshape(1, BRANCH_W)
    tq = min(DIFF_Q, L)
    ns = _seqs_per_step(B, L, DIFF_Q)
    in_specs = [_resident((4, DIFF_HD)), _resident((1, BRANCH_W)), _resident((BRANCH_W, BRANCH_W)),
                pl.BlockSpec((ns, tq, BRANCH_W), lambda b, i: (b, i, 0)),
                pl.BlockSpec((ns, L, BRANCH_W), lambda b, i: (b, 0, 1)),
                pl.BlockSpec((ns, L, BRANCH_W), lambda b, i: (b, 0, 2))]
    args = [lam_vec, gain, ones, dif3, dif3, dif3]
    if ctx is not None:
        P = ctx[0].shape[1]
        in_specs += [pl.BlockSpec((ns, P, BRANCH_W), lambda b, i: (b, 0, 0))] * 2
        args += list(ctx)
    return pl.pallas_call(
        functools.partial(_diff_kernel, lam_init=lam_init, has_ctx=ctx is not None),
        grid=(B // ns, L // tq),
        in_specs=in_specs,
        out_specs=pl.BlockSpec((ns, tq, BRANCH_W), lambda b, i: (b, i, 0)),
        out_shape=jax.ShapeDtypeStruct((B, L, BRANCH_W), F32),
        compiler_params=_cparams(56),
        name="diff_attention",
    )(*args).reshape(B * L, BRANCH_W)


def _filter_kernel(z_ref, w1_ref, b1_ref, w2_ref, b2_ref, w3_ref, fr_ref, ld_ref, o_ref):
    z = z_ref[...]
    tn = z[:, 0:1]
    fr = fr_ref[0]
    g = jnp.sin(fr * (_dot_exact(z, w1_ref[0]) + b1_ref[0]))
    g = jnp.sin(fr * (_dot_exact(g, w2_ref[0]) + b2_ref[0]))
    hf = _dot_exact(g, w3_ref[0]) * jnp.exp(-jnp.exp(ld_ref[0]) * tn)
    row = lax.broadcasted_iota(jnp.int32, (z.shape[0], 1), 0)
    for o in range(HY_ORDER):
        pos = hf[:, (2 * o) * HY_WIDTH:(2 * o + 1) * HY_WIDTH]
        neg = jnp.where(row == 0, 0.0, hf[:, (2 * o + 1) * HY_WIDTH:(2 * o + 2) * HY_WIDTH])
        norm = (jnp.sum(jnp.abs(pos), axis=0, keepdims=True)
                + jnp.sum(jnp.abs(neg), axis=0, keepdims=True) + EPS)
        inv = 1.0 / norm
        o_ref[0, o, 0] = (neg + pos) * inv
        o_ref[0, o, 1] = (neg - pos) * inv


def hyena_filters(L, p):
    tn = jnp.arange(L, dtype=F32) / L
    ang = 2.0 * math.pi * tn[:, None] * jnp.arange(1, HY_FREQS + 1, dtype=F32)[None, :]
    z = jnp.concatenate([tn[:, None], jnp.cos(ang), jnp.sin(ang)], axis=-1)
    z = jnp.pad(z, ((0, 0), (0, HY_HIDDEN - HY_EMB)))
    w1 = jnp.pad(p['hy_w1'], ((0, 0), (0, HY_HIDDEN - HY_EMB), (0, 0)))
    fw = HY_ORDER * 2 * HY_WIDTH
    per_layer = lambda *shape: pl.BlockSpec((1,) + shape, lambda l: (l,) + (0,) * len(shape))
    return pl.pallas_call(
        _filter_kernel,
        grid=(DEPTH,),
        in_specs=[pl.BlockSpec((L, HY_HIDDEN), lambda l: (0, 0)),
                  per_layer(HY_HIDDEN, HY_HIDDEN), per_layer(1, HY_HIDDEN),
                  per_layer(HY_HIDDEN, HY_HIDDEN), per_layer(1, HY_HIDDEN),
                  per_layer(HY_HIDDEN, fw), per_layer(1, HY_HIDDEN), per_layer(1, fw)],
        out_specs=per_layer(HY_ORDER, 2, L, HY_WIDTH),
        out_shape=jax.ShapeDtypeStruct((DEPTH, HY_ORDER, 2, L, HY_WIDTH), F32),
        compiler_params=_cparams(56),
        name="hyena_filters",
    )(z, w1, p['hy_b1'].reshape(DEPTH, 1, HY_HIDDEN), p['hy_w2'], p['hy_b2'].reshape(DEPTH, 1, HY_HIDDEN),
      p['hy_w3'], p['hy_sin_freq'].reshape(DEPTH, 1, HY_HIDDEN), p['hy_log_decay'].reshape(DEPTH, 1, fw))


def dft_table(L, tk, half_sample):
    assert L % LANES == 0 and L & (L - 1) == 0 and L // DFT_LO <= LANES
    return pl.pallas_call(
        functools.partial(_dft_table_kernel, half_sample=half_sample),
        grid=(L // tk,),
        out_specs=pl.BlockSpec((1, 2 * tk, L), lambda i: (i, 0, 0)),
        out_shape=jax.ShapeDtypeStruct((L // tk, 2 * tk, L), BF16),
        compiler_params=_cparams(48),
        name="dft_table",
    )()


DFT_LO = 64


def _dft_table_kernel(o_ref, *, half_sample):
    _, tk2, L = o_ref.shape
    tk = tk2 // 2
    k2 = 2 * (pl.program_id(0) * tk + lax.broadcasted_iota(jnp.int32, (tk, LANES), 0)) + 1
    lane = lax.broadcasted_iota(jnp.int32, (tk, LANES), 1)

    def cos_sin(m):
        ang = ((k2 * m) & (8 * L - 1)).astype(F32) * (math.pi / (4 * L))
        return jnp.cos(ang), jnp.sin(ang)

    c_hi, s_hi = cos_sin(2 * DFT_LO * lane)
    c_lo, s_lo = cos_sin(2 * lane + (1 if half_sample else 0))
    n = lax.broadcasted_iota(jnp.int32, (LANES, L), 1)
    r = lax.broadcasted_iota(jnp.int32, (LANES, L), 0)
    pick_hi = jnp.where(r == (n >> (DFT_LO.bit_length() - 1)), 1.0, 0.0).astype(BF16)
    pick_lo = jnp.where(r == (n & (DFT_LO - 1)), 1.0, 0.0).astype(BF16)

    def copy(t, pick):
        head = t.astype(BF16)
        return _dot(head, pick) + _dot((t - head.astype(F32)).astype(BF16), pick)

    ch, sh = copy(c_hi, pick_hi), copy(s_hi, pick_hi)
    cl, sl = copy(c_lo, pick_lo), copy(s_lo, pick_lo)
    o_ref[0, :tk, :] = (ch * cl - sh * sl).astype(BF16)
    o_ref[0, tk:, :] = (sh * cl + ch * sl).astype(BF16)


def _spectrum_kernel(t_ref, f_ref, o_ref):
    nkt, tk2, _ = t_ref.shape
    tk = tk2 // 2
    sums, diffs = f_ref[0, 0].astype(BF16), f_ref[0, 1].astype(BF16)
    for i in range(nkt):
        rows = slice(i * tk, (i + 1) * tk)
        o_ref[0, 0, rows, :] = _dot(t_ref[i, :tk, :], sums)
        o_ref[0, 1, rows, :] = _dot(t_ref[i, tk:, :], diffs)


def filter_spectra(filt, fwd):
    G, _, L, C = filt.shape
    return pl.pallas_call(
        _spectrum_kernel,
        grid=(G,),
        in_specs=[_resident(fwd.shape), pl.BlockSpec((1, 2, L, C), lambda g: (g, 0, 0, 0))],
        out_specs=pl.BlockSpec((1, 2, L, C), lambda g: (g, 0, 0, 0)),
        out_shape=jax.ShapeDtypeStruct((G, 2, L, C), F32),
        compiler_params=_cparams(48),
        name="filter_spectra",
    )(fwd, filt)


def _hyena_kernel(t_ref, h_ref, u_ref, w_ref, b_ref, skip_ref, o_ref, z_ref, y_ref):
    nkt, tk2, L = t_ref.shape
    tk = tk2 // 2
    C = HY_WIDTH
    row = lax.broadcasted_iota(jnp.int32, (L, 1), 0)

    def short_conv(col):
        sl = slice(col * C, (col + 1) * C)
        u = u_ref[0, :, sl]
        prev = jnp.where(row == 0, 0.0, pltpu.roll(u, 1, 0))
        nxt = jnp.where(row == L - 1, 0.0, pltpu.roll(u, L - 1, 0))
        return prev * w_ref[0:1, sl] + u * w_ref[1:2, sl] + nxt * w_ref[2:3, sl] + b_ref[:, sl]

    y_ref[...] = short_conv(0)
    for o in range(HY_ORDER):
        y = y_ref[...].astype(BF16)
        for i in range(nkt):
            rows = slice(i * tk, (i + 1) * tk)
            acc = _dot(t_ref[i], y)
            yc, ys = acc[:tk], acc[tk:]
            hr = h_ref[0, o, 0, rows, :]
            hi = h_ref[0, o, 1, rows, :]
            z_ref[0, rows, :] = (yc * hr + ys * hi).astype(BF16)
            z_ref[1, rows, :] = (yc * hi - ys * hr).astype(BF16)
        gate = short_conv(1 + o)
        zr, zi = z_ref[0], z_ref[1]
        for i in range(nkt):
            rows = slice(i * tk, (i + 1) * tk)
            conv = (_dot(t_ref[i, :tk, :], zr) - _dot(t_ref[i, tk:, :], zi)) * (1.0 / L)
            new = gate[rows] * (conv + skip_ref[o:o + 1, :] * y_ref[rows, :])
            if o == HY_ORDER - 1:
                o_ref[0, rows, :] = new
            else:
                y_ref[rows, :] = new


def hyena(hy, lp, layer, spec, table, B, L):
    C = HY_WIDTH
    return pl.pallas_call(
        _hyena_kernel,
        grid=(B,),
        in_specs=[_resident(table.shape),
                  pl.BlockSpec((1, HY_ORDER, 2, L, C), lambda b: (layer, 0, 0, 0, 0), pipeline_mode=pl.Buffered(1)),
                  pl.BlockSpec((1, L, HY_IN_W), lambda b: (b, 0, 0)),
                  _resident((3, HY_IN_W)), _resident((1, HY_IN_W)), _resident((HY_ORDER, C))],
        out_specs=pl.BlockSpec((1, L, C), lambda b: (b, 0, 0)),
        out_shape=jax.ShapeDtypeStruct((B, L, C), F32),
        scratch_shapes=[pltpu.VMEM((2, L, C), BF16), pltpu.VMEM((L, C), F32)],
        compiler_params=_cparams(56),
        name="hyena",
    )(table, spec, hy.reshape(B, L, HY_IN_W), lp['hy_short_w'], lp['hy_short_b'].reshape(1, HY_IN_W),
      lp['hy_skip']).reshape(B * L, C)


def _merge_kernel(a_ref, b_ref, c_ref, d_ref, x_ref, mod_ref, nw_ref, wg_ref, wb_ref, wo_ref, *rest, n_cast):
    cast_in, o_ref, cast_out = rest[:n_cast], rest[n_cast], rest[n_cast + 1:]
    _cast_blocks(cast_in, cast_out)
    rows = x_ref.shape[0] // MERGE_SPLIT
    groups = [slice(g * rows, (g + 1) * rows) for g in range(MERGE_SPLIT)]
    hs = [_norm_mod(x_ref[g, :], nw_ref[...], mod_ref[0, 3:4, :], mod_ref[0, 4:5, :]).astype(BF16) for g in groups]
    for g, h in zip(groups, hs):
        merged = None
        for i, br in enumerate((a_ref, b_ref, c_ref, d_ref)):
            gate = _sigmoid(_dot(h, wg_ref[:, MIX_W + i * D_MODEL:MIX_W + (i + 1) * D_MODEL]))
            t = gate * _dot(br[g, :].astype(BF16), wb_ref[i * BRANCH_W:(i + 1) * BRANCH_W, :])
            merged = t if merged is None else merged + t
        o_ref[g, :] = x_ref[g, :] + mod_ref[0, 5:6, :] * _dot(merged.astype(BF16), wo_ref[...])


def merge_block(branches, x, mod, nw, wg, wb, wo, rows_per_b, cast=None):
    T = x.shape[0]
    tm = MERGE_ROWS
    nb = mod.shape[0]
    row = lambda w: pl.BlockSpec((tm, w), lambda i: (i, 0))
    c_in, c_out, c_shape, c_args = _cast_specs(cast, T // tm)
    outs = pl.pallas_call(
        functools.partial(_merge_kernel, n_cast=len(c_args)),
        grid=(T // tm,),
        in_specs=[row(BRANCH_W)] * N_BRANCH + [row(D_MODEL), _mod_spec(nb, rows_per_b // tm),
                                               _resident((1, D_MODEL)),
                                               _resident(wg.shape), _resident(wb.shape), _resident(wo.shape)] + c_in,
        out_specs=[row(D_MODEL)] + c_out,
        out_shape=[jax.ShapeDtypeStruct((T, D_MODEL), F32)] + c_shape,
        compiler_params=_cparams(56),
        name="merge_block",
    )(*branches, x, mod, nw.reshape(1, D_MODEL), wg, wb, wo, *c_args)
    return outs[0], list(outs[1:])


def _heads_in(t):
    B, H, P, d = t.shape
    return t.transpose(0, 2, 1, 3).reshape(B, P, H * d)


_WEIGHT_GROUPS = ('ffn1', 'mix', 'ffn2')


def _run_pass(x, mod_all, p, layer_w, w_f32, final_norm, B, L, spec, tabs, caches):
    ctx_pass = caches is None
    collected = []
    if not ctx_pass:
        rope = rope_tables(L, HEAD_DIM, SWA_QW + SWA_KVW) + rope_tables(L, DIFF_HD, 2 * BRANCH_W)
        bias_all = na_bias_tables(p['na_rpb'], L // GRID_W)
    for l in range(DEPTH):
        lp = {k: v[l] for k, v in p.items()}
        w = layer_w[l]
        mod = mod_all[l]
        cast = {g: (l + 1, w_f32[g]) if ctx_pass and l + 1 < DEPTH else None for g in _WEIGHT_GROUPS}
        nxt = {}
        x, nxt['ffn1'] = ffn_block(x, mod, lp['norm_ffn1'], *w['ffn1'], 0, L, cast=cast['ffn1'])
        na, swa, hy, dif = in_projection(x, mod, lp['norm_mix'], w['mix'][0], L,
                                         None if ctx_pass else rope, F32 if ctx_pass else BF16)
        if ctx_pass:
            a_o = dense_attention(na, (0, 1, 2), B, L)
            b_o = dense_attention(swa, (0, 2, 3), B, L, gqa=True, sink=lp['swa_sink'])
            d_o = diff_attention(dif, lp['diff_lambda'], lp['diff_subln'], l, B, L)
            collected.append((na, swa, dif))
        else:
            ck_na, cv_na, ck_swa, cv_swa, ck_d, cv_d = (_heads_in(t[:, l]).astype(BF16) for t in caches)
            a_o = neighbourhood_attention(na, ck_na, cv_na, bias_all, l, B, L)
            b_o = dense_attention(swa, (0, 2, 3), B, L, window=True, gqa=True,
                                  ctx=(ck_swa, cv_swa), sink=lp['swa_sink'])
            d_o = diff_attention(dif, lp['diff_lambda'], lp['diff_subln'], l, B, L, ctx=(ck_d, cv_d))
        c_o = hyena(hy, lp, l, spec, tabs, B, L)
        x, nxt['mix'] = merge_block((a_o, b_o, c_o, d_o), x, mod, lp['norm_mix'], *w['mix'], L, cast=cast['mix'])
        x, nxt['ffn2'] = ffn_block(x, mod, lp['norm_ffn2'], *w['ffn2'], 6, L,
                                   final_w=final_norm if l == DEPTH - 1 else None, cast=cast['ffn2'])
        if ctx_pass and l + 1 < DEPTH:
            layer_w.append(nxt)
    return x, collected


_CACHE_SLOTS = ((0, N_HEADS, N_HEADS), (0, 2 * N_HEADS, N_HEADS),
                (1, N_HEADS, SWA_KV_HEADS), (1, N_HEADS + SWA_KV_HEADS, SWA_KV_HEADS),
                (2, N_HEADS, N_HEADS), (2, 2 * N_HEADS, N_HEADS))


def _cache_kernel(*refs):
    srcs, outs = refs[:3 * DEPTH], refs[3 * DEPTH:]
    for l in range(DEPTH):
        for o_ref, (src, slot0, n) in zip(outs, _CACHE_SLOTS):
            x_ref = srcs[3 * l + src]
            for h in range(n):
                o_ref[0, l, h] = x_ref[:, (slot0 + h) * HEAD_DIM:(slot0 + h + 1) * HEAD_DIM]


def _new_caches(collected, B, L):
    srcs = [t for layer in collected for t in layer]
    return pl.pallas_call(
        _cache_kernel,
        grid=(B,),
        in_specs=[pl.BlockSpec((L, t.shape[1]), lambda b: (b, 0)) for t in srcs],
        out_specs=[pl.BlockSpec((1, DEPTH, n, L, HEAD_DIM), lambda b: (b, 0, 0, 0, 0)) for _, _, n in _CACHE_SLOTS],
        out_shape=[jax.ShapeDtypeStruct((B, DEPTH, n, L, HEAD_DIM), F32) for _, _, n in _CACHE_SLOTS],
        compiler_params=_cparams(48),
        name="cache_outputs",
    )(*srcs)


def _hyena_setup(L, p, tk):
    filt = hyena_filters(L, p)
    spec = filter_spectra(filt.reshape(DEPTH * HY_ORDER, 2, L, HY_WIDTH), dft_table(L, tk, half_sample=False))
    return spec.reshape(DEPTH, HY_ORDER, 2, L, HY_WIDTH), dft_table(L, tk, half_sample=True)


def kernel(x_prompt, x_sample, cache_na_k, cache_na_v, cache_swa_k, cache_swa_v, cache_diff_k, cache_diff_v, c, c_ctx, w_ada, b_ada, norm_ffn1, norm_mix, norm_ffn2, final_norm, ffn1_w1, ffn1_w3, ffn1_w2, ffn2_w1, ffn2_w3, ffn2_w2, w_in, w_branch, w_out, na_rpb, swa_sink, hy_short_w, hy_short_b, hy_w1, hy_b1, hy_w2, hy_b2, hy_w3, hy_sin_freq, hy_log_decay, hy_skip, diff_lambda, diff_subln):
    B_ctx, L_ctx, _ = x_prompt.shape
    B_den, L_den, _ = x_sample.shape
    p = {
        'norm_ffn1': norm_ffn1, 'norm_mix': norm_mix, 'norm_ffn2': norm_ffn2,
        'na_rpb': na_rpb, 'swa_sink': swa_sink, 'hy_short_w': hy_short_w, 'hy_short_b': hy_short_b,
        'hy_w1': hy_w1, 'hy_b1': hy_b1, 'hy_w2': hy_w2, 'hy_b2': hy_b2, 'hy_w3': hy_w3,
        'hy_sin_freq': hy_sin_freq, 'hy_log_decay': hy_log_decay, 'hy_skip': hy_skip,
        'diff_lambda': diff_lambda, 'diff_subln': diff_subln,
    }
    w_f32 = {'ffn1': [ffn1_w1, ffn1_w3, ffn1_w2], 'ffn2': [ffn2_w1, ffn2_w3, ffn2_w2],
             'mix': [w_in, w_branch.reshape(DEPTH, N_BRANCH * BRANCH_W, D_MODEL), w_out]}
    layer_w = [{g: [w[0].astype(BF16) for w in ws] for g, ws in w_f32.items()}]

    cond = jnp.concatenate([c, c_ctx[None, :]], axis=0)
    rows = 8 * ((cond.shape[0] + 7) // 8)
    cond = jnp.pad(cond, ((0, rows - cond.shape[0]), (0, 0)))
    mod = ada_modulation(cond, w_ada, b_ada).reshape(DEPTH, rows, N_MOD, D_MODEL)
    mod_den = mod[:, :B_den]
    mod_ctx = mod[:, B_den:B_den + 1]

    spec_c, tabs_c = _hyena_setup(L_ctx, p, min(L_ctx, HY_FREQ_TILE))
    y_ctx, collected = _run_pass(x_prompt.reshape(B_ctx * L_ctx, D_MODEL), mod_ctx, p, layer_w, w_f32,
                                 final_norm, B_ctx, L_ctx, spec_c, tabs_c, None)
    new_caches = _new_caches(collected, B_ctx, L_ctx)

    spec_d, tabs_d = _hyena_setup(L_den, p, min(L_den, HY_FREQ_TILE))
    caches = (cache_na_k, cache_na_v, cache_swa_k, cache_swa_v, cache_diff_k, cache_diff_v)
    y_den, _ = _run_pass(x_sample.reshape(B_den * L_den, D_MODEL), mod_den, p, layer_w, w_f32,
                         final_norm, B_den, L_den, spec_d, tabs_d, caches)
    return (y_ctx.reshape(B_ctx, L_ctx, D_MODEL), y_den.reshape(B_den, L_den, D_MODEL), *new_caches)
```

```python
import functools
import math

import jax
import jax.numpy as jnp
from jax import lax
from jax.experimental import pallas as pl
from jax.experimental.pallas import tpu as pltpu

F32 = jnp.float32
BF16 = jnp.bfloat16

D_MODEL = 1024
DEPTH = 4
GRID_W = 64
N_BRANCH = 4
BRANCH_W = D_MODEL // 4
HEAD_DIM = 64
N_HEADS = BRANCH_W // HEAD_DIM
NA_WIN_R = 8
NA_WIN_C = 16
SWA_KV_HEADS = N_HEADS // 2
SWA_WINDOW = 128
HY_WIDTH = BRANCH_W
HY_ORDER = 2
HY_FREQS = 16
HY_EMB = 1 + 2 * HY_FREQS
HY_HIDDEN = 64
DIFF_HD = 32
D_FF = 128 * ((8 * D_MODEL // 3 + 127) // 128)
ROPE_BASE = 10000.0
EPS = 1e-6
NEG = -1e30
N_MOD = 9
NA_W = 3 * BRANCH_W
SWA_QW = BRANCH_W
SWA_KVW = SWA_KV_HEADS * HEAD_DIM
SWA_W = SWA_QW + 2 * SWA_KVW
HY_IN_W = 3 * HY_WIDTH
DIFF_W = 3 * BRANCH_W
GATE_W = N_BRANCH * D_MODEL
MIX_W = NA_W + SWA_W + HY_IN_W + DIFF_W
IN_W = MIX_W + GATE_W

LOG2E = math.log2(math.e)
QK_SCALE = HEAD_DIM ** -0.5 * LOG2E
DIFF_QK_SCALE = DIFF_HD ** -0.5 * LOG2E

LANES = 128
MXU_DIM = 256
MIB = 1024 * 1024

FFN_ROWS = 1024
FFN_CAST_ROWS = 512
FFN_SPLIT = 2
MERGE_ROWS = 512
MERGE_SPLIT = 2
PROJ_ROWS = 512
PROJ_SPLIT = 2
FFN_CHUNK = 768
ATTN_Q = 256
ATTN_TILES = 4
NA_TILES = 4
HY_FREQ_TILE = 512
DIFF_Q = 512
NA_Q_ROWS = ATTN_Q // GRID_W
NA_SLAB_ROWS = NA_Q_ROWS + NA_WIN_R
SWA_SLAB = ATTN_Q + 2 * SWA_WINDOW
assert FFN_CHUNK % MXU_DIM == 0 and ATTN_Q % GRID_W == 0 and SWA_SLAB % SWA_WINDOW == 0


def _cparams(vmem_mib):
    return pltpu.CompilerParams(vmem_limit_bytes=vmem_mib * MIB)


def _resident(shape):
    nd = len(shape)
    return pl.BlockSpec(shape, lambda *_: (0,) * nd, pipeline_mode=pl.Buffered(1))


def _dot(a, b):
    return jnp.dot(a, b, preferred_element_type=F32)


def _dot_nt(a, b):
    return lax.dot_general(a, b, (((1,), (1,)), ((), ())), preferred_element_type=F32)


def _dot_exact(a, b):
    return jnp.dot(a, b, preferred_element_type=F32, precision=lax.Precision.HIGHEST)


def _sigmoid(x):
    return 1.0 / (1.0 + jnp.exp(-x))


def _norm_mod(x, nw, shift, scale):
    return x * lax.rsqrt(jnp.mean(x * x, axis=-1, keepdims=True) + EPS) * (nw * (1.0 + scale)) + shift


def _mod_spec(nb, tiles_per_b):
    if nb == 1:
        return pl.BlockSpec((1, N_MOD, D_MODEL), lambda i: (0, 0, 0))
    return pl.BlockSpec((1, N_MOD, D_MODEL), lambda i: (i // tiles_per_b, 0, 0))


def _ada_kernel(c_ref, w_ref, b_ref, o_ref):
    c = c_ref[...]
    s = (c * _sigmoid(c)).astype(BF16)
    o_ref[0] = _dot(s, w_ref[0].astype(BF16)) + b_ref[0]


def ada_modulation(cond, w_ada, b_ada):
    rows = cond.shape[0]
    width = N_MOD * D_MODEL
    tn = 9 * LANES
    return pl.pallas_call(
        _ada_kernel,
        grid=(DEPTH, width // tn),
        in_specs=[pl.BlockSpec((rows, D_MODEL), lambda l, j: (0, 0)),
                  pl.BlockSpec((1, D_MODEL, tn), lambda l, j: (l, 0, j)),
                  pl.BlockSpec((1, 1, tn), lambda l, j: (l, 0, j))],
        out_specs=pl.BlockSpec((1, rows, tn), lambda l, j: (l, 0, j)),
        out_shape=jax.ShapeDtypeStruct((DEPTH, rows, width), F32),
        compiler_params=_cparams(32),
        name="ada_modulation",
    )(cond, w_ada, b_ada.reshape(DEPTH, 1, width))


def _cast_specs(cast, steps):
    in_specs, out_specs, out_shape, args = [], [], [], []
    if cast is not None:
        layer, sources = cast
        for src in sources:
            _, rows, cols = src.shape
            in_specs.append(pl.BlockSpec((None, rows // steps, cols), lambda i: (layer, i, 0)))
            out_specs.append(pl.BlockSpec((rows // steps, cols), lambda i: (i, 0)))
            out_shape.append(jax.ShapeDtypeStruct((rows, cols), BF16))
            args.append(src)
    return in_specs, out_specs, out_shape, args


def _cast_blocks(cast_in, cast_out):
    for src, dst in zip(cast_in, cast_out):
        dst[...] = src[...].astype(BF16)


def _ffn_kernel(x_ref, mod_ref, nw_ref, w1_ref, w3_ref, w2_ref, *rest, mod_base, final, n_cast):
    rest = list(rest)
    fw_ref = rest.pop(0) if final else None
    cast_in, o_ref, cast_out = rest[:n_cast], rest[n_cast], rest[n_cast + 1:]
    _cast_blocks(cast_in, cast_out)
    shift = mod_ref[0, mod_base:mod_base + 1, :]
    scale = mod_ref[0, mod_base + 1:mod_base + 2, :]
    gate = mod_ref[0, mod_base + 2:mod_base + 3, :]
    rows = x_ref.shape[0] // FFN_SPLIT
    groups = [slice(g * rows, (g + 1) * rows) for g in range(FFN_SPLIT)]
    hs = [_norm_mod(x_ref[g, :], nw_ref[...], shift, scale).astype(BF16) for g in groups]
    for g, h in zip(groups, hs):
        acc = jnp.zeros((rows, D_MODEL), F32)
        for lo in range(0, D_FF, FFN_CHUNK):
            hi = min(lo + FFN_CHUNK, D_FF)
            a = _dot(h, w1_ref[:, lo:hi])
            b = _dot(h, w3_ref[:, lo:hi])
            u = (a * _sigmoid(a) * b).astype(BF16)
            acc = acc + _dot(u, w2_ref[lo:hi, :])
        y = x_ref[g, :] + 0.5 * gate * acc
        if final:
            y = y * lax.rsqrt(jnp.mean(y * y, axis=-1, keepdims=True) + EPS) * fw_ref[...]
        o_ref[g, :] = y


def ffn_block(x, mod, nw, w1, w3, w2, mod_base, rows_per_b, final_w=None, cast=None):
    T = x.shape[0]
    tm = FFN_ROWS if cast is None else FFN_CAST_ROWS
    steps = T // tm
    nb = mod.shape[0]
    final = final_w is not None
    in_specs = [pl.BlockSpec((tm, D_MODEL), lambda i: (i, 0)),
                _mod_spec(nb, rows_per_b // tm),
                _resident((1, D_MODEL)),
                _resident(w1.shape), _resident(w3.shape), _resident(w2.shape)]
    args = [x, mod, nw.reshape(1, D_MODEL), w1, w3, w2]
    if final:
        in_specs.append(_resident((1, D_MODEL)))
        args.append(final_w.reshape(1, D_MODEL))
    c_in, c_out, c_shape, c_args = _cast_specs(cast, steps)
    outs = pl.pallas_call(
        functools.partial(_ffn_kernel, mod_base=mod_base, final=final, n_cast=len(c_args)),
        grid=(steps,),
        in_specs=in_specs + c_in,
        out_specs=[pl.BlockSpec((tm, D_MODEL), lambda i: (i, 0))] + c_out,
        out_shape=[jax.ShapeDtypeStruct((T, D_MODEL), F32)] + c_shape,
        compiler_params=_cparams(56),
        name="ffn_block",
    )(*args, *c_args)
    return outs[0], list(outs[1:])


def _rope_chunk(x, cos, sin_a, sin_b, dist):
    return x * cos + pltpu.roll(x, LANES - dist, 1) * sin_a + pltpu.roll(x, dist, 1) * sin_b


def _proj_kernel(x_ref, mod_ref, nw_ref, w_ref, *rest, rope):
    if rope:
        (cs_ref, sa_ref, sb_ref, cd_ref, da_ref, db_ref,
         na_ref, swa_ref, hy_ref, dif_ref) = rest
    else:
        na_ref, swa_ref, hy_ref, dif_ref = rest
    rows = x_ref.shape[0] // PROJ_SPLIT
    groups = [slice(g * rows, (g + 1) * rows) for g in range(PROJ_SPLIT)]
    hs = [_norm_mod(x_ref[g, :], nw_ref[...], mod_ref[0, 3:4, :], mod_ref[0, 4:5, :]).astype(BF16) for g in groups]
    q_chunks = BRANCH_W // LANES
    for g, h in zip(groups, hs):
        o = 0
        s = _dot(h, w_ref[:, o:o + NA_W])
        o += NA_W
        na_ref[g, :BRANCH_W] = (s[:, :BRANCH_W] * QK_SCALE).astype(na_ref.dtype)
        na_ref[g, BRANCH_W:] = s[:, BRANCH_W:].astype(na_ref.dtype)
        s = _dot(h, w_ref[:, o:o + SWA_W])
        o += SWA_W
        n_rot = (SWA_QW + SWA_KVW) // LANES
        for c in range(SWA_W // LANES):
            sl = slice(c * LANES, (c + 1) * LANES)
            chunk = s[:, sl]
            if rope and c < n_rot:
                chunk = _rope_chunk(chunk, cs_ref[g, sl], sa_ref[g, sl], sb_ref[g, sl], HEAD_DIM // 4)
            if c < q_chunks:
                chunk = chunk * QK_SCALE
            swa_ref[g, sl] = chunk.astype(swa_ref.dtype)
        hy_ref[g, :] = _dot(h, w_ref[:, o:o + HY_IN_W])
        o += HY_IN_W
        s = _dot(h, w_ref[:, o:o + DIFF_W])
        o += DIFF_W
        n_rot = 2 * BRANCH_W // LANES
        for c in range(DIFF_W // LANES):
            sl = slice(c * LANES, (c + 1) * LANES)
            chunk = s[:, sl]
            if rope and c < n_rot:
                chunk = _rope_chunk(chunk, cd_ref[g, sl], da_ref[g, sl], db_ref[g, sl], DIFF_HD // 4)
            if c < q_chunks:
                chunk = chunk * DIFF_QK_SCALE
            dif_ref[g, sl] = chunk.astype(dif_ref.dtype)


def in_projection(x, mod, nw, w_in, rows_per_b, rope_tabs, qkv_dtype):
    T = x.shape[0]
    tm = PROJ_ROWS
    nb = mod.shape[0]
    rope = rope_tabs is not None
    if rope:
        pos_tiles = rows_per_b // tm
        grid = (pos_tiles, T // rows_per_b)
        row_tile = lambda p, b: (b * pos_tiles + p, 0)
        mod_spec = pl.BlockSpec((1, N_MOD, D_MODEL), lambda p, b: (b, 0, 0))
    else:
        grid = (T // tm,)
        row_tile = lambda i: (i, 0)
        mod_spec = _mod_spec(nb, rows_per_b // tm)
    in_specs = [pl.BlockSpec((tm, D_MODEL), row_tile),
                mod_spec,
                _resident((1, D_MODEL)),
                pl.BlockSpec((D_MODEL, MIX_W), lambda *_: (0, 0), pipeline_mode=pl.Buffered(1))]
    args = [x, mod, nw.reshape(1, D_MODEL), w_in]
    if rope:
        for t in rope_tabs:
            in_specs.append(pl.BlockSpec((tm, t.shape[1]), lambda p, b: (p, 0)))
            args.append(t)
    widths = (NA_W, SWA_W, HY_IN_W, DIFF_W)
    dtypes = (qkv_dtype, qkv_dtype, F32, qkv_dtype)
    return pl.pallas_call(
        functools.partial(_proj_kernel, rope=rope),
        grid=grid,
        in_specs=in_specs,
        out_specs=[pl.BlockSpec((tm, w), row_tile) for w in widths],
        out_shape=[jax.ShapeDtypeStruct((T, w), dt) for w, dt in zip(widths, dtypes)],
        compiler_params=_cparams(56),
        name="in_projection",
    )(*args)


def rope_tables(L, dh, width):
    nf = dh // 4
    t = jnp.arange(L)
    freqs = ROPE_BASE ** (-jnp.arange(nf, dtype=F32) / nf)
    ang_r = (t // GRID_W).astype(F32)[:, None] * freqs
    ang_c = (t % GRID_W).astype(F32)[:, None] * freqs
    cr, sr, cc, sc = jnp.cos(ang_r), jnp.sin(ang_r), jnp.cos(ang_c), jnp.sin(ang_c)
    z = jnp.zeros_like(sr)
    reps = width // dh
    cos = jnp.tile(jnp.concatenate([cr, cr, cc, cc], axis=-1), (1, reps))
    sin_a = jnp.tile(jnp.concatenate([-sr, z, -sc, z], axis=-1), (1, reps))
    sin_b = jnp.tile(jnp.concatenate([z, sr, z, sc], axis=-1), (1, reps))
    return cos, sin_a, sin_b


def _lane_half(head):
    lo = head * HEAD_DIM // LANES * LANES
    return slice(lo, lo + LANES)


def _lane_mask(width, lo, n):
    lane = lax.broadcasted_iota(jnp.int32, (1, width), 1)
    return (lane >= lo) & (lane < lo + n)


def _softmax_parts(parts, extra_logit=None):
    m = parts[0].max(axis=-1, keepdims=True)
    for s in parts[1:]:
        m = jnp.maximum(m, s.max(axis=-1, keepdims=True))
    if extra_logit is not None:
        m = jnp.maximum(m, extra_logit)
    ps = [jnp.exp2(s - m) for s in parts]
    l = ps[0].sum(axis=-1, keepdims=True)
    for p in ps[1:]:
        l = l + p.sum(axis=-1, keepdims=True)
    if extra_logit is not None:
        l = l + jnp.exp2(extra_logit - m)
    return ps, l


def _na_kernel(q_ref, k_ref, v_ref, kc_ref, vc_ref, *rest, n_rows):
    bias_refs, o_ref = rest[:-1], rest[-1]
    slab = NA_SLAB_ROWS * GRID_W
    kc = kc_ref[0].astype(BF16)
    vc = vc_ref[0].astype(BF16)
    for sub, bias_ref in enumerate(bias_refs):
        tile = pl.program_id(1) * len(bias_refs) + sub
        rows = slice(sub * ATTN_Q, (sub + 1) * ATTN_Q)
        row0 = jnp.clip(NA_Q_ROWS * tile - NA_WIN_R // 2, 0, n_rows - NA_SLAB_ROWS)
        start = pl.multiple_of(row0 * GRID_W, GRID_W)
        ks = k_ref[0, pl.ds(start, slab), :].astype(BF16)
        vs = v_ref[0, pl.ds(start, slab), :].astype(BF16)
        q = q_ref[0, rows, :].astype(BF16)
        out = jnp.zeros(q.shape, F32)
        for h in range(N_HEADS):
            hm = _lane_mask(BRANCH_W, h * HEAD_DIM, HEAD_DIM)
            half = _lane_half(h)
            qh = jnp.where(_lane_mask(LANES, (h * HEAD_DIM) % LANES, HEAD_DIM), q[:, half], 0.0)
            s_win = _dot_nt(qh, ks[:, half]) + bias_ref[0, 0, h]
            s_ctx = _dot_nt(qh, kc[:, half])
            (p_win, p_ctx), l = _softmax_parts([s_win, s_ctx])
            o = _dot(p_win.astype(BF16), vs) + _dot(p_ctx.astype(BF16), vc)
            out = out + jnp.where(hm, o / l, 0.0)
        o_ref[0, rows, :] = out


NA_DROWS = 2 * NA_WIN_R - 1
NA_DCOLS = 2 * NA_WIN_C - 1


def _na_tile_geometry(n_rows):
    n_tiles = n_rows // NA_Q_ROWS

    def geometry(tile):
        slab0 = min(max(NA_Q_ROWS * tile - NA_WIN_R // 2, 0), n_rows - NA_SLAB_ROWS)
        rows = []
        for j in range(NA_Q_ROWS):
            qr = NA_Q_ROWS * tile + j
            r0 = min(max(qr - NA_WIN_R // 2, 0), n_rows - NA_WIN_R)
            rows.append([slab0 + m - qr + NA_WIN_R - 1 if r0 <= slab0 + m < r0 + NA_WIN_R else None
                         for m in range(NA_SLAB_ROWS)])
        return rows

    kinds = [geometry(0), geometry(1), geometry(n_tiles - 1)]
    assert all(geometry(t) == kinds[1] for t in range(1, n_tiles - 1))
    return kinds


def _na_bias_kernel(rpb_ref, o_ref, band_ref, *, kinds):
    base = (pl.program_id(0) * N_HEADS + pl.program_id(1)) * (NA_DROWS * NA_DCOLS)
    qc = lax.broadcasted_iota(jnp.int32, (GRID_W, LANES), 0)
    lane = lax.broadcasted_iota(jnp.int32, (GRID_W, LANES), 1)
    kc = lane % GRID_W
    d_col = kc - qc + (NA_WIN_C - 1)
    c0 = jnp.clip(qc - NA_WIN_C // 2, 0, GRID_W - NA_WIN_C)
    col_ok = (kc >= c0) & (kc < c0 + NA_WIN_C)
    for d in range(NA_DROWS):
        t = jnp.full((GRID_W, LANES), NEG, F32)
        for e in range(NA_DCOLS):
            t = jnp.where(d_col == e, rpb_ref[base + d * NA_DCOLS + e] * LOG2E, t)
        band_ref[d] = jnp.where(col_ok, t, NEG)
    masked = jnp.full((GRID_W, LANES), NEG, F32)
    left = lane < GRID_W
    for kind, rows in enumerate(kinds):
        for j, drow in enumerate(rows):
            for m in range(0, NA_SLAB_ROWS, 2):
                a = masked if drow[m] is None else band_ref[drow[m]]
                b = masked if drow[m + 1] is None else band_ref[drow[m + 1]]
                o_ref[0, kind, 0, j * GRID_W:(j + 1) * GRID_W, m * GRID_W:(m + 2) * GRID_W] = jnp.where(left, a, b)


def na_bias_tables(rpb, n_rows):
    slab = NA_SLAB_ROWS * GRID_W
    return pl.pallas_call(
        functools.partial(_na_bias_kernel, kinds=_na_tile_geometry(n_rows)),
        grid=(DEPTH, N_HEADS),
        in_specs=[pl.BlockSpec(memory_space=pltpu.SMEM)],
        out_specs=pl.BlockSpec((1, 3, 1, ATTN_Q, slab), lambda l, h: (l, 0, h, 0, 0)),
        out_shape=jax.ShapeDtypeStruct((DEPTH, 3, N_HEADS, ATTN_Q, slab), F32),
        scratch_shapes=[pltpu.VMEM((NA_DROWS, GRID_W, LANES), F32)],
        compiler_params=_cparams(32),
        name="na_bias_tables",
    )(rpb.astype(F32).reshape(-1))


def neighbourhood_attention(na, kc, vc, bias, layer, B, L):
    n_rows = L // GRID_W
    n_tiles = L // ATTN_Q
    na3 = na.reshape(B, L, NA_W)
    P = kc.shape[1]
    slab = NA_SLAB_ROWS * GRID_W

    def kind(sub):
        def index(b, i):
            tile = i * NA_TILES + sub
            return (layer, jnp.where(tile == 0, 0, jnp.where(tile == n_tiles - 1, 2, 1)), 0, 0, 0)
        return index

    tq = NA_TILES * ATTN_Q
    return pl.pallas_call(
        functools.partial(_na_kernel, n_rows=n_rows),
        grid=(B, L // tq),
        in_specs=[pl.BlockSpec((1, tq, BRANCH_W), lambda b, i: (b, i, 0)),
                  pl.BlockSpec((1, L, BRANCH_W), lambda b, i: (b, 0, 1)),
                  pl.BlockSpec((1, L, BRANCH_W), lambda b, i: (b, 0, 2)),
                  pl.BlockSpec((1, P, BRANCH_W), lambda b, i: (b, 0, 0)),
                  pl.BlockSpec((1, P, BRANCH_W), lambda b, i: (b, 0, 0))]
                 + [pl.BlockSpec((1, 1, N_HEADS, ATTN_Q, slab), kind(sub)) for sub in range(NA_TILES)],
        out_specs=pl.BlockSpec((1, tq, BRANCH_W), lambda b, i: (b, i, 0)),
        out_shape=jax.ShapeDtypeStruct((B, L, BRANCH_W), F32),
        compiler_params=_cparams(56),
        name="neighbourhood_attention",
    )(na3, na3, na3, kc, vc, *([bias] * NA_TILES)).reshape(B * L, BRANCH_W)


def _attn_kernel(*refs, L, window, gqa, has_ctx, has_sink):
    refs = list(refs)
    q_ref, k_ref, v_ref = refs[:3]
    pos = 3
    if has_ctx:
        kc_ref, vc_ref = refs[pos:pos + 2]
        pos += 2
    if has_sink:
        sink_ref = refs[pos]
        pos += 1
    o_ref = refs[pos]
    n_seq = q_ref.shape[0]
    n_sub = q_ref.shape[1] // ATTN_Q
    for seq, sub in [(a, b) for a in range(n_seq) for b in range(n_sub)]:
        rows = slice(sub * ATTN_Q, (sub + 1) * ATTN_Q)
        if has_ctx:
            kc = kc_ref[seq].astype(BF16)
            vc = vc_ref[seq].astype(BF16)
        if window:
            slab = SWA_SLAB
            q0 = (pl.program_id(1) * n_sub + sub) * ATTN_Q
            start = pl.multiple_of(jnp.clip(q0 - SWA_WINDOW, 0, L - slab), SWA_WINDOW)
            q_pos = q0 + lax.broadcasted_iota(jnp.int32, (ATTN_Q, 1), 0)
            k_pos = start + lax.broadcasted_iota(jnp.int32, (1, slab), 1)
            ok = jnp.abs(k_pos - q_pos) <= SWA_WINDOW
            ks = k_ref[seq, pl.ds(start, slab), :].astype(BF16)
            vs = v_ref[seq, pl.ds(start, slab), :].astype(BF16)
        else:
            ks = k_ref[seq].astype(BF16)
            vs = v_ref[seq].astype(BF16)
        q = q_ref[seq, rows, :].astype(F32)
        kv_w = ks.shape[1]
        halves = [jnp.zeros((ATTN_Q, LANES), F32), jnp.zeros((ATTN_Q, LANES), F32)]
        out = jnp.zeros(q.shape, F32)
        for h in range(N_HEADS):
            if gqa:
                kvh, slot = h // 2, h % 2
                qh = q[:, kvh * LANES:(kvh + 1) * LANES]
                if slot != kvh:
                    qh = pltpu.roll(qh, HEAD_DIM, 1)
                hm = _lane_mask(kv_w, kvh * HEAD_DIM, HEAD_DIM)
            else:
                qh = q
                hm = _lane_mask(kv_w, h * HEAD_DIM, HEAD_DIM)
            qh = jnp.where(hm, qh, 0.0).astype(BF16)
            s = _dot_nt(qh, ks)
            if window:
                s = jnp.where(ok, s, NEG)
            parts = [s]
            if has_ctx:
                parts.append(_dot_nt(qh, kc))
            ps, l = _softmax_parts(parts, sink_ref[h] * LOG2E if has_sink else None)
            o = _dot(ps[0].astype(BF16), vs)
            if has_ctx:
                o = o + _dot(ps[1].astype(BF16), vc)
            o = jnp.where(hm, o / l, 0.0)
            if gqa:
                if slot != kvh:
                    o = pltpu.roll(o, HEAD_DIM, 1)
                halves[kvh] = halves[kvh] + o
            else:
                out = out + o
        if gqa:
            o_ref[seq, rows, 0:LANES] = halves[0]
            o_ref[seq, rows, LANES:2 * LANES] = halves[1]
        else:
            o_ref[seq, rows, :] = out


def _seqs_per_step(B, L, queries):
    ns = max(1, queries // L)
    return ns if B % ns == 0 else 1


def dense_attention(src, cols, B, L, *, window=False, gqa=False, ctx=None, sink=None):
    W = src.shape[1]
    src3 = src.reshape(B, L, W)
    kv_w = SWA_KVW if gqa else BRANCH_W
    qc, kcol, vcol = cols
    tq = min(ATTN_TILES * ATTN_Q, L)
    ns = _seqs_per_step(B, L, ATTN_TILES * ATTN_Q)
    in_specs = [pl.BlockSpec((ns, tq, BRANCH_W), lambda b, i: (b, i, qc)),
                pl.BlockSpec((ns, L, kv_w), lambda b, i: (b, 0, kcol)),
                pl.BlockSpec((ns, L, kv_w), lambda b, i: (b, 0, vcol))]
    args = [src3, src3, src3]
    if ctx is not None:
        P = ctx[0].shape[1]
        in_specs += [pl.BlockSpec((ns, P, kv_w), lambda b, i: (b, 0, 0))] * 2
        args += list(ctx)
    if sink is not None:
        in_specs.append(pl.BlockSpec(memory_space=pltpu.SMEM))
        args.append(sink)
    return pl.pallas_call(
        functools.partial(_attn_kernel, L=L, window=window, gqa=gqa,
                          has_ctx=ctx is not None, has_sink=sink is not None),
        grid=(B // ns, L // tq),
        in_specs=in_specs,
        out_specs=pl.BlockSpec((ns, tq, BRANCH_W), lambda b, i: (b, i, 0)),
        out_shape=jax.ShapeDtypeStruct((B, L, BRANCH_W), F32),
        compiler_params=_cparams(48),
        name="window_attention" if window else "dense_attention",
    )(*args).reshape(B * L, BRANCH_W)


def _diff_kernel(*refs, lam_init, has_ctx):
    refs = list(refs)
    lam_ref, gain_ref, ones_ref, q_ref, k_ref, v_ref = refs[:6]
    if has_ctx:
        kc_ref, vc_ref, o_ref = refs[6:]
    else:
        (o_ref,) = refs[6:]
    lv = lam_ref[...]
    lam = (jnp.exp(jnp.sum(lv[0:1] * lv[1:2], keepdims=True))
           - jnp.exp(jnp.sum(lv[2:3] * lv[3:4], keepdims=True)) + lam_init)
    for seq in range(q_ref.shape[0]):
        kl = k_ref[seq].astype(BF16)
        vl = v_ref[seq].astype(BF16)
        if has_ctx:
            kc = kc_ref[seq].astype(BF16)
            vc = vc_ref[seq].astype(BF16)
        q = q_ref[seq].astype(BF16)
        out = jnp.zeros(q.shape, F32)
        for h in range(N_HEADS):
            o = None
            half = _lane_half(h)
            for mp in range(2):
                mm = _lane_mask(LANES, (h * HEAD_DIM + mp * DIFF_HD) % LANES, DIFF_HD)
                qm = jnp.where(mm, q[:, half], 0.0)
                parts = [_dot_nt(qm, kl[:, half])]
                if has_ctx:
                    parts.append(_dot_nt(qm, kc[:, half]))
                ps, l = _softmax_parts(parts)
                pv = _dot(ps[0].astype(BF16), vl)
                if has_ctx:
                    pv = pv + _dot(ps[1].astype(BF16), vc)
                o = pv / l if mp == 0 else o - pv * (lam / l)
            out = out + jnp.where(_lane_mask(BRANCH_W, h * HEAD_DIM, HEAD_DIM), o, 0.0)
        ms = _dot_exact(out * out, ones_ref[...]) * (1.0 / HEAD_DIM)
        o_ref[seq] = out * lax.rsqrt(ms + EPS) * gain_ref[...] * (1.0 - lam_init)


def diff_attention(dif, lam_vec, subln, layer, B, L, ctx=None):
    lam_init = 0.8 - 0.6 * math.exp(-0.3 * layer)
    dif3 = dif.reshape(B, L, DIFF_W)
    head = jnp.arange(BRANCH_W) // HEAD_DIM
    ones = (head[:, None] == head[None, :]).astype(F32)
    gain = jnp.tile(subln.astype(F32), N_HEADS).reshape(1, BRANCH_W)
    tq = min(DIFF_Q, L)
    ns = _seqs_per_step(B, L, DIFF_Q)
    in_specs = [_resident((4, DIFF_HD)), _resident((1, BRANCH_W)), _resident((BRANCH_W, BRANCH_W)),
                pl.BlockSpec((ns, tq, BRANCH_W), lambda b, i: (b, i, 0)),
                pl.BlockSpec((ns, L, BRANCH_W), lambda b, i: (b, 0, 1)),
                pl.BlockSpec((ns, L, BRANCH_W), lambda b, i: (b, 0, 2))]
    args = [lam_vec, gain, ones, dif3, dif3, dif3]
    if ctx is not None:
        P = ctx[0].shape[1]
        in_specs += [pl.BlockSpec((ns, P, BRANCH_W), lambda b, i: (b, 0, 0))] * 2
        args += list(ctx)
    return pl.pallas_call(
        functools.partial(_diff_kernel, lam_init=lam_init, has_ctx=ctx is not None),
        grid=(B // ns, L // tq),
        in_specs=in_specs,
        out_specs=pl.BlockSpec((ns, tq, BRANCH_W), lambda b, i: (b, i, 0)),
        out_shape=jax.ShapeDtypeStruct((B, L, BRANCH_W), F32),
        compiler_params=_cparams(56),
        name="diff_attention",
    )(*args).reshape(B * L, BRANCH_W)


def _filter_kernel(z_ref, w1_ref, b1_ref, w2_ref, b2_ref, w3_ref, fr_ref, ld_ref, o_ref):
    z = z_ref[...]
    tn = z[:, 0:1]
    fr = fr_ref[0]
    g = jnp.sin(fr * (_dot_exact(z, w1_ref[0]) + b1_ref[0]))
    g = jnp.sin(fr * (_dot_exact(g, w2_ref[0]) + b2_ref[0]))
    hf = _dot_exact(g, w3_ref[0]) * jnp.exp(-jnp.exp(ld_ref[0]) * tn)
    row = lax.broadcasted_iota(jnp.int32, (z.shape[0], 1), 0)
    for o in range(HY_ORDER):
        pos = hf[:, (2 * o) * HY_WIDTH:(2 * o + 1) * HY_WIDTH]
        neg = jnp.where(row == 0, 0.0, hf[:, (2 * o + 1) * HY_WIDTH:(2 * o + 2) * HY_WIDTH])
        norm = (jnp.sum(jnp.abs(pos), axis=0, keepdims=True)
                + jnp.sum(jnp.abs(neg), axis=0, keepdims=True) + EPS)
        inv = 1.0 / norm
        o_ref[0, o, 0] = (neg + pos) * inv
        o_ref[0, o, 1] = (neg - pos) * inv


def hyena_filters(L, p):
    tn = jnp.arange(L, dtype=F32) / L
    ang = 2.0 * math.pi * tn[:, None] * jnp.arange(1, HY_FREQS + 1, dtype=F32)[None, :]
    z = jnp.concatenate([tn[:, None], jnp.cos(ang), jnp.sin(ang)], axis=-1)
    z = jnp.pad(z, ((0, 0), (0, HY_HIDDEN - HY_EMB)))
    w1 = jnp.pad(p['hy_w1'], ((0, 0), (0, HY_HIDDEN - HY_EMB), (0, 0)))
    fw = HY_ORDER * 2 * HY_WIDTH
    per_layer = lambda *shape: pl.BlockSpec((1,) + shape, lambda l: (l,) + (0,) * len(shape))
    return pl.pallas_call(
        _filter_kernel,
        grid=(DEPTH,),
        in_specs=[pl.BlockSpec((L, HY_HIDDEN), lambda l: (0, 0)),
                  per_layer(HY_HIDDEN, HY_HIDDEN), per_layer(1, HY_HIDDEN),
                  per_layer(HY_HIDDEN, HY_HIDDEN), per_layer(1, HY_HIDDEN),
                  per_layer(HY_HIDDEN, fw), per_layer(1, HY_HIDDEN), per_layer(1, fw)],
        out_specs=per_layer(HY_ORDER, 2, L, HY_WIDTH),
        out_shape=jax.ShapeDtypeStruct((DEPTH, HY_ORDER, 2, L, HY_WIDTH), F32),
        compiler_params=_cparams(56),
        name="hyena_filters",
    )(z, w1, p['hy_b1'].reshape(DEPTH, 1, HY_HIDDEN), p['hy_w2'], p['hy_b2'].reshape(DEPTH, 1, HY_HIDDEN),
      p['hy_w3'], p['hy_sin_freq'].reshape(DEPTH, 1, HY_HIDDEN), p['hy_log_decay'].reshape(DEPTH, 1, fw))


def dft_table(L, tk, half_sample):
    assert L % LANES == 0 and L & (L - 1) == 0 and L // DFT_LO <= LANES
    return pl.pallas_call(
        functools.partial(_dft_table_kernel, half_sample=half_sample),
        grid=(L // tk,),
        out_specs=pl.BlockSpec((1, 2 * tk, L), lambda i: (i, 0, 0)),
        out_shape=jax.ShapeDtypeStruct((L // tk, 2 * tk, L), BF16),
        compiler_params=_cparams(48),
        name="dft_table",
    )()


DFT_LO = 64


def _dft_table_kernel(o_ref, *, half_sample):
    _, tk2, L = o_ref.shape
    tk = tk2 // 2
    k2 = 2 * (pl.program_id(0) * tk + lax.broadcasted_iota(jnp.int32, (tk, LANES), 0)) + 1
    lane = lax.broadcasted_iota(jnp.int32, (tk, LANES), 1)

    def cos_sin(m):
        ang = ((k2 * m) & (8 * L - 1)).astype(F32) * (math.pi / (4 * L))
        return jnp.cos(ang), jnp.sin(ang)

    c_hi, s_hi = cos_sin(2 * DFT_LO * lane)
    c_lo, s_lo = cos_sin(2 * lane + (1 if half_sample else 0))
    n = lax.broadcasted_iota(jnp.int32, (LANES, L), 1)
    r = lax.broadcasted_iota(jnp.int32, (LANES, L), 0)
    pick_hi = jnp.where(r == (n >> (DFT_LO.bit_length() - 1)), 1.0, 0.0).astype(BF16)
    pick_lo = jnp.where(r == (n & (DFT_LO - 1)), 1.0, 0.0).astype(BF16)

    def copy(t, pick):
        head = t.astype(BF16)
        return _dot(head, pick) + _dot((t - head.astype(F32)).astype(BF16), pick)

    ch, sh = copy(c_hi, pick_hi), copy(s_hi, pick_hi)
    cl, sl = copy(c_lo, pick_lo), copy(s_lo, pick_lo)
    o_ref[0, :tk, :] = (ch * cl - sh * sl).astype(BF16)
    o_ref[0, tk:, :] = (sh * cl + ch * sl).astype(BF16)


def _spectrum_kernel(t_ref, f_ref, o_ref):
    nkt, tk2, _ = t_ref.shape
    tk = tk2 // 2
    sums, diffs = f_ref[0, 0].astype(BF16), f_ref[0, 1].astype(BF16)
    for i in range(nkt):
        rows = slice(i * tk, (i + 1) * tk)
        o_ref[0, 0, rows, :] = _dot(t_ref[i, :tk, :], sums)
        o_ref[0, 1, rows, :] = _dot(t_ref[i, tk:, :], diffs)


def filter_spectra(filt, fwd):
    G, _, L, C = filt.shape
    return pl.pallas_call(
        _spectrum_kernel,
        grid=(G,),
        in_specs=[_resident(fwd.shape), pl.BlockSpec((1, 2, L, C), lambda g: (g, 0, 0, 0))],
        out_specs=pl.BlockSpec((1, 2, L, C), lambda g: (g, 0, 0, 0)),
        out_shape=jax.ShapeDtypeStruct((G, 2, L, C), F32),
        compiler_params=_cparams(48),
        name="filter_spectra",
    )(fwd, filt)


def _hyena_kernel(t_ref, h_ref, u_ref, w_ref, b_ref, skip_ref, o_ref, z_ref, y_ref):
    nkt, tk2, L = t_ref.shape
    tk = tk2 // 2
    C = HY_WIDTH
    row = lax.broadcasted_iota(jnp.int32, (L, 1), 0)

    def short_conv(col):
        sl = slice(col * C, (col + 1) * C)
        u = u_ref[0, :, sl]
        prev = jnp.where(row == 0, 0.0, pltpu.roll(u, 1, 0))
        nxt = jnp.where(row == L - 1, 0.0, pltpu.roll(u, L - 1, 0))
        return prev * w_ref[0:1, sl] + u * w_ref[1:2, sl] + nxt * w_ref[2:3, sl] + b_ref[:, sl]

    y_ref[...] = short_conv(0)
    for o in range(HY_ORDER):
        y = y_ref[...].astype(BF16)
        for i in range(nkt):
            rows = slice(i * tk, (i + 1) * tk)
            acc = _dot(t_ref[i], y)
            yc, ys = acc[:tk], acc[tk:]
            hr = h_ref[0, o, 0, rows, :]
            hi = h_ref[0, o, 1, rows, :]
            z_ref[0, rows, :] = (yc * hr + ys * hi).astype(BF16)
            z_ref[1, rows, :] = (yc * hi - ys * hr).astype(BF16)
        gate = short_conv(1 + o)
        zr, zi = z_ref[0], z_ref[1]
        for i in range(nkt):
            rows = slice(i * tk, (i + 1) * tk)
            conv = (_dot(t_ref[i, :tk, :], zr) - _dot(t_ref[i, tk:, :], zi)) * (1.0 / L)
            new = gate[rows] * (conv + skip_ref[o:o + 1, :] * y_ref[rows, :])
            if o == HY_ORDER - 1:
                o_ref[0, rows, :] = new
            else:
                y_ref[rows, :] = new


def hyena(hy, lp, layer, spec, table, B, L):
    C = HY_WIDTH
    return pl.pallas_call(
        _hyena_kernel,
        grid=(B,),
        in_specs=[_resident(table.shape),
                  pl.BlockSpec((1, HY_ORDER, 2, L, C), lambda b: (layer, 0, 0, 0, 0), pipeline_mode=pl.Buffered(1)),
                  pl.BlockSpec((1, L, HY_IN_W), lambda b: (b, 0, 0)),
                  _resident((3, HY_IN_W)), _resident((1, HY_IN_W)), _resident((HY_ORDER, C))],
        out_specs=pl.BlockSpec((1, L, C), lambda b: (b, 0, 0)),
        out_shape=jax.ShapeDtypeStruct((B, L, C), F32),
        scratch_shapes=[pltpu.VMEM((2, L, C), BF16), pltpu.VMEM((L, C), F32)],
        compiler_params=_cparams(56),
        name="hyena",
    )(table, spec, hy.reshape(B, L, HY_IN_W), lp['hy_short_w'], lp['hy_short_b'].reshape(1, HY_IN_W),
      lp['hy_skip']).reshape(B * L, C)


def _merge_kernel(a_ref, b_ref, c_ref, d_ref, x_ref, mod_ref, nw_ref, wg_ref, wb_ref, wo_ref, *rest, n_cast):
    cast_in, o_ref, cast_out = rest[:n_cast], rest[n_cast], rest[n_cast + 1:]
    _cast_blocks(cast_in, cast_out)
    rows = x_ref.shape[0] // MERGE_SPLIT
    groups = [slice(g * rows, (g + 1) * rows) for g in range(MERGE_SPLIT)]
    hs = [_norm_mod(x_ref[g, :], nw_ref[...], mod_ref[0, 3:4, :], mod_ref[0, 4:5, :]).astype(BF16) for g in groups]
    for g, h in zip(groups, hs):
        merged = None
        for i, br in enumerate((a_ref, b_ref, c_ref, d_ref)):
            gate = _sigmoid(_dot(h, wg_ref[:, MIX_W + i * D_MODEL:MIX_W + (i + 1) * D_MODEL]))
            t = gate * _dot(br[g, :].astype(BF16), wb_ref[i * BRANCH_W:(i + 1) * BRANCH_W, :])
            merged = t if merged is None else merged + t
        o_ref[g, :] = x_ref[g, :] + mod_ref[0, 5:6, :] * _dot(merged.astype(BF16), wo_ref[...])


def merge_block(branches, x, mod, nw, wg, wb, wo, rows_per_b, cast=None):
    T = x.shape[0]
    tm = MERGE_ROWS
    nb = mod.shape[0]
    row = lambda w: pl.BlockSpec((tm, w), lambda i: (i, 0))
    c_in, c_out, c_shape, c_args = _cast_specs(cast, T // tm)
    outs = pl.pallas_call(
        functools.partial(_merge_kernel, n_cast=len(c_args)),
        grid=(T // tm,),
        in_specs=[row(BRANCH_W)] * N_BRANCH + [row(D_MODEL), _mod_spec(nb, rows_per_b // tm),
                                               _resident((1, D_MODEL)),
                                               _resident(wg.shape), _resident(wb.shape), _resident(wo.shape)] + c_in,
        out_specs=[row(D_MODEL)] + c_out,
        out_shape=[jax.ShapeDtypeStruct((T, D_MODEL), F32)] + c_shape,
        compiler_params=_cparams(56),
        name="merge_block",
    )(*branches, x, mod, nw.reshape(1, D_MODEL), wg, wb, wo, *c_args)
    return outs[0], list(outs[1:])


def _heads_in(t):
    B, H, P, d = t.shape
    return t.transpose(0, 2, 1, 3).reshape(B, P, H * d)


_WEIGHT_GROUPS = ('ffn1', 'mix', 'ffn2')


def _run_pass(x, mod_all, p, layer_w, w_f32, final_norm, B, L, spec, tabs, caches):
    ctx_pass = caches is None
    collected = []
    if not ctx_pass:
        rope = rope_tables(L, HEAD_DIM, SWA_QW + SWA_KVW) + rope_tables(L, DIFF_HD, 2 * BRANCH_W)
        bias_all = na_bias_tables(p['na_rpb'], L // GRID_W)
    for l in range(DEPTH):
        lp = {k: v[l] for k, v in p.items()}
        w = layer_w[l]
        mod = mod_all[l]
        cast = {g: (l + 1, w_f32[g]) if ctx_pass and l + 1 < DEPTH else None for g in _WEIGHT_GROUPS}
        nxt = {}
        x, nxt['ffn1'] = ffn_block(x, mod, lp['norm_ffn1'], *w['ffn1'], 0, L, cast=cast['ffn1'])
        na, swa, hy, dif = in_projection(x, mod, lp['norm_mix'], w['mix'][0], L,
                                         None if ctx_pass else rope, F32 if ctx_pass else BF16)
        if ctx_pass:
            a_o = dense_attention(na, (0, 1, 2), B, L)
            b_o = dense_attention(swa, (0, 2, 3), B, L, gqa=True, sink=lp['swa_sink'])
            d_o = diff_attention(dif, lp['diff_lambda'], lp['diff_subln'], l, B, L)
            collected.append((na, swa, dif))
        else:
            ck_na, cv_na, ck_swa, cv_swa, ck_d, cv_d = (_heads_in(t[:, l]).astype(BF16) for t in caches)
            a_o = neighbourhood_attention(na, ck_na, cv_na, bias_all, l, B, L)
            b_o = dense_attention(swa, (0, 2, 3), B, L, window=True, gqa=True,
                                  ctx=(ck_swa, cv_swa), sink=lp['swa_sink'])
            d_o = diff_attention(dif, lp['diff_lambda'], lp['diff_subln'], l, B, L, ctx=(ck_d, cv_d))
        c_o = hyena(hy, lp, l, spec, tabs, B, L)
        x, nxt['mix'] = merge_block((a_o, b_o, c_o, d_o), x, mod, lp['norm_mix'], *w['mix'], L, cast=cast['mix'])
        x, nxt['ffn2'] = ffn_block(x, mod, lp['norm_ffn2'], *w['ffn2'], 6, L,
                                   final_w=final_norm if l == DEPTH - 1 else None, cast=cast['ffn2'])
        if ctx_pass and l + 1 < DEPTH:
            layer_w.append(nxt)
    return x, collected


_CACHE_SLOTS = ((0, N_HEADS, N_HEADS), (0, 2 * N_HEADS, N_HEADS),
                (1, N_HEADS, SWA_KV_HEADS), (1, N_HEADS + SWA_KV_HEADS, SWA_KV_HEADS),
                (2, N_HEADS, N_HEADS), (2, 2 * N_HEADS, N_HEADS))


def _cache_kernel(*refs):
    srcs, outs = refs[:3 * DEPTH], refs[3 * DEPTH:]
    for l in range(DEPTH):
        for o_ref, (src, slot0, n) in zip(outs, _CACHE_SLOTS):
            x_ref = srcs[3 * l + src]
            for h in range(n):
                o_ref[0, l, h] = x_ref[:, (slot0 + h) * HEAD_DIM:(slot0 + h + 1) * HEAD_DIM]


def _new_caches(collected, B, L):
    srcs = [t for layer in collected for t in layer]
    return pl.pallas_call(
        _cache_kernel,
        grid=(B,),
        in_specs=[pl.BlockSpec((L, t.shape[1]), lambda b: (b, 0)) for t in srcs],
        out_specs=[pl.BlockSpec((1, DEPTH, n, L, HEAD_DIM), lambda b: (b, 0, 0, 0, 0)) for _, _, n in _CACHE_SLOTS],
        out_shape=[jax.ShapeDtypeStruct((B, DEPTH, n, L, HEAD_DIM), F32) for _, _, n in _CACHE_SLOTS],
        compiler_params=_cparams(48),
        name="cache_outputs",
    )(*srcs)


def _hyena_setup(L, p, tk):
    filt = hyena_filters(L, p)
    spec = filter_spectra(filt.reshape(DEPTH * HY_ORDER, 2, L, HY_WIDTH), dft_table(L, tk, half_sample=False))
    return spec.reshape(DEPTH, HY_ORDER, 2, L, HY_WIDTH), dft_table(L, tk, half_sample=True)


def kernel(x_prompt, x_sample, cache_na_k, cache_na_v, cache_swa_k, cache_swa_v, cache_diff_k, cache_diff_v, c, c_ctx, w_ada, b_ada, norm_ffn1, norm_mix, norm_ffn2, final_norm, ffn1_w1, ffn1_w3, ffn1_w2, ffn2_w1, ffn2_w3, ffn2_w2, w_in, w_branch, w_out, na_rpb, swa_sink, hy_short_w, hy_short_b, hy_w1, hy_b1, hy_w2, hy_b2, hy_w3, hy_sin_freq, hy_log_decay, hy_skip, diff_lambda, diff_subln):
    B_ctx, L_ctx, _ = x_prompt.shape
    B_den, L_den, _ = x_sample.shape
    p = {
        'norm_ffn1': norm_ffn1, 'norm_mix': norm_mix, 'norm_ffn2': norm_ffn2,
        'na_rpb': na_rpb, 'swa_sink': swa_sink, 'hy_short_w': hy_short_w, 'hy_short_b': hy_short_b,
        'hy_w1': hy_w1, 'hy_b1': hy_b1, 'hy_w2': hy_w2, 'hy_b2': hy_b2, 'hy_w3': hy_w3,
        'hy_sin_freq': hy_sin_freq, 'hy_log_decay': hy_log_decay, 'hy_skip': hy_skip,
        'diff_lambda': diff_lambda, 'diff_subln': diff_subln,
    }
    w_f32 = {'ffn1': [ffn1_w1, ffn1_w3, ffn1_w2], 'ffn2': [ffn2_w1, ffn2_w3, ffn2_w2],
             'mix': [w_in, w_branch.reshape(DEPTH, N_BRANCH * BRANCH_W, D_MODEL), w_out]}
    layer_w = [{g: [w[0].astype(BF16) for w in ws] for g, ws in w_f32.items()}]

    cond = jnp.concatenate([c, c_ctx[None, :]], axis=0)
    rows = 8 * ((cond.shape[0] + 7) // 8)
    cond = jnp.pad(cond, ((0, rows - cond.shape[0]), (0, 0)))
    mod = ada_modulation(cond, w_ada, b_ada).reshape(DEPTH, rows, N_MOD, D_MODEL)
    mod_den = mod[:, :B_den]
    mod_ctx = mod[:, B_den:B_den + 1]

    spec_c, tabs_c = _hyena_setup(L_ctx, p, min(L_ctx, HY_FREQ_TILE))
    y_ctx, collected = _run_pass(x_prompt.reshape(B_ctx * L_ctx, D_MODEL), mod_ctx, p, layer_w, w_f32,
                                 final_norm, B_ctx, L_ctx, spec_c, tabs_c, None)
    new_caches = _new_caches(collected, B_ctx, L_ctx)

    spec_d, tabs_d = _hyena_setup(L_den, p, min(L_den, HY_FREQ_TILE))
    caches = (cache_na_k, cache_na_v, cache_swa_k, cache_swa_v, cache_diff_k, cache_diff_v)
    y_den, _ = _run_pass(x_sample.reshape(B_den * L_den, D_MODEL), mod_den, p, layer_w, w_f32,
                         final_norm, B_den, L_den, spec_d, tabs_d, caches)
    return (y_ctx.reshape(B_ctx, L_ctx, D_MODEL), y_den.reshape(B_den, L_den, D_MODEL), *new_caches)
```

```python
import functools
import math

import jax
import jax.numpy as jnp
from jax import lax
from jax.experimental import pallas as pl
from jax.experimental.pallas import tpu as pltpu

F32 = jnp.float32
BF16 = jnp.bfloat16

D_MODEL = 1024
DEPTH = 4
GRID_W = 64
N_BRANCH = 4
BRANCH_W = D_MODEL // 4
HEAD_DIM = 64
N_HEADS = BRANCH_W // HEAD_DIM
NA_WIN_R = 8
NA_WIN_C = 16
SWA_KV_HEADS = N_HEADS // 2
SWA_WINDOW = 128
HY_WIDTH = BRANCH_W
HY_ORDER = 2
HY_FREQS = 16
HY_EMB = 1 + 2 * HY_FREQS
HY_HIDDEN = 64
DIFF_HD = 32
D_FF = 128 * ((8 * D_MODEL // 3 + 127) // 128)
ROPE_BASE = 10000.0
EPS = 1e-6
NEG = -1e30
N_MOD = 9
NA_W = 3 * BRANCH_W
SWA_QW = BRANCH_W
SWA_KVW = SWA_KV_HEADS * HEAD_DIM
SWA_W = SWA_QW + 2 * SWA_KVW
HY_IN_W = 3 * HY_WIDTH
DIFF_W = 3 * BRANCH_W
GATE_W = N_BRANCH * D_MODEL
MIX_W = NA_W + SWA_W + HY_IN_W + DIFF_W
IN_W = MIX_W + GATE_W

LOG2E = math.log2(math.e)
QK_SCALE = HEAD_DIM ** -0.5 * LOG2E
DIFF_QK_SCALE = DIFF_HD ** -0.5 * LOG2E

LANES = 128
MXU_DIM = 256
MIB = 1024 * 1024

FFN_ROWS = 1024
FFN_CAST_ROWS = 512
FFN_SPLIT = 2
MERGE_ROWS = 512
MERGE_SPLIT = 2
PROJ_ROWS = 512
PROJ_SPLIT = 2
FFN_CHUNK = 768
ATTN_Q = 256
ATTN_TILES = 4
NA_TILES = 4
HY_FREQ_TILE = 512
DIFF_Q = 512
NA_Q_ROWS = ATTN_Q // GRID_W
NA_SLAB_ROWS = NA_Q_ROWS + NA_WIN_R
SWA_SLAB = ATTN_Q + 2 * SWA_WINDOW
assert FFN_CHUNK % MXU_DIM == 0 and ATTN_Q % GRID_W == 0 and SWA_SLAB % SWA_WINDOW == 0


def _cparams(vmem_mib):
    return pltpu.CompilerParams(vmem_limit_bytes=vmem_mib * MIB)


def _resident(shape):
    nd = len(shape)
    return pl.BlockSpec(shape, lambda *_: (0,) * nd, pipeline_mode=pl.Buffered(1))


def _dot(a, b):
    return jnp.dot(a, b, preferred_element_type=F32)


def _dot_nt(a, b):
    return lax.dot_general(a, b, (((1,), (1,)), ((), ())), preferred_element_type=F32)


def _dot_exact(a, b):
    return jnp.dot(a, b, preferred_element_type=F32, precision=lax.Precision.HIGHEST)


def _sigmoid(x):
    return 1.0 / (1.0 + jnp.exp(-x))


def _norm_mod(x, nw, shift, scale):
    return x * lax.rsqrt(jnp.mean(x * x, axis=-1, keepdims=True) + EPS) * (nw * (1.0 + scale)) + shift


def _mod_spec(nb, tiles_per_b):
    if nb == 1:
        return pl.BlockSpec((1, N_MOD, D_MODEL), lambda i: (0, 0, 0))
    return pl.BlockSpec((1, N_MOD, D_MODEL), lambda i: (i // tiles_per_b, 0, 0))


def _ada_kernel(c_ref, w_ref, b_ref, o_ref):
    c = c_ref[...]
    s = (c * _sigmoid(c)).astype(BF16)
    o_ref[0] = _dot(s, w_ref[0].astype(BF16)) + b_ref[0]


def ada_modulation(cond, w_ada, b_ada):
    rows = cond.shape[0]
    width = N_MOD * D_MODEL
    tn = 9 * LANES
    return pl.pallas_call(
        _ada_kernel,
        grid=(DEPTH, width // tn),
        in_specs=[pl.BlockSpec((rows, D_MODEL), lambda l, j: (0, 0)),
                  pl.BlockSpec((1, D_MODEL, tn), lambda l, j: (l, 0, j)),
                  pl.BlockSpec((1, 1, tn), lambda l, j: (l, 0, j))],
        out_specs=pl.BlockSpec((1, rows, tn), lambda l, j: (l, 0, j)),
        out_shape=jax.ShapeDtypeStruct((DEPTH, rows, width), F32),
        compiler_params=_cparams(32),
        name="ada_modulation",
    )(cond, w_ada, b_ada.reshape(DEPTH, 1, width))


def _cast_specs(cast, steps):
    in_specs, out_specs, out_shape, args = [], [], [], []
    if cast is not None:
        layer, sources = cast
        for src in sources:
            _, rows, cols = src.shape
            in_specs.append(pl.BlockSpec((None, rows // steps, cols), lambda i: (layer, i, 0)))
            out_specs.append(pl.BlockSpec((rows // steps, cols), lambda i: (i, 0)))
            out_shape.append(jax.ShapeDtypeStruct((rows, cols), BF16))
            args.append(src)
    return in_specs, out_specs, out_shape, args


def _cast_blocks(cast_in, cast_out):
    for src, dst in zip(cast_in, cast_out):
        dst[...] = src[...].astype(BF16)


def _ffn_kernel(x_ref, mod_ref, nw_ref, w1_ref, w3_ref, w2_ref, *rest, mod_base, final, n_cast):
    rest = list(rest)
    fw_ref = rest.pop(0) if final else None
    cast_in, o_ref, cast_out = rest[:n_cast], rest[n_cast], rest[n_cast + 1:]
    _cast_blocks(cast_in, cast_out)
    shift = mod_ref[0, mod_base:mod_base + 1, :]
    scale = mod_ref[0, mod_base + 1:mod_base + 2, :]
    gate = mod_ref[0, mod_base + 2:mod_base + 3, :]
    rows = x_ref.shape[0] // FFN_SPLIT
    groups = [slice(g * rows, (g + 1) * rows) for g in range(FFN_SPLIT)]
    hs = [_norm_mod(x_ref[g, :], nw_ref[...], shift, scale).astype(BF16) for g in groups]
    for g, h in zip(groups, hs):
        acc = jnp.zeros((rows, D_MODEL), F32)
        for lo in range(0, D_FF, FFN_CHUNK):
            hi = min(lo + FFN_CHUNK, D_FF)
            a = _dot(h, w1_ref[:, lo:hi])
            b = _dot(h, w3_ref[:, lo:hi])
            u = (a * _sigmoid(a) * b).astype(BF16)
            acc = acc + _dot(u, w2_ref[lo:hi, :])
        y = x_ref[g, :] + 0.5 * gate * acc
        if final:
            y = y * lax.rsqrt(jnp.mean(y * y, axis=-1, keepdims=True) + EPS) * fw_ref[...]
        o_ref[g, :] = y


def ffn_block(x, mod, nw, w1, w3, w2, mod_base, rows_per_b, final_w=None, cast=None):
    T = x.shape[0]
    tm = FFN_ROWS if cast is None else FFN_CAST_ROWS
    steps = T // tm
    nb = mod.shape[0]
    final = final_w is not None
    in_specs = [pl.BlockSpec((tm, D_MODEL), lambda i: (i, 0)),
                _mod_spec(nb, rows_per_b // tm),
                _resident((1, D_MODEL)),
                _resident(w1.shape), _resident(w3.shape), _resident(w2.shape)]
    args = [x, mod, nw.reshape(1, D_MODEL), w1, w3, w2]
    if final:
        in_specs.append(_resident((1, D_MODEL)))
        args.append(final_w.reshape(1, D_MODEL))
    c_in, c_out, c_shape, c_args = _cast_specs(cast, steps)
    outs = pl.pallas_call(
        functools.partial(_ffn_kernel, mod_base=mod_base, final=final, n_cast=len(c_args)),
        grid=(steps,),
        in_specs=in_specs + c_in,
        out_specs=[pl.BlockSpec((tm, D_MODEL), lambda i: (i, 0))] + c_out,
        out_shape=[jax.ShapeDtypeStruct((T, D_MODEL), F32)] + c_shape,
        compiler_params=_cparams(56),
        name="ffn_block",
    )(*args, *c_args)
    return outs[0], list(outs[1:])


def _rope_chunk(x, cos, sin_a, sin_b, dist):
    return x * cos + pltpu.roll(x, LANES - dist, 1) * sin_a + pltpu.roll(x, dist, 1) * sin_b


def _proj_kernel(x_ref, mod_ref, nw_ref, w_ref, *rest, rope):
    if rope:
        (cs_ref, sa_ref, sb_ref, cd_ref, da_ref, db_ref,
         na_ref, swa_ref, hy_ref, dif_ref) = rest
    else:
        na_ref, swa_ref, hy_ref, dif_ref = rest
    rows = x_ref.shape[0] // PROJ_SPLIT
    groups = [slice(g * rows, (g + 1) * rows) for g in range(PROJ_SPLIT)]
    hs = [_norm_mod(x_ref[g, :], nw_ref[...], mod_ref[0, 3:4, :], mod_ref[0, 4:5, :]).astype(BF16) for g in groups]
    q_chunks = BRANCH_W // LANES
    for g, h in zip(groups, hs):
        o = 0
        s = _dot(h, w_ref[:, o:o + NA_W])
        o += NA_W
        na_ref[g, :BRANCH_W] = (s[:, :BRANCH_W] * QK_SCALE).astype(na_ref.dtype)
        na_ref[g, BRANCH_W:] = s[:, BRANCH_W:].astype(na_ref.dtype)
        s = _dot(h, w_ref[:, o:o + SWA_W])
        o += SWA_W
        n_rot = (SWA_QW + SWA_KVW) // LANES
        for c in range(SWA_W // LANES):
            sl = slice(c * LANES, (c + 1) * LANES)
            chunk = s[:, sl]
            if rope and c < n_rot:
                chunk = _rope_chunk(chunk, cs_ref[g, sl], sa_ref[g, sl], sb_ref[g, sl], HEAD_DIM // 4)
            if c < q_chunks:
                chunk = chunk * QK_SCALE
            swa_ref[g, sl] = chunk.astype(swa_ref.dtype)
        hy_ref[g, :] = _dot(h, w_ref[:, o:o + HY_IN_W])
        o += HY_IN_W
        s = _dot(h, w_ref[:, o:o + DIFF_W])
        o += DIFF_W
        n_rot = 2 * BRANCH_W // LANES
        for c in range(DIFF_W // LANES):
            sl = slice(c * LANES, (c + 1) * LANES)
            chunk = s[:, sl]
            if rope and c < n_rot:
                chunk = _rope_chunk(chunk, cd_ref[g, sl], da_ref[g, sl], db_ref[g, sl], DIFF_HD // 4)
            if c < q_chunks:
                chunk = chunk * DIFF_QK_SCALE
            dif_ref[g, sl] = chunk.astype(dif_ref.dtype)


def in_projection(x, mod, nw, w_in, rows_per_b, rope_tabs, qkv_dtype):
    T = x.shape[0]
    tm = PROJ_ROWS
    nb = mod.shape[0]
    rope = rope_tabs is not None
    if rope:
        pos_tiles = rows_per_b // tm
        grid = (pos_tiles, T // rows_per_b)
        row_tile = lambda p, b: (b * pos_tiles + p, 0)
        mod_spec = pl.BlockSpec((1, N_MOD, D_MODEL), lambda p, b: (b, 0, 0))
    else:
        grid = (T // tm,)
        row_tile = lambda i: (i, 0)
        mod_spec = _mod_spec(nb, rows_per_b // tm)
    in_specs = [pl.BlockSpec((tm, D_MODEL), row_tile),
                mod_spec,
                _resident((1, D_MODEL)),
                pl.BlockSpec((D_MODEL, MIX_W), lambda *_: (0, 0), pipeline_mode=pl.Buffered(1))]
    args = [x, mod, nw.reshape(1, D_MODEL), w_in]
    if rope:
        for t in rope_tabs:
            in_specs.append(pl.BlockSpec((tm, t.shape[1]), lambda p, b: (p, 0)))
            args.append(t)
    widths = (NA_W, SWA_W, HY_IN_W, DIFF_W)
    dtypes = (qkv_dtype, qkv_dtype, F32, qkv_dtype)
    return pl.pallas_call(
        functools.partial(_proj_kernel, rope=rope),
        grid=grid,
        in_specs=in_specs,
        out_specs=[pl.BlockSpec((tm, w), row_tile) for w in widths],
        out_shape=[jax.ShapeDtypeStruct((T, w), dt) for w, dt in zip(widths, dtypes)],
        compiler_params=_cparams(56),
        name="in_projection",
    )(*args)


def rope_tables(L, dh, width):
    nf = dh // 4
    t = jnp.arange(L)
    freqs = ROPE_BASE ** (-jnp.arange(nf, dtype=F32) / nf)
    ang_r = (t // GRID_W).astype(F32)[:, None] * freqs
    ang_c = (t % GRID_W).astype(F32)[:, None] * freqs
    cr, sr, cc, sc = jnp.cos(ang_r), jnp.sin(ang_r), jnp.cos(ang_c), jnp.sin(ang_c)
    z = jnp.zeros_like(sr)
    reps = width // dh
    cos = jnp.tile(jnp.concatenate([cr, cr, cc, cc], axis=-1), (1, reps))
    sin_a = jnp.tile(jnp.concatenate([-sr, z, -sc, z], axis=-1), (1, reps))
    sin_b = jnp.tile(jnp.concatenate([z, sr, z, sc], axis=-1), (1, reps))
    return cos, sin_a, sin_b


def _lane_half(head):
    lo = head * HEAD_DIM // LANES * LANES
    return slice(lo, lo + LANES)


def _lane_mask(width, lo, n):
    lane = lax.broadcasted_iota(jnp.int32, (1, width), 1)
    return (lane >= lo) & (lane < lo + n)


def _softmax_parts(parts, extra_logit=None):
    m = parts[0].max(axis=-1, keepdims=True)
    for s in parts[1:]:
        m = jnp.maximum(m, s.max(axis=-1, keepdims=True))
    if extra_logit is not None:
        m = jnp.maximum(m, extra_logit)
    ps = [jnp.exp2(s - m) for s in parts]
    l = ps[0].sum(axis=-1, keepdims=True)
    for p in ps[1:]:
        l = l + p.sum(axis=-1, keepdims=True)
    if extra_logit is not None:
        l = l + jnp.exp2(extra_logit - m)
    return ps, l


def _na_kernel(q_ref, k_ref, v_ref, kc_ref, vc_ref, *rest, n_rows):
    bias_refs, o_ref = rest[:-1], rest[-1]
    slab = NA_SLAB_ROWS * GRID_W
    kc = kc_ref[0].astype(BF16)
    vc = vc_ref[0].astype(BF16)
    for sub, bias_ref in enumerate(bias_refs):
        tile = pl.program_id(1) * len(bias_refs) + sub
        rows = slice(sub * ATTN_Q, (sub + 1) * ATTN_Q)
        row0 = jnp.clip(NA_Q_ROWS * tile - NA_WIN_R // 2, 0, n_rows - NA_SLAB_ROWS)
        start = pl.multiple_of(row0 * GRID_W, GRID_W)
        ks = k_ref[0, pl.ds(start, slab), :].astype(BF16)
        vs = v_ref[0, pl.ds(start, slab), :].astype(BF16)
        q = q_ref[0, rows, :].astype(BF16)
        out = [jnp.zeros((ATTN_Q, LANES), F32) for _ in range(BRANCH_W // LANES)]
        for h in range(N_HEADS):
            half = _lane_half(h)
            hm = _lane_mask(LANES, (h * HEAD_DIM) % LANES, HEAD_DIM)
            qh = jnp.where(hm, q[:, half], 0.0)
            s_win = _dot_nt(qh, ks[:, half]) + bias_ref[0, 0, h]
            s_ctx = _dot_nt(qh, kc[:, half])
            (p_win, p_ctx), l = _softmax_parts([s_win, s_ctx])
            o = _dot(p_win.astype(BF16), vs[:, half]) + _dot(p_ctx.astype(BF16), vc[:, half])
            out[half.start // LANES] = out[half.start // LANES] + jnp.where(hm, o / l, 0.0)
        for i, part in enumerate(out):
            o_ref[0, rows, i * LANES:(i + 1) * LANES] = part


NA_DROWS = 2 * NA_WIN_R - 1
NA_DCOLS = 2 * NA_WIN_C - 1


def _na_tile_geometry(n_rows):
    n_tiles = n_rows // NA_Q_ROWS

    def geometry(tile):
        slab0 = min(max(NA_Q_ROWS * tile - NA_WIN_R // 2, 0), n_rows - NA_SLAB_ROWS)
        rows = []
        for j in range(NA_Q_ROWS):
            qr = NA_Q_ROWS * tile + j
            r0 = min(max(qr - NA_WIN_R // 2, 0), n_rows - NA_WIN_R)
            rows.append([slab0 + m - qr + NA_WIN_R - 1 if r0 <= slab0 + m < r0 + NA_WIN_R else None
                         for m in range(NA_SLAB_ROWS)])
        return rows

    kinds = [geometry(0), geometry(1), geometry(n_tiles - 1)]
    assert all(geometry(t) == kinds[1] for t in range(1, n_tiles - 1))
    return kinds


def _na_bias_kernel(rpb_ref, o_ref, band_ref, *, kinds):
    base = (pl.program_id(0) * N_HEADS + pl.program_id(1)) * (NA_DROWS * NA_DCOLS)
    qc = lax.broadcasted_iota(jnp.int32, (GRID_W, LANES), 0)
    lane = lax.broadcasted_iota(jnp.int32, (GRID_W, LANES), 1)
    kc = lane % GRID_W
    d_col = kc - qc + (NA_WIN_C - 1)
    c0 = jnp.clip(qc - NA_WIN_C // 2, 0, GRID_W - NA_WIN_C)
    col_ok = (kc >= c0) & (kc < c0 + NA_WIN_C)
    for d in range(NA_DROWS):
        t = jnp.full((GRID_W, LANES), NEG, F32)
        for e in range(NA_DCOLS):
            t = jnp.where(d_col == e, rpb_ref[base + d * NA_DCOLS + e] * LOG2E, t)
        band_ref[d] = jnp.where(col_ok, t, NEG)
    masked = jnp.full((GRID_W, LANES), NEG, F32)
    left = lane < GRID_W
    for kind, rows in enumerate(kinds):
        for j, drow in enumerate(rows):
            for m in range(0, NA_SLAB_ROWS, 2):
                a = masked if drow[m] is None else band_ref[drow[m]]
                b = masked if drow[m + 1] is None else band_ref[drow[m + 1]]
                o_ref[0, kind, 0, j * GRID_W:(j + 1) * GRID_W, m * GRID_W:(m + 2) * GRID_W] = jnp.where(left, a, b)


def na_bias_tables(rpb, n_rows):
    slab = NA_SLAB_ROWS * GRID_W
    return pl.pallas_call(
        functools.partial(_na_bias_kernel, kinds=_na_tile_geometry(n_rows)),
        grid=(DEPTH, N_HEADS),
        in_specs=[pl.BlockSpec(memory_space=pltpu.SMEM)],
        out_specs=pl.BlockSpec((1, 3, 1, ATTN_Q, slab), lambda l, h: (l, 0, h, 0, 0)),
        out_shape=jax.ShapeDtypeStruct((DEPTH, 3, N_HEADS, ATTN_Q, slab), F32),
        scratch_shapes=[pltpu.VMEM((NA_DROWS, GRID_W, LANES), F32)],
        compiler_params=_cparams(32),
        name="na_bias_tables",
    )(rpb.astype(F32).reshape(-1))


def neighbourhood_attention(na, kc, vc, bias, layer, B, L):
    n_rows = L // GRID_W
    n_tiles = L // ATTN_Q
    na3 = na.reshape(B, L, NA_W)
    P = kc.shape[1]
    slab = NA_SLAB_ROWS * GRID_W

    def kind(sub):
        def index(b, i):
            tile = i * NA_TILES + sub
            return (layer, jnp.where(tile == 0, 0, jnp.where(tile == n_tiles - 1, 2, 1)), 0, 0, 0)
        return index

    tq = NA_TILES * ATTN_Q
    return pl.pallas_call(
        functools.partial(_na_kernel, n_rows=n_rows),
        grid=(B, L // tq),
        in_specs=[pl.BlockSpec((1, tq, BRANCH_W), lambda b, i: (b, i, 0)),
                  pl.BlockSpec((1, L, BRANCH_W), lambda b, i: (b, 0, 1)),
                  pl.BlockSpec((1, L, BRANCH_W), lambda b, i: (b, 0, 2)),
                  pl.BlockSpec((1, P, BRANCH_W), lambda b, i: (b, 0, 0)),
                  pl.BlockSpec((1, P, BRANCH_W), lambda b, i: (b, 0, 0))]
                 + [pl.BlockSpec((1, 1, N_HEADS, ATTN_Q, slab), kind(sub)) for sub in range(NA_TILES)],
        out_specs=pl.BlockSpec((1, tq, BRANCH_W), lambda b, i: (b, i, 0)),
        out_shape=jax.ShapeDtypeStruct((B, L, BRANCH_W), F32),
        compiler_params=_cparams(56),
        name="neighbourhood_attention",
    )(na3, na3, na3, kc, vc, *([bias] * NA_TILES)).reshape(B * L, BRANCH_W)


def _attn_kernel(*refs, L, window, gqa, has_ctx, has_sink):
    refs = list(refs)
    q_ref, k_ref, v_ref = refs[:3]
    pos = 3
    if has_ctx:
        kc_ref, vc_ref = refs[pos:pos + 2]
        pos += 2
    if has_sink:
        sink_ref = refs[pos]
        pos += 1
    o_ref = refs[pos]
    n_seq = q_ref.shape[0]
    n_sub = q_ref.shape[1] // ATTN_Q
    for seq, sub in [(a, b) for a in range(n_seq) for b in range(n_sub)]:
        rows = slice(sub * ATTN_Q, (sub + 1) * ATTN_Q)
        if has_ctx:
            kc = kc_ref[seq].astype(BF16)
            vc = vc_ref[seq].astype(BF16)
        if window:
            slab = SWA_SLAB
            q0 = (pl.program_id(1) * n_sub + sub) * ATTN_Q
            start = pl.multiple_of(jnp.clip(q0 - SWA_WINDOW, 0, L - slab), SWA_WINDOW)
            q_pos = q0 + lax.broadcasted_iota(jnp.int32, (ATTN_Q, 1), 0)
            k_pos = start + lax.broadcasted_iota(jnp.int32, (1, slab), 1)
            ok = jnp.abs(k_pos - q_pos) <= SWA_WINDOW
            ks = k_ref[seq, pl.ds(start, slab), :].astype(BF16)
            vs = v_ref[seq, pl.ds(start, slab), :].astype(BF16)
        else:
            ks = k_ref[seq].astype(BF16)
            vs = v_ref[seq].astype(BF16)
        q = q_ref[seq, rows, :].astype(F32)
        kv_w = ks.shape[1]
        halves = [jnp.zeros((ATTN_Q, LANES), F32), jnp.zeros((ATTN_Q, LANES), F32)]
        out = jnp.zeros(q.shape, F32)
        for h in range(N_HEADS):
            if gqa:
                kvh, slot = h // 2, h % 2
                qh = q[:, kvh * LANES:(kvh + 1) * LANES]
                if slot != kvh:
                    qh = pltpu.roll(qh, HEAD_DIM, 1)
                hm = _lane_mask(kv_w, kvh * HEAD_DIM, HEAD_DIM)
            else:
                qh = q
                hm = _lane_mask(kv_w, h * HEAD_DIM, HEAD_DIM)
            qh = jnp.where(hm, qh, 0.0).astype(BF16)
            s = _dot_nt(qh, ks)
            if window:
                s = jnp.where(ok, s, NEG)
            parts = [s]
            if has_ctx:
                parts.append(_dot_nt(qh, kc))
            ps, l = _softmax_parts(parts, sink_ref[h] * LOG2E if has_sink else None)
            o = _dot(ps[0].astype(BF16), vs)
            if has_ctx:
                o = o + _dot(ps[1].astype(BF16), vc)
            o = jnp.where(hm, o / l, 0.0)
            if gqa:
                if slot != kvh:
                    o = pltpu.roll(o, HEAD_DIM, 1)
                halves[kvh] = halves[kvh] + o
            else:
                out = out + o
        if gqa:
            o_ref[seq, rows, 0:LANES] = halves[0]
            o_ref[seq, rows, LANES:2 * LANES] = halves[1]
        else:
            o_ref[seq, rows, :] = out


def _seqs_per_step(B, L, queries):
    ns = max(1, queries // L)
    return ns if B % ns == 0 else 1


def dense_attention(src, cols, B, L, *, window=False, gqa=False, ctx=None, sink=None):
    W = src.shape[1]
    src3 = src.reshape(B, L, W)
    kv_w = SWA_KVW if gqa else BRANCH_W
    qc, kcol, vcol = cols
    tq = min(ATTN_TILES * ATTN_Q, L)
    ns = _seqs_per_step(B, L, ATTN_TILES * ATTN_Q)
    in_specs = [pl.BlockSpec((ns, tq, BRANCH_W), lambda b, i: (b, i, qc)),
                pl.BlockSpec((ns, L, kv_w), lambda b, i: (b, 0, kcol)),
                pl.BlockSpec((ns, L, kv_w), lambda b, i: (b, 0, vcol))]
    args = [src3, src3, src3]
    if ctx is not None:
        P = ctx[0].shape[1]
        in_specs += [pl.BlockSpec((ns, P, kv_w), lambda b, i: (b, 0, 0))] * 2
        args += list(ctx)
    if sink is not None:
        in_specs.append(pl.BlockSpec(memory_space=pltpu.SMEM))
        args.append(sink)
    return pl.pallas_call(
        functools.partial(_attn_kernel, L=L, window=window, gqa=gqa,
                          has_ctx=ctx is not None, has_sink=sink is not None),
        grid=(B // ns, L // tq),
        in_specs=in_specs,
        out_specs=pl.BlockSpec((ns, tq, BRANCH_W), lambda b, i: (b, i, 0)),
        out_shape=jax.ShapeDtypeStruct((B, L, BRANCH_W), F32),
        compiler_params=_cparams(48),
        name="window_attention" if window else "dense_attention",
    )(*args).reshape(B * L, BRANCH_W)


def _diff_kernel(*refs, lam_init, has_ctx):
    refs = list(refs)
    lam_ref, gain_ref, ones_ref, q_ref, k_ref, v_ref = refs[:6]
    if has_ctx:
        kc_ref, vc_ref, o_ref = refs[6:]
    else:
        (o_ref,) = refs[6:]
    lv = lam_ref[...]
    lam = (jnp.exp(jnp.sum(lv[0:1] * lv[1:2], keepdims=True))
           - jnp.exp(jnp.sum(lv[2:3] * lv[3:4], keepdims=True)) + lam_init)
    for seq in range(q_ref.shape[0]):
        kl = k_ref[seq].astype(BF16)
        vl = v_ref[seq].astype(BF16)
        if has_ctx:
            kc = kc_ref[seq].astype(BF16)
            vc = vc_ref[seq].astype(BF16)
        q = q_ref[seq].astype(BF16)
        halves = [jnp.zeros((q.shape[0], LANES), F32) for _ in range(BRANCH_W // LANES)]
        for h in range(N_HEADS):
            o = None
            half = _lane_half(h)
            for mp in range(2):
                mm = _lane_mask(LANES, (h * HEAD_DIM + mp * DIFF_HD) % LANES, DIFF_HD)
                qm = jnp.where(mm, q[:, half], 0.0)
                parts = [_dot_nt(qm, kl[:, half])]
                if has_ctx:
                    parts.append(_dot_nt(qm, kc[:, half]))
                ps, l = _softmax_parts(parts)
                pv = _dot(ps[0].astype(BF16), vl[:, half])
                if has_ctx:
                    pv = pv + _dot(ps[1].astype(BF16), vc[:, half])
                o = pv / l if mp == 0 else o - pv * (lam / l)
            i = half.start // LANES
            halves[i] = halves[i] + jnp.where(_lane_mask(LANES, (h * HEAD_DIM) % LANES, HEAD_DIM), o, 0.0)
        out = jnp.concatenate(halves, axis=1)
        ms = _dot_exact(out * out, ones_ref[...]) * (1.0 / HEAD_DIM)
        o_ref[seq] = out * lax.rsqrt(ms + EPS) * gain_ref[...] * (1.0 - lam_init)


def diff_attention(dif, lam_vec, subln, layer, B, L, ctx=None):
    lam_init = 0.8 - 0.6 * math.exp(-0.3 * layer)
    dif3 = dif.reshape(B, L, DIFF_W)
    head = jnp.arange(BRANCH_W) // HEAD_DIM
    ones = (head[:, None] == head[None, :]).astype(F32)
    gain = jnp.tile(subln.astype(F32), N_HEADS).reshape(1, BRANCH_W)
    tq = min(DIFF_Q, L)
    ns = _seqs_per_step(B, L, DIFF_Q)
    in_specs = [_resident((4, DIFF_HD)), _resident((1, BRANCH_W)), _resident((BRANCH_W, BRANCH_W)),
                pl.BlockSpec((ns, tq, BRANCH_W), lambda b, i: (b, i, 0)),
                pl.BlockSpec((ns, L, BRANCH_W), lambda b, i: (b, 0, 1)),
                pl.BlockSpec((ns, L, BRANCH_W), lambda b, i: (b, 0, 2))]
    args = [lam_vec, gain, ones, dif3, dif3, dif3]
    if ctx is not None:
        P = ctx[0].shape[1]
        in_specs += [pl.BlockSpec((ns, P, BRANCH_W), lambda b, i: (b, 0, 0))] * 2
        args += list(ctx)
    return pl.pallas_call(
        functools.partial(_diff_kernel, lam_init=lam_init, has_ctx=ctx is not None),
        grid=(B // ns, L // tq),
        in_specs=in_specs,
        out_specs=pl.BlockSpec((ns, tq, BRANCH_W), lambda b, i: (b, i, 0)),
        out_shape=jax.ShapeDtypeStruct((B, L, BRANCH_W), F32),
        compiler_params=_cparams(56),
        name="diff_attention",
    )(*args).reshape(B * L, BRANCH_W)


def _filter_kernel(z_ref, w1_ref, b1_ref, w2_ref, b2_ref, w3_ref, fr_ref, ld_ref, o_ref):
    z = z_ref[...]
    tn = z[:, 0:1]
    fr = fr_ref[0]
    g = jnp.sin(fr * (_dot_exact(z, w1_ref[0]) + b1_ref[0]))
    g = jnp.sin(fr * (_dot_exact(g, w2_ref[0]) + b2_ref[0]))
    hf = _dot_exact(g, w3_ref[0]) * jnp.exp(-jnp.exp(ld_ref[0]) * tn)
    row = lax.broadcasted_iota(jnp.int32, (z.shape[0], 1), 0)
    for o in range(HY_ORDER):
        pos = hf[:, (2 * o) * HY_WIDTH:(2 * o + 1) * HY_WIDTH]
        neg = jnp.where(row == 0, 0.0, hf[:, (2 * o + 1) * HY_WIDTH:(2 * o + 2) * HY_WIDTH])
        norm = (jnp.sum(jnp.abs(pos), axis=0, keepdims=True)
                + jnp.sum(jnp.abs(neg), axis=0, keepdims=True) + EPS)
        inv = 1.0 / norm
        o_ref[0, o, 0] = (neg + pos) * inv
        o_ref[0, o, 1] = (neg - pos) * inv


def hyena_filters(L, p):
    tn = jnp.arange(L, dtype=F32) / L
    ang = 2.0 * math.pi * tn[:, None] * jnp.arange(1, HY_FREQS + 1, dtype=F32)[None, :]
    z = jnp.concatenate([tn[:, None], jnp.cos(ang), jnp.sin(ang)], axis=-1)
    z = jnp.pad(z, ((0, 0), (0, HY_HIDDEN - HY_EMB)))
    w1 = jnp.pad(p['hy_w1'], ((0, 0), (0, HY_HIDDEN - HY_EMB), (0, 0)))
    fw = HY_ORDER * 2 * HY_WIDTH
    per_layer = lambda *shape: pl.BlockSpec((1,) + shape, lambda l: (l,) + (0,) * len(shape))
    return pl.pallas_call(
        _filter_kernel,
        grid=(DEPTH,),
        in_specs=[pl.BlockSpec((L, HY_HIDDEN), lambda l: (0, 0)),
                  per_layer(HY_HIDDEN, HY_HIDDEN), per_layer(1, HY_HIDDEN),
                  per_layer(HY_HIDDEN, HY_HIDDEN), per_layer(1, HY_HIDDEN),
                  per_layer(HY_HIDDEN, fw), per_layer(1, HY_HIDDEN), per_layer(1, fw)],
        out_specs=per_layer(HY_ORDER, 2, L, HY_WIDTH),
        out_shape=jax.ShapeDtypeStruct((DEPTH, HY_ORDER, 2, L, HY_WIDTH), F32),
        compiler_params=_cparams(56),
        name="hyena_filters",
    )(z, w1, p['hy_b1'].reshape(DEPTH, 1, HY_HIDDEN), p['hy_w2'], p['hy_b2'].reshape(DEPTH, 1, HY_HIDDEN),
      p['hy_w3'], p['hy_sin_freq'].reshape(DEPTH, 1, HY_HIDDEN), p['hy_log_decay'].reshape(DEPTH, 1, fw))


def dft_table(L, tk, half_sample):
    assert L % LANES == 0 and L & (L - 1) == 0 and L // DFT_LO <= LANES
    return pl.pallas_call(
        functools.partial(_dft_table_kernel, half_sample=half_sample),
        grid=(L // tk,),
        out_specs=pl.BlockSpec((1, 2 * tk, L), lambda i: (i, 0, 0)),
        out_shape=jax.ShapeDtypeStruct((L // tk, 2 * tk, L), BF16),
        compiler_params=_cparams(48),
        name="dft_table",
    )()


DFT_LO = 64


def _dft_table_kernel(o_ref, *, half_sample):
    _, tk2, L = o_ref.shape
    tk = tk2 // 2
    k2 = 2 * (pl.program_id(0) * tk + lax.broadcasted_iota(jnp.int32, (tk, LANES), 0)) + 1
    lane = lax.broadcasted_iota(jnp.int32, (tk, LANES), 1)

    def cos_sin(m):
        ang = ((k2 * m) & (8 * L - 1)).astype(F32) * (math.pi / (4 * L))
        return jnp.cos(ang), jnp.sin(ang)

    c_hi, s_hi = cos_sin(2 * DFT_LO * lane)
    c_lo, s_lo = cos_sin(2 * lane + (1 if half_sample else 0))
    n = lax.broadcasted_iota(jnp.int32, (LANES, L), 1)
    r = lax.broadcasted_iota(jnp.int32, (LANES, L), 0)
    pick_hi = jnp.where(r == (n >> (DFT_LO.bit_length() - 1)), 1.0, 0.0).astype(BF16)
    pick_lo = jnp.where(r == (n & (DFT_LO - 1)), 1.0, 0.0).astype(BF16)

    def copy(t, pick):
        head = t.astype(BF16)
        return _dot(head, pick) + _dot((t - head.astype(F32)).astype(BF16), pick)

    ch, sh = copy(c_hi, pick_hi), copy(s_hi, pick_hi)
    cl, sl = copy(c_lo, pick_lo), copy(s_lo, pick_lo)
    o_ref[0, :tk, :] = (ch * cl - sh * sl).astype(BF16)
    o_ref[0, tk:, :] = (sh * cl + ch * sl).astype(BF16)


def _spectrum_kernel(t_ref, f_ref, o_ref):
    nkt, tk2, _ = t_ref.shape
    tk = tk2 // 2
    sums, diffs = f_ref[0, 0].astype(BF16), f_ref[0, 1].astype(BF16)
    for i in range(nkt):
        rows = slice(i * tk, (i + 1) * tk)
        o_ref[0, 0, rows, :] = _dot(t_ref[i, :tk, :], sums)
        o_ref[0, 1, rows, :] = _dot(t_ref[i, tk:, :], diffs)


def filter_spectra(filt, fwd):
    G, _, L, C = filt.shape
    return pl.pallas_call(
        _spectrum_kernel,
        grid=(G,),
        in_specs=[_resident(fwd.shape), pl.BlockSpec((1, 2, L, C), lambda g: (g, 0, 0, 0))],
        out_specs=pl.BlockSpec((1, 2, L, C), lambda g: (g, 0, 0, 0)),
        out_shape=jax.ShapeDtypeStruct((G, 2, L, C), F32),
        compiler_params=_cparams(48),
        name="filter_spectra",
    )(fwd, filt)


def _hyena_kernel(t_ref, h_ref, u_ref, w_ref, b_ref, skip_ref, o_ref, z_ref, y_ref):
    nkt, tk2, L = t_ref.shape
    tk = tk2 // 2
    C = HY_WIDTH
    row = lax.broadcasted_iota(jnp.int32, (L, 1), 0)

    def short_conv(col):
        sl = slice(col * C, (col + 1) * C)
        u = u_ref[0, :, sl]
        prev = jnp.where(row == 0, 0.0, pltpu.roll(u, 1, 0))
        nxt = jnp.where(row == L - 1, 0.0, pltpu.roll(u, L - 1, 0))
        return prev * w_ref[0:1, sl] + u * w_ref[1:2, sl] + nxt * w_ref[2:3, sl] + b_ref[:, sl]

    y_ref[...] = short_conv(0)
    for o in range(HY_ORDER):
        y = y_ref[...].astype(BF16)
        for i in range(nkt):
            rows = slice(i * tk, (i + 1) * tk)
            acc = _dot(t_ref[i], y)
            yc, ys = acc[:tk], acc[tk:]
            hr = h_ref[0, o, 0, rows, :]
            hi = h_ref[0, o, 1, rows, :]
            z_ref[0, rows, :] = (yc * hr + ys * hi).astype(BF16)
            z_ref[1, rows, :] = (yc * hi - ys * hr).astype(BF16)
        gate = short_conv(1 + o)
        zr, zi = z_ref[0], z_ref[1]
        for i in range(nkt):
            rows = slice(i * tk, (i + 1) * tk)
            conv = (_dot(t_ref[i, :tk, :], zr) - _dot(t_ref[i, tk:, :], zi)) * (1.0 / L)
            new = gate[rows] * (conv + skip_ref[o:o + 1, :] * y_ref[rows, :])
            if o == HY_ORDER - 1:
                o_ref[0, rows, :] = new
            else:
                y_ref[rows, :] = new


def hyena(hy, lp, layer, spec, table, B, L):
    C = HY_WIDTH
    return pl.pallas_call(
        _hyena_kernel,
        grid=(B,),
        in_specs=[_resident(table.shape),
                  pl.BlockSpec((1, HY_ORDER, 2, L, C), lambda b: (layer, 0, 0, 0, 0), pipeline_mode=pl.Buffered(1)),
                  pl.BlockSpec((1, L, HY_IN_W), lambda b: (b, 0, 0)),
                  _resident((3, HY_IN_W)), _resident((1, HY_IN_W)), _resident((HY_ORDER, C))],
        out_specs=pl.BlockSpec((1, L, C), lambda b: (b, 0, 0)),
        out_shape=jax.ShapeDtypeStruct((B, L, C), F32),
        scratch_shapes=[pltpu.VMEM((2, L, C), BF16), pltpu.VMEM((L, C), F32)],
        compiler_params=_cparams(56),
        name="hyena",
    )(table, spec, hy.reshape(B, L, HY_IN_W), lp['hy_short_w'], lp['hy_short_b'].reshape(1, HY_IN_W),
      lp['hy_skip']).reshape(B * L, C)


def _merge_kernel(a_ref, b_ref, c_ref, d_ref, x_ref, mod_ref, nw_ref, wg_ref, wb_ref, wo_ref, *rest, n_cast):
    cast_in, o_ref, cast_out = rest[:n_cast], rest[n_cast], rest[n_cast + 1:]
    _cast_blocks(cast_in, cast_out)
    rows = x_ref.shape[0] // MERGE_SPLIT
    groups = [slice(g * rows, (g + 1) * rows) for g in range(MERGE_SPLIT)]
    hs = [_norm_mod(x_ref[g, :], nw_ref[...], mod_ref[0, 3:4, :], mod_ref[0, 4:5, :]).astype(BF16) for g in groups]
    for g, h in zip(groups, hs):
        merged = None
        for i, br in enumerate((a_ref, b_ref, c_ref, d_ref)):
            gate = _sigmoid(_dot(h, wg_ref[:, MIX_W + i * D_MODEL:MIX_W + (i + 1) * D_MODEL]))
            t = gate * _dot(br[g, :].astype(BF16), wb_ref[i * BRANCH_W:(i + 1) * BRANCH_W, :])
            merged = t if merged is None else merged + t
        o_ref[g, :] = x_ref[g, :] + mod_ref[0, 5:6, :] * _dot(merged.astype(BF16), wo_ref[...])


def merge_block(branches, x, mod, nw, wg, wb, wo, rows_per_b, cast=None):
    T = x.shape[0]
    tm = MERGE_ROWS
    nb = mod.shape[0]
    row = lambda w: pl.BlockSpec((tm, w), lambda i: (i, 0))
    c_in, c_out, c_shape, c_args = _cast_specs(cast, T // tm)
    outs = pl.pallas_call(
        functools.partial(_merge_kernel, n_cast=len(c_args)),
        grid=(T // tm,),
        in_specs=[row(BRANCH_W)] * N_BRANCH + [row(D_MODEL), _mod_spec(nb, rows_per_b // tm),
                                               _resident((1, D_MODEL)),
                                               _resident(wg.shape), _resident(wb.shape), _resident(wo.shape)] + c_in,
        out_specs=[row(D_MODEL)] + c_out,
        out_shape=[jax.ShapeDtypeStruct((T, D_MODEL), F32)] + c_shape,
        compiler_params=_cparams(56),
        name="merge_block",
    )(*branches, x, mod, nw.reshape(1, D_MODEL), wg, wb, wo, *c_args)
    return outs[0], list(outs[1:])


def _heads_in(t):
    B, H, P, d = t.shape
    return t.transpose(0, 2, 1, 3).reshape(B, P, H * d)


_WEIGHT_GROUPS = ('ffn1', 'mix', 'ffn2')


def _run_pass(x, mod_all, p, layer_w, w_f32, final_norm, B, L, spec, tabs, caches):
    ctx_pass = caches is None
    collected = []
    if not ctx_pass:
        rope = rope_tables(L, HEAD_DIM, SWA_QW + SWA_KVW) + rope_tables(L, DIFF_HD, 2 * BRANCH_W)
        bias_all = na_bias_tables(p['na_rpb'], L // GRID_W)
    for l in range(DEPTH):
        lp = {k: v[l] for k, v in p.items()}
        w = layer_w[l]
        mod = mod_all[l]
        cast = {g: (l + 1, w_f32[g]) if ctx_pass and l + 1 < DEPTH else None for g in _WEIGHT_GROUPS}
        nxt = {}
        x, nxt['ffn1'] = ffn_block(x, mod, lp['norm_ffn1'], *w['ffn1'], 0, L, cast=cast['ffn1'])
        na, swa, hy, dif = in_projection(x, mod, lp['norm_mix'], w['mix'][0], L,
                                         None if ctx_pass else rope, F32 if ctx_pass else BF16)
        if ctx_pass:
            a_o = dense_attention(na, (0, 1, 2), B, L)
            b_o = dense_attention(swa, (0, 2, 3), B, L, gqa=True, sink=lp['swa_sink'])
            d_o = diff_attention(dif, lp['diff_lambda'], lp['diff_subln'], l, B, L)
            collected.append((na, swa, dif))
        else:
            ck_na, cv_na, ck_swa, cv_swa, ck_d, cv_d = (_heads_in(t[:, l]).astype(BF16) for t in caches)
            a_o = neighbourhood_attention(na, ck_na, cv_na, bias_all, l, B, L)
            b_o = dense_attention(swa, (0, 2, 3), B, L, window=True, gqa=True,
                                  ctx=(ck_swa, cv_swa), sink=lp['swa_sink'])
            d_o = diff_attention(dif, lp['diff_lambda'], lp['diff_subln'], l, B, L, ctx=(ck_d, cv_d))
        c_o = hyena(hy, lp, l, spec, tabs, B, L)
        x, nxt['mix'] = merge_block((a_o, b_o, c_o, d_o), x, mod, lp['norm_mix'], *w['mix'], L, cast=cast['mix'])
        x, nxt['ffn2'] = ffn_block(x, mod, lp['norm_ffn2'], *w['ffn2'], 6, L,
                                   final_w=final_norm if l == DEPTH - 1 else None, cast=cast['ffn2'])
        if ctx_pass and l + 1 < DEPTH:
            layer_w.append(nxt)
    return x, collected


_CACHE_SLOTS = ((0, N_HEADS, N_HEADS), (0, 2 * N_HEADS, N_HEADS),
                (1, N_HEADS, SWA_KV_HEADS), (1, N_HEADS + SWA_KV_HEADS, SWA_KV_HEADS),
                (2, N_HEADS, N_HEADS), (2, 2 * N_HEADS, N_HEADS))


def _cache_kernel(*refs):
    srcs, outs = refs[:3 * DEPTH], refs[3 * DEPTH:]
    for l in range(DEPTH):
        for o_ref, (src, slot0, n) in zip(outs, _CACHE_SLOTS):
            x_ref = srcs[3 * l + src]
            for h in range(n):
                o_ref[0, l, h] = x_ref[:, (slot0 + h) * HEAD_DIM:(slot0 + h + 1) * HEAD_DIM]


def _new_caches(collected, B, L):
    srcs = [t for layer in collected for t in layer]
    return pl.pallas_call(
        _cache_kernel,
        grid=(B,),
        in_specs=[pl.BlockSpec((L, t.shape[1]), lambda b: (b, 0)) for t in srcs],
        out_specs=[pl.BlockSpec((1, DEPTH, n, L, HEAD_DIM), lambda b: (b, 0, 0, 0, 0)) for _, _, n in _CACHE_SLOTS],
        out_shape=[jax.ShapeDtypeStruct((B, DEPTH, n, L, HEAD_DIM), F32) for _, _, n in _CACHE_SLOTS],
        compiler_params=_cparams(48),
        name="cache_outputs",
    )(*srcs)


def _hyena_setup(L, p, tk):
    filt = hyena_filters(L, p)
    spec = filter_spectra(filt.reshape(DEPTH * HY_ORDER, 2, L, HY_WIDTH), dft_table(L, tk, half_sample=False))
    return spec.reshape(DEPTH, HY_ORDER, 2, L, HY_WIDTH), dft_table(L, tk, half_sample=True)


def kernel(x_prompt, x_sample, cache_na_k, cache_na_v, cache_swa_k, cache_swa_v, cache_diff_k, cache_diff_v, c, c_ctx, w_ada, b_ada, norm_ffn1, norm_mix, norm_ffn2, final_norm, ffn1_w1, ffn1_w3, ffn1_w2, ffn2_w1, ffn2_w3, ffn2_w2, w_in, w_branch, w_out, na_rpb, swa_sink, hy_short_w, hy_short_b, hy_w1, hy_b1, hy_w2, hy_b2, hy_w3, hy_sin_freq, hy_log_decay, hy_skip, diff_lambda, diff_subln):
    B_ctx, L_ctx, _ = x_prompt.shape
    B_den, L_den, _ = x_sample.shape
    p = {
        'norm_ffn1': norm_ffn1, 'norm_mix': norm_mix, 'norm_ffn2': norm_ffn2,
        'na_rpb': na_rpb, 'swa_sink': swa_sink, 'hy_short_w': hy_short_w, 'hy_short_b': hy_short_b,
        'hy_w1': hy_w1, 'hy_b1': hy_b1, 'hy_w2': hy_w2, 'hy_b2': hy_b2, 'hy_w3': hy_w3,
        'hy_sin_freq': hy_sin_freq, 'hy_log_decay': hy_log_decay, 'hy_skip': hy_skip,
        'diff_lambda': diff_lambda, 'diff_subln': diff_subln,
    }
    w_f32 = {'ffn1': [ffn1_w1, ffn1_w3, ffn1_w2], 'ffn2': [ffn2_w1, ffn2_w3, ffn2_w2],
             'mix': [w_in, w_branch.reshape(DEPTH, N_BRANCH * BRANCH_W, D_MODEL), w_out]}
    layer_w = [{g: [w[0].astype(BF16) for w in ws] for g, ws in w_f32.items()}]

    cond = jnp.concatenate([c, c_ctx[None, :]], axis=0)
    rows = 8 * ((cond.shape[0] + 7) // 8)
    cond = jnp.pad(cond, ((0, rows - cond.shape[0]), (0, 0)))
    mod = ada_modulation(cond, w_ada, b_ada).reshape(DEPTH, rows, N_MOD, D_MODEL)
    mod_den = mod[:, :B_den]
    mod_ctx = mod[:, B_den:B_den + 1]

    spec_c, tabs_c = _hyena_setup(L_ctx, p, min(L_ctx, HY_FREQ_TILE))
    y_ctx, collected = _run_pass(x_prompt.reshape(B_ctx * L_ctx, D_MODEL), mod_ctx, p, layer_w, w_f32,
                                 final_norm, B_ctx, L_ctx, spec_c, tabs_c, None)
    new_caches = _new_caches(collected, B_ctx, L_ctx)

    spec_d, tabs_d = _hyena_setup(L_den, p, min(L_den, HY_FREQ_TILE))
    caches = (cache_na_k, cache_na_v, cache_swa_k, cache_swa_v, cache_diff_k, cache_diff_v)
    y_den, _ = _run_pass(x_sample.reshape(B_den * L_den, D_MODEL), mod_den, p, layer_w, w_f32,
                         final_norm, B_den, L_den, spec_d, tabs_d, caches)
    return (y_ctx.reshape(B_ctx, L_ctx, D_MODEL), y_den.reshape(B_den, L_den, D_MODEL), *new_caches)
```

```python
import functools
import math

import jax
import jax.numpy as jnp
from jax import lax
from jax.experimental import pallas as pl
from jax.experimental.pallas import tpu as pltpu

F32 = jnp.float32
BF16 = jnp.bfloat16

D_MODEL = 1024
DEPTH = 4
GRID_W = 64
N_BRANCH = 4
BRANCH_W = D_MODEL // 4
HEAD_DIM = 64
N_HEADS = BRANCH_W // HEAD_DIM
NA_WIN_R = 8
NA_WIN_C = 16
SWA_KV_HEADS = N_HEADS // 2
SWA_WINDOW = 128
HY_WIDTH = BRANCH_W
HY_ORDER = 2
HY_FREQS = 16
HY_EMB = 1 + 2 * HY_FREQS
HY_HIDDEN = 64
DIFF_HD = 32
D_FF = 128 * ((8 * D_MODEL // 3 + 127) // 128)
ROPE_BASE = 10000.0
EPS = 1e-6
NEG = -1e30
N_MOD = 9
NA_W = 3 * BRANCH_W
SWA_QW = BRANCH_W
SWA_KVW = SWA_KV_HEADS * HEAD_DIM
SWA_W = SWA_QW + 2 * SWA_KVW
HY_IN_W = 3 * HY_WIDTH
DIFF_W = 3 * BRANCH_W
GATE_W = N_BRANCH * D_MODEL
MIX_W = NA_W + SWA_W + HY_IN_W + DIFF_W
IN_W = MIX_W + GATE_W

LOG2E = math.log2(math.e)
QK_SCALE = HEAD_DIM ** -0.5 * LOG2E
DIFF_QK_SCALE = DIFF_HD ** -0.5 * LOG2E

LANES = 128
MXU_DIM = 256
MIB = 1024 * 1024

FFN_ROWS = 1024
FFN_CAST_ROWS = 512
FFN_SPLIT = 2
MERGE_ROWS = 512
MERGE_SPLIT = 2
PROJ_ROWS = 512
PROJ_SPLIT = 2
FFN_CHUNK = 768
ATTN_Q = 256
ATTN_TILES = 4
NA_TILES = 4
HY_FREQ_TILE = 512
DIFF_Q = 512
NA_Q_ROWS = ATTN_Q // GRID_W
NA_SLAB_ROWS = NA_Q_ROWS + NA_WIN_R
SWA_SLAB = ATTN_Q + 2 * SWA_WINDOW
assert FFN_CHUNK % MXU_DIM == 0 and ATTN_Q % GRID_W == 0 and SWA_SLAB % SWA_WINDOW == 0


def _cparams(vmem_mib):
    return pltpu.CompilerParams(vmem_limit_bytes=vmem_mib * MIB)


def _resident(shape):
    nd = len(shape)
    return pl.BlockSpec(shape, lambda *_: (0,) * nd, pipeline_mode=pl.Buffered(1))


def _dot(a, b):
    return jnp.dot(a, b, preferred_element_type=F32)


def _dot_nt(a, b):
    return lax.dot_general(a, b, (((1,), (1,)), ((), ())), preferred_element_type=F32)


def _dot_exact(a, b):
    return jnp.dot(a, b, preferred_element_type=F32, precision=lax.Precision.HIGHEST)


def _sigmoid(x):
    return 1.0 / (1.0 + jnp.exp(-x))


def _norm_mod(x, nw, shift, scale):
    return x * lax.rsqrt(jnp.mean(x * x, axis=-1, keepdims=True) + EPS) * (nw * (1.0 + scale)) + shift


def _mod_spec(nb, tiles_per_b):
    if nb == 1:
        return pl.BlockSpec((1, N_MOD, D_MODEL), lambda i: (0, 0, 0))
    return pl.BlockSpec((1, N_MOD, D_MODEL), lambda i: (i // tiles_per_b, 0, 0))


def _ada_kernel(c_ref, w_ref, b_ref, o_ref):
    c = c_ref[...]
    s = (c * _sigmoid(c)).astype(BF16)
    o_ref[0] = _dot(s, w_ref[0].astype(BF16)) + b_ref[0]


def ada_modulation(cond, w_ada, b_ada):
    rows = cond.shape[0]
    width = N_MOD * D_MODEL
    tn = 9 * LANES
    return pl.pallas_call(
        _ada_kernel,
        grid=(DEPTH, width // tn),
        in_specs=[pl.BlockSpec((rows, D_MODEL), lambda l, j: (0, 0)),
                  pl.BlockSpec((1, D_MODEL, tn), lambda l, j: (l, 0, j)),
                  pl.BlockSpec((1, 1, tn), lambda l, j: (l, 0, j))],
        out_specs=pl.BlockSpec((1, rows, tn), lambda l, j: (l, 0, j)),
        out_shape=jax.ShapeDtypeStruct((DEPTH, rows, width), F32),
        compiler_params=_cparams(32),
        name="ada_modulation",
    )(cond, w_ada, b_ada.reshape(DEPTH, 1, width))


def _cast_specs(cast, steps):
    in_specs, out_specs, out_shape, args = [], [], [], []
    if cast is not None:
        layer, sources = cast
        for src in sources:
            _, rows, cols = src.shape
            in_specs.append(pl.BlockSpec((None, rows // steps, cols), lambda i: (layer, i, 0)))
            out_specs.append(pl.BlockSpec((rows // steps, cols), lambda i: (i, 0)))
            out_shape.append(jax.ShapeDtypeStruct((rows, cols), BF16))
            args.append(src)
    return in_specs, out_specs, out_shape, args


def _cast_blocks(cast_in, cast_out):
    for src, dst in zip(cast_in, cast_out):
        dst[...] = src[...].astype(BF16)


def _ffn_kernel(x_ref, mod_ref, nw_ref, w1_ref, w3_ref, w2_ref, *rest, mod_base, final, n_cast):
    rest = list(rest)
    fw_ref = rest.pop(0) if final else None
    cast_in, o_ref, cast_out = rest[:n_cast], rest[n_cast], rest[n_cast + 1:]
    _cast_blocks(cast_in, cast_out)
    shift = mod_ref[0, mod_base:mod_base + 1, :]
    scale = mod_ref[0, mod_base + 1:mod_base + 2, :]
    gate = mod_ref[0, mod_base + 2:mod_base + 3, :]
    rows = x_ref.shape[0] // FFN_SPLIT
    groups = [slice(g * rows, (g + 1) * rows) for g in range(FFN_SPLIT)]
    hs = [_norm_mod(x_ref[g, :], nw_ref[...], shift, scale).astype(BF16) for g in groups]
    for g, h in zip(groups, hs):
        acc = jnp.zeros((rows, D_MODEL), F32)
        for lo in range(0, D_FF, FFN_CHUNK):
            hi = min(lo + FFN_CHUNK, D_FF)
            a = _dot(h, w1_ref[:, lo:hi])
            b = _dot(h, w3_ref[:, lo:hi])
            u = (a * _sigmoid(a) * b).astype(BF16)
            acc = acc + _dot(u, w2_ref[lo:hi, :])
        y = x_ref[g, :] + 0.5 * gate * acc
        if final:
            y = y * lax.rsqrt(jnp.mean(y * y, axis=-1, keepdims=True) + EPS) * fw_ref[...]
        o_ref[g, :] = y


def ffn_block(x, mod, nw, w1, w3, w2, mod_base, rows_per_b, final_w=None, cast=None):
    T = x.shape[0]
    tm = FFN_ROWS if cast is None else FFN_CAST_ROWS
    steps = T // tm
    nb = mod.shape[0]
    final = final_w is not None
    in_specs = [pl.BlockSpec((tm, D_MODEL), lambda i: (i, 0)),
                _mod_spec(nb, rows_per_b // tm),
                _resident((1, D_MODEL)),
                _resident(w1.shape), _resident(w3.shape), _resident(w2.shape)]
    args = [x, mod, nw.reshape(1, D_MODEL), w1, w3, w2]
    if final:
        in_specs.append(_resident((1, D_MODEL)))
        args.append(final_w.reshape(1, D_MODEL))
    c_in, c_out, c_shape, c_args = _cast_specs(cast, steps)
    outs = pl.pallas_call(
        functools.partial(_ffn_kernel, mod_base=mod_base, final=final, n_cast=len(c_args)),
        grid=(steps,),
        in_specs=in_specs + c_in,
        out_specs=[pl.BlockSpec((tm, D_MODEL), lambda i: (i, 0))] + c_out,
        out_shape=[jax.ShapeDtypeStruct((T, D_MODEL), F32)] + c_shape,
        compiler_params=_cparams(56),
        name="ffn_block",
    )(*args, *c_args)
    return outs[0], list(outs[1:])


def _rope_chunk(x, cos, sin_a, sin_b, dist):
    return x * cos + pltpu.roll(x, LANES - dist, 1) * sin_a + pltpu.roll(x, dist, 1) * sin_b


def _proj_kernel(x_ref, mod_ref, nw_ref, w_ref, *rest, rope):
    if rope:
        (cs_ref, sa_ref, sb_ref, cd_ref, da_ref, db_ref,
         na_ref, swa_ref, hy_ref, dif_ref) = rest
    else:
        na_ref, swa_ref, hy_ref, dif_ref = rest
    rows = x_ref.shape[0] // PROJ_SPLIT
    groups = [slice(g * rows, (g + 1) * rows) for g in range(PROJ_SPLIT)]
    hs = [_norm_mod(x_ref[g, :], nw_ref[...], mod_ref[0, 3:4, :], mod_ref[0, 4:5, :]).astype(BF16) for g in groups]
    q_chunks = BRANCH_W // LANES
    for g, h in zip(groups, hs):
        o = 0
        s = _dot(h, w_ref[:, o:o + NA_W])
        o += NA_W
        na_ref[g, :BRANCH_W] = (s[:, :BRANCH_W] * QK_SCALE).astype(na_ref.dtype)
        na_ref[g, BRANCH_W:] = s[:, BRANCH_W:].astype(na_ref.dtype)
        s = _dot(h, w_ref[:, o:o + SWA_W])
        o += SWA_W
        n_rot = (SWA_QW + SWA_KVW) // LANES
        for c in range(SWA_W // LANES):
            sl = slice(c * LANES, (c + 1) * LANES)
            chunk = s[:, sl]
            if rope and c < n_rot:
                chunk = _rope_chunk(chunk, cs_ref[g, sl], sa_ref[g, sl], sb_ref[g, sl], HEAD_DIM // 4)
            if c < q_chunks:
                chunk = chunk * QK_SCALE
            swa_ref[g, sl] = chunk.astype(swa_ref.dtype)
        hy_ref[g, :] = _dot(h, w_ref[:, o:o + HY_IN_W])
        o += HY_IN_W
        s = _dot(h, w_ref[:, o:o + DIFF_W])
        o += DIFF_W
        n_rot = 2 * BRANCH_W // LANES
        for c in range(DIFF_W // LANES):
            sl = slice(c * LANES, (c + 1) * LANES)
            chunk = s[:, sl]
            if rope and c < n_rot:
                chunk = _rope_chunk(chunk, cd_ref[g, sl], da_ref[g, sl], db_ref[g, sl], DIFF_HD // 4)
            if c < q_chunks:
                chunk = chunk * DIFF_QK_SCALE
            dif_ref[g, sl] = chunk.astype(dif_ref.dtype)


def in_projection(x, mod, nw, w_in, rows_per_b, rope_tabs, qkv_dtype):
    T = x.shape[0]
    tm = PROJ_ROWS
    nb = mod.shape[0]
    rope = rope_tabs is not None
    if rope:
        pos_tiles = rows_per_b // tm
        grid = (pos_tiles, T // rows_per_b)
        row_tile = lambda p, b: (b * pos_tiles + p, 0)
        mod_spec = pl.BlockSpec((1, N_MOD, D_MODEL), lambda p, b: (b, 0, 0))
    else:
        grid = (T // tm,)
        row_tile = lambda i: (i, 0)
        mod_spec = _mod_spec(nb, rows_per_b // tm)
    in_specs = [pl.BlockSpec((tm, D_MODEL), row_tile),
                mod_spec,
                _resident((1, D_MODEL)),
                pl.BlockSpec((D_MODEL, MIX_W), lambda *_: (0, 0), pipeline_mode=pl.Buffered(1))]
    args = [x, mod, nw.reshape(1, D_MODEL), w_in]
    if rope:
        for t in rope_tabs:
            in_specs.append(pl.BlockSpec((tm, t.shape[1]), lambda p, b: (p, 0)))
            args.append(t)
    widths = (NA_W, SWA_W, HY_IN_W, DIFF_W)
    dtypes = (qkv_dtype, qkv_dtype, F32, qkv_dtype)
    return pl.pallas_call(
        functools.partial(_proj_kernel, rope=rope),
        grid=grid,
        in_specs=in_specs,
        out_specs=[pl.BlockSpec((tm, w), row_tile) for w in widths],
        out_shape=[jax.ShapeDtypeStruct((T, w), dt) for w, dt in zip(widths, dtypes)],
        compiler_params=_cparams(56),
        name="in_projection",
    )(*args)


def rope_tables(L, dh, width):
    nf = dh // 4
    t = jnp.arange(L)
    freqs = ROPE_BASE ** (-jnp.arange(nf, dtype=F32) / nf)
    ang_r = (t // GRID_W).astype(F32)[:, None] * freqs
    ang_c = (t % GRID_W).astype(F32)[:, None] * freqs
    cr, sr, cc, sc = jnp.cos(ang_r), jnp.sin(ang_r), jnp.cos(ang_c), jnp.sin(ang_c)
    z = jnp.zeros_like(sr)
    reps = width // dh
    cos = jnp.tile(jnp.concatenate([cr, cr, cc, cc], axis=-1), (1, reps))
    sin_a = jnp.tile(jnp.concatenate([-sr, z, -sc, z], axis=-1), (1, reps))
    sin_b = jnp.tile(jnp.concatenate([z, sr, z, sc], axis=-1), (1, reps))
    return cos, sin_a, sin_b


def _lane_half(head):
    lo = head * HEAD_DIM // LANES * LANES
    return slice(lo, lo + LANES)


def _lane_mask(width, lo, n):
    lane = lax.broadcasted_iota(jnp.int32, (1, width), 1)
    return (lane >= lo) & (lane < lo + n)


def _softmax_parts(parts, extra_logit=None):
    m = parts[0].max(axis=-1, keepdims=True)
    for s in parts[1:]:
        m = jnp.maximum(m, s.max(axis=-1, keepdims=True))
    if extra_logit is not None:
        m = jnp.maximum(m, extra_logit)
    ps = [jnp.exp2(s - m) for s in parts]
    l = ps[0].sum(axis=-1, keepdims=True)
    for p in ps[1:]:
        l = l + p.sum(axis=-1, keepdims=True)
    if extra_logit is not None:
        l = l + jnp.exp2(extra_logit - m)
    return ps, l


def _na_kernel(q_ref, k_ref, v_ref, kc_ref, vc_ref, *rest, n_rows):
    bias_refs, o_ref = rest[:-1], rest[-1]
    slab = NA_SLAB_ROWS * GRID_W
    kc = kc_ref[0].astype(BF16)
    vc = vc_ref[0].astype(BF16)
    for sub, bias_ref in enumerate(bias_refs):
        tile = pl.program_id(1) * len(bias_refs) + sub
        rows = slice(sub * ATTN_Q, (sub + 1) * ATTN_Q)
        row0 = jnp.clip(NA_Q_ROWS * tile - NA_WIN_R // 2, 0, n_rows - NA_SLAB_ROWS)
        start = pl.multiple_of(row0 * GRID_W, GRID_W)
        ks = k_ref[0, pl.ds(start, slab), :].astype(BF16)
        vs = v_ref[0, pl.ds(start, slab), :].astype(BF16)
        q = q_ref[0, rows, :].astype(BF16)
        out = jnp.zeros(q.shape, F32)
        for h in range(N_HEADS):
            hm = _lane_mask(BRANCH_W, h * HEAD_DIM, HEAD_DIM)
            half = _lane_half(h)
            qh = jnp.where(_lane_mask(LANES, (h * HEAD_DIM) % LANES, HEAD_DIM), q[:, half], 0.0)
            s_win = _dot_nt(qh, ks[:, half]) + bias_ref[0, 0, h]
            s_ctx = _dot_nt(qh, kc[:, half])
            (p_win, p_ctx), l = _softmax_parts([s_win, s_ctx])
            o = _dot(p_win.astype(BF16), vs) + _dot(p_ctx.astype(BF16), vc)
            out = out + jnp.where(hm, o / l, 0.0)
        o_ref[0, rows, :] = out


NA_DROWS = 2 * NA_WIN_R - 1
NA_DCOLS = 2 * NA_WIN_C - 1


def _na_tile_geometry(n_rows):
    n_tiles = n_rows // NA_Q_ROWS

    def geometry(tile):
        slab0 = min(max(NA_Q_ROWS * tile - NA_WIN_R // 2, 0), n_rows - NA_SLAB_ROWS)
        rows = []
        for j in range(NA_Q_ROWS):
            qr = NA_Q_ROWS * tile + j
            r0 = min(max(qr - NA_WIN_R // 2, 0), n_rows - NA_WIN_R)
            rows.append([slab0 + m - qr + NA_WIN_R - 1 if r0 <= slab0 + m < r0 + NA_WIN_R else None
                         for m in range(NA_SLAB_ROWS)])
        return rows

    kinds = [geometry(0), geometry(1), geometry(n_tiles - 1)]
    assert all(geometry(t) == kinds[1] for t in range(1, n_tiles - 1))
    return kinds


def _na_bias_kernel(rpb_ref, o_ref, band_ref, *, kinds):
    base = (pl.program_id(0) * N_HEADS + pl.program_id(1)) * (NA_DROWS * NA_DCOLS)
    qc = lax.broadcasted_iota(jnp.int32, (GRID_W, LANES), 0)
    lane = lax.broadcasted_iota(jnp.int32, (GRID_W, LANES), 1)
    kc = lane % GRID_W
    d_col = kc - qc + (NA_WIN_C - 1)
    c0 = jnp.clip(qc - NA_WIN_C // 2, 0, GRID_W - NA_WIN_C)
    col_ok = (kc >= c0) & (kc < c0 + NA_WIN_C)
    for d in range(NA_DROWS):
        t = jnp.full((GRID_W, LANES), NEG, F32)
        for e in range(NA_DCOLS):
            t = jnp.where(d_col == e, rpb_ref[base + d * NA_DCOLS + e] * LOG2E, t)
        band_ref[d] = jnp.where(col_ok, t, NEG)
    masked = jnp.full((GRID_W, LANES), NEG, F32)
    left = lane < GRID_W
    for kind, rows in enumerate(kinds):
        for j, drow in enumerate(rows):
            for m in range(0, NA_SLAB_ROWS, 2):
                a = masked if drow[m] is None else band_ref[drow[m]]
                b = masked if drow[m + 1] is None else band_ref[drow[m + 1]]
                o_ref[0, kind, 0, j * GRID_W:(j + 1) * GRID_W, m * GRID_W:(m + 2) * GRID_W] = jnp.where(left, a, b)


def na_bias_tables(rpb, n_rows):
    slab = NA_SLAB_ROWS * GRID_W
    return pl.pallas_call(
        functools.partial(_na_bias_kernel, kinds=_na_tile_geometry(n_rows)),
        grid=(DEPTH, N_HEADS),
        in_specs=[pl.BlockSpec(memory_space=pltpu.SMEM)],
        out_specs=pl.BlockSpec((1, 3, 1, ATTN_Q, slab), lambda l, h: (l, 0, h, 0, 0)),
        out_shape=jax.ShapeDtypeStruct((DEPTH, 3, N_HEADS, ATTN_Q, slab), F32),
        scratch_shapes=[pltpu.VMEM((NA_DROWS, GRID_W, LANES), F32)],
        compiler_params=_cparams(32),
        name="na_bias_tables",
    )(rpb.astype(F32).reshape(-1))


def neighbourhood_attention(na, kc, vc, bias, layer, B, L):
    n_rows = L // GRID_W
    n_tiles = L // ATTN_Q
    na3 = na.reshape(B, L, NA_W)
    P = kc.shape[1]
    slab = NA_SLAB_ROWS * GRID_W

    def kind(sub):
        def index(b, i):
            tile = i * NA_TILES + sub
            return (layer, jnp.where(tile == 0, 0, jnp.where(tile == n_tiles - 1, 2, 1)), 0, 0, 0)
        return index

    tq = NA_TILES * ATTN_Q
    return pl.pallas_call(
        functools.partial(_na_kernel, n_rows=n_rows),
        grid=(B, L // tq),
        in_specs=[pl.BlockSpec((1, tq, BRANCH_W), lambda b, i: (b, i, 0)),
                  pl.BlockSpec((1, L, BRANCH_W), lambda b, i: (b, 0, 1)),
                  pl.BlockSpec((1, L, BRANCH_W), lambda b, i: (b, 0, 2)),
                  pl.BlockSpec((1, P, BRANCH_W), lambda b, i: (b, 0, 0)),
                  pl.BlockSpec((1, P, BRANCH_W), lambda b, i: (b, 0, 0))]
                 + [pl.BlockSpec((1, 1, N_HEADS, ATTN_Q, slab), kind(sub)) for sub in range(NA_TILES)],
        out_specs=pl.BlockSpec((1, tq, BRANCH_W), lambda b, i: (b, i, 0)),
        out_shape=jax.ShapeDtypeStruct((B, L, BRANCH_W), F32),
        compiler_params=_cparams(56),
        name="neighbourhood_attention",
    )(na3, na3, na3, kc, vc, *([bias] * NA_TILES)).reshape(B * L, BRANCH_W)


def _attn_kernel(*refs, L, window, gqa, has_ctx, has_sink):
    refs = list(refs)
    q_ref, k_ref, v_ref = refs[:3]
    pos = 3
    if has_ctx:
        kc_ref, vc_ref = refs[pos:pos + 2]
        pos += 2
    if has_sink:
        sink_ref = refs[pos]
        pos += 1
    o_ref = refs[pos]
    n_seq = q_ref.shape[0]
    n_sub = q_ref.shape[1] // ATTN_Q
    for seq, sub in [(a, b) for a in range(n_seq) for b in range(n_sub)]:
        rows = slice(sub * ATTN_Q, (sub + 1) * ATTN_Q)
        if has_ctx:
            kc = kc_ref[seq].astype(BF16)
            vc = vc_ref[seq].astype(BF16)
        if window:
            slab = SWA_SLAB
            q0 = (pl.program_id(1) * n_sub + sub) * ATTN_Q
            start = pl.multiple_of(jnp.clip(q0 - SWA_WINDOW, 0, L - slab), SWA_WINDOW)
            q_pos = q0 + lax.broadcasted_iota(jnp.int32, (ATTN_Q, 1), 0)
            k_pos = start + lax.broadcasted_iota(jnp.int32, (1, slab), 1)
            ok = jnp.abs(k_pos - q_pos) <= SWA_WINDOW
            ks = k_ref[seq, pl.ds(start, slab), :].astype(BF16)
            vs = v_ref[seq, pl.ds(start, slab), :].astype(BF16)
        else:
            ks = k_ref[seq].astype(BF16)
            vs = v_ref[seq].astype(BF16)
        q = q_ref[seq, rows, :].astype(F32)
        kv_w = ks.shape[1]
        halves = [jnp.zeros((ATTN_Q, LANES), F32), jnp.zeros((ATTN_Q, LANES), F32)]
        out = jnp.zeros(q.shape, F32)
        for h in range(N_HEADS):
            if gqa:
                kvh, slot = h // 2, h % 2
                qh = q[:, kvh * LANES:(kvh + 1) * LANES]
                if slot != kvh:
                    qh = pltpu.roll(qh, HEAD_DIM, 1)
                hm = _lane_mask(kv_w, kvh * HEAD_DIM, HEAD_DIM)
            else:
                qh = q
                hm = _lane_mask(kv_w, h * HEAD_DIM, HEAD_DIM)
            qh = jnp.where(hm, qh, 0.0).astype(BF16)
            s = _dot_nt(qh, ks)
            if window:
                s = jnp.where(ok, s, NEG)
            parts = [s]
            if has_ctx:
                parts.append(_dot_nt(qh, kc))
            ps, l = _softmax_parts(parts, sink_ref[h] * LOG2E if has_sink else None)
            o = _dot(ps[0].astype(BF16), vs)
            if has_ctx:
                o = o + _dot(ps[1].astype(BF16), vc)
            o = jnp.where(hm, o / l, 0.0)
            if gqa:
                if slot != kvh:
                    o = pltpu.roll(o, HEAD_DIM, 1)
                halves[kvh] = halves[kvh] + o
            else:
                out = out + o
        if gqa:
            o_ref[seq, rows, 0:LANES] = halves[0]
            o_ref[seq, rows, LANES:2 * LANES] = halves[1]
        else:
            o_ref[seq, rows, :] = out


def _seqs_per_step(B, L, queries):
    ns = max(1, queries // L)
    return ns if B % ns == 0 else 1


def dense_attention(src, cols, B, L, *, window=False, gqa=False, ctx=None, sink=None):
    W = src.shape[1]
    src3 = src.reshape(B, L, W)
    kv_w = SWA_KVW if gqa else BRANCH_W
    qc, kcol, vcol = cols
    tq = min(ATTN_TILES * ATTN_Q, L)
    ns = _seqs_per_step(B, L, ATTN_TILES * ATTN_Q)
    in_specs = [pl.BlockSpec((ns, tq, BRANCH_W), lambda b, i: (b, i, qc)),
                pl.BlockSpec((ns, L, kv_w), lambda b, i: (b, 0, kcol)),
                pl.BlockSpec((ns, L, kv_w), lambda b, i: (b, 0, vcol))]
    args = [src3, src3, src3]
    if ctx is not None:
        P = ctx[0].shape[1]
        in_specs += [pl.BlockSpec((ns, P, kv_w), lambda b, i: (b, 0, 0))] * 2
        args += list(ctx)
    if sink is not None:
        in_specs.append(pl.BlockSpec(memory_space=pltpu.SMEM))
        args.append(sink)
    return pl.pallas_call(
        functools.partial(_attn_kernel, L=L, window=window, gqa=gqa,
                          has_ctx=ctx is not None, has_sink=sink is not None),
        grid=(B // ns, L // tq),
        in_specs=in_specs,
        out_specs=pl.BlockSpec((ns, tq, BRANCH_W), lambda b, i: (b, i, 0)),
        out_shape=jax.ShapeDtypeStruct((B, L, BRANCH_W), F32),
        compiler_params=_cparams(48),
        name="window_attention" if window else "dense_attention",
    )(*args).reshape(B * L, BRANCH_W)


def _diff_kernel(*refs, lam_init, has_ctx):
    refs = list(refs)
    lam_ref, gain_ref, ones_ref, q_ref, k_ref, v_ref = refs[:6]
    if has_ctx:
        kc_ref, vc_ref, o_ref = refs[6:]
    else:
        (o_ref,) = refs[6:]
    lv = lam_ref[...]
    lam = (jnp.exp(jnp.sum(lv[0:1] * lv[1:2], keepdims=True))
           - jnp.exp(jnp.sum(lv[2:3] * lv[3:4], keepdims=True)) + lam_init)
    for seq in range(q_ref.shape[0]):
        kl = k_ref[seq].astype(BF16)
        vl = v_ref[seq].astype(BF16)
        if has_ctx:
            kc = kc_ref[seq].astype(BF16)
            vc = vc_ref[seq].astype(BF16)
        q = q_ref[seq].astype(BF16)
        halves = [jnp.zeros((q.shape[0], LANES), F32) for _ in range(BRANCH_W // LANES)]
        for h in range(N_HEADS):
            o = None
            half = _lane_half(h)
            for mp in range(2):
                mm = _lane_mask(LANES, (h * HEAD_DIM + mp * DIFF_HD) % LANES, DIFF_HD)
                qm = jnp.where(mm, q[:, half], 0.0)
                parts = [_dot_nt(qm, kl[:, half])]
                if has_ctx:
                    parts.append(_dot_nt(qm, kc[:, half]))
                ps, l = _softmax_parts(parts)
                pv = _dot(ps[0].astype(BF16), vl[:, half])
                if has_ctx:
                    pv = pv + _dot(ps[1].astype(BF16), vc[:, half])
                o = pv / l if mp == 0 else o - pv * (lam / l)
            i = half.start // LANES
            halves[i] = halves[i] + jnp.where(_lane_mask(LANES, (h * HEAD_DIM) % LANES, HEAD_DIM), o, 0.0)
        out = jnp.concatenate(halves, axis=1)
        ms = _dot_exact(out * out, ones_ref[...]) * (1.0 / HEAD_DIM)
        o_ref[seq] = out * lax.rsqrt(ms + EPS) * gain_ref[...] * (1.0 - lam_init)


def diff_attention(dif, lam_vec, subln, layer, B, L, ctx=None):
    lam_init = 0.8 - 0.6 * math.exp(-0.3 * layer)
    dif3 = dif.reshape(B, L, DIFF_W)
    head = jnp.arange(BRANCH_W) // HEAD_DIM
    ones = (head[:, None] == head[None, :]).astype(F32)
    gain = jnp.tile(subln.astype(F32), N_HEADS).reshape(1, BRANCH_W)
    tq = min(DIFF_Q, L)
    ns = _seqs_per_step(B, L, DIFF_Q)
    in_specs = [_resident((4, DIFF_HD)), _resident((1, BRANCH_W)), _resident((BRANCH_W, BRANCH_W)),
                pl.BlockSpec((ns, tq, BRANCH_W), lambda b, i: (b, i, 0)),
                pl.BlockSpec((ns, L, BRANCH_W), lambda b, i: (b, 0, 1)),
                pl.BlockSpec((ns, L, BRANCH_W), lambda b, i: (b, 0, 2))]
    args = [lam_vec, gain, ones, dif3, dif3, dif3]
    if ctx is not None:
        P = ctx[0].shape[1]
        in_specs += [pl.BlockSpec((ns, P, BRANCH_W), lambda b, i: (b, 0, 0))] * 2
        args += list(ctx)
    return pl.pallas_call(
        functools.partial(_diff_kernel, lam_init=lam_init, has_ctx=ctx is not None),
        grid=(B // ns, L // tq),
        in_specs=in_specs,
        out_specs=pl.BlockSpec((ns, tq, BRANCH_W), lambda b, i: (b, i, 0)),
        out_shape=jax.ShapeDtypeStruct((B, L, BRANCH_W), F32),
        compiler_params=_cparams(56),
        name="diff_attention",
    )(*args).reshape(B * L, BRANCH_W)


def _filter_kernel(z_ref, w1_ref, b1_ref, w2_ref, b2_ref, w3_ref, fr_ref, ld_ref, o_ref):
    z = z_ref[...]
    tn = z[:, 0:1]
    fr = fr_ref[0]
    g = jnp.sin(fr * (_dot_exact(z, w1_ref[0]) + b1_ref[0]))
    g = jnp.sin(fr * (_dot_exact(g, w2_ref[0]) + b2_ref[0]))
    hf = _dot_exact(g, w3_ref[0]) * jnp.exp(-jnp.exp(ld_ref[0]) * tn)
    row = lax.broadcasted_iota(jnp.int32, (z.shape[0], 1), 0)
    for o in range(HY_ORDER):
        pos = hf[:, (2 * o) * HY_WIDTH:(2 * o + 1) * HY_WIDTH]
        neg = jnp.where(row == 0, 0.0, hf[:, (2 * o + 1) * HY_WIDTH:(2 * o + 2) * HY_WIDTH])
        norm = (jnp.sum(jnp.abs(pos), axis=0, keepdims=True)
                + jnp.sum(jnp.abs(neg), axis=0, keepdims=True) + EPS)
        inv = 1.0 / norm
        o_ref[0, o, 0] = (neg + pos) * inv
        o_ref[0, o, 1] = (neg - pos) * inv


def hyena_filters(L, p):
    tn = jnp.arange(L, dtype=F32) / L
    ang = 2.0 * math.pi * tn[:, None] * jnp.arange(1, HY_FREQS + 1, dtype=F32)[None, :]
    z = jnp.concatenate([tn[:, None], jnp.cos(ang), jnp.sin(ang)], axis=-1)
    z = jnp.pad(z, ((0, 0), (0, HY_HIDDEN - HY_EMB)))
    w1 = jnp.pad(p['hy_w1'], ((0, 0), (0, HY_HIDDEN - HY_EMB), (0, 0)))
    fw = HY_ORDER * 2 * HY_WIDTH
    per_layer = lambda *shape: pl.BlockSpec((1,) + shape, lambda l: (l,) + (0,) * len(shape))
    return pl.pallas_call(
        _filter_kernel,
        grid=(DEPTH,),
        in_specs=[pl.BlockSpec((L, HY_HIDDEN), lambda l: (0, 0)),
                  per_layer(HY_HIDDEN, HY_HIDDEN), per_layer(1, HY_HIDDEN),
                  per_layer(HY_HIDDEN, HY_HIDDEN), per_layer(1, HY_HIDDEN),
                  per_layer(HY_HIDDEN, fw), per_layer(1, HY_HIDDEN), per_layer(1, fw)],
        out_specs=per_layer(HY_ORDER, 2, L, HY_WIDTH),
        out_shape=jax.ShapeDtypeStruct((DEPTH, HY_ORDER, 2, L, HY_WIDTH), F32),
        compiler_params=_cparams(56),
        name="hyena_filters",
    )(z, w1, p['hy_b1'].reshape(DEPTH, 1, HY_HIDDEN), p['hy_w2'], p['hy_b2'].reshape(DEPTH, 1, HY_HIDDEN),
      p['hy_w3'], p['hy_sin_freq'].reshape(DEPTH, 1, HY_HIDDEN), p['hy_log_decay'].reshape(DEPTH, 1, fw))


def dft_table(L, tk, half_sample):
    assert L % LANES == 0 and L & (L - 1) == 0 and L // DFT_LO <= LANES
    return pl.pallas_call(
        functools.partial(_dft_table_kernel, half_sample=half_sample),
        grid=(L // tk,),
        out_specs=pl.BlockSpec((1, 2 * tk, L), lambda i: (i, 0, 0)),
        out_shape=jax.ShapeDtypeStruct((L // tk, 2 * tk, L), BF16),
        compiler_params=_cparams(48),
        name="dft_table",
    )()


DFT_LO = 64


def _dft_table_kernel(o_ref, *, half_sample):
    _, tk2, L = o_ref.shape
    tk = tk2 // 2
    k2 = 2 * (pl.program_id(0) * tk + lax.broadcasted_iota(jnp.int32, (tk, LANES), 0)) + 1
    lane = lax.broadcasted_iota(jnp.int32, (tk, LANES), 1)

    def cos_sin(m):
        ang = ((k2 * m) & (8 * L - 1)).astype(F32) * (math.pi / (4 * L))
        return jnp.cos(ang), jnp.sin(ang)

    c_hi, s_hi = cos_sin(2 * DFT_LO * lane)
    c_lo, s_lo = cos_sin(2 * lane + (1 if half_sample else 0))
    n = lax.broadcasted_iota(jnp.int32, (LANES, L), 1)
    r = lax.broadcasted_iota(jnp.int32, (LANES, L), 0)
    pick_hi = jnp.where(r == (n >> (DFT_LO.bit_length() - 1)), 1.0, 0.0).astype(BF16)
    pick_lo = jnp.where(r == (n & (DFT_LO - 1)), 1.0, 0.0).astype(BF16)

    def copy(t, pick):
        head = t.astype(BF16)
        return _dot(head, pick) + _dot((t - head.astype(F32)).astype(BF16), pick)

    ch, sh = copy(c_hi, pick_hi), copy(s_hi, pick_hi)
    cl, sl = copy(c_lo, pick_lo), copy(s_lo, pick_lo)
    o_ref[0, :tk, :] = (ch * cl - sh * sl).astype(BF16)
    o_ref[0, tk:, :] = (sh * cl + ch * sl).astype(BF16)


def _spectrum_kernel(t_ref, f_ref, o_ref):
    nkt, tk2, _ = t_ref.shape
    tk = tk2 // 2
    sums, diffs = f_ref[0, 0].astype(BF16), f_ref[0, 1].astype(BF16)
    for i in range(nkt):
        rows = slice(i * tk, (i + 1) * tk)
        o_ref[0, 0, rows, :] = _dot(t_ref[i, :tk, :], sums)
        o_ref[0, 1, rows, :] = _dot(t_ref[i, tk:, :], diffs)


def filter_spectra(filt, fwd):
    G, _, L, C = filt.shape
    return pl.pallas_call(
        _spectrum_kernel,
        grid=(G,),
        in_specs=[_resident(fwd.shape), pl.BlockSpec((1, 2, L, C), lambda g: (g, 0, 0, 0))],
        out_specs=pl.BlockSpec((1, 2, L, C), lambda g: (g, 0, 0, 0)),
        out_shape=jax.ShapeDtypeStruct((G, 2, L, C), F32),
        compiler_params=_cparams(48),
        name="filter_spectra",
    )(fwd, filt)


def _hyena_kernel(t_ref, h_ref, u_ref, w_ref, b_ref, skip_ref, o_ref, z_ref, y_ref):
    nkt, tk2, L = t_ref.shape
    tk = tk2 // 2
    C = HY_WIDTH
    row = lax.broadcasted_iota(jnp.int32, (L, 1), 0)

    def short_conv(col):
        sl = slice(col * C, (col + 1) * C)
        u = u_ref[0, :, sl]
        prev = jnp.where(row == 0, 0.0, pltpu.roll(u, 1, 0))
        nxt = jnp.where(row == L - 1, 0.0, pltpu.roll(u, L - 1, 0))
        return prev * w_ref[0:1, sl] + u * w_ref[1:2, sl] + nxt * w_ref[2:3, sl] + b_ref[:, sl]

    y_ref[...] = short_conv(0)
    for o in range(HY_ORDER):
        y = y_ref[...].astype(BF16)
        for i in range(nkt):
            rows = slice(i * tk, (i + 1) * tk)
            acc = _dot(t_ref[i], y)
            yc, ys = acc[:tk], acc[tk:]
            hr = h_ref[0, o, 0, rows, :]
            hi = h_ref[0, o, 1, rows, :]
            z_ref[0, rows, :] = (yc * hr + ys * hi).astype(BF16)
            z_ref[1, rows, :] = (yc * hi - ys * hr).astype(BF16)
        gate = short_conv(1 + o)
        zr, zi = z_ref[0], z_ref[1]
        for i in range(nkt):
            rows = slice(i * tk, (i + 1) * tk)
            conv = (_dot(t_ref[i, :tk, :], zr) - _dot(t_ref[i, tk:, :], zi)) * (1.0 / L)
            new = gate[rows] * (conv + skip_ref[o:o + 1, :] * y_ref[rows, :])
            if o == HY_ORDER - 1:
                o_ref[0, rows, :] = new
            else:
                y_ref[rows, :] = new


def hyena(hy, lp, layer, spec, table, B, L):
    C = HY_WIDTH
    return pl.pallas_call(
        _hyena_kernel,
        grid=(B,),
        in_specs=[_resident(table.shape),
                  pl.BlockSpec((1, HY_ORDER, 2, L, C), lambda b: (layer, 0, 0, 0, 0), pipeline_mode=pl.Buffered(1)),
                  pl.BlockSpec((1, L, HY_IN_W), lambda b: (b, 0, 0)),
                  _resident((3, HY_IN_W)), _resident((1, HY_IN_W)), _resident((HY_ORDER, C))],
        out_specs=pl.BlockSpec((1, L, C), lambda b: (b, 0, 0)),
        out_shape=jax.ShapeDtypeStruct((B, L, C), F32),
        scratch_shapes=[pltpu.VMEM((2, L, C), BF16), pltpu.VMEM((L, C), F32)],
        compiler_params=_cparams(56),
        name="hyena",
    )(table, spec, hy.reshape(B, L, HY_IN_W), lp['hy_short_w'], lp['hy_short_b'].reshape(1, HY_IN_W),
      lp['hy_skip']).reshape(B * L, C)


def _merge_kernel(a_ref, b_ref, c_ref, d_ref, x_ref, mod_ref, nw_ref, wg_ref, wb_ref, wo_ref, *rest, n_cast):
    cast_in, o_ref, cast_out = rest[:n_cast], rest[n_cast], rest[n_cast + 1:]
    _cast_blocks(cast_in, cast_out)
    rows = x_ref.shape[0] // MERGE_SPLIT
    groups = [slice(g * rows, (g + 1) * rows) for g in range(MERGE_SPLIT)]
    hs = [_norm_mod(x_ref[g, :], nw_ref[...], mod_ref[0, 3:4, :], mod_ref[0, 4:5, :]).astype(BF16) for g in groups]
    for g, h in zip(groups, hs):
        merged = None
        for i, br in enumerate((a_ref, b_ref, c_ref, d_ref)):
            gate = _sigmoid(_dot(h, wg_ref[:, MIX_W + i * D_MODEL:MIX_W + (i + 1) * D_MODEL]))
            t = gate * _dot(br[g, :].astype(BF16), wb_ref[i * BRANCH_W:(i + 1) * BRANCH_W, :])
            merged = t if merged is None else merged + t
        o_ref[g, :] = x_ref[g, :] + mod_ref[0, 5:6, :] * _dot(merged.astype(BF16), wo_ref[...])


def merge_block(branches, x, mod, nw, wg, wb, wo, rows_per_b, cast=None):
    T = x.shape[0]
    tm = MERGE_ROWS
    nb = mod.shape[0]
    row = lambda w: pl.BlockSpec((tm, w), lambda i: (i, 0))
    c_in, c_out, c_shape, c_args = _cast_specs(cast, T // tm)
    outs = pl.pallas_call(
        functools.partial(_merge_kernel, n_cast=len(c_args)),
        grid=(T // tm,),
        in_specs=[row(BRANCH_W)] * N_BRANCH + [row(D_MODEL), _mod_spec(nb, rows_per_b // tm),
                                               _resident((1, D_MODEL)),
                                               _resident(wg.shape), _resident(wb.shape), _resident(wo.shape)] + c_in,
        out_specs=[row(D_MODEL)] + c_out,
        out_shape=[jax.ShapeDtypeStruct((T, D_MODEL), F32)] + c_shape,
        compiler_params=_cparams(56),
        name="merge_block",
    )(*branches, x, mod, nw.reshape(1, D_MODEL), wg, wb, wo, *c_args)
    return outs[0], list(outs[1:])


def _heads_in(t):
    B, H, P, d = t.shape
    return t.transpose(0, 2, 1, 3).reshape(B, P, H * d)


_WEIGHT_GROUPS = ('ffn1', 'mix', 'ffn2')


def _run_pass(x, mod_all, p, layer_w, w_f32, final_norm, B, L, spec, tabs, caches):
    ctx_pass = caches is None
    collected = []
    if not ctx_pass:
        rope = rope_tables(L, HEAD_DIM, SWA_QW + SWA_KVW) + rope_tables(L, DIFF_HD, 2 * BRANCH_W)
        bias_all = na_bias_tables(p['na_rpb'], L // GRID_W)
    for l in range(DEPTH):
        lp = {k: v[l] for k, v in p.items()}
        w = layer_w[l]
        mod = mod_all[l]
        cast = {g: (l + 1, w_f32[g]) if ctx_pass and l + 1 < DEPTH else None for g in _WEIGHT_GROUPS}
        nxt = {}
        x, nxt['ffn1'] = ffn_block(x, mod, lp['norm_ffn1'], *w['ffn1'], 0, L, cast=cast['ffn1'])
        na, swa, hy, dif = in_projection(x, mod, lp['norm_mix'], w['mix'][0], L,
                                         None if ctx_pass else rope, F32 if ctx_pass else BF16)
        if ctx_pass:
            a_o = dense_attention(na, (0, 1, 2), B, L)
            b_o = dense_attention(swa, (0, 2, 3), B, L, gqa=True, sink=lp['swa_sink'])
            d_o = diff_attention(dif, lp['diff_lambda'], lp['diff_subln'], l, B, L)
            collected.append((na, swa, dif))
        else:
            ck_na, cv_na, ck_swa, cv_swa, ck_d, cv_d = (_heads_in(t[:, l]).astype(BF16) for t in caches)
            a_o = neighbourhood_attention(na, ck_na, cv_na, bias_all, l, B, L)
            b_o = dense_attention(swa, (0, 2, 3), B, L, window=True, gqa=True,
                                  ctx=(ck_swa, cv_swa), sink=lp['swa_sink'])
            d_o = diff_attention(dif, lp['diff_lambda'], lp['diff_subln'], l, B, L, ctx=(ck_d, cv_d))
        c_o = hyena(hy, lp, l, spec, tabs, B, L)
        x, nxt['mix'] = merge_block((a_o, b_o, c_o, d_o), x, mod, lp['norm_mix'], *w['mix'], L, cast=cast['mix'])
        x, nxt['ffn2'] = ffn_block(x, mod, lp['norm_ffn2'], *w['ffn2'], 6, L,
                                   final_w=final_norm if l == DEPTH - 1 else None, cast=cast['ffn2'])
        if ctx_pass and l + 1 < DEPTH:
            layer_w.append(nxt)
    return x, collected


_CACHE_SLOTS = ((0, N_HEADS, N_HEADS), (0, 2 * N_HEADS, N_HEADS),
                (1, N_HEADS, SWA_KV_HEADS), (1, N_HEADS + SWA_KV_HEADS, SWA_KV_HEADS),
                (2, N_HEADS, N_HEADS), (2, 2 * N_HEADS, N_HEADS))


def _cache_kernel(*refs):
    srcs, outs = refs[:3 * DEPTH], refs[3 * DEPTH:]
    for l in range(DEPTH):
        for o_ref, (src, slot0, n) in zip(outs, _CACHE_SLOTS):
            x_ref = srcs[3 * l + src]
            for h in range(n):
                o_ref[0, l, h] = x_ref[:, (slot0 + h) * HEAD_DIM:(slot0 + h + 1) * HEAD_DIM]


def _new_caches(collected, B, L):
    srcs = [t for layer in collected for t in layer]
    return pl.pallas_call(
        _cache_kernel,
        grid=(B,),
        in_specs=[pl.BlockSpec((L, t.shape[1]), lambda b: (b, 0)) for t in srcs],
        out_specs=[pl.BlockSpec((1, DEPTH, n, L, HEAD_DIM), lambda b: (b, 0, 0, 0, 0)) for _, _, n in _CACHE_SLOTS],
        out_shape=[jax.ShapeDtypeStruct((B, DEPTH, n, L, HEAD_DIM), F32) for _, _, n in _CACHE_SLOTS],
        compiler_params=_cparams(48),
        name="cache_outputs",
    )(*srcs)


def _hyena_setup(L, p, tk):
    filt = hyena_filters(L, p)
    spec = filter_spectra(filt.reshape(DEPTH * HY_ORDER, 2, L, HY_WIDTH), dft_table(L, tk, half_sample=False))
    return spec.reshape(DEPTH, HY_ORDER, 2, L, HY_WIDTH), dft_table(L, tk, half_sample=True)


def kernel(x_prompt, x_sample, cache_na_k, cache_na_v, cache_swa_k, cache_swa_v, cache_diff_k, cache_diff_v, c, c_ctx, w_ada, b_ada, norm_ffn1, norm_mix, norm_ffn2, final_norm, ffn1_w1, ffn1_w3, ffn1_w2, ffn2_w1, ffn2_w3, ffn2_w2, w_in, w_branch, w_out, na_rpb, swa_sink, hy_short_w, hy_short_b, hy_w1, hy_b1, hy_w2, hy_b2, hy_w3, hy_sin_freq, hy_log_decay, hy_skip, diff_lambda, diff_subln):
    B_ctx, L_ctx, _ = x_prompt.shape
    B_den, L_den, _ = x_sample.shape
    p = {
        'norm_ffn1': norm_ffn1, 'norm_mix': norm_mix, 'norm_ffn2': norm_ffn2,
        'na_rpb': na_rpb, 'swa_sink': swa_sink, 'hy_short_w': hy_short_w, 'hy_short_b': hy_short_b,
        'hy_w1': hy_w1, 'hy_b1': hy_b1, 'hy_w2': hy_w2, 'hy_b2': hy_b2, 'hy_w3': hy_w3,
        'hy_sin_freq': hy_sin_freq, 'hy_log_decay': hy_log_decay, 'hy_skip': hy_skip,
        'diff_lambda': diff_lambda, 'diff_subln': diff_subln,
    }
    w_f32 = {'ffn1': [ffn1_w1, ffn1_w3, ffn1_w2], 'ffn2': [ffn2_w1, ffn2_w3, ffn2_w2],
             'mix': [w_in, w_branch.reshape(DEPTH, N_BRANCH * BRANCH_W, D_MODEL), w_out]}
    layer_w = [{g: [w[0].astype(BF16) for w in ws] for g, ws in w_f32.items()}]

    cond = jnp.concatenate([c, c_ctx[None, :]], axis=0)
    rows = 8 * ((cond.shape[0] + 7) // 8)
    cond = jnp.pad(cond, ((0, rows - cond.shape[0]), (0, 0)))
    mod = ada_modulation(cond, w_ada, b_ada).reshape(DEPTH, rows, N_MOD, D_MODEL)
    mod_den = mod[:, :B_den]
    mod_ctx = mod[:, B_den:B_den + 1]

    spec_c, tabs_c = _hyena_setup(L_ctx, p, min(L_ctx, HY_FREQ_TILE))
    y_ctx, collected = _run_pass(x_prompt.reshape(B_ctx * L_ctx, D_MODEL), mod_ctx, p, layer_w, w_f32,
                                 final_norm, B_ctx, L_ctx, spec_c, tabs_c, None)
    new_caches = _new_caches(collected, B_ctx, L_ctx)

    spec_d, tabs_d = _hyena_setup(L_den, p, min(L_den, HY_FREQ_TILE))
    caches = (cache_na_k, cache_na_v, cache_swa_k, cache_swa_v, cache_diff_k, cache_diff_v)
    y_den, _ = _run_pass(x_sample.reshape(B_den * L_den, D_MODEL), mod_den, p, layer_w, w_f32,
                         final_norm, B_den, L_den, spec_d, tabs_d, caches)
    return (y_ctx.reshape(B_ctx, L_ctx, D_MODEL), y_den.reshape(B_den, L_den, D_MODEL), *new_caches)
```
